```python
import jax, jax.numpy as jnp
from jax import lax
import numpy as np

D_MODEL = 1024
BATCH = 16
SEQ = 256
DEPTH = 4
DEC_BATCH = 8
DEC_SEQ = 1024
PAST_LEN = 512

GRID_W = 64
BLOCK = 128
ROPE_BASE = 10000.0
EPS = 1e-6
NEG_INF = -1e30

N_EVEN = (DEPTH + 1) // 2
N_ODD = DEPTH // 2

H_A = 8
NOPE_A = 64
ROPE_A = 32
V_A = 64
Q_RANK = 256
KV_RANK = 128
QK_A = NOPE_A + ROPE_A
W_A = H_A * V_A
W_B = 512
CONV_W = 3
H_C = 8
KV_C = 2
HD_C = 64
WINDOW = 128
W_C = H_C * HD_C
G_D = 8
W_D = 512
CHUNK = 128
DG_D = W_D // G_D

EVEN_SPLIT = (Q_RANK, KV_RANK, ROPE_A, W_A, W_B, W_B, W_B, W_B)
ODD_SPLIT = (W_C, KV_C * HD_C, KV_C * HD_C, W_C, W_D, W_D, W_D)
P_EVEN = sum(EVEN_SPLIT)
P_ODD = sum(ODD_SPLIT)
MIX_EVEN = W_A + W_B
MIX_ODD = W_C + W_D

kernel_name = "hybrid_diffusion_mla_conv_swa_gmlp_step"


def rmsnorm(x, g):
    xf = x.astype(jnp.float32)
    y = xf * lax.rsqrt(jnp.mean(xf * xf, axis=-1, keepdims=True) + EPS)
    return (y * g.astype(jnp.float32)).astype(x.dtype)


def split_cols(x, sizes):
    return jnp.split(x, np.cumsum(sizes)[:-1].tolist(), axis=-1)


def modulation(cvec, w, bias):
    m = (jax.nn.silu(cvec) @ w + bias)[..., None, :]
    return jnp.split(m, 3, axis=-1)


def axial_tables(n, rot_dim):
    rows = n // GRID_W
    r = jnp.repeat(jnp.arange(rows, dtype=jnp.float32), GRID_W)
    col = jnp.tile(jnp.arange(GRID_W, dtype=jnp.float32), rows)
    nf = rot_dim // 4
    inv = ROPE_BASE ** (-jnp.arange(nf, dtype=jnp.float32) / nf)
    ar = r[:, None] * inv
    ac = col[:, None] * inv
    return (jnp.cos(ar), jnp.sin(ar), jnp.cos(ac), jnp.sin(ac))


def axial_rope(x, tabs):
    cos_r, sin_r, cos_c, sin_c = tabs
    bshape = (x.shape[1],) + (1,) * (x.ndim - 3) + (cos_r.shape[-1],)

    def rot(xa, cos, sin):
        cos = cos.reshape(bshape).astype(x.dtype)
        sin = sin.reshape(bshape).astype(x.dtype)
        x1, x2 = jnp.split(xa, 2, axis=-1)
        return jnp.concatenate([x1 * cos - x2 * sin, x2 * cos + x1 * sin], axis=-1)

    xr, xc = jnp.split(x, 2, axis=-1)
    return jnp.concatenate([rot(xr, cos_r, sin_r), rot(xc, cos_c, sin_c)], axis=-1)


def _attend(q, k, v, mask, sink, scale):
    s = jnp.einsum('bqhgd,bkhd->bhgqk', q, k).astype(jnp.float32) * scale
    if mask is not None:
        s = jnp.where(mask, s, NEG_INF)
    if sink is not None:
        sk = jnp.broadcast_to(sink.astype(jnp.float32)[None, :, :, None, None], s.shape[:-1] + (1,))
        p = jax.nn.softmax(jnp.concatenate([s, sk], axis=-1), axis=-1)[..., :-1]
    else:
        p = jax.nn.softmax(s, axis=-1)
    return jnp.einsum('bhgqk,bkhd->bqhgd', p.astype(v.dtype), v)


def dense_attention(q, k, v, sink, scale):
    b, n, h, d = q.shape
    hk = k.shape[2]
    g = h // hk
    nb = n // BLOCK
    sink_g = None if sink is None else sink.reshape(hk, g)
    qb = q.reshape(b, nb, BLOCK, hk, g, d).transpose(1, 0, 2, 3, 4, 5)
    out = lax.map(lambda qq: _attend(qq, k, v, None, sink_g, scale), qb)
    return out.transpose(1, 0, 2, 3, 4, 5).reshape(b, n, h, v.shape[-1])


def window_attention(q, k, v, k_ctx, v_ctx, sink, scale):
    b, n, h, d = q.shape
    hk = k.shape[2]
    g = h // hk
    nb = n // BLOCK
    span = BLOCK + 2 * WINDOW
    sink_g = sink.reshape(hk, g)
    qb = q.reshape(b, nb, BLOCK, hk, g, d).transpose(1, 0, 2, 3, 4, 5)
    pad = ((0, 0), (WINDOW, WINDOW), (0, 0), (0, 0))
    kp = jnp.pad(k, pad)
    vp = jnp.pad(v, pad)
    ctx_mask = jnp.ones((BLOCK, k_ctx.shape[1]), dtype=bool)

    def one_block(args):
        i, qq = args
        start = i * BLOCK
        kl = lax.dynamic_slice_in_dim(kp, start, span, axis=1)
        vl = lax.dynamic_slice_in_dim(vp, start, span, axis=1)
        qpos = start + jnp.arange(BLOCK)
        kpos = start - WINDOW + jnp.arange(span)
        local = (jnp.abs(qpos[:, None] - kpos[None, :]) <= WINDOW) & (kpos >= 0)[None, :] & (kpos < n)[None, :]
        mask = jnp.concatenate([ctx_mask, local], axis=1)
        return _attend(qq, jnp.concatenate([k_ctx, kl], axis=1), jnp.concatenate([v_ctx, vl], axis=1),
                       mask, sink_g, scale)

    out = lax.map(one_block, (jnp.arange(nb), qb))
    return out.transpose(1, 0, 2, 3, 4, 5).reshape(b, n, h, v.shape[-1])


def even_mixer(h, w_in, w_out, gq, gkv, w_uq, w_uk, w_uv, conv_w, conv_b, tabs, ctx):
    b, n, _ = h.shape
    q_dn, kv_dn, k_r, g_a, gate_b, gate_c, x_in, g_b = split_cols(h @ w_in, EVEN_SPLIT)
    q = (rmsnorm(q_dn, gq) @ w_uq).reshape(b, n, H_A, QK_A)
    ckv = rmsnorm(kv_dn, gkv)
    q_nope, q_rope = q[..., :NOPE_A], q[..., NOPE_A:]
    if ctx is None:
        ckv_all, kr_all = ckv, k_r
    else:
        q_rope = axial_rope(q_rope, tabs)
        ckv_all = jnp.concatenate([ctx[0], ckv], axis=1)
        kr_all = jnp.concatenate([ctx[1], axial_rope(k_r, tabs)], axis=1)
    nk = ckv_all.shape[1]
    k_nope = (ckv_all @ w_uk).reshape(b, nk, H_A, NOPE_A)
    v = (ckv_all @ w_uv).reshape(b, nk, H_A, V_A)
    k = jnp.concatenate([k_nope, jnp.broadcast_to(kr_all[:, :, None, :], (b, nk, H_A, ROPE_A))], axis=-1)
    attn = dense_attention(jnp.concatenate([q_nope, q_rope], axis=-1), k, v, None, QK_A ** -0.5)
    a_out = attn.reshape(b, n, W_A) * jax.nn.silu(g_a)
    t = gate_c * x_in
    half = CONV_W // 2
    tp = jnp.pad(t, ((0, 0), (half, half), (0, 0)))
    conv = conv_b
    for j in range(CONV_W):
        conv = conv + tp[:, j:j + n] * conv_w[j]
    b_out = gate_b * conv * jax.nn.silu(g_b)
    y = jnp.concatenate([a_out, b_out], axis=-1) @ w_out
    return y, ckv, k_r


def odd_mixer(h, w_in, w_out, sink, gg, ws, bs, tabs, ctx):
    b, n, _ = h.shape
    q, k, v, g_c, u, vv, g_d = split_cols(h @ w_in, ODD_SPLIT)
    q = q.reshape(b, n, H_C, HD_C)
    k = k.reshape(b, n, KV_C, HD_C)
    v = v.reshape(b, n, KV_C, HD_C)
    scale = HD_C ** -0.5
    if ctx is None:
        attn = dense_attention(q, k, v, sink, scale)
    else:
        attn = window_attention(axial_rope(q, tabs), axial_rope(k, tabs), v, ctx[0], ctx[1], sink, scale)
    c_out = attn.reshape(b, n, W_C) * jax.nn.silu(g_c)
    vv = rmsnorm(jax.nn.gelu(vv), gg).reshape(b, n // CHUNK, CHUNK, G_D, DG_D)
    mixed = jnp.einsum('gpq,bcqgd->bcpgd', ws, vv) + bs.T[:, :, None]
    d_out = jax.nn.gelu(u) * mixed.reshape(b, n, W_D) * jax.nn.silu(g_d)
    y = jnp.concatenate([c_out, d_out], axis=-1) @ w_out
    return y, k, v


def setup_inputs(seed: int = 0) -> dict:
    key = jax.random.key(seed)
    ks = iter(jax.random.split(key, 40))

    def nrm(shape, s):
        return jax.random.normal(next(ks), shape, jnp.float32) * s

    def gain(shape):
        return 1.0 + nrm(shape, 0.02)

    D = D_MODEL
    return {
        "x_prompt": nrm((BATCH, SEQ, D), 1.0),
        "x_sample": nrm((DEC_BATCH, DEC_SEQ, D), 1.0),
        "cache_mla_ckv": nrm((DEC_BATCH, N_EVEN, PAST_LEN, KV_RANK), 1.0),
        "cache_mla_krope": nrm((DEC_BATCH, N_EVEN, PAST_LEN, ROPE_A), 1.0),
        "cache_swa_k": nrm((DEC_BATCH, N_ODD, PAST_LEN, KV_C, HD_C), 1.0),
        "cache_swa_v": nrm((DEC_BATCH, N_ODD, PAST_LEN, KV_C, HD_C), 1.0),
        "c": nrm((DEC_BATCH, D), 1.0),
        "c_ctx": nrm((D,), 1.0),
        "ada_w": nrm((DEPTH, D, 3 * D), 0.5 * D ** -0.5),
        "ada_b": nrm((DEPTH, 3 * D), 0.01),
        "norm_g": gain((DEPTH, D)),
        "ev_w_in": nrm((N_EVEN, D, P_EVEN), D ** -0.5),
        "ev_w_out": nrm((N_EVEN, MIX_EVEN, D), MIX_EVEN ** -0.5),
        "mla_gq": gain((N_EVEN, Q_RANK)),
        "mla_gkv": gain((N_EVEN, KV_RANK)),
        "mla_w_uq": nrm((N_EVEN, Q_RANK, H_A * QK_A), Q_RANK ** -0.5),
        "mla_w_uk": nrm((N_EVEN, KV_RANK, H_A * NOPE_A), KV_RANK ** -0.5),
        "mla_w_uv": nrm((N_EVEN, KV_RANK, W_A), KV_RANK ** -0.5),
        "conv_w": nrm((N_EVEN, CONV_W, W_B), CONV_W ** -0.5),
        "conv_b": nrm((N_EVEN, W_B), 0.01),
        "od_w_in": nrm((N_ODD, D, P_ODD), D ** -0.5),
        "od_w_out": nrm((N_ODD, MIX_ODD, D), MIX_ODD ** -0.5),
        "swa_sink": nrm((N_ODD, H_C), 0.5),
        "gmlp_g": gain((N_ODD, W_D)),
        "gmlp_ws": nrm((N_ODD, G_D, CHUNK, CHUNK), CHUNK ** -0.5),
        "gmlp_bs": nrm((N_ODD, G_D, CHUNK), 0.01),
        "final_g": gain((D,)),
    }


def reference(x_prompt, x_sample, cache_mla_ckv, cache_mla_krope, cache_swa_k, cache_swa_v, c, c_ctx,
              ada_w, ada_b, norm_g, ev_w_in, ev_w_out, mla_gq, mla_gkv, mla_w_uq, mla_w_uk, mla_w_uv,
              conv_w, conv_b, od_w_in, od_w_out, swa_sink, gmlp_g, gmlp_ws, gmlp_bs, final_g):
    def layer(l, x, cvec, tabs_a, tabs_c, ctx):
        shift, scale, gate = modulation(cvec, ada_w[l], ada_b[l])
        h = rmsnorm(x, norm_g[l]) * (1 + scale) + shift
        i = l // 2
        if l % 2 == 0:
            y, s0, s1 = even_mixer(h, ev_w_in[i], ev_w_out[i], mla_gq[i], mla_gkv[i], mla_w_uq[i],
                                   mla_w_uk[i], mla_w_uv[i], conv_w[i], conv_b[i], tabs_a, ctx)
        else:
            y, s0, s1 = odd_mixer(h, od_w_in[i], od_w_out[i], swa_sink[i], gmlp_g[i], gmlp_ws[i],
                                  gmlp_bs[i], tabs_c, ctx)
        return x + gate * y, s0, s1

    xp = x_prompt
    ckv_l, kr_l, k_l, v_l = [], [], [], []
    for l in range(DEPTH):
        xp, s0, s1 = layer(l, xp, c_ctx, None, None, None)
        if l % 2 == 0:
            ckv_l.append(s0)
            kr_l.append(s1)
        else:
            k_l.append(s0)
            v_l.append(s1)
    y_prompt = rmsnorm(xp, final_g)

    n = x_sample.shape[1]
    tabs_a = axial_tables(n, ROPE_A)
    tabs_c = axial_tables(n, HD_C)
    xs = x_sample
    for l in range(DEPTH):
        i = l // 2
        if l % 2 == 0:
            ctx = (cache_mla_ckv[:, i], cache_mla_krope[:, i])
        else:
            ctx = (cache_swa_k[:, i], cache_swa_v[:, i])
        xs, _, _ = layer(l, xs, c, tabs_a, tabs_c, ctx)
    y_sample = rmsnorm(xs, final_g)

    return (y_prompt, y_sample, jnp.stack(ckv_l, axis=1), jnp.stack(kr_l, axis=1),
            jnp.stack(k_l, axis=1), jnp.stack(v_l, axis=1))
```

```python
import functools

import numpy as np
import jax
import jax.numpy as jnp
from jax import lax
from jax.experimental import pallas as pl
from jax.experimental.pallas import tpu as pltpu

D_MODEL = 1024
DEPTH = 4
GRID_W = 64
ROPE_BASE = 10000.0
EPS = 1e-6
NEG_INF = -1e30

H_A = 8
NOPE_A = 64
ROPE_A = 32
V_A = 64
Q_RANK = 256
KV_RANK = 128
QK_A = NOPE_A + ROPE_A
W_A = H_A * V_A
W_B = 512
CONV_W = 3
H_C = 8
KV_C = 2
HD_C = 64
WINDOW = 128
W_C = H_C * HD_C
G_D = 8
W_D = 512
CHUNK = 128
DG_D = W_D // G_D

EVEN_SPLIT = (Q_RANK, KV_RANK, ROPE_A, W_A, W_B, W_B, W_B, W_B)
ODD_SPLIT = (W_C, KV_C * HD_C, KV_C * HD_C, W_C, W_D, W_D, W_D)

LANES = 128
ROW_BLOCK = 256
CONV_PAD = 8
VMEM_LIMIT = 60 * 1024 * 1024

F32 = jnp.float32
BF16 = jnp.bfloat16


def _dot(a, b):
    return jnp.dot(a, b, preferred_element_type=F32)


def _dot_t(a, b):
    return lax.dot_general(a, b, (((1,), (1,)), ((), ())), preferred_element_type=F32)


def _rms(x, g):
    return x * lax.rsqrt(jnp.mean(x * x, axis=-1, keepdims=True) + EPS) * g


def _silu(x):
    return x * (1.0 / (1.0 + jnp.exp(-x)))


def _gelu(x):
    c = np.float32(np.sqrt(2.0 / np.pi))
    return 0.5 * x * (1.0 + jnp.tanh(c * (x + 0.044715 * (x * x * x))))


def _pair_select(lo, hi):
    lane = lax.broadcasted_iota(jnp.int32, lo.shape, 1)
    return jnp.where(lane < LANES // 2, lo, hi)


def _modulated_norm(x, mod_ref, ng_ref):
    shift = mod_ref[0, 0:1, :]
    scale = mod_ref[0, 1:2, :]
    return _rms(x, ng_ref[...]) * (1.0 + scale) + shift


def _row_loop(n_blocks, body):
    if n_blocks == 1:
        body(0)
    else:
        def step(i, carry):
            body(pl.multiple_of(i * ROW_BLOCK, ROW_BLOCK))
            return carry
        lax.fori_loop(0, n_blocks, step, 0)


def _even_kernel(*refs, n, n_ctx, final):
    rope = n_ctx > 0
    it = iter(refs)
    x_ref, mod_ref, ng_ref = next(it), next(it), next(it)
    w1_ref, w3_ref, wuq_ref, wkv_ref, wout_ref = next(it), next(it), next(it), next(it), next(it)
    gq_ref, gkv_ref, cw_ref, cb_ref = next(it), next(it), next(it), next(it)
    if rope:
        ctx_ref, tabq_ref, tabk_ref = next(it), next(it), next(it)
    if final:
        fg_ref = next(it)
    o_ref = next(it)
    if not rope:
        ckv_out, kr_out = next(it), next(it)
    h_s, ckvkr_s, k_s, v_s, t_s = next(it), next(it), next(it), next(it), next(it)

    nk = n_ctx + n
    scale_a = np.float32(QK_A ** -0.5)

    if rope:
        ckvkr_s[0:n_ctx, :] = ctx_ref[0]
    zpad = jnp.zeros((CONV_PAD, W_B), F32)
    t_s[0:CONV_PAD, :] = zpad
    t_s[CONV_PAD + n:2 * CONV_PAD + n, :] = zpad

    def phase1(r0):
        rows = pl.ds(r0, ROW_BLOCK)
        hb = _modulated_norm(x_ref[0, rows, :], mod_ref, ng_ref).astype(BF16)
        h_s[rows, :] = hb
        p1 = _dot(hb, w1_ref[...])
        ckv = _rms(p1[:, 0:KV_RANK], gkv_ref[...])
        kr = p1[:, LANES:2 * LANES]
        if rope:
            tk = tabk_ref[rows, :]
            kr = kr * tk[:, 0:LANES] + p1[:, 2 * LANES:3 * LANES] * tk[:, LANES:2 * LANES]
        else:
            ckv_out[0, rows, :] = ckv
            kr_out[0, rows, :] = kr[:, 0:ROPE_A]
        krows = pl.ds(n_ctx + r0, ROW_BLOCK)
        ckvkr_s[krows, 0:LANES] = ckv.astype(BF16)
        ckvkr_s[krows, LANES:2 * LANES] = kr.astype(BF16)
        c0 = 3 * LANES
        t_s[pl.ds(CONV_PAD + r0, ROW_BLOCK), :] = p1[:, c0:c0 + W_B] * p1[:, c0 + W_B:c0 + 2 * W_B]

    _row_loop(n // ROW_BLOCK, phase1)

    def phase2(r0):
        rows = pl.ds(r0, ROW_BLOCK)
        kv = _dot(ckvkr_s[rows, :], wkv_ref[...])
        k_s[rows, :] = kv[:, 0:H_A * LANES].astype(BF16)
        v_s[rows, :] = kv[:, H_A * LANES:].astype(BF16)

    _row_loop(nk // ROW_BLOCK, phase2)

    def phase3(r0):
        rows = pl.ds(r0, ROW_BLOCK)
        p3 = _dot(h_s[rows, :], w3_ref[...])
        g_a = p3[:, Q_RANK:Q_RANK + W_A]
        gate_b = p3[:, Q_RANK + W_A:Q_RANK + W_A + W_B]
        g_b = p3[:, Q_RANK + W_A + W_B:Q_RANK + W_A + 2 * W_B]
        qn = _rms(p3[:, 0:Q_RANK], gq_ref[...]).astype(BF16)
        hw = H_A * LANES
        if rope:
            qq = _dot(qn, wuq_ref[...])
            tq = tabq_ref[rows, :]
        else:
            qq = _dot(qn, wuq_ref[:, 0:hw])
        pairs = []
        for j in range(H_A // 2):
            vp = v_s[:, j * LANES:(j + 1) * LANES]
            o2 = []
            for h in (2 * j, 2 * j + 1):
                qh = qq[:, h * LANES:(h + 1) * LANES]
                if rope:
                    qh = qh * tq[:, 0:LANES] + qq[:, hw + h * LANES:hw + (h + 1) * LANES] * tq[:, LANES:2 * LANES]
                s = _dot_t(qh.astype(BF16), k_s[:, h * LANES:(h + 1) * LANES]) * scale_a
                e = jnp.exp(s - jnp.max(s, axis=-1, keepdims=True))
                inv = 1.0 / jnp.sum(e, axis=-1, keepdims=True)
                o2.append(_dot(e.astype(BF16), vp) * inv)
            pairs.append(_pair_select(o2[0], o2[1]))
        a_out = jnp.concatenate(pairs, axis=-1) * _silu(g_a)

        t_cur = t_s[pl.ds(CONV_PAD + r0, ROW_BLOCK), :]
        t_up = t_s[pl.ds(r0, CONV_PAD), :]
        t_dn = t_s[pl.ds(CONV_PAD + r0 + ROW_BLOCK, CONV_PAD), :]
        rid = lax.broadcasted_iota(jnp.int32, (ROW_BLOCK, W_B), 0)
        t_prev = jnp.where(rid == 0, t_up[CONV_PAD - 1:CONV_PAD, :], pltpu.roll(t_cur, 1, 0))
        t_next = jnp.where(rid == ROW_BLOCK - 1, t_dn[0:1, :], pltpu.roll(t_cur, ROW_BLOCK - 1, 0))
        conv = cb_ref[...] + t_prev * cw_ref[0:1, :]
        conv = conv + t_cur * cw_ref[1:2, :]
        conv = conv + t_next * cw_ref[2:3, :]
        b_out = gate_b * conv * _silu(g_b)

        y = _dot(a_out.astype(BF16), wout_ref[0:W_A, :]) + _dot(b_out.astype(BF16), wout_ref[W_A:W_A + W_B, :])
        res = x_ref[0, rows, :] + mod_ref[0, 2:3, :] * y
        if final:
            res = _rms(res, fg_ref[...])
        o_ref[0, rows, :] = res

    _row_loop(n // ROW_BLOCK, phase3)


def _odd_kernel(*refs, n, n_ctx, final):
    rope = n_ctx > 0
    it = iter(refs)
    x_ref, mod_ref, ng_ref = next(it), next(it), next(it)
    w1_ref, w3_ref, wout_ref = next(it), next(it), next(it)
    sink_ref, gg_ref, ws_ref, bs_ref = next(it), next(it), next(it), next(it)
    if rope:
        kctx_ref, vctx_ref, tab_ref = next(it), next(it), next(it)
    else:
        wkv_ref = next(it)
    if final:
        fg_ref = next(it)
    o_ref = next(it)
    if not rope:
        k_out, v_out = next(it), next(it)
    h_s, kloc_s, vloc_s = next(it), next(it), next(it)

    pad = WINDOW if rope else 0
    kvw = KV_C * LANES
    hw = H_C * LANES

    if rope:
        zrows = jnp.zeros((pad, kvw), BF16)
        kloc_s[0:pad, :] = zrows
        vloc_s[0:pad, :] = zrows
        kloc_s[pad + n:2 * pad + n, :] = zrows
        vloc_s[pad + n:2 * pad + n, :] = zrows

    def phase1(r0):
        rows = pl.ds(r0, ROW_BLOCK)
        hb = _modulated_norm(x_ref[0, rows, :], mod_ref, ng_ref).astype(BF16)
        h_s[rows, :] = hb
        p1 = _dot(hb, w1_ref[...])
        k = p1[:, 0:kvw]
        if rope:
            tb = tab_ref[rows, :]
            ksw = p1[:, kvw:2 * kvw]
            k = jnp.concatenate(
                [k[:, j * LANES:(j + 1) * LANES] * tb[:, 0:LANES]
                 + ksw[:, j * LANES:(j + 1) * LANES] * tb[:, LANES:2 * LANES] for j in range(KV_C)], axis=-1)
        else:
            kv = _dot(hb, wkv_ref[...])
            k_out[0, rows, :] = kv[:, 0:KV_C * HD_C]
            v_out[0, rows, :] = kv[:, KV_C * HD_C:2 * KV_C * HD_C]
        lrows = pl.ds(pad + r0, ROW_BLOCK)
        kloc_s[lrows, :] = k.astype(BF16)
        vloc_s[lrows, :] = p1[:, 2 * kvw:3 * kvw].astype(BF16)

    _row_loop(n // ROW_BLOCK, phase1)

    def phase3(r0):
        rows = pl.ds(r0, ROW_BLOCK)
        p3 = _dot(h_s[rows, :], w3_ref[...])
        c0 = 2 * hw
        g_c = p3[:, c0:c0 + W_C]
        u = p3[:, c0 + W_C:c0 + W_C + W_D]
        vv = p3[:, c0 + W_C + W_D:c0 + W_C + 2 * W_D]
        g_d = p3[:, c0 + W_C + 2 * W_D:c0 + W_C + 3 * W_D]

        if rope:
            tb = tab_ref[rows, :]
            span = ROW_BLOCK + 2 * WINDOW
            kl = kloc_s[pl.ds(r0, span), :]
            vl = vloc_s[pl.ds(r0, span), :]
            ii = lax.broadcasted_iota(jnp.int32, (ROW_BLOCK, span), 0)
            jj = lax.broadcasted_iota(jnp.int32, (ROW_BLOCK, span), 1)
            kpos = jj + (r0 - WINDOW)
            dlt = jj - ii
            mask = (dlt >= 0) & (dlt <= 2 * WINDOW) & (kpos >= 0) & (kpos < n)
        else:
            kl = kloc_s[...]
            vl = vloc_s[...]

        blocks = []
        for m in range(H_C // 2):
            j = m // (H_C // 2 // KV_C)
            klj = kl[:, j * LANES:(j + 1) * LANES]
            vlj = vl[:, j * LANES:(j + 1) * LANES]
            if rope:
                kcj = kctx_ref[0, :, j * LANES:(j + 1) * LANES]
                vcj = vctx_ref[0, :, j * LANES:(j + 1) * LANES]
            o2 = []
            for h in (2 * m, 2 * m + 1):
                qh = p3[:, h * LANES:(h + 1) * LANES]
                if rope:
                    qh = qh * tb[:, 0:LANES] + p3[:, hw + h * LANES:hw + (h + 1) * LANES] * tb[:, LANES:2 * LANES]
                qh = (qh * np.float32(HD_C ** -0.5)).astype(BF16)
                sk = sink_ref[h]
                sl = _dot_t(qh, klj)
                if rope:
                    sl = jnp.where(mask, sl, NEG_INF)
                    sc = _dot_t(qh, kcj)
                    mx = jnp.maximum(jnp.max(sl, axis=-1, keepdims=True), jnp.max(sc, axis=-1, keepdims=True))
                else:
                    mx = jnp.max(sl, axis=-1, keepdims=True)
                mx = jnp.maximum(mx, sk)
                el = jnp.exp(sl - mx)
                den = jnp.sum(el, axis=-1, keepdims=True) + jnp.exp(sk - mx)
                o = _dot(el.astype(BF16), vlj)
                if rope:
                    ec = jnp.exp(sc - mx)
                    den = den + jnp.sum(ec, axis=-1, keepdims=True)
                    o = o + _dot(ec.astype(BF16), vcj)
                o2.append(o * (1.0 / den))
            blocks.append(_pair_select(o2[0], o2[1]))
        c_out = jnp.concatenate(blocks, axis=-1) * _silu(g_c)

        vvn = _rms(_gelu(vv), gg_ref[...]).astype(BF16)
        chunks = []
        for c in range(ROW_BLOCK // CHUNK):
            cols = []
            for jp in range(G_D // 2):
                vp = vvn[c * CHUNK:(c + 1) * CHUNK, jp * LANES:(jp + 1) * LANES]
                o = _dot(ws_ref[jp], vp)
                cols.append(_pair_select(o[0:CHUNK, :], o[CHUNK:2 * CHUNK, :]))
            chunks.append(jnp.concatenate(cols, axis=-1) + bs_ref[...])
        mixed = jnp.concatenate(chunks, axis=0)
        d_out = _gelu(u) * mixed * _silu(g_d)

        y = _dot(c_out.astype(BF16), wout_ref[0:W_C, :]) + _dot(d_out.astype(BF16), wout_ref[W_C:W_C + W_D, :])
        res = x_ref[0, rows, :] + mod_ref[0, 2:3, :] * y
        if final:
            res = _rms(res, fg_ref[...])
        o_ref[0, rows, :] = res

    _row_loop(n // ROW_BLOCK, phase3)


def _mod_kernel(c_ref, w_ref, b_ref, o_ref):
    a = _silu(c_ref[...]).astype(BF16)
    o_ref[0] = _dot(a, w_ref[0].astype(BF16)) + b_ref[0]


def _modulation(cvec, ada_w, ada_b):
    rows = cvec.shape[0]
    tn = D_MODEL
    return pl.pallas_call(
        _mod_kernel,
        grid=(DEPTH, 3 * D_MODEL // tn),
        in_specs=[
            pl.BlockSpec((rows, D_MODEL), lambda l, j: (0, 0)),
            pl.BlockSpec((1, D_MODEL, tn), lambda l, j: (l, 0, j)),
            pl.BlockSpec((1, 1, tn), lambda l, j: (l, 0, j)),
        ],
        out_specs=pl.BlockSpec((1, rows, tn), lambda l, j: (l, 0, j)),
        out_shape=jax.ShapeDtypeStruct((DEPTH, rows, 3 * D_MODEL), F32),
        compiler_params=pltpu.CompilerParams(dimension_semantics=("arbitrary", "arbitrary")),
        name="modulation",
    )(cvec, ada_w, ada_b.reshape(DEPTH, 1, 3 * D_MODEL))


def _full(a):
    nd = a.ndim
    return pl.BlockSpec(a.shape, lambda b, _nd=nd: (0,) * _nd)


def _per_seq(a):
    nd = a.ndim
    return pl.BlockSpec((1,) + a.shape[1:], lambda b, _nd=nd: (b,) + (0,) * (_nd - 1))


def _swap_quarters(w):
    q = w.shape[-1] // 4
    return jnp.concatenate([w[..., q:2 * q], w[..., 0:q], w[..., 3 * q:], w[..., 2 * q:3 * q]], axis=-1)


def _rope_cs(n, rot_dim):
    rows = n // GRID_W
    r = jnp.repeat(jnp.arange(rows, dtype=F32), GRID_W)
    col = jnp.tile(jnp.arange(GRID_W, dtype=F32), rows)
    nf = rot_dim // 4
    inv = ROPE_BASE ** (-jnp.arange(nf, dtype=F32) / nf)
    ar = r[:, None] * inv
    ac = col[:, None] * inv
    cos = jnp.concatenate([jnp.cos(ar), jnp.cos(ar), jnp.cos(ac), jnp.cos(ac)], axis=-1)
    sin = jnp.concatenate([-jnp.sin(ar), jnp.sin(ar), -jnp.sin(ac), jnp.sin(ac)], axis=-1)
    return cos, sin


def _split(w, sizes):
    return jnp.split(w, np.cumsum(sizes)[:-1].tolist(), axis=-1)


def _prep_even(w_in, w_uq, w_uk, w_uv):
    q_dn, kv_dn, k_r, g_a, gate_b, gate_c, x_in, g_b = _split(w_in, EVEN_SPLIT)
    zc = jnp.zeros((D_MODEL, LANES - ROPE_A), F32)
    w1 = jnp.concatenate([kv_dn, k_r, zc, _swap_quarters(k_r), zc, gate_c, x_in], axis=1)
    w3 = jnp.concatenate([q_dn, g_a, gate_b, g_b], axis=1)
    uq = w_uq.reshape(Q_RANK, H_A, QK_A)
    nope, ropew = uq[..., :NOPE_A], uq[..., NOPE_A:]
    zq = jnp.zeros((Q_RANK, H_A, LANES - QK_A), F32)
    uq_pad = jnp.concatenate([nope, ropew, zq], axis=-1).reshape(Q_RANK, H_A * LANES)
    uq_sw = jnp.concatenate([jnp.zeros_like(nope), _swap_quarters(ropew), zq], axis=-1).reshape(Q_RANK, H_A * LANES)
    wuq = jnp.concatenate([uq_pad, uq_sw], axis=1)
    uk = w_uk.reshape(KV_RANK, H_A, NOPE_A)
    k_top = jnp.concatenate([uk, jnp.zeros((KV_RANK, H_A, LANES - NOPE_A), F32)], axis=-1).reshape(KV_RANK, H_A * LANES)
    eye = jnp.broadcast_to(jnp.eye(ROPE_A, dtype=F32)[:, None, :], (ROPE_A, H_A, ROPE_A))
    k_mid = jnp.concatenate([jnp.zeros((ROPE_A, H_A, NOPE_A), F32), eye,
                             jnp.zeros((ROPE_A, H_A, LANES - QK_A), F32)], axis=-1).reshape(ROPE_A, H_A * LANES)
    k_bot = jnp.zeros((LANES - ROPE_A, H_A * LANES), F32)
    wk = jnp.concatenate([k_top, k_mid, k_bot], axis=0)
    wv = jnp.concatenate([w_uv, jnp.zeros((LANES, W_A), F32)], axis=0)
    wkv = jnp.concatenate([wk, wv], axis=1)
    return tuple(a.astype(BF16) for a in (w1, w3, wuq, wkv))


def _pad_heads(w, heads):
    w = w.reshape(w.shape[0], heads, HD_C)
    return jnp.concatenate([w, jnp.zeros_like(w)], axis=-1).reshape(w.shape[0], heads * LANES)


def _dup_heads(w, heads):
    w = w.reshape(w.shape[0], heads, HD_C)
    return jnp.concatenate([w, w], axis=-1).reshape(w.shape[0], heads * LANES)


def _swap_heads(w, heads):
    return _swap_quarters(w.reshape(w.shape[0], heads, HD_C)).reshape(w.shape[0], heads * HD_C)


def _prep_odd(w_in):
    q, k, v, g_c, u, vv, g_d = _split(w_in, ODD_SPLIT)
    w1 = jnp.concatenate([_pad_heads(k, KV_C), _pad_heads(_swap_heads(k, KV_C), KV_C), _dup_heads(v, KV_C)], axis=1)
    w3 = jnp.concatenate([_pad_heads(q, H_C), _pad_heads(_swap_heads(q, H_C), H_C), g_c, u, vv, g_d], axis=1)
    wkv = jnp.concatenate([k, v], axis=1)
    return tuple(a.astype(BF16) for a in (w1, w3, wkv))


def _layer_call(body, x, operands, extra_out, scratch, name):
    b, n, _ = x.shape
    in_specs = [spec for _, spec in operands]
    args = [a for a, _ in operands]
    out_shape = [jax.ShapeDtypeStruct(x.shape, F32)] + [jax.ShapeDtypeStruct((b, n, w), F32) for w in extra_out]
    out_specs = [pl.BlockSpec((1, n, D_MODEL), lambda i: (i, 0, 0))] + [
        pl.BlockSpec((1, n, w), lambda i: (i, 0, 0)) for w in extra_out]
    return pl.pallas_call(
        body,
        grid=(b,),
        in_specs=in_specs,
        out_specs=out_specs,
        out_shape=out_shape,
        scratch_shapes=scratch,
        compiler_params=pltpu.CompilerParams(dimension_semantics=("arbitrary",), vmem_limit_bytes=VMEM_LIMIT),
        name=name,
    )(*args)


def _even_layer(x, mod, ng, weights, wout, gq, gkv, cw, cb, ctx, tabs, final_g, name):
    b, n, _ = x.shape
    w1, w3, wuq, wkv = weights
    n_ctx = 0 if ctx is None else ctx.shape[1]
    ops = [(x, _per_seq(x)), (mod, _per_seq(mod) if mod.shape[0] == b else _full(mod)), (ng, _full(ng)),
           (w1, _full(w1)), (w3, _full(w3)), (wuq, _full(wuq)), (wkv, _full(wkv)), (wout, _full(wout)),
           (gq, _full(gq)), (gkv, _full(gkv)), (cw, _full(cw)), (cb, _full(cb))]
    if ctx is not None:
        ops += [(ctx, _per_seq(ctx)), (tabs[0], _full(tabs[0])), (tabs[1], _full(tabs[1]))]
    if final_g is not None:
        ops.append((final_g, _full(final_g)))
    nk = n_ctx + n
    scratch = [pltpu.VMEM((n, D_MODEL), BF16), pltpu.VMEM((nk, 2 * LANES), BF16),
               pltpu.VMEM((nk, H_A * LANES), BF16), pltpu.VMEM((nk, W_A), BF16),
               pltpu.VMEM((n + 2 * CONV_PAD, W_B), F32)]
    body = functools.partial(_even_kernel, n=n, n_ctx=n_ctx, final=final_g is not None)
    extra = () if ctx is not None else (KV_RANK, ROPE_A)
    return _layer_call(body, x, ops, extra, scratch, name)


def _odd_layer(x, mod, ng, weights, wout, sink, gg, ws, bs, ctx, tab, final_g, name):
    b, n, _ = x.shape
    w1, w3, wkv = weights
    n_ctx = 0 if ctx is None else ctx[0].shape[1]
    ops = [(x, _per_seq(x)), (mod, _per_seq(mod) if mod.shape[0] == b else _full(mod)), (ng, _full(ng)),
           (w1, _full(w1)), (w3, _full(w3)), (wout, _full(wout)),
           (sink, pl.BlockSpec(memory_space=pltpu.SMEM)), (gg, _full(gg)), (ws, _full(ws)), (bs, _full(bs))]
    if ctx is not None:
        ops += [(ctx[0], _per_seq(ctx[0])), (ctx[1], _per_seq(ctx[1])), (tab, _full(tab))]
    else:
        ops.append((wkv, _full(wkv)))
    if final_g is not None:
        ops.append((final_g, _full(final_g)))
    pad = WINDOW if ctx is not None else 0
    scratch = [pltpu.VMEM((n, D_MODEL), BF16), pltpu.VMEM((n + 2 * pad, KV_C * LANES), BF16),
               pltpu.VMEM((n + 2 * pad, KV_C * LANES), BF16)]
    body = functools.partial(_odd_kernel, n=n, n_ctx=n_ctx, final=final_g is not None)
    extra = () if ctx is not None else (KV_C * HD_C, KV_C * HD_C)
    return _layer_call(body, x, ops, extra, scratch, name)


def kernel(x_prompt, x_sample, cache_mla_ckv, cache_mla_krope, cache_swa_k, cache_swa_v, c, c_ctx, ada_w, ada_b, norm_g, ev_w_in, ev_w_out, mla_gq, mla_gkv, mla_w_uq, mla_w_uk, mla_w_uv, conv_w, conv_b, od_w_in, od_w_out, swa_sink, gmlp_g, gmlp_ws, gmlp_bs, final_g):
    bp, n_p, _ = x_prompt.shape
    bs_, n_s, _ = x_sample.shape
    n_past = cache_mla_ckv.shape[2]

    n_rows = -(-(bs_ + 1) // 8) * 8
    cvec = jnp.concatenate([c, c_ctx[None, :], jnp.zeros((n_rows - bs_ - 1, D_MODEL), F32)], axis=0)
    mod = _modulation(cvec, ada_w, ada_b).reshape(DEPTH, n_rows, 3, D_MODEL)

    cos_a, sin_a = _rope_cs(n_s, ROPE_A)
    cos_c, sin_c = _rope_cs(n_s, HD_C)
    one = jnp.ones((n_s, NOPE_A), F32)
    zero = lambda w: jnp.zeros((n_s, w), F32)
    tab_q = jnp.concatenate([one, cos_a, zero(LANES - QK_A), zero(NOPE_A), sin_a, zero(LANES - QK_A)], axis=1)
    tab_k = jnp.concatenate([cos_a, zero(LANES - ROPE_A), sin_a, zero(LANES - ROPE_A)], axis=1)
    tab_c = jnp.concatenate([cos_c, zero(LANES - HD_C), sin_c, zero(LANES - HD_C)], axis=1)

    fg = final_g.reshape(1, D_MODEL)
    xp, xs = x_prompt, x_sample
    ckv_l, kr_l, k_l, v_l = [], [], [], []
    for l in range(DEPTH):
        i = l // 2
        last = fg if l == DEPTH - 1 else None
        ng = norm_g[l].reshape(1, D_MODEL)
        mod_p = mod[l, bs_:bs_ + 1]
        mod_s = mod[l, 0:bs_]
        if l % 2 == 0:
            weights = _prep_even(ev_w_in[i], mla_w_uq[i], mla_w_uk[i], mla_w_uv[i])
            wout = ev_w_out[i].astype(BF16)
            small = (mla_gq[i].reshape(1, Q_RANK), mla_gkv[i].reshape(1, KV_RANK), conv_w[i], conv_b[i].reshape(1, W_B))
            xp, ckv, kr = _even_layer(xp, mod_p, ng, weights, wout, *small, None, None, last, f"even{l}_ctx")
            ckv_l.append(ckv)
            kr_l.append(kr)
            ctx = jnp.concatenate([cache_mla_ckv[:, i], cache_mla_krope[:, i],
                                   jnp.zeros((bs_, n_past, LANES - ROPE_A), F32)], axis=-1).astype(BF16)
            (xs,) = _even_layer(xs, mod_s, ng, weights, wout, *small, ctx, (tab_q, tab_k), last, f"even{l}_lat")
        else:
            weights = _prep_odd(od_w_in[i])
            wout = od_w_out[i].astype(BF16)
            ws = gmlp_ws[i].reshape(G_D // 2, 2 * CHUNK, CHUNK).astype(BF16)
            bias = jnp.repeat(gmlp_bs[i].T, DG_D, axis=1)
            small = (swa_sink[i], gmlp_g[i].reshape(1, W_D), ws, bias)
            xp, k_new, v_new = _odd_layer(xp, mod_p, ng, weights, wout, *small, None, None, last, f"odd{l}_ctx")
            k_l.append(k_new.reshape(bp, n_p, KV_C, HD_C))
            v_l.append(v_new.reshape(bp, n_p, KV_C, HD_C))
            kc = cache_swa_k[:, i]
            vc = cache_swa_v[:, i]
            kctx = jnp.concatenate([kc, jnp.zeros_like(kc)], axis=-1).reshape(bs_, n_past, KV_C * LANES).astype(BF16)
            vctx = jnp.concatenate([vc, vc], axis=-1).reshape(bs_, n_past, KV_C * LANES).astype(BF16)
            (xs,) = _odd_layer(xs, mod_s, ng, weights, wout, *small, (kctx, vctx), tab_c, last, f"odd{l}_lat")

    return (xp, xs, jnp.stack(ckv_l, axis=1), jnp.stack(kr_l, axis=1), jnp.stack(k_l, axis=1), jnp.stack(v_l, axis=1))
```

```python
import functools

import numpy as np
import jax
import jax.numpy as jnp
from jax import lax
from jax.experimental import pallas as pl
from jax.experimental.pallas import tpu as pltpu

D_MODEL = 1024
DEPTH = 4
GRID_W = 64
ROPE_BASE = 10000.0
EPS = 1e-6
NEG_INF = -1e30

H_A = 8
NOPE_A = 64
ROPE_A = 32
V_A = 64
Q_RANK = 256
KV_RANK = 128
QK_A = NOPE_A + ROPE_A
W_A = H_A * V_A
W_B = 512
CONV_W = 3
H_C = 8
KV_C = 2
HD_C = 64
WINDOW = 128
W_C = H_C * HD_C
G_D = 8
W_D = 512
CHUNK = 128
DG_D = W_D // G_D

EVEN_SPLIT = (Q_RANK, KV_RANK, ROPE_A, W_A, W_B, W_B, W_B, W_B)
ODD_SPLIT = (W_C, KV_C * HD_C, KV_C * HD_C, W_C, W_D, W_D, W_D)
EVEN_HEAD = Q_RANK + KV_RANK + ROPE_A

LANES = 128
ROW_BLOCK = 256
CONV_PAD = 8
VMEM_LIMIT = 60 * 1024 * 1024

F32 = jnp.float32
BF16 = jnp.bfloat16


def _dot(a, b):
    return jnp.dot(a, b, preferred_element_type=F32)


def _dot_t(a, b):
    return lax.dot_general(a, b, (((1,), (1,)), ((), ())), preferred_element_type=F32)


def _rms(x, g):
    return x * lax.rsqrt(jnp.mean(x * x, axis=-1, keepdims=True) + EPS) * g


def _silu(x):
    return x * (1.0 / (1.0 + jnp.exp(-x)))


def _gelu(x):
    c = np.float32(np.sqrt(2.0 / np.pi))
    return 0.5 * x * (1.0 + jnp.tanh(c * (x + 0.044715 * (x * x * x))))


def _lane(shape):
    return lax.broadcasted_iota(jnp.int32, shape, 1)


def _pair_select(lo, hi):
    return jnp.where(_lane(lo.shape) < LANES // 2, lo, hi)


def _rope(x, tab, quarter):
    fwd = pltpu.roll(x, LANES - quarter, 1)
    bwd = pltpu.roll(x, quarter, 1)
    partner = jnp.where((_lane(x.shape) & quarter) == 0, fwd, bwd)
    return x * tab[:, 0:LANES] + partner * tab[:, LANES:2 * LANES]


def _dup_halves(x):
    sw = pltpu.roll(x, LANES // 2, 1)
    lo = _lane(x.shape) < LANES // 2
    return jnp.where(lo, x, sw), jnp.where(lo, sw, x)


def _modulated_norm(x, mod_ref, ng_ref):
    shift = mod_ref[0, 0, 0:1, :]
    scale = mod_ref[0, 0, 1:2, :]
    return _rms(x, ng_ref[0]) * (1.0 + scale) + shift


def _row_loop(n_blocks, body):
    if n_blocks == 1:
        body(0)
    else:
        def step(i, carry):
            body(pl.multiple_of(i * ROW_BLOCK, ROW_BLOCK))
            return carry
        lax.fori_loop(0, n_blocks, step, 0)


def _even_kernel(*refs, n, n_ctx, final):
    rope = n_ctx > 0
    it = iter(refs)
    x_ref, mod_ref, ng_ref = next(it), next(it), next(it)
    wh_ref, wm_ref, wuq_ref, wuk_ref, wuv_ref, wout_ref = (next(it) for _ in range(6))
    gq_ref, gkv_ref, cw_ref, cb_ref = next(it), next(it), next(it), next(it)
    if rope:
        cckv_ref, ckr_ref, tab_ref = next(it), next(it), next(it)
    if final:
        fg_ref = next(it)
    o_ref = next(it)
    if not rope:
        ckv_out, kr_out = next(it), next(it)
    h_s, q_s, ckv_s, kr_s, k_s, v_s, t_s = (next(it) for _ in range(7))

    nk = n_ctx + n
    scale_a = np.float32(QK_A ** -0.5)
    hw = H_A * LANES

    if rope:
        ckv_s[0:n_ctx, :] = cckv_ref[0, 0].astype(BF16)
        kr_s[0:n_ctx, :] = ckr_ref[0].astype(BF16)
    zpad = jnp.zeros((CONV_PAD, W_B), F32)
    t_s[0:CONV_PAD, :] = zpad
    t_s[CONV_PAD + n:2 * CONV_PAD + n, :] = zpad

    def phase1(r0):
        rows = pl.ds(r0, ROW_BLOCK)
        hb = _modulated_norm(x_ref[0, rows, :], mod_ref, ng_ref).astype(BF16)
        h_s[rows, :] = hb
        ph = _dot(hb, wh_ref[0])
        qn = _rms(ph[:, 0:Q_RANK], gq_ref[0]).astype(BF16)
        qq = _dot(qn, wuq_ref[0])
        ckv = _rms(ph[:, Q_RANK:Q_RANK + KV_RANK], gkv_ref[0])
        kr = ph[:, Q_RANK + KV_RANK:Q_RANK + KV_RANK + LANES]
        if rope:
            tab = tab_ref[rows, :]
            kr = _rope(kr, tab, ROPE_A // 4)
            for h in range(H_A):
                q_s[rows, h * LANES:(h + 1) * LANES] = _rope(qq[:, h * LANES:(h + 1) * LANES], tab, ROPE_A // 4).astype(BF16)
        else:
            q_s[rows, :] = qq.astype(BF16)
            ckv_out[0, rows, :] = ckv
            kr_out[0, rows, :] = kr
        krows = pl.ds(n_ctx + r0, ROW_BLOCK)
        ckv_s[krows, :] = ckv.astype(BF16)
        kr_s[krows, :] = kr.astype(BF16)
        pc = _dot(hb, wm_ref[0, :, 2 * W_B:4 * W_B])
        t_s[pl.ds(CONV_PAD + r0, ROW_BLOCK), :] = pc[:, 0:W_B] * pc[:, W_B:2 * W_B]

    _row_loop(n // ROW_BLOCK, phase1)

    def phase2(r0):
        rows = pl.ds(r0, ROW_BLOCK)
        c = ckv_s[rows, :]
        kn = _dot(c, wuk_ref[0])
        krf = kr_s[rows, :].astype(F32)
        for h in range(H_A):
            k_s[rows, h * LANES:(h + 1) * LANES] = (kn[:, h * LANES:(h + 1) * LANES] + krf).astype(BF16)
        v_s[rows, :] = _dot(c, wuv_ref[0]).astype(BF16)

    _row_loop(nk // ROW_BLOCK, phase2)

    def phase3(r0):
        rows = pl.ds(r0, ROW_BLOCK)
        hb = h_s[rows, :]
        pairs = []
        for j in range(H_A // 2):
            vp = v_s[:, j * LANES:(j + 1) * LANES]
            o2 = []
            for h in (2 * j, 2 * j + 1):
                s = _dot_t(q_s[rows, h * LANES:(h + 1) * LANES], k_s[:, h * LANES:(h + 1) * LANES]) * scale_a
                e = jnp.exp(s - jnp.max(s, axis=-1, keepdims=True))
                inv = 1.0 / jnp.sum(e, axis=-1, keepdims=True)
                o2.append(_dot(e.astype(BF16), vp) * inv)
            pairs.append(_pair_select(o2[0], o2[1]))
        pa = _dot(hb, wm_ref[0, :, 0:2 * W_B])
        a_out = jnp.concatenate(pairs, axis=-1) * _silu(pa[:, 0:W_A])
        gate_b = pa[:, W_A:W_A + W_B]
        g_b = _dot(hb, wm_ref[0, :, 4 * W_B:5 * W_B])

        t_cur = t_s[pl.ds(CONV_PAD + r0, ROW_BLOCK), :]
        t_up = t_s[pl.ds(r0, CONV_PAD), :]
        t_dn = t_s[pl.ds(CONV_PAD + r0 + ROW_BLOCK, CONV_PAD), :]
        rid = lax.broadcasted_iota(jnp.int32, (ROW_BLOCK, W_B), 0)
        t_prev = jnp.where(rid == 0, t_up[CONV_PAD - 1:CONV_PAD, :], pltpu.roll(t_cur, 1, 0))
        t_next = jnp.where(rid == ROW_BLOCK - 1, t_dn[0:1, :], pltpu.roll(t_cur, ROW_BLOCK - 1, 0))
        conv = cb_ref[0] + t_prev * cw_ref[0, 0:1, :]
        conv = conv + t_cur * cw_ref[0, 1:2, :]
        conv = conv + t_next * cw_ref[0, 2:3, :]
        b_out = gate_b * conv * _silu(g_b)

        y = _dot(a_out.astype(BF16), wout_ref[0, 0:W_A, :]) + _dot(b_out.astype(BF16), wout_ref[0, W_A:W_A + W_B, :])
        res = x_ref[0, rows, :] + mod_ref[0, 0, 2:3, :] * y
        if final:
            res = _rms(res, fg_ref[...])
        o_ref[0, rows, :] = res

    _row_loop(n // ROW_BLOCK, phase3)


def _odd_kernel(*refs, n, n_ctx, final, layer):
    rope = n_ctx > 0
    it = iter(refs)
    x_ref, mod_ref, ng_ref = next(it), next(it), next(it)
    win_ref, wout_ref = next(it), next(it)
    sink_ref, gg_ref, ws_ref, bs_ref = next(it), next(it), next(it), next(it)
    if rope:
        kc_ref, vc_ref, tab_ref = next(it), next(it), next(it)
    if final:
        fg_ref = next(it)
    o_ref = next(it)
    if not rope:
        k_out, v_out = next(it), next(it)
    h_s, kloc_s, vloc_s = next(it), next(it), next(it)
    if rope:
        kctx_s, vctx_s = next(it), next(it)

    pad = WINDOW if rope else 0
    kvw = KV_C * LANES
    q0, k0, v0, g0 = 0, W_C, W_C + KV_C * HD_C, W_C + 2 * KV_C * HD_C

    def store_dup(dst, rows, x):
        a, b = _dup_halves(x)
        dst[rows, 0:LANES] = a.astype(BF16)
        dst[rows, LANES:2 * LANES] = b.astype(BF16)

    if rope:
        zrows = jnp.zeros((pad, kvw), BF16)
        kloc_s[0:pad, :] = zrows
        vloc_s[0:pad, :] = zrows
        kloc_s[pad + n:2 * pad + n, :] = zrows
        vloc_s[pad + n:2 * pad + n, :] = zrows
        store_dup(kctx_s, slice(None), kc_ref[0, 0])
        store_dup(vctx_s, slice(None), vc_ref[0, 0])

    def phase1(r0):
        rows = pl.ds(r0, ROW_BLOCK)
        hb = _modulated_norm(x_ref[0, rows, :], mod_ref, ng_ref).astype(BF16)
        h_s[rows, :] = hb
        p1 = _dot(hb, win_ref[0, :, k0:g0])
        k = p1[:, 0:LANES]
        v = p1[:, LANES:2 * LANES]
        if rope:
            k = _rope(k, tab_ref[rows, :], HD_C // 4)
        else:
            k_out[0, rows, :] = k
            v_out[0, rows, :] = v
        lrows = pl.ds(pad + r0, ROW_BLOCK)
        store_dup(kloc_s, lrows, k)
        store_dup(vloc_s, lrows, v)

    _row_loop(n // ROW_BLOCK, phase1)

    def phase3(r0):
        rows = pl.ds(r0, ROW_BLOCK)
        hb = h_s[rows, :]
        pq = _dot(hb, win_ref[0, :, q0:k0])

        if rope:
            tab = tab_ref[rows, :]
            span = ROW_BLOCK + 2 * WINDOW
            kl = kloc_s[pl.ds(r0, span), :]
            vl = vloc_s[pl.ds(r0, span), :]
            ii = lax.broadcasted_iota(jnp.int32, (ROW_BLOCK, span), 0)
            jj = lax.broadcasted_iota(jnp.int32, (ROW_BLOCK, span), 1)
            kpos = jj + (r0 - WINDOW)
            dlt = jj - ii
            mask = (dlt >= 0) & (dlt <= 2 * WINDOW) & (kpos >= 0) & (kpos < n)
        else:
            kl = kloc_s[...]
            vl = vloc_s[...]

        blocks = []
        for m in range(H_C // 2):
            j = m // (H_C // 2 // KV_C)
            klj = kl[:, j * LANES:(j + 1) * LANES]
            vlj = vl[:, j * LANES:(j + 1) * LANES]
            if rope:
                kcj = kctx_s[:, j * LANES:(j + 1) * LANES]
                vcj = vctx_s[:, j * LANES:(j + 1) * LANES]
            qb = pq[:, m * LANES:(m + 1) * LANES]
            if rope:
                qb = _rope(qb, tab, HD_C // 4)
            qb = qb * np.float32(HD_C ** -0.5)
            lo = _lane(qb.shape) < LANES // 2
            o2 = []
            for hh in range(2):
                h = 2 * m + hh
                qh = jnp.where(lo if hh == 0 else jnp.logical_not(lo), qb, 0.0).astype(BF16)
                sk = sink_ref[layer, h]
                sl = _dot_t(qh, klj)
                if rope:
                    sl = jnp.where(mask, sl, NEG_INF)
                    sc = _dot_t(qh, kcj)
                    mx = jnp.maximum(jnp.max(sl, axis=-1, keepdims=True), jnp.max(sc, axis=-1, keepdims=True))
                else:
                    mx = jnp.max(sl, axis=-1, keepdims=True)
                mx = jnp.maximum(mx, sk)
                el = jnp.exp(sl - mx)
                den = jnp.sum(el, axis=-1, keepdims=True) + jnp.exp(sk - mx)
                o = _dot(el.astype(BF16), vlj)
                if rope:
                    ec = jnp.exp(sc - mx)
                    den = den + jnp.sum(ec, axis=-1, keepdims=True)
                    o = o + _dot(ec.astype(BF16), vcj)
                o2.append(o * (1.0 / den))
            blocks.append(_pair_select(o2[0], o2[1]))
        pg = _dot(hb, win_ref[0, :, g0:g0 + W_C + 3 * W_D])
        c_out = jnp.concatenate(blocks, axis=-1) * _silu(pg[:, 0:W_C])
        u = pg[:, W_C:W_C + W_D]
        vv = pg[:, W_C + W_D:W_C + 2 * W_D]
        g_d = pg[:, W_C + 2 * W_D:W_C + 3 * W_D]

        vvn = _rms(_gelu(vv), gg_ref[0]).astype(BF16)
        chunks = []
        for c in range(ROW_BLOCK // CHUNK):
            cols = []
            for jp in range(G_D // 2):
                vp = vvn[c * CHUNK:(c + 1) * CHUNK, jp * LANES:(jp + 1) * LANES]
                o = _dot(ws_ref[0, jp], vp)
                cols.append(_pair_select(o[0:CHUNK, :], o[CHUNK:2 * CHUNK, :]))
            chunks.append(jnp.concatenate(cols, axis=-1) + bs_ref[0])
        mixed = jnp.concatenate(chunks, axis=0)
        d_out = _gelu(u) * mixed * _silu(g_d)

        y = _dot(c_out.astype(BF16), wout_ref[0, 0:W_C, :]) + _dot(d_out.astype(BF16), wout_ref[0, W_C:W_C + W_D, :])
        res = x_ref[0, rows, :] + mod_ref[0, 0, 2:3, :] * y
        if final:
            res = _rms(res, fg_ref[...])
        o_ref[0, rows, :] = res

    _row_loop(n // ROW_BLOCK, phase3)


def _mod_kernel(c_ref, w_ref, b_ref, o_ref):
    a = _silu(c_ref[...]).astype(BF16)
    o_ref[0] = _dot(a, w_ref[0].astype(BF16)) + b_ref[0]


def _modulation(cvec, ada_w, ada_b):
    rows = cvec.shape[0]
    tn = D_MODEL
    return pl.pallas_call(
        _mod_kernel,
        grid=(DEPTH, 3 * D_MODEL // tn),
        in_specs=[
            pl.BlockSpec((rows, D_MODEL), lambda l, j: (0, 0)),
            pl.BlockSpec((1, D_MODEL, tn), lambda l, j: (l, 0, j)),
            pl.BlockSpec((1, 1, tn), lambda l, j: (l, 0, j)),
        ],
        out_specs=pl.BlockSpec((1, rows, tn), lambda l, j: (l, 0, j)),
        out_shape=jax.ShapeDtypeStruct((DEPTH, rows, 3 * D_MODEL), F32),
        compiler_params=pltpu.CompilerParams(dimension_semantics=("arbitrary", "arbitrary")),
        name="modulation",
    )(cvec, ada_w, ada_b.reshape(DEPTH, 1, 3 * D_MODEL))


def _full(a):
    nd = a.ndim
    return a, pl.BlockSpec(a.shape, lambda b, _nd=nd: (0,) * _nd)


def _per_seq(a):
    nd = a.ndim
    return a, pl.BlockSpec((1,) + a.shape[1:], lambda b, _nd=nd: (b,) + (0,) * (_nd - 1))


def _of_layer(a, i):
    nd = a.ndim
    return a, pl.BlockSpec((1,) + a.shape[1:], lambda b, _nd=nd, _i=i: (_i,) + (0,) * (_nd - 1))


def _seq_of_layer(a, i):
    nd = a.ndim
    return a, pl.BlockSpec((1, 1) + a.shape[2:], lambda b, _nd=nd, _i=i: (b, _i) + (0,) * (_nd - 2))


def _mod_spec(mod, l, row):
    if row is None:
        return mod, pl.BlockSpec((1, 1, 3, D_MODEL), lambda b, _l=l: (_l, b, 0, 0))
    return mod, pl.BlockSpec((1, 1, 3, D_MODEL), lambda b, _l=l, _r=row: (_l, _r, 0, 0))


def _rope_cs(n, rot_dim):
    rows = n // GRID_W
    r = jnp.repeat(jnp.arange(rows, dtype=F32), GRID_W)
    col = jnp.tile(jnp.arange(GRID_W, dtype=F32), rows)
    nf = rot_dim // 4
    inv = ROPE_BASE ** (-jnp.arange(nf, dtype=F32) / nf)
    ar = r[:, None] * inv
    ac = col[:, None] * inv
    cos = jnp.concatenate([jnp.cos(ar), jnp.cos(ar), jnp.cos(ac), jnp.cos(ac)], axis=-1)
    sin = jnp.concatenate([-jnp.sin(ar), jnp.sin(ar), -jnp.sin(ac), jnp.sin(ac)], axis=-1)
    return cos, sin


def _layer_call(body, x, operands, extra_out, scratch, name):
    b, n, _ = x.shape
    in_specs = [spec for _, spec in operands]
    args = [a for a, _ in operands]
    out_shape = [jax.ShapeDtypeStruct(x.shape, F32)] + [jax.ShapeDtypeStruct((b, n, w), F32) for w in extra_out]
    out_specs = [pl.BlockSpec((1, n, D_MODEL), lambda i: (i, 0, 0))] + [
        pl.BlockSpec((1, n, w), lambda i: (i, 0, 0)) for w in extra_out]
    return pl.pallas_call(
        body,
        grid=(b,),
        in_specs=in_specs,
        out_specs=out_specs,
        out_shape=out_shape,
        scratch_shapes=scratch,
        compiler_params=pltpu.CompilerParams(dimension_semantics=("arbitrary",), vmem_limit_bytes=VMEM_LIMIT),
        name=name,
    )(*args)


def _even_layer(x, mod_op, i, p, ctx, tab, final_g, name):
    b, n, _ = x.shape
    n_ctx = 0 if ctx is None else ctx[0].shape[2]
    ops = [_per_seq(x), mod_op, _of_layer(p["norm_g"], 2 * i),
           _of_layer(p["w_head"], i), _of_layer(p["w_main"], i), _of_layer(p["w_uq"], i), _of_layer(p["w_uk"], i),
           _of_layer(p["w_uv"], i), _of_layer(p["ev_w_out"], i),
           _of_layer(p["gq"], i), _of_layer(p["gkv"], i), _of_layer(p["conv_w"], i), _of_layer(p["conv_b"], i)]
    if ctx is not None:
        ops += [_seq_of_layer(ctx[0], i), _per_seq(ctx[1][i]), _full(tab)]
    if final_g is not None:
        ops.append(_full(final_g))
    nk = n_ctx + n
    scratch = [pltpu.VMEM((n, D_MODEL), BF16), pltpu.VMEM((n, H_A * LANES), BF16),
               pltpu.VMEM((nk, LANES), BF16), pltpu.VMEM((nk, LANES), BF16),
               pltpu.VMEM((nk, H_A * LANES), BF16), pltpu.VMEM((nk, W_A), BF16),
               pltpu.VMEM((n + 2 * CONV_PAD, W_B), F32)]
    body = functools.partial(_even_kernel, n=n, n_ctx=n_ctx, final=final_g is not None)
    extra = () if ctx is not None else (KV_RANK, LANES)
    return _layer_call(body, x, ops, extra, scratch, name)


def _odd_layer(x, mod_op, i, p, ctx, tab, final_g, name):
    b, n, _ = x.shape
    n_ctx = 0 if ctx is None else ctx[0].shape[2]
    ops = [_per_seq(x), mod_op, _of_layer(p["norm_g"], 2 * i + 1),
           _of_layer(p["od_w_in"], i), _of_layer(p["od_w_out"], i),
           (p["sink"], pl.BlockSpec(memory_space=pltpu.SMEM)), _of_layer(p["gg"], i), _of_layer(p["ws"], i),
           _of_layer(p["bs"], i)]
    if ctx is not None:
        ops += [_seq_of_layer(ctx[0], i), _seq_of_layer(ctx[1], i), _full(tab)]
    if final_g is not None:
        ops.append(_full(final_g))
    pad = WINDOW if ctx is not None else 0
    scratch = [pltpu.VMEM((n, D_MODEL), BF16), pltpu.VMEM((n + 2 * pad, KV_C * LANES), BF16),
               pltpu.VMEM((n + 2 * pad, KV_C * LANES), BF16)]
    if ctx is not None:
        scratch += [pltpu.VMEM((n_ctx, KV_C * LANES), BF16), pltpu.VMEM((n_ctx, KV_C * LANES), BF16)]
    body = functools.partial(_odd_kernel, n=n, n_ctx=n_ctx, final=final_g is not None, layer=i)
    extra = () if ctx is not None else (KV_C * HD_C, KV_C * HD_C)
    return _layer_call(body, x, ops, extra, scratch, name)


def kernel(x_prompt, x_sample, cache_mla_ckv, cache_mla_krope, cache_swa_k, cache_swa_v, c, c_ctx, ada_w, ada_b, norm_g, ev_w_in, ev_w_out, mla_gq, mla_gkv, mla_w_uq, mla_w_uk, mla_w_uv, conv_w, conv_b, od_w_in, od_w_out, swa_sink, gmlp_g, gmlp_ws, gmlp_bs, final_g):
    bp, n_p, _ = x_prompt.shape
    bs_, n_s, _ = x_sample.shape
    n_past = cache_mla_ckv.shape[2]
    n_even, n_odd = ev_w_in.shape[0], od_w_in.shape[0]

    n_rows = -(-(bs_ + 1) // 8) * 8
    cvec = jnp.concatenate([c, c_ctx[None, :], jnp.zeros((n_rows - bs_ - 1, D_MODEL), F32)], axis=0)
    mod = _modulation(cvec, ada_w, ada_b).reshape(DEPTH, n_rows, 3, D_MODEL)

    cos_a, sin_a = _rope_cs(n_s, ROPE_A)
    cos_c, sin_c = _rope_cs(n_s, HD_C)
    zero = lambda w: jnp.zeros((n_s, w), F32)
    tab_a = jnp.concatenate([jnp.ones((n_s, NOPE_A), F32), cos_a, zero(LANES - QK_A),
                             zero(NOPE_A), sin_a, zero(LANES - QK_A)], axis=1)
    tab_c = jnp.concatenate([cos_c, cos_c, sin_c, sin_c], axis=1)

    zc = lambda w: jnp.zeros((n_even, D_MODEL, w), F32)
    kv_end = Q_RANK + KV_RANK
    uq = mla_w_uq.reshape(n_even, Q_RANK, H_A, QK_A)
    uk = mla_w_uk.reshape(n_even, KV_RANK, H_A, NOPE_A)
    p = {
        "norm_g": norm_g.reshape(DEPTH, 1, D_MODEL),
        "w_head": jnp.concatenate([ev_w_in[:, :, 0:kv_end], zc(NOPE_A), ev_w_in[:, :, kv_end:EVEN_HEAD],
                                   zc(LANES - QK_A)], axis=-1).astype(BF16),
        "w_main": ev_w_in[:, :, EVEN_HEAD:].astype(BF16),
        "w_uq": jnp.pad(uq, ((0, 0), (0, 0), (0, 0), (0, LANES - QK_A))).reshape(n_even, Q_RANK, H_A * LANES).astype(BF16),
        "w_uk": jnp.pad(uk, ((0, 0), (0, 0), (0, 0), (0, LANES - NOPE_A))).reshape(n_even, KV_RANK, H_A * LANES).astype(BF16),
        "w_uv": mla_w_uv.astype(BF16),
        "ev_w_out": ev_w_out.astype(BF16),
        "gq": mla_gq.reshape(n_even, 1, Q_RANK),
        "gkv": mla_gkv.reshape(n_even, 1, KV_RANK),
        "conv_w": conv_w,
        "conv_b": conv_b.reshape(n_even, 1, W_B),
        "od_w_in": od_w_in.astype(BF16),
        "od_w_out": od_w_out.astype(BF16),
        "sink": swa_sink,
        "gg": gmlp_g.reshape(n_odd, 1, W_D),
        "ws": gmlp_ws.reshape(n_odd, G_D // 2, 2 * CHUNK, CHUNK).astype(BF16),
        "bs": jnp.repeat(jnp.swapaxes(gmlp_bs, 1, 2), DG_D, axis=2),
    }
    ckr = jnp.pad(cache_mla_krope, ((0, 0), (0, 0), (0, 0), (NOPE_A, LANES - QK_A)))
    ckr = [ckr[:, i] for i in range(n_even)]
    swa_k = cache_swa_k.reshape(bs_, n_odd, n_past, KV_C * HD_C)
    swa_v = cache_swa_v.reshape(bs_, n_odd, n_past, KV_C * HD_C)

    fg = final_g.reshape(1, D_MODEL)
    xp, xs = x_prompt, x_sample
    ckv_l, kr_l, k_l, v_l = [], [], [], []
    for l in range(DEPTH):
        i = l // 2
        last = fg if l == DEPTH - 1 else None
        mod_p = _mod_spec(mod, l, bs_)
        mod_s = _mod_spec(mod, l, None)
        if l % 2 == 0:
            xp, ckv, kr = _even_layer(xp, mod_p, i, p, None, None, last, f"even{l}_ctx")
            ckv_l.append(ckv)
            kr_l.append(kr[:, :, NOPE_A:QK_A])
            (xs,) = _even_layer(xs, mod_s, i, p, (cache_mla_ckv, ckr), tab_a, last, f"even{l}_lat")
        else:
            xp, k_new, v_new = _odd_layer(xp, mod_p, i, p, None, None, last, f"odd{l}_ctx")
            k_l.append(k_new.reshape(bp, n_p, KV_C, HD_C))
            v_l.append(v_new.reshape(bp, n_p, KV_C, HD_C))
            (xs,) = _odd_layer(xs, mod_s, i, p, (swa_k, swa_v), tab_c, last, f"odd{l}_lat")

    return (xp, xs, jnp.stack(ckv_l, axis=1), jnp.stack(kr_l, axis=1), jnp.stack(k_l, axis=1), jnp.stack(v_l, axis=1))
```

```python
import functools

import numpy as np
import jax
import jax.numpy as jnp
from jax import lax
from jax.experimental import pallas as pl
from jax.experimental.pallas import tpu as pltpu

D_MODEL = 1024
DEPTH = 4
GRID_W = 64
ROPE_BASE = 10000.0
EPS = 1e-6
NEG_INF = -1e30

H_A = 8
NOPE_A = 64
ROPE_A = 32
V_A = 64
Q_RANK = 256
KV_RANK = 128
QK_A = NOPE_A + ROPE_A
W_A = H_A * V_A
W_B = 512
CONV_W = 3
H_C = 8
KV_C = 2
HD_C = 64
WINDOW = 128
W_C = H_C * HD_C
G_D = 8
W_D = 512
CHUNK = 128
DG_D = W_D // G_D

EVEN_SPLIT = (Q_RANK, KV_RANK, ROPE_A, W_A, W_B, W_B, W_B, W_B)
ODD_SPLIT = (W_C, KV_C * HD_C, KV_C * HD_C, W_C, W_D, W_D, W_D)
EVEN_HEAD = Q_RANK + KV_RANK + ROPE_A

LANES = 128
ROW_BLOCK = 256
CONV_PAD = 8
VMEM_LIMIT = 60 * 1024 * 1024

F32 = jnp.float32
BF16 = jnp.bfloat16
LOG2E = float(np.log2(np.e))


def _dot(a, b):
    return jnp.dot(a, b, preferred_element_type=F32)


def _dot_t(a, b):
    return lax.dot_general(a, b, (((1,), (1,)), ((), ())), preferred_element_type=F32)


def _rms(x, g):
    return x * lax.rsqrt(jnp.mean(x * x, axis=-1, keepdims=True) + EPS) * g


def _silu(x):
    return x * (1.0 / (1.0 + jnp.exp(-x)))


def _gelu(x):
    c = np.float32(np.sqrt(2.0 / np.pi))
    return 0.5 * x * (1.0 + jnp.tanh(c * (x + 0.044715 * (x * x * x))))


def _lane(shape):
    return lax.broadcasted_iota(jnp.int32, shape, 1)


def _pair_select(lo, hi):
    return jnp.where(_lane(lo.shape) < LANES // 2, lo, hi)


def _rope(x, tab, quarter):
    fwd = pltpu.roll(x, LANES - quarter, 1)
    bwd = pltpu.roll(x, quarter, 1)
    partner = jnp.where((_lane(x.shape) & quarter) == 0, fwd, bwd)
    return x * tab[:, 0:LANES] + partner * tab[:, LANES:2 * LANES]


def _dup_halves(x):
    sw = pltpu.roll(x, LANES // 2, 1)
    lo = _lane(x.shape) < LANES // 2
    return jnp.where(lo, x, sw), jnp.where(lo, sw, x)


def _modulated_norm(x, mod_ref, ng_ref):
    shift = mod_ref[0, 0, 0:1, :]
    scale = mod_ref[0, 0, 1:2, :]
    return _rms(x, ng_ref[0]) * (1.0 + scale) + shift


def _pipelined(n, produce, consume):
    produce(0)
    outs = []
    for i in range(n):
        if i + 1 < n:
            produce(i + 1)
        outs.append(consume(i))
    return outs


def _normalised(o):
    return o[:, 0:LANES] * (1.0 / o[:, LANES:2 * LANES])


def _row_loop(n_blocks, body):
    if n_blocks == 1:
        body(0)
    else:
        def step(i, carry):
            body(pl.multiple_of(i * ROW_BLOCK, ROW_BLOCK))
            return carry
        lax.fori_loop(0, n_blocks, step, 0)


def _even_kernel(*refs, n, n_ctx, final):
    rope = n_ctx > 0
    it = iter(refs)
    x_ref, mod_ref, ng_ref = next(it), next(it), next(it)
    wh_ref, wm_ref, wuq_ref, wuk_ref, wuv_ref, wout_ref = (next(it) for _ in range(6))
    gq_ref, gkv_ref, cw_ref, cb_ref = next(it), next(it), next(it), next(it)
    if rope:
        cckv_ref, ckr_ref, tab_ref = next(it), next(it), next(it)
    if final:
        fg_ref = next(it)
    o_ref = next(it)
    if not rope:
        ckv_out, kr_out = next(it), next(it)
    h_s, q_s, ckv_s, kr_s, k_s, v_s, t_s, s_s = (next(it) for _ in range(8))

    nk = n_ctx + n
    q_scale = np.float32(QK_A ** -0.5 * LOG2E)

    if rope:
        ckv_s[0:n_ctx, :] = cckv_ref[0, 0].astype(BF16)
        kr_s[0:n_ctx, :] = ckr_ref[0].astype(BF16)
    zpad = jnp.zeros((CONV_PAD, W_B), F32)
    t_s[0:CONV_PAD, :] = zpad
    t_s[CONV_PAD + n:2 * CONV_PAD + n, :] = zpad

    def phase1(r0):
        rows = pl.ds(r0, ROW_BLOCK)
        hb = _modulated_norm(x_ref[0, rows, :], mod_ref, ng_ref).astype(BF16)
        h_s[rows, :] = hb
        ph = _dot(hb, wh_ref[0])
        qn = _rms(ph[:, 0:Q_RANK], gq_ref[0]).astype(BF16)
        qq = _dot(qn, wuq_ref[0])
        ckv = _rms(ph[:, Q_RANK:Q_RANK + KV_RANK], gkv_ref[0])
        kr = ph[:, Q_RANK + KV_RANK:Q_RANK + KV_RANK + LANES]
        if rope:
            tab = tab_ref[rows, :]
            kr = _rope(kr, tab, ROPE_A // 4)
            for h in range(H_A):
                qh = _rope(qq[:, h * LANES:(h + 1) * LANES], tab, ROPE_A // 4)
                q_s[rows, h * LANES:(h + 1) * LANES] = (qh * q_scale).astype(BF16)
        else:
            q_s[rows, :] = (qq * q_scale).astype(BF16)
            ckv_out[0, rows, :] = ckv
            kr_out[0, rows, :] = kr
        krows = pl.ds(n_ctx + r0, ROW_BLOCK)
        ckv_s[krows, :] = ckv.astype(BF16)
        kr_s[krows, :] = kr.astype(BF16)
        pc = _dot(hb, wm_ref[0, :, 2 * W_B:4 * W_B])
        t_s[pl.ds(CONV_PAD + r0, ROW_BLOCK), :] = pc[:, 0:W_B] * pc[:, W_B:2 * W_B]

    _row_loop(n // ROW_BLOCK, phase1)

    def phase2(r0):
        rows = pl.ds(r0, ROW_BLOCK)
        c = ckv_s[rows, :]
        kn = _dot(c, wuk_ref[0])
        krf = kr_s[rows, :].astype(F32)
        for h in range(H_A):
            k_s[rows, h * LANES:(h + 1) * LANES] = (kn[:, h * LANES:(h + 1) * LANES] + krf).astype(BF16)
        v = _dot(c, wuv_ref[0]).astype(BF16)
        ones = jnp.ones((ROW_BLOCK, LANES), BF16)
        for j in range(H_A // 2):
            v_s[rows, 2 * j * LANES:(2 * j + 1) * LANES] = v[:, j * LANES:(j + 1) * LANES]
            v_s[rows, (2 * j + 1) * LANES:(2 * j + 2) * LANES] = ones

    _row_loop(nk // ROW_BLOCK, phase2)

    def phase3(r0):
        rows = pl.ds(r0, ROW_BLOCK)
        hb = h_s[rows, :]

        def scores(h):
            s_s[h % 2] = _dot_t(q_s[rows, h * LANES:(h + 1) * LANES], k_s[:, h * LANES:(h + 1) * LANES])

        def attend(h):
            s = s_s[h % 2]
            e = jnp.exp2(s - jnp.max(s, axis=-1, keepdims=True)).astype(BF16)
            j = h // 2
            return _normalised(_dot(e, v_s[:, 2 * j * LANES:(2 * j + 2) * LANES]))

        heads = _pipelined(H_A, scores, attend)
        pairs = [_pair_select(heads[2 * j], heads[2 * j + 1]) for j in range(H_A // 2)]
        pa = _dot(hb, wm_ref[0, :, 0:2 * W_B])
        a_out = jnp.concatenate(pairs, axis=-1) * _silu(pa[:, 0:W_A])
        gate_b = pa[:, W_A:W_A + W_B]
        g_b = _dot(hb, wm_ref[0, :, 4 * W_B:5 * W_B])

        t_cur = t_s[pl.ds(CONV_PAD + r0, ROW_BLOCK), :]
        t_up = t_s[pl.ds(r0, CONV_PAD), :]
        t_dn = t_s[pl.ds(CONV_PAD + r0 + ROW_BLOCK, CONV_PAD), :]
        rid = lax.broadcasted_iota(jnp.int32, (ROW_BLOCK, W_B), 0)
        t_prev = jnp.where(rid == 0, t_up[CONV_PAD - 1:CONV_PAD, :], pltpu.roll(t_cur, 1, 0))
        t_next = jnp.where(rid == ROW_BLOCK - 1, t_dn[0:1, :], pltpu.roll(t_cur, ROW_BLOCK - 1, 0))
        conv = cb_ref[0] + t_prev * cw_ref[0, 0:1, :]
        conv = conv + t_cur * cw_ref[0, 1:2, :]
        conv = conv + t_next * cw_ref[0, 2:3, :]
        b_out = gate_b * conv * _silu(g_b)

        y = _dot(a_out.astype(BF16), wout_ref[0, 0:W_A, :]) + _dot(b_out.astype(BF16), wout_ref[0, W_A:W_A + W_B, :])
        res = x_ref[0, rows, :] + mod_ref[0, 0, 2:3, :] * y
        if final:
            res = _rms(res, fg_ref[...])
        o_ref[0, rows, :] = res

    _row_loop(n // ROW_BLOCK, phase3)


def _odd_kernel(*refs, n, n_ctx, final, layer):
    rope = n_ctx > 0
    it = iter(refs)
    x_ref, mod_ref, ng_ref = next(it), next(it), next(it)
    win_ref, wout_ref = next(it), next(it)
    sink_ref, gg_ref, ws_ref, bs_ref = next(it), next(it), next(it), next(it)
    if rope:
        kc_ref, vc_ref, tab_ref = next(it), next(it), next(it)
    if final:
        fg_ref = next(it)
    o_ref = next(it)
    if not rope:
        k_out, v_out = next(it), next(it)
    h_s, kloc_s, vloc_s, s_s = next(it), next(it), next(it), next(it)
    if rope:
        kctx_s, vctx_s = next(it), next(it)

    pad = WINDOW if rope else 0
    kw, vw = LANES, 2 * LANES
    q0, k0, g0 = 0, W_C, W_C + 2 * KV_C * HD_C
    q_scale = np.float32(HD_C ** -0.5 * LOG2E)

    def store_keys(dst, rows, x):
        a, b = _dup_halves(x)
        dst[rows, 0:LANES] = a.astype(BF16)
        dst[rows, LANES:2 * LANES] = b.astype(BF16)

    def store_values(dst, rows, x):
        a, b = _dup_halves(x)
        ones = jnp.ones(x.shape, BF16)
        dst[rows, 0:LANES] = a.astype(BF16)
        dst[rows, LANES:2 * LANES] = ones
        dst[rows, 2 * LANES:3 * LANES] = b.astype(BF16)
        dst[rows, 3 * LANES:4 * LANES] = ones

    if rope:
        kloc_s[0:pad, :] = jnp.zeros((pad, KV_C * kw), BF16)
        kloc_s[pad + n:2 * pad + n, :] = jnp.zeros((pad, KV_C * kw), BF16)
        vloc_s[0:pad, :] = jnp.zeros((pad, KV_C * vw), BF16)
        vloc_s[pad + n:2 * pad + n, :] = jnp.zeros((pad, KV_C * vw), BF16)
        store_keys(kctx_s, slice(None), kc_ref[0, 0])
        store_values(vctx_s, slice(None), vc_ref[0, 0])

    def phase1(r0):
        rows = pl.ds(r0, ROW_BLOCK)
        hb = _modulated_norm(x_ref[0, rows, :], mod_ref, ng_ref).astype(BF16)
        h_s[rows, :] = hb
        p1 = _dot(hb, win_ref[0, :, k0:g0])
        k = p1[:, 0:LANES]
        v = p1[:, LANES:2 * LANES]
        if rope:
            k = _rope(k, tab_ref[rows, :], HD_C // 4)
        else:
            k_out[0, rows, :] = k
            v_out[0, rows, :] = v
        lrows = pl.ds(pad + r0, ROW_BLOCK)
        store_keys(kloc_s, lrows, k)
        store_values(vloc_s, lrows, v)

    _row_loop(n // ROW_BLOCK, phase1)

    def phase3(r0):
        rows = pl.ds(r0, ROW_BLOCK)
        hb = h_s[rows, :]
        pq = _dot(hb, win_ref[0, :, q0:k0])

        if rope:
            tab = tab_ref[rows, :]
            span = ROW_BLOCK + 2 * WINDOW
            kl = kloc_s[pl.ds(r0, span), :]
            vl = vloc_s[pl.ds(r0, span), :]
            ii = lax.broadcasted_iota(jnp.int32, (ROW_BLOCK, span), 0)
            jj = lax.broadcasted_iota(jnp.int32, (ROW_BLOCK, span), 1)
            kpos = jj + (r0 - WINDOW)
            dlt = jj - ii
            mask = (dlt >= 0) & (dlt <= 2 * WINDOW) & (kpos >= 0) & (kpos < n)
        else:
            kl = kloc_s[...]
            vl = vloc_s[...]

        qhs = []
        for m in range(H_C // 2):
            qb = pq[:, m * LANES:(m + 1) * LANES]
            if rope:
                qb = _rope(qb, tab, HD_C // 4)
            qb = qb * q_scale
            lo = _lane(qb.shape) < LANES // 2
            qhs.append(jnp.where(lo, qb, 0.0).astype(BF16))
            qhs.append(jnp.where(lo, 0.0, qb).astype(BF16))

        def kv_head(h):
            return h // (H_C // KV_C)

        def scores(h):
            j = kv_head(h)
            if rope:
                s_s[h % 2, :, 0:n_ctx] = _dot_t(qhs[h], kctx_s[:, j * kw:(j + 1) * kw])
                s_s[h % 2, :, n_ctx:] = _dot_t(qhs[h], kl[:, j * kw:(j + 1) * kw])
            else:
                s_s[h % 2] = _dot_t(qhs[h], kl[:, j * kw:(j + 1) * kw])

        def attend(h):
            j = kv_head(h)
            sk = sink_ref[layer, h] * np.float32(LOG2E)
            if rope:
                sc = s_s[h % 2, :, 0:n_ctx]
                sl = jnp.where(mask, s_s[h % 2, :, n_ctx:], NEG_INF)
                mx = jnp.maximum(jnp.max(sl, axis=-1, keepdims=True), jnp.max(sc, axis=-1, keepdims=True))
            else:
                sl = s_s[h % 2]
                mx = jnp.max(sl, axis=-1, keepdims=True)
            mx = jnp.maximum(mx, sk)
            o = _dot(jnp.exp2(sl - mx).astype(BF16), vl[:, j * vw:(j + 1) * vw])
            if rope:
                o = o + _dot(jnp.exp2(sc - mx).astype(BF16), vctx_s[:, j * vw:(j + 1) * vw])
            return o[:, 0:LANES] * (1.0 / (o[:, LANES:2 * LANES] + jnp.exp2(sk - mx)))

        heads = _pipelined(H_C, scores, attend)
        blocks = [_pair_select(heads[2 * m], heads[2 * m + 1]) for m in range(H_C // 2)]
        pg = _dot(hb, win_ref[0, :, g0:g0 + W_C + 3 * W_D])
        c_out = jnp.concatenate(blocks, axis=-1) * _silu(pg[:, 0:W_C])
        u = pg[:, W_C:W_C + W_D]
        vv = pg[:, W_C + W_D:W_C + 2 * W_D]
        g_d = pg[:, W_C + 2 * W_D:W_C + 3 * W_D]

        vvn = _rms(_gelu(vv), gg_ref[0]).astype(BF16)
        chunks = []
        for c in range(ROW_BLOCK // CHUNK):
            cols = []
            for jp in range(G_D // 2):
                vp = vvn[c * CHUNK:(c + 1) * CHUNK, jp * LANES:(jp + 1) * LANES]
                o = _dot(ws_ref[0, jp], vp)
                cols.append(_pair_select(o[0:CHUNK, :], o[CHUNK:2 * CHUNK, :]))
            chunks.append(jnp.concatenate(cols, axis=-1) + bs_ref[0])
        mixed = jnp.concatenate(chunks, axis=0)
        d_out = _gelu(u) * mixed * _silu(g_d)

        y = _dot(c_out.astype(BF16), wout_ref[0, 0:W_C, :]) + _dot(d_out.astype(BF16), wout_ref[0, W_C:W_C + W_D, :])
        res = x_ref[0, rows, :] + mod_ref[0, 0, 2:3, :] * y
        if final:
            res = _rms(res, fg_ref[...])
        o_ref[0, rows, :] = res

    _row_loop(n // ROW_BLOCK, phase3)


def _mod_kernel(c_ref, w_ref, b_ref, o_ref):
    a = _silu(c_ref[...]).astype(BF16)
    o_ref[0] = _dot(a, w_ref[0].astype(BF16)) + b_ref[0]


def _modulation(cvec, ada_w, ada_b):
    rows = cvec.shape[0]
    tn = D_MODEL
    return pl.pallas_call(
        _mod_kernel,
        grid=(DEPTH, 3 * D_MODEL // tn),
        in_specs=[
            pl.BlockSpec((rows, D_MODEL), lambda l, j: (0, 0)),
            pl.BlockSpec((1, D_MODEL, tn), lambda l, j: (l, 0, j)),
            pl.BlockSpec((1, 1, tn), lambda l, j: (l, 0, j)),
        ],
        out_specs=pl.BlockSpec((1, rows, tn), lambda l, j: (l, 0, j)),
        out_shape=jax.ShapeDtypeStruct((DEPTH, rows, 3 * D_MODEL), F32),
        compiler_params=pltpu.CompilerParams(dimension_semantics=("arbitrary", "arbitrary")),
        name="modulation",
    )(cvec, ada_w, ada_b.reshape(DEPTH, 1, 3 * D_MODEL))


def _full(a):
    nd = a.ndim
    return a, pl.BlockSpec(a.shape, lambda b, _nd=nd: (0,) * _nd)


def _per_seq(a):
    nd = a.ndim
    return a, pl.BlockSpec((1,) + a.shape[1:], lambda b, _nd=nd: (b,) + (0,) * (_nd - 1))


def _of_layer(a, i):
    nd = a.ndim
    return a, pl.BlockSpec((1,) + a.shape[1:], lambda b, _nd=nd, _i=i: (_i,) + (0,) * (_nd - 1))


def _seq_of_layer(a, i):
    nd = a.ndim
    return a, pl.BlockSpec((1, 1) + a.shape[2:], lambda b, _nd=nd, _i=i: (b, _i) + (0,) * (_nd - 2))


def _mod_spec(mod, l, row):
    if row is None:
        return mod, pl.BlockSpec((1, 1, 3, D_MODEL), lambda b, _l=l: (_l, b, 0, 0))
    return mod, pl.BlockSpec((1, 1, 3, D_MODEL), lambda b, _l=l, _r=row: (_l, _r, 0, 0))


def _rope_cs(n, rot_dim):
    rows = n // GRID_W
    r = jnp.repeat(jnp.arange(rows, dtype=F32), GRID_W)
    col = jnp.tile(jnp.arange(GRID_W, dtype=F32), rows)
    nf = rot_dim // 4
    inv = ROPE_BASE ** (-jnp.arange(nf, dtype=F32) / nf)
    ar = r[:, None] * inv
    ac = col[:, None] * inv
    cos = jnp.concatenate([jnp.cos(ar), jnp.cos(ar), jnp.cos(ac), jnp.cos(ac)], axis=-1)
    sin = jnp.concatenate([-jnp.sin(ar), jnp.sin(ar), -jnp.sin(ac), jnp.sin(ac)], axis=-1)
    return cos, sin


def _layer_call(body, x, operands, extra_out, scratch, name):
    b, n, _ = x.shape
    in_specs = [spec for _, spec in operands]
    args = [a for a, _ in operands]
    out_shape = [jax.ShapeDtypeStruct(x.shape, F32)] + [jax.ShapeDtypeStruct((b, n, w), F32) for w in extra_out]
    out_specs = [pl.BlockSpec((1, n, D_MODEL), lambda i: (i, 0, 0))] + [
        pl.BlockSpec((1, n, w), lambda i: (i, 0, 0)) for w in extra_out]
    return pl.pallas_call(
        body,
        grid=(b,),
        in_specs=in_specs,
        out_specs=out_specs,
        out_shape=out_shape,
        scratch_shapes=scratch,
        compiler_params=pltpu.CompilerParams(dimension_semantics=("arbitrary",), vmem_limit_bytes=VMEM_LIMIT),
        name=name,
    )(*args)


def _even_layer(x, mod_op, i, p, ctx, tab, final_g, name):
    b, n, _ = x.shape
    n_ctx = 0 if ctx is None else ctx[0].shape[2]
    ops = [_per_seq(x), mod_op, _of_layer(p["norm_g"], 2 * i),
           _of_layer(p["w_head"], i), _of_layer(p["w_main"], i), _of_layer(p["w_uq"], i), _of_layer(p["w_uk"], i),
           _of_layer(p["w_uv"], i), _of_layer(p["ev_w_out"], i),
           _of_layer(p["gq"], i), _of_layer(p["gkv"], i), _of_layer(p["conv_w"], i), _of_layer(p["conv_b"], i)]
    if ctx is not None:
        ops += [_seq_of_layer(ctx[0], i), _per_seq(ctx[1][i]), _full(tab)]
    if final_g is not None:
        ops.append(_full(final_g))
    nk = n_ctx + n
    scratch = [pltpu.VMEM((n, D_MODEL), BF16), pltpu.VMEM((n, H_A * LANES), BF16),
               pltpu.VMEM((nk, LANES), BF16), pltpu.VMEM((nk, LANES), BF16),
               pltpu.VMEM((nk, H_A * LANES), BF16), pltpu.VMEM((nk, 2 * W_A), BF16),
               pltpu.VMEM((n + 2 * CONV_PAD, W_B), F32), pltpu.VMEM((2, ROW_BLOCK, nk), F32)]
    body = functools.partial(_even_kernel, n=n, n_ctx=n_ctx, final=final_g is not None)
    extra = () if ctx is not None else (KV_RANK, LANES)
    return _layer_call(body, x, ops, extra, scratch, name)


def _odd_layer(x, mod_op, i, p, ctx, tab, final_g, name):
    b, n, _ = x.shape
    n_ctx = 0 if ctx is None else ctx[0].shape[2]
    ops = [_per_seq(x), mod_op, _of_layer(p["norm_g"], 2 * i + 1),
           _of_layer(p["od_w_in"], i), _of_layer(p["od_w_out"], i),
           (p["sink"], pl.BlockSpec(memory_space=pltpu.SMEM)), _of_layer(p["gg"], i), _of_layer(p["ws"], i),
           _of_layer(p["bs"], i)]
    if ctx is not None:
        ops += [_seq_of_layer(ctx[0], i), _seq_of_layer(ctx[1], i), _full(tab)]
    if final_g is not None:
        ops.append(_full(final_g))
    pad = WINDOW if ctx is not None else 0
    n_keys = n_ctx + ROW_BLOCK + 2 * pad if ctx is not None else n
    scratch = [pltpu.VMEM((n, D_MODEL), BF16), pltpu.VMEM((n + 2 * pad, KV_C * LANES), BF16),
               pltpu.VMEM((n + 2 * pad, KV_C * 2 * LANES), BF16), pltpu.VMEM((2, ROW_BLOCK, n_keys), F32)]
    if ctx is not None:
        scratch += [pltpu.VMEM((n_ctx, KV_C * LANES), BF16), pltpu.VMEM((n_ctx, KV_C * 2 * LANES), BF16)]
    body = functools.partial(_odd_kernel, n=n, n_ctx=n_ctx, final=final_g is not None, layer=i)
    extra = () if ctx is not None else (KV_C * HD_C, KV_C * HD_C)
    return _layer_call(body, x, ops, extra, scratch, name)


def kernel(x_prompt, x_sample, cache_mla_ckv, cache_mla_krope, cache_swa_k, cache_swa_v, c, c_ctx, ada_w, ada_b, norm_g, ev_w_in, ev_w_out, mla_gq, mla_gkv, mla_w_uq, mla_w_uk, mla_w_uv, conv_w, conv_b, od_w_in, od_w_out, swa_sink, gmlp_g, gmlp_ws, gmlp_bs, final_g):
    bp, n_p, _ = x_prompt.shape
    bs_, n_s, _ = x_sample.shape
    n_past = cache_mla_ckv.shape[2]
    n_even, n_odd = ev_w_in.shape[0], od_w_in.shape[0]

    n_rows = -(-(bs_ + 1) // 8) * 8
    cvec = jnp.concatenate([c, c_ctx[None, :], jnp.zeros((n_rows - bs_ - 1, D_MODEL), F32)], axis=0)
    mod = _modulation(cvec, ada_w, ada_b).reshape(DEPTH, n_rows, 3, D_MODEL)

    cos_a, sin_a = _rope_cs(n_s, ROPE_A)
    cos_c, sin_c = _rope_cs(n_s, HD_C)
    zero = lambda w: jnp.zeros((n_s, w), F32)
    tab_a = jnp.concatenate([jnp.ones((n_s, NOPE_A), F32), cos_a, zero(LANES - QK_A),
                             zero(NOPE_A), sin_a, zero(LANES - QK_A)], axis=1)
    tab_c = jnp.concatenate([cos_c, cos_c, sin_c, sin_c], axis=1)

    zc = lambda w: jnp.zeros((n_even, D_MODEL, w), F32)
    kv_end = Q_RANK + KV_RANK
    uq = mla_w_uq.reshape(n_even, Q_RANK, H_A, QK_A)
    uk = mla_w_uk.reshape(n_even, KV_RANK, H_A, NOPE_A)
    p = {
        "norm_g": norm_g.reshape(DEPTH, 1, D_MODEL),
        "w_head": jnp.concatenate([ev_w_in[:, :, 0:kv_end], zc(NOPE_A), ev_w_in[:, :, kv_end:EVEN_HEAD],
                                   zc(LANES - QK_A)], axis=-1).astype(BF16),
        "w_main": ev_w_in[:, :, EVEN_HEAD:].astype(BF16),
        "w_uq": jnp.pad(uq, ((0, 0), (0, 0), (0, 0), (0, LANES - QK_A))).reshape(n_even, Q_RANK, H_A * LANES).astype(BF16),
        "w_uk": jnp.pad(uk, ((0, 0), (0, 0), (0, 0), (0, LANES - NOPE_A))).reshape(n_even, KV_RANK, H_A * LANES).astype(BF16),
        "w_uv": mla_w_uv.astype(BF16),
        "ev_w_out": ev_w_out.astype(BF16),
        "gq": mla_gq.reshape(n_even, 1, Q_RANK),
        "gkv": mla_gkv.reshape(n_even, 1, KV_RANK),
        "conv_w": conv_w,
        "conv_b": conv_b.reshape(n_even, 1, W_B),
        "od_w_in": od_w_in.astype(BF16),
        "od_w_out": od_w_out.astype(BF16),
        "sink": swa_sink,
        "gg": gmlp_g.reshape(n_odd, 1, W_D),
        "ws": gmlp_ws.reshape(n_odd, G_D // 2, 2 * CHUNK, CHUNK).astype(BF16),
        "bs": jnp.repeat(jnp.swapaxes(gmlp_bs, 1, 2), DG_D, axis=2),
    }
    ckr = jnp.pad(cache_mla_krope, ((0, 0), (0, 0), (0, 0), (NOPE_A, LANES - QK_A)))
    ckr = [ckr[:, i] for i in range(n_even)]
    swa_k = cache_swa_k.reshape(bs_, n_odd, n_past, KV_C * HD_C)
    swa_v = cache_swa_v.reshape(bs_, n_odd, n_past, KV_C * HD_C)

    fg = final_g.reshape(1, D_MODEL)
    xp, xs = x_prompt, x_sample
    ckv_l, kr_l, k_l, v_l = [], [], [], []
    for l in range(DEPTH):
        i = l // 2
        last = fg if l == DEPTH - 1 else None
        mod_p = _mod_spec(mod, l, bs_)
        mod_s = _mod_spec(mod, l, None)
        if l % 2 == 0:
            xp, ckv, kr = _even_layer(xp, mod_p, i, p, None, None, last, f"even{l}_ctx")
            ckv_l.append(ckv)
            kr_l.append(kr[:, :, NOPE_A:QK_A])
            (xs,) = _even_layer(xs, mod_s, i, p, (cache_mla_ckv, ckr), tab_a, last, f"even{l}_lat")
        else:
            xp, k_new, v_new = _odd_layer(xp, mod_p, i, p, None, None, last, f"odd{l}_ctx")
            k_l.append(k_new.reshape(bp, n_p, KV_C, HD_C))
            v_l.append(v_new.reshape(bp, n_p, KV_C, HD_C))
            (xs,) = _odd_layer(xs, mod_s, i, p, (swa_k, swa_v), tab_c, last, f"odd{l}_lat")

    return (xp, xs, jnp.stack(ckv_l, axis=1), jnp.stack(kr_l, axis=1), jnp.stack(k_l, axis=1), jnp.stack(v_l, axis=1))
```

```python
import functools

import numpy as np
import jax
import jax.numpy as jnp
from jax import lax
from jax.experimental import pallas as pl
from jax.experimental.pallas import tpu as pltpu

D_MODEL = 1024
DEPTH = 4
GRID_W = 64
ROPE_BASE = 10000.0
EPS = 1e-6
NEG_INF = -1e30

H_A = 8
NOPE_A = 64
ROPE_A = 32
V_A = 64
Q_RANK = 256
KV_RANK = 128
QK_A = NOPE_A + ROPE_A
W_A = H_A * V_A
W_B = 512
CONV_W = 3
H_C = 8
KV_C = 2
HD_C = 64
WINDOW = 128
W_C = H_C * HD_C
G_D = 8
W_D = 512
CHUNK = 128
DG_D = W_D // G_D

EVEN_SPLIT = (Q_RANK, KV_RANK, ROPE_A, W_A, W_B, W_B, W_B, W_B)
ODD_SPLIT = (W_C, KV_C * HD_C, KV_C * HD_C, W_C, W_D, W_D, W_D)
EVEN_HEAD = Q_RANK + KV_RANK + ROPE_A

LANES = 128
ROW_BLOCK = 256
CONV_PAD = 8
CTX_SEQS_PER_STEP = 4
VMEM_LIMIT = 60 * 1024 * 1024

F32 = jnp.float32
BF16 = jnp.bfloat16
LOG2E = float(np.log2(np.e))


def _dot(a, b):
    return jnp.dot(a, b, preferred_element_type=F32)


def _dot_t(a, b):
    return lax.dot_general(a, b, (((1,), (1,)), ((), ())), preferred_element_type=F32)


def _rms(x, g):
    return x * lax.rsqrt(jnp.mean(x * x, axis=-1, keepdims=True) + EPS) * g


def _silu(x):
    return x * (1.0 / (1.0 + jnp.exp(-x)))


def _gelu(x):
    c = np.float32(np.sqrt(2.0 / np.pi))
    return 0.5 * x * (1.0 + jnp.tanh(c * (x + 0.044715 * (x * x * x))))


def _lane(shape):
    return lax.broadcasted_iota(jnp.int32, shape, 1)


def _pair_select(lo, hi):
    return jnp.where(_lane(lo.shape) < LANES // 2, lo, hi)


def _rope(x, tab, quarter):
    fwd = pltpu.roll(x, LANES - quarter, 1)
    bwd = pltpu.roll(x, quarter, 1)
    partner = jnp.where((_lane(x.shape) & quarter) == 0, fwd, bwd)
    return x * tab[:, 0:LANES] + partner * tab[:, LANES:2 * LANES]


def _dup_halves(x):
    sw = pltpu.roll(x, LANES // 2, 1)
    lo = _lane(x.shape) < LANES // 2
    return jnp.where(lo, x, sw), jnp.where(lo, sw, x)


def _modulated_norm(x, mod_ref, ng_ref):
    shift = mod_ref[0, 0, 0:1, :]
    scale = mod_ref[0, 0, 1:2, :]
    return _rms(x, ng_ref[0]) * (1.0 + scale) + shift


def _pipelined(n, produce, consume):
    produce(0)
    outs = []
    for i in range(n):
        if i + 1 < n:
            produce(i + 1)
        outs.append(consume(i))
    return outs


def _normalised(o):
    return o[:, 0:LANES] * (1.0 / o[:, LANES:2 * LANES])


def _loop(n, body):
    if n == 1:
        body(0)
    else:
        def step(i, carry):
            body(i)
            return carry
        lax.fori_loop(0, n, step, 0)


def _row_loop(n_blocks, body):
    _loop(n_blocks, lambda i: body(i * ROW_BLOCK if isinstance(i, int) else pl.multiple_of(i * ROW_BLOCK, ROW_BLOCK)))


def _even_kernel(*refs, n, n_ctx, n_seq, final):
    rope = n_ctx > 0
    it = iter(refs)
    x_ref, mod_ref, ng_ref = next(it), next(it), next(it)
    wh_ref, wm_ref, wuq_ref, wuk_ref, wuv_ref, wout_ref = (next(it) for _ in range(6))
    gq_ref, gkv_ref, cw_ref, cb_ref = next(it), next(it), next(it), next(it)
    if rope:
        cckv_ref, ckr_ref, tab_ref = next(it), next(it), next(it)
    if final:
        fg_ref = next(it)
    o_ref = next(it)
    if not rope:
        ckv_out, kr_out = next(it), next(it)
    h_s, q_s, ckv_s, kr_s, k_s, v_s, t_s, s_s = (next(it) for _ in range(8))

    nk = n_ctx + n
    q_scale = np.float32(QK_A ** -0.5 * LOG2E)

    zpad = jnp.zeros((CONV_PAD, W_B), F32)
    t_s[0:CONV_PAD, :] = zpad
    t_s[CONV_PAD + n:2 * CONV_PAD + n, :] = zpad

    def sequence(sq):
        if rope:
            ckv_s[0:n_ctx, :] = cckv_ref[0, 0].astype(BF16)
            kr_s[0:n_ctx, :] = ckr_ref[0].astype(BF16)

        def phase1(r0):
            rows = pl.ds(r0, ROW_BLOCK)
            hb = _modulated_norm(x_ref[sq, rows, :], mod_ref, ng_ref).astype(BF16)
            h_s[rows, :] = hb
            ph = _dot(hb, wh_ref[0])
            qn = _rms(ph[:, 0:Q_RANK], gq_ref[0]).astype(BF16)
            qq = _dot(qn, wuq_ref[0])
            ckv = _rms(ph[:, Q_RANK:Q_RANK + KV_RANK], gkv_ref[0])
            kr = ph[:, Q_RANK + KV_RANK:Q_RANK + KV_RANK + LANES]
            if rope:
                tab = tab_ref[rows, :]
                kr = _rope(kr, tab, ROPE_A // 4)
                for h in range(H_A):
                    qh = _rope(qq[:, h * LANES:(h + 1) * LANES], tab, ROPE_A // 4)
                    q_s[rows, h * LANES:(h + 1) * LANES] = (qh * q_scale).astype(BF16)
            else:
                q_s[rows, :] = (qq * q_scale).astype(BF16)
                ckv_out[sq, rows, :] = ckv
                kr_out[sq, rows, :] = kr
            krows = pl.ds(n_ctx + r0, ROW_BLOCK)
            ckv_s[krows, :] = ckv.astype(BF16)
            kr_s[krows, :] = kr.astype(BF16)
            pc = _dot(hb, wm_ref[0, :, 2 * W_B:4 * W_B])
            t_s[pl.ds(CONV_PAD + r0, ROW_BLOCK), :] = pc[:, 0:W_B] * pc[:, W_B:2 * W_B]

        _row_loop(n // ROW_BLOCK, phase1)

        def phase2(r0):
            rows = pl.ds(r0, ROW_BLOCK)
            c = ckv_s[rows, :]
            kn = _dot(c, wuk_ref[0])
            krf = kr_s[rows, :].astype(F32)
            for h in range(H_A):
                k_s[rows, h * LANES:(h + 1) * LANES] = (kn[:, h * LANES:(h + 1) * LANES] + krf).astype(BF16)
            v = _dot(c, wuv_ref[0]).astype(BF16)
            ones = jnp.ones((ROW_BLOCK, LANES), BF16)
            for j in range(H_A // 2):
                v_s[rows, 2 * j * LANES:(2 * j + 1) * LANES] = v[:, j * LANES:(j + 1) * LANES]
                v_s[rows, (2 * j + 1) * LANES:(2 * j + 2) * LANES] = ones

        _row_loop(nk // ROW_BLOCK, phase2)

        def phase3(r0):
            rows = pl.ds(r0, ROW_BLOCK)
            hb = h_s[rows, :]

            def scores(h):
                s_s[h % 2] = _dot_t(q_s[rows, h * LANES:(h + 1) * LANES], k_s[:, h * LANES:(h + 1) * LANES])

            def attend(h):
                s = s_s[h % 2]
                e = jnp.exp2(s - jnp.max(s, axis=-1, keepdims=True)).astype(BF16)
                j = h // 2
                return _normalised(_dot(e, v_s[:, 2 * j * LANES:(2 * j + 2) * LANES]))

            heads = _pipelined(H_A, scores, attend)
            pairs = [_pair_select(heads[2 * j], heads[2 * j + 1]) for j in range(H_A // 2)]
            pa = _dot(hb, wm_ref[0, :, 0:2 * W_B])
            a_out = jnp.concatenate(pairs, axis=-1) * _silu(pa[:, 0:W_A])
            gate_b = pa[:, W_A:W_A + W_B]
            g_b = _dot(hb, wm_ref[0, :, 4 * W_B:5 * W_B])

            t_cur = t_s[pl.ds(CONV_PAD + r0, ROW_BLOCK), :]
            t_up = t_s[pl.ds(r0, CONV_PAD), :]
            t_dn = t_s[pl.ds(CONV_PAD + r0 + ROW_BLOCK, CONV_PAD), :]
            rid = lax.broadcasted_iota(jnp.int32, (ROW_BLOCK, W_B), 0)
            t_prev = jnp.where(rid == 0, t_up[CONV_PAD - 1:CONV_PAD, :], pltpu.roll(t_cur, 1, 0))
            t_next = jnp.where(rid == ROW_BLOCK - 1, t_dn[0:1, :], pltpu.roll(t_cur, ROW_BLOCK - 1, 0))
            conv = cb_ref[0] + t_prev * cw_ref[0, 0:1, :]
            conv = conv + t_cur * cw_ref[0, 1:2, :]
            conv = conv + t_next * cw_ref[0, 2:3, :]
            b_out = gate_b * conv * _silu(g_b)

            y = (_dot(a_out.astype(BF16), wout_ref[0, 0:W_A, :])
                 + _dot(b_out.astype(BF16), wout_ref[0, W_A:W_A + W_B, :]))
            res = x_ref[sq, rows, :] + mod_ref[0, 0, 2:3, :] * y
            if final:
                res = _rms(res, fg_ref[...])
            o_ref[sq, rows, :] = res

        _row_loop(n // ROW_BLOCK, phase3)

    _loop(n_seq, sequence)


def _odd_kernel(*refs, n, n_ctx, n_seq, final, layer):
    rope = n_ctx > 0
    it = iter(refs)
    x_ref, mod_ref, ng_ref = next(it), next(it), next(it)
    win_ref, wout_ref = next(it), next(it)
    sink_ref, gg_ref, ws_ref, bs_ref = next(it), next(it), next(it), next(it)
    if rope:
        kc_ref, vc_ref, tab_ref = next(it), next(it), next(it)
    if final:
        fg_ref = next(it)
    o_ref = next(it)
    if not rope:
        k_out, v_out = next(it), next(it)
    h_s, kloc_s, vloc_s, s_s = next(it), next(it), next(it), next(it)
    if rope:
        kctx_s, vctx_s = next(it), next(it)

    pad = WINDOW if rope else 0
    kw, vw = LANES, 2 * LANES
    q0, k0, g0 = 0, W_C, W_C + 2 * KV_C * HD_C
    q_scale = np.float32(HD_C ** -0.5 * LOG2E)

    def store_keys(dst, rows, x):
        a, b = _dup_halves(x)
        dst[rows, 0:LANES] = a.astype(BF16)
        dst[rows, LANES:2 * LANES] = b.astype(BF16)

    def store_values(dst, rows, x):
        a, b = _dup_halves(x)
        ones = jnp.ones(x.shape, BF16)
        dst[rows, 0:LANES] = a.astype(BF16)
        dst[rows, LANES:2 * LANES] = ones
        dst[rows, 2 * LANES:3 * LANES] = b.astype(BF16)
        dst[rows, 3 * LANES:4 * LANES] = ones

    def sequence(sq):
        if rope:
            kloc_s[0:pad, :] = jnp.zeros((pad, KV_C * kw), BF16)
            kloc_s[pad + n:2 * pad + n, :] = jnp.zeros((pad, KV_C * kw), BF16)
            vloc_s[0:pad, :] = jnp.zeros((pad, KV_C * vw), BF16)
            vloc_s[pad + n:2 * pad + n, :] = jnp.zeros((pad, KV_C * vw), BF16)
            store_keys(kctx_s, slice(None), kc_ref[0, 0])
            store_values(vctx_s, slice(None), vc_ref[0, 0])

        def phase1(r0):
            rows = pl.ds(r0, ROW_BLOCK)
            hb = _modulated_norm(x_ref[sq, rows, :], mod_ref, ng_ref).astype(BF16)
            h_s[rows, :] = hb
            p1 = _dot(hb, win_ref[0, :, k0:g0])
            k = p1[:, 0:LANES]
            v = p1[:, LANES:2 * LANES]
            if rope:
                k = _rope(k, tab_ref[rows, :], HD_C // 4)
            else:
                k_out[sq, rows, :] = k
                v_out[sq, rows, :] = v
            lrows = pl.ds(pad + r0, ROW_BLOCK)
            store_keys(kloc_s, lrows, k)
            store_values(vloc_s, lrows, v)

        _row_loop(n // ROW_BLOCK, phase1)

        def phase3(r0):
            rows = pl.ds(r0, ROW_BLOCK)
            hb = h_s[rows, :]
            pq = _dot(hb, win_ref[0, :, q0:k0])

            if rope:
                tab = tab_ref[rows, :]
                span = ROW_BLOCK + 2 * WINDOW
                kl = kloc_s[pl.ds(r0, span), :]
                vl = vloc_s[pl.ds(r0, span), :]
                ii = lax.broadcasted_iota(jnp.int32, (ROW_BLOCK, span), 0)
                jj = lax.broadcasted_iota(jnp.int32, (ROW_BLOCK, span), 1)
                kpos = jj + (r0 - WINDOW)
                dlt = jj - ii
                mask = (dlt >= 0) & (dlt <= 2 * WINDOW) & (kpos >= 0) & (kpos < n)
            else:
                kl = kloc_s[...]
                vl = vloc_s[...]

            qhs = []
            for m in range(H_C // 2):
                qb = pq[:, m * LANES:(m + 1) * LANES]
                if rope:
                    qb = _rope(qb, tab, HD_C // 4)
                qb = qb * q_scale
                lo = _lane(qb.shape) < LANES // 2
                qhs.append(jnp.where(lo, qb, 0.0).astype(BF16))
                qhs.append(jnp.where(lo, 0.0, qb).astype(BF16))

            def kv_head(h):
                return h // (H_C // KV_C)

            def scores(h):
                j = kv_head(h)
                if rope:
                    s_s[h % 2, :, 0:n_ctx] = _dot_t(qhs[h], kctx_s[:, j * kw:(j + 1) * kw])
                    s_s[h % 2, :, n_ctx:] = _dot_t(qhs[h], kl[:, j * kw:(j + 1) * kw])
                else:
                    s_s[h % 2] = _dot_t(qhs[h], kl[:, j * kw:(j + 1) * kw])

            def attend(h):
                j = kv_head(h)
                sk = sink_ref[layer, h] * np.float32(LOG2E)
                if rope:
                    sc = s_s[h % 2, :, 0:n_ctx]
                    sl = jnp.where(mask, s_s[h % 2, :, n_ctx:], NEG_INF)
                    mx = jnp.maximum(jnp.max(sl, axis=-1, keepdims=True), jnp.max(sc, axis=-1, keepdims=True))
                else:
                    sl = s_s[h % 2]
                    mx = jnp.max(sl, axis=-1, keepdims=True)
                mx = jnp.maximum(mx, sk)
                o = _dot(jnp.exp2(sl - mx).astype(BF16), vl[:, j * vw:(j + 1) * vw])
                if rope:
                    o = o + _dot(jnp.exp2(sc - mx).astype(BF16), vctx_s[:, j * vw:(j + 1) * vw])
                return o[:, 0:LANES] * (1.0 / (o[:, LANES:2 * LANES] + jnp.exp2(sk - mx)))

            heads = _pipelined(H_C, scores, attend)
            blocks = [_pair_select(heads[2 * m], heads[2 * m + 1]) for m in range(H_C // 2)]
            pg = _dot(hb, win_ref[0, :, g0:g0 + W_C + 3 * W_D])
            c_out = jnp.concatenate(blocks, axis=-1) * _silu(pg[:, 0:W_C])
            u = pg[:, W_C:W_C + W_D]
            vv = pg[:, W_C + W_D:W_C + 2 * W_D]
            g_d = pg[:, W_C + 2 * W_D:W_C + 3 * W_D]

            vvn = _rms(_gelu(vv), gg_ref[0]).astype(BF16)
            chunks = []
            for c in range(ROW_BLOCK // CHUNK):
                cols = []
                for jp in range(G_D // 2):
                    vp = vvn[c * CHUNK:(c + 1) * CHUNK, jp * LANES:(jp + 1) * LANES]
                    o = _dot(ws_ref[0, jp], vp)
                    cols.append(_pair_select(o[0:CHUNK, :], o[CHUNK:2 * CHUNK, :]))
                chunks.append(jnp.concatenate(cols, axis=-1) + bs_ref[0])
            mixed = jnp.concatenate(chunks, axis=0)
            d_out = _gelu(u) * mixed * _silu(g_d)

            y = (_dot(c_out.astype(BF16), wout_ref[0, 0:W_C, :])
                 + _dot(d_out.astype(BF16), wout_ref[0, W_C:W_C + W_D, :]))
            res = x_ref[sq, rows, :] + mod_ref[0, 0, 2:3, :] * y
            if final:
                res = _rms(res, fg_ref[...])
            o_ref[sq, rows, :] = res

        _row_loop(n // ROW_BLOCK, phase3)

    _loop(n_seq, sequence)


def _mod_kernel(c_ref, w_ref, b_ref, o_ref):
    a = _silu(c_ref[...]).astype(BF16)
    o_ref[0] = _dot(a, w_ref[0].astype(BF16)) + b_ref[0]


def _modulation(cvec, ada_w, ada_b):
    rows = cvec.shape[0]
    tn = D_MODEL
    return pl.pallas_call(
        _mod_kernel,
        grid=(DEPTH, 3 * D_MODEL // tn),
        in_specs=[
            pl.BlockSpec((rows, D_MODEL), lambda l, j: (0, 0)),
            pl.BlockSpec((1, D_MODEL, tn), lambda l, j: (l, 0, j)),
            pl.BlockSpec((1, 1, tn), lambda l, j: (l, 0, j)),
        ],
        out_specs=pl.BlockSpec((1, rows, tn), lambda l, j: (l, 0, j)),
        out_shape=jax.ShapeDtypeStruct((DEPTH, rows, 3 * D_MODEL), F32),
        compiler_params=pltpu.CompilerParams(dimension_semantics=("arbitrary", "arbitrary")),
        name="modulation",
    )(cvec, ada_w, ada_b.reshape(DEPTH, 1, 3 * D_MODEL))


def _full(a):
    nd = a.ndim
    return a, pl.BlockSpec(a.shape, lambda b, _nd=nd: (0,) * _nd)


def _per_step(a, n_seq):
    nd = a.ndim
    return a, pl.BlockSpec((n_seq,) + a.shape[1:], lambda b, _nd=nd: (b,) + (0,) * (_nd - 1))


def _of_layer(a, i):
    nd = a.ndim
    return a, pl.BlockSpec((1,) + a.shape[1:], lambda b, _nd=nd, _i=i: (_i,) + (0,) * (_nd - 1))


def _seq_of_layer(a, i):
    nd = a.ndim
    return a, pl.BlockSpec((1, 1) + a.shape[2:], lambda b, _nd=nd, _i=i: (b, _i) + (0,) * (_nd - 2))


def _mod_spec(mod, l, row):
    if row is None:
        return mod, pl.BlockSpec((1, 1, 3, D_MODEL), lambda b, _l=l: (_l, b, 0, 0))
    return mod, pl.BlockSpec((1, 1, 3, D_MODEL), lambda b, _l=l, _r=row: (_l, _r, 0, 0))


def _rope_cs(n, rot_dim):
    rows = n // GRID_W
    r = jnp.repeat(jnp.arange(rows, dtype=F32), GRID_W)
    col = jnp.tile(jnp.arange(GRID_W, dtype=F32), rows)
    nf = rot_dim // 4
    inv = ROPE_BASE ** (-jnp.arange(nf, dtype=F32) / nf)
    ar = r[:, None] * inv
    ac = col[:, None] * inv
    cos = jnp.concatenate([jnp.cos(ar), jnp.cos(ar), jnp.cos(ac), jnp.cos(ac)], axis=-1)
    sin = jnp.concatenate([-jnp.sin(ar), jnp.sin(ar), -jnp.sin(ac), jnp.sin(ac)], axis=-1)
    return cos, sin


def _layer_call(body, x, n_seq, operands, extra_out, scratch, name):
    b, n, _ = x.shape
    in_specs = [spec for _, spec in operands]
    args = [a for a, _ in operands]
    out_shape = [jax.ShapeDtypeStruct(x.shape, F32)] + [jax.ShapeDtypeStruct((b, n, w), F32) for w in extra_out]
    out_specs = [pl.BlockSpec((n_seq, n, D_MODEL), lambda i: (i, 0, 0))] + [
        pl.BlockSpec((n_seq, n, w), lambda i: (i, 0, 0)) for w in extra_out]
    return pl.pallas_call(
        body,
        grid=(b // n_seq,),
        in_specs=in_specs,
        out_specs=out_specs,
        out_shape=out_shape,
        scratch_shapes=scratch,
        compiler_params=pltpu.CompilerParams(dimension_semantics=("arbitrary",), vmem_limit_bytes=VMEM_LIMIT),
        name=name,
    )(*args)


def _seqs_per_step(b, ctx):
    return 1 if ctx is not None else int(np.gcd(b, CTX_SEQS_PER_STEP))


def _even_layer(x, mod_op, i, p, ctx, tab, final_g, name):
    b, n, _ = x.shape
    n_ctx = 0 if ctx is None else ctx[0].shape[2]
    n_seq = _seqs_per_step(b, ctx)
    ops = [_per_step(x, n_seq), mod_op, _of_layer(p["norm_g"], 2 * i),
           _of_layer(p["w_head"], i), _of_layer(p["w_main"], i), _of_layer(p["w_uq"], i), _of_layer(p["w_uk"], i),
           _of_layer(p["w_uv"], i), _of_layer(p["ev_w_out"], i),
           _of_layer(p["gq"], i), _of_layer(p["gkv"], i), _of_layer(p["conv_w"], i), _of_layer(p["conv_b"], i)]
    if ctx is not None:
        ops += [_seq_of_layer(ctx[0], i), _per_step(ctx[1][i], 1), _full(tab)]
    if final_g is not None:
        ops.append(_full(final_g))
    nk = n_ctx + n
    scratch = [pltpu.VMEM((n, D_MODEL), BF16), pltpu.VMEM((n, H_A * LANES), BF16),
               pltpu.VMEM((nk, LANES), BF16), pltpu.VMEM((nk, LANES), BF16),
               pltpu.VMEM((nk, H_A * LANES), BF16), pltpu.VMEM((nk, 2 * W_A), BF16),
               pltpu.VMEM((n + 2 * CONV_PAD, W_B), F32), pltpu.VMEM((2, ROW_BLOCK, nk), F32)]
    body = functools.partial(_even_kernel, n=n, n_ctx=n_ctx, n_seq=n_seq, final=final_g is not None)
    extra = () if ctx is not None else (KV_RANK, LANES)
    return _layer_call(body, x, n_seq, ops, extra, scratch, name)


def _odd_layer(x, mod_op, i, p, ctx, tab, final_g, name):
    b, n, _ = x.shape
    n_ctx = 0 if ctx is None else ctx[0].shape[2]
    n_seq = _seqs_per_step(b, ctx)
    ops = [_per_step(x, n_seq), mod_op, _of_layer(p["norm_g"], 2 * i + 1),
           _of_layer(p["od_w_in"], i), _of_layer(p["od_w_out"], i),
           (p["sink"], pl.BlockSpec(memory_space=pltpu.SMEM)), _of_layer(p["gg"], i), _of_layer(p["ws"], i),
           _of_layer(p["bs"], i)]
    if ctx is not None:
        ops += [_seq_of_layer(ctx[0], i), _seq_of_layer(ctx[1], i), _full(tab)]
    if final_g is not None:
        ops.append(_full(final_g))
    pad = WINDOW if ctx is not None else 0
    n_keys = n_ctx + ROW_BLOCK + 2 * pad if ctx is not None else n
    scratch = [pltpu.VMEM((n, D_MODEL), BF16), pltpu.VMEM((n + 2 * pad, KV_C * LANES), BF16),
               pltpu.VMEM((n + 2 * pad, KV_C * 2 * LANES), BF16), pltpu.VMEM((2, ROW_BLOCK, n_keys), F32)]
    if ctx is not None:
        scratch += [pltpu.VMEM((n_ctx, KV_C * LANES), BF16), pltpu.VMEM((n_ctx, KV_C * 2 * LANES), BF16)]
    body = functools.partial(_odd_kernel, n=n, n_ctx=n_ctx, n_seq=n_seq, final=final_g is not None, layer=i)
    extra = () if ctx is not None else (KV_C * HD_C, KV_C * HD_C)
    return _layer_call(body, x, n_seq, ops, extra, scratch, name)


def kernel(x_prompt, x_sample, cache_mla_ckv, cache_mla_krope, cache_swa_k, cache_swa_v, c, c_ctx, ada_w, ada_b, norm_g, ev_w_in, ev_w_out, mla_gq, mla_gkv, mla_w_uq, mla_w_uk, mla_w_uv, conv_w, conv_b, od_w_in, od_w_out, swa_sink, gmlp_g, gmlp_ws, gmlp_bs, final_g):
    bp, n_p, _ = x_prompt.shape
    bs_, n_s, _ = x_sample.shape
    n_past = cache_mla_ckv.shape[2]
    n_even, n_odd = ev_w_in.shape[0], od_w_in.shape[0]

    n_rows = -(-(bs_ + 1) // 8) * 8
    cvec = jnp.concatenate([c, c_ctx[None, :], jnp.zeros((n_rows - bs_ - 1, D_MODEL), F32)], axis=0)
    mod = _modulation(cvec, ada_w, ada_b).reshape(DEPTH, n_rows, 3, D_MODEL)

    cos_a, sin_a = _rope_cs(n_s, ROPE_A)
    cos_c, sin_c = _rope_cs(n_s, HD_C)
    zero = lambda w: jnp.zeros((n_s, w), F32)
    tab_a = jnp.concatenate([jnp.ones((n_s, NOPE_A), F32), cos_a, zero(LANES - QK_A),
                             zero(NOPE_A), sin_a, zero(LANES - QK_A)], axis=1)
    tab_c = jnp.concatenate([cos_c, cos_c, sin_c, sin_c], axis=1)

    zc = lambda w: jnp.zeros((n_even, D_MODEL, w), F32)
    kv_end = Q_RANK + KV_RANK
    uq = mla_w_uq.reshape(n_even, Q_RANK, H_A, QK_A)
    uk = mla_w_uk.reshape(n_even, KV_RANK, H_A, NOPE_A)
    p = {
        "norm_g": norm_g.reshape(DEPTH, 1, D_MODEL),
        "w_head": jnp.concatenate([ev_w_in[:, :, 0:kv_end], zc(NOPE_A), ev_w_in[:, :, kv_end:EVEN_HEAD],
                                   zc(LANES - QK_A)], axis=-1).astype(BF16),
        "w_main": ev_w_in[:, :, EVEN_HEAD:].astype(BF16),
        "w_uq": jnp.pad(uq, ((0, 0), (0, 0), (0, 0), (0, LANES - QK_A))).reshape(n_even, Q_RANK, H_A * LANES).astype(BF16),
        "w_uk": jnp.pad(uk, ((0, 0), (0, 0), (0, 0), (0, LANES - NOPE_A))).reshape(n_even, KV_RANK, H_A * LANES).astype(BF16),
        "w_uv": mla_w_uv.astype(BF16),
        "ev_w_out": ev_w_out.astype(BF16),
        "gq": mla_gq.reshape(n_even, 1, Q_RANK),
        "gkv": mla_gkv.reshape(n_even, 1, KV_RANK),
        "conv_w": conv_w,
        "conv_b": conv_b.reshape(n_even, 1, W_B),
        "od_w_in": od_w_in.astype(BF16),
        "od_w_out": od_w_out.astype(BF16),
        "sink": swa_sink,
        "gg": gmlp_g.reshape(n_odd, 1, W_D),
        "ws": gmlp_ws.reshape(n_odd, G_D // 2, 2 * CHUNK, CHUNK).astype(BF16),
        "bs": jnp.repeat(jnp.swapaxes(gmlp_bs, 1, 2), DG_D, axis=2),
    }
    ckr = jnp.pad(cache_mla_krope, ((0, 0), (0, 0), (0, 0), (NOPE_A, LANES - QK_A)))
    ckr = [ckr[:, i] for i in range(n_even)]
    swa_k = cache_swa_k.reshape(bs_, n_odd, n_past, KV_C * HD_C)
    swa_v = cache_swa_v.reshape(bs_, n_odd, n_past, KV_C * HD_C)

    fg = final_g.reshape(1, D_MODEL)
    xp, xs = x_prompt, x_sample
    ckv_l, kr_l, k_l, v_l = [], [], [], []
    for l in range(DEPTH):
        i = l // 2
        last = fg if l == DEPTH - 1 else None
        mod_p = _mod_spec(mod, l, bs_)
        mod_s = _mod_spec(mod, l, None)
        if l % 2 == 0:
            xp, ckv, kr = _even_layer(xp, mod_p, i, p, None, None, last, f"even{l}_ctx")
            ckv_l.append(ckv)
            kr_l.append(kr[:, :, NOPE_A:QK_A])
            (xs,) = _even_layer(xs, mod_s, i, p, (cache_mla_ckv, ckr), tab_a, last, f"even{l}_lat")
        else:
            xp, k_new, v_new = _odd_layer(xp, mod_p, i, p, None, None, last, f"odd{l}_ctx")
            k_l.append(k_new.reshape(bp, n_p, KV_C, HD_C))
            v_l.append(v_new.reshape(bp, n_p, KV_C, HD_C))
            (xs,) = _odd_layer(xs, mod_s, i, p, (swa_k, swa_v), tab_c, last, f"odd{l}_lat")

    return (xp, xs, jnp.stack(ckv_l, axis=1), jnp.stack(kr_l, axis=1), jnp.stack(k_l, axis=1), jnp.stack(v_l, axis=1))
```

```python
import functools

import numpy as np
import jax
import jax.numpy as jnp
from jax import lax
from jax.experimental import pallas as pl
from jax.experimental.pallas import tpu as pltpu

D_MODEL = 1024
DEPTH = 4
GRID_W = 64
ROPE_BASE = 10000.0
EPS = 1e-6
NEG_INF = -1e30

H_A = 8
NOPE_A = 64
ROPE_A = 32
V_A = 64
Q_RANK = 256
KV_RANK = 128
QK_A = NOPE_A + ROPE_A
W_A = H_A * V_A
W_B = 512
CONV_W = 3
H_C = 8
KV_C = 2
HD_C = 64
WINDOW = 128
W_C = H_C * HD_C
G_D = 8
W_D = 512
CHUNK = 128
DG_D = W_D // G_D

EVEN_SPLIT = (Q_RANK, KV_RANK, ROPE_A, W_A, W_B, W_B, W_B, W_B)
ODD_SPLIT = (W_C, KV_C * HD_C, KV_C * HD_C, W_C, W_D, W_D, W_D)
EVEN_HEAD = Q_RANK + KV_RANK + ROPE_A

LANES = 128
ROW_BLOCK = 256
CONV_PAD = 8
CTX_SEQS_PER_STEP = 4
VMEM_LIMIT = 60 * 1024 * 1024

F32 = jnp.float32
BF16 = jnp.bfloat16
LOG2E = float(np.log2(np.e))


def _dot(a, b):
    return jnp.dot(a, b, preferred_element_type=F32)


def _dot_t(a, b):
    return lax.dot_general(a, b, (((1,), (1,)), ((), ())), preferred_element_type=F32)


def _rms(x, g):
    return x * lax.rsqrt(jnp.mean(x * x, axis=-1, keepdims=True) + EPS) * g


def _silu(x):
    return x * (1.0 / (1.0 + jnp.exp(-x)))


def _gelu(x):
    c = np.float32(np.sqrt(2.0 / np.pi))
    return 0.5 * x * (1.0 + jnp.tanh(c * (x + 0.044715 * (x * x * x))))


def _lane(shape):
    return lax.broadcasted_iota(jnp.int32, shape, 1)


def _pair_select(lo, hi):
    return jnp.where(_lane(lo.shape) < LANES // 2, lo, hi)


def _rope(x, tab, quarter):
    fwd = pltpu.roll(x, LANES - quarter, 1)
    bwd = pltpu.roll(x, quarter, 1)
    partner = jnp.where((_lane(x.shape) & quarter) == 0, fwd, bwd)
    return x * tab[:, 0:LANES] + partner * tab[:, LANES:2 * LANES]


def _dup_halves(x):
    sw = pltpu.roll(x, LANES // 2, 1)
    lo = _lane(x.shape) < LANES // 2
    return jnp.where(lo, x, sw), jnp.where(lo, sw, x)


def _modulated_norm(x, mod_ref, ng_ref):
    shift = mod_ref[0, 0, 0:1, :]
    scale = mod_ref[0, 0, 1:2, :]
    return _rms(x, ng_ref[0]) * (1.0 + scale) + shift


def _pipelined(n, produce, consume):
    produce(0)
    outs = []
    for i in range(n):
        if i + 1 < n:
            produce(i + 1)
        outs.append(consume(i))
    return outs


def _normalised(o):
    return o[:, 0:LANES] * (1.0 / o[:, LANES:2 * LANES])


def _loop(n, body):
    if n == 1:
        body(0)
    else:
        def step(i, carry):
            body(i)
            return carry
        lax.fori_loop(0, n, step, 0)


def _row_loop(n_blocks, body):
    _loop(n_blocks, lambda i: body(i * ROW_BLOCK if isinstance(i, int) else pl.multiple_of(i * ROW_BLOCK, ROW_BLOCK)))


def _even_kernel(*refs, n, n_ctx, n_seq, final):
    rope = n_ctx > 0
    it = iter(refs)
    x_ref, mod_ref, ng_ref = next(it), next(it), next(it)
    wh_ref, wm_ref, wuq_ref, wuk_ref, wuv_ref, wout_ref = (next(it) for _ in range(6))
    gq_ref, gkv_ref, cw_ref, cb_ref = next(it), next(it), next(it), next(it)
    if rope:
        cckv_ref, ckr_ref, tab_ref = next(it), next(it), next(it)
    if final:
        fg_ref = next(it)
    if not rope:
        next(it), next(it)
    o_ref = next(it)
    if not rope:
        ckv_out, kr_out = next(it), next(it)
    h_s, q_s, ckv_s, kr_s, k_s, v_s, t_s, s_s = (next(it) for _ in range(8))

    nk = n_ctx + n
    q_scale = np.float32(QK_A ** -0.5 * LOG2E)

    zpad = jnp.zeros((CONV_PAD, W_B), F32)
    t_s[0:CONV_PAD, :] = zpad
    t_s[CONV_PAD + n:2 * CONV_PAD + n, :] = zpad

    def sequence(sq):
        if rope:
            ckv_s[0:n_ctx, :] = cckv_ref[0, 0].astype(BF16)
            kr_s[0:n_ctx, :] = ckr_ref[0].astype(BF16)

        def phase1(r0):
            rows = pl.ds(r0, ROW_BLOCK)
            hb = _modulated_norm(x_ref[sq, rows, :], mod_ref, ng_ref).astype(BF16)
            h_s[rows, :] = hb
            ph = _dot(hb, wh_ref[0])
            qn = _rms(ph[:, 0:Q_RANK], gq_ref[0]).astype(BF16)
            qq = _dot(qn, wuq_ref[0])
            ckv = _rms(ph[:, Q_RANK:Q_RANK + KV_RANK], gkv_ref[0])
            kr = ph[:, Q_RANK + KV_RANK:Q_RANK + KV_RANK + LANES]
            if rope:
                tab = tab_ref[rows, :]
                kr = _rope(kr, tab, ROPE_A // 4)
                for h in range(H_A):
                    qh = _rope(qq[:, h * LANES:(h + 1) * LANES], tab, ROPE_A // 4)
                    q_s[rows, h * LANES:(h + 1) * LANES] = (qh * q_scale).astype(BF16)
            else:
                q_s[rows, :] = (qq * q_scale).astype(BF16)
                ckv_out[sq, 0, rows, :] = ckv
                kr_out[sq, 0, rows, :] = kr[:, NOPE_A:QK_A]
            krows = pl.ds(n_ctx + r0, ROW_BLOCK)
            ckv_s[krows, :] = ckv.astype(BF16)
            kr_s[krows, :] = kr.astype(BF16)
            pc = _dot(hb, wm_ref[0, :, 2 * W_B:4 * W_B])
            t_s[pl.ds(CONV_PAD + r0, ROW_BLOCK), :] = pc[:, 0:W_B] * pc[:, W_B:2 * W_B]

        _row_loop(n // ROW_BLOCK, phase1)

        def phase2(r0):
            rows = pl.ds(r0, ROW_BLOCK)
            c = ckv_s[rows, :]
            kn = _dot(c, wuk_ref[0])
            krf = kr_s[rows, :].astype(F32)
            for h in range(H_A):
                k_s[rows, h * LANES:(h + 1) * LANES] = (kn[:, h * LANES:(h + 1) * LANES] + krf).astype(BF16)
            v = _dot(c, wuv_ref[0]).astype(BF16)
            ones = jnp.ones((ROW_BLOCK, LANES), BF16)
            for j in range(H_A // 2):
                v_s[rows, 2 * j * LANES:(2 * j + 1) * LANES] = v[:, j * LANES:(j + 1) * LANES]
                v_s[rows, (2 * j + 1) * LANES:(2 * j + 2) * LANES] = ones

        _row_loop(nk // ROW_BLOCK, phase2)

        def phase3(r0):
            rows = pl.ds(r0, ROW_BLOCK)
            hb = h_s[rows, :]

            def scores(h):
                s_s[h % 2] = _dot_t(q_s[rows, h * LANES:(h + 1) * LANES], k_s[:, h * LANES:(h + 1) * LANES])

            def attend(h):
                s = s_s[h % 2]
                e = jnp.exp2(s - jnp.max(s, axis=-1, keepdims=True)).astype(BF16)
                j = h // 2
                return _normalised(_dot(e, v_s[:, 2 * j * LANES:(2 * j + 2) * LANES]))

            heads = _pipelined(H_A, scores, attend)
            pairs = [_pair_select(heads[2 * j], heads[2 * j + 1]) for j in range(H_A // 2)]
            pa = _dot(hb, wm_ref[0, :, 0:2 * W_B])
            a_out = jnp.concatenate(pairs, axis=-1) * _silu(pa[:, 0:W_A])
            gate_b = pa[:, W_A:W_A + W_B]
            g_b = _dot(hb, wm_ref[0, :, 4 * W_B:5 * W_B])

            t_cur = t_s[pl.ds(CONV_PAD + r0, ROW_BLOCK), :]
            t_up = t_s[pl.ds(r0, CONV_PAD), :]
            t_dn = t_s[pl.ds(CONV_PAD + r0 + ROW_BLOCK, CONV_PAD), :]
            rid = lax.broadcasted_iota(jnp.int32, (ROW_BLOCK, W_B), 0)
            t_prev = jnp.where(rid == 0, t_up[CONV_PAD - 1:CONV_PAD, :], pltpu.roll(t_cur, 1, 0))
            t_next = jnp.where(rid == ROW_BLOCK - 1, t_dn[0:1, :], pltpu.roll(t_cur, ROW_BLOCK - 1, 0))
            conv = cb_ref[0] + t_prev * cw_ref[0, 0:1, :]
            conv = conv + t_cur * cw_ref[0, 1:2, :]
            conv = conv + t_next * cw_ref[0, 2:3, :]
            b_out = gate_b * conv * _silu(g_b)

            y = (_dot(a_out.astype(BF16), wout_ref[0, 0:W_A, :])
                 + _dot(b_out.astype(BF16), wout_ref[0, W_A:W_A + W_B, :]))
            res = x_ref[sq, rows, :] + mod_ref[0, 0, 2:3, :] * y
            if final:
                res = _rms(res, fg_ref[...])
            o_ref[sq, rows, :] = res

        _row_loop(n // ROW_BLOCK, phase3)

    _loop(n_seq, sequence)


def _odd_kernel(*refs, n, n_ctx, n_seq, final, layer):
    rope = n_ctx > 0
    it = iter(refs)
    x_ref, mod_ref, ng_ref = next(it), next(it), next(it)
    win_ref, wout_ref = next(it), next(it)
    sink_ref, gg_ref, ws_ref, bs_ref = next(it), next(it), next(it), next(it)
    if rope:
        kc_ref, vc_ref, tab_ref = next(it), next(it), next(it)
    if final:
        fg_ref = next(it)
    if not rope:
        next(it), next(it)
    o_ref = next(it)
    if not rope:
        k_out, v_out = next(it), next(it)
    h_s, kloc_s, vloc_s, s_s = next(it), next(it), next(it), next(it)
    if rope:
        kctx_s, vctx_s = next(it), next(it)

    pad = WINDOW if rope else 0
    kw, vw = LANES, 2 * LANES
    q0, k0, g0 = 0, W_C, W_C + 2 * KV_C * HD_C
    q_scale = np.float32(HD_C ** -0.5 * LOG2E)

    def store_keys(dst, rows, x):
        a, b = _dup_halves(x)
        dst[rows, 0:LANES] = a.astype(BF16)
        dst[rows, LANES:2 * LANES] = b.astype(BF16)

    def store_values(dst, rows, x):
        a, b = _dup_halves(x)
        ones = jnp.ones(x.shape, BF16)
        dst[rows, 0:LANES] = a.astype(BF16)
        dst[rows, LANES:2 * LANES] = ones
        dst[rows, 2 * LANES:3 * LANES] = b.astype(BF16)
        dst[rows, 3 * LANES:4 * LANES] = ones

    def sequence(sq):
        if rope:
            kloc_s[0:pad, :] = jnp.zeros((pad, KV_C * kw), BF16)
            kloc_s[pad + n:2 * pad + n, :] = jnp.zeros((pad, KV_C * kw), BF16)
            vloc_s[0:pad, :] = jnp.zeros((pad, KV_C * vw), BF16)
            vloc_s[pad + n:2 * pad + n, :] = jnp.zeros((pad, KV_C * vw), BF16)
            store_keys(kctx_s, slice(None), kc_ref[0, 0])
            store_values(vctx_s, slice(None), vc_ref[0, 0])

        def phase1(r0):
            rows = pl.ds(r0, ROW_BLOCK)
            hb = _modulated_norm(x_ref[sq, rows, :], mod_ref, ng_ref).astype(BF16)
            h_s[rows, :] = hb
            p1 = _dot(hb, win_ref[0, :, k0:g0])
            k = p1[:, 0:LANES]
            v = p1[:, LANES:2 * LANES]
            if rope:
                k = _rope(k, tab_ref[rows, :], HD_C // 4)
            else:
                k_out[sq, 0, rows, :] = k
                v_out[sq, 0, rows, :] = v
            lrows = pl.ds(pad + r0, ROW_BLOCK)
            store_keys(kloc_s, lrows, k)
            store_values(vloc_s, lrows, v)

        _row_loop(n // ROW_BLOCK, phase1)

        def phase3(r0):
            rows = pl.ds(r0, ROW_BLOCK)
            hb = h_s[rows, :]
            pq = _dot(hb, win_ref[0, :, q0:k0])

            if rope:
                tab = tab_ref[rows, :]
                span = ROW_BLOCK + 2 * WINDOW
                kl = kloc_s[pl.ds(r0, span), :]
                vl = vloc_s[pl.ds(r0, span), :]
                ii = lax.broadcasted_iota(jnp.int32, (ROW_BLOCK, span), 0)
                jj = lax.broadcasted_iota(jnp.int32, (ROW_BLOCK, span), 1)
                kpos = jj + (r0 - WINDOW)
                dlt = jj - ii
                mask = (dlt >= 0) & (dlt <= 2 * WINDOW) & (kpos >= 0) & (kpos < n)
            else:
                kl = kloc_s[...]
                vl = vloc_s[...]

            qhs = []
            for m in range(H_C // 2):
                qb = pq[:, m * LANES:(m + 1) * LANES]
                if rope:
                    qb = _rope(qb, tab, HD_C // 4)
                qb = qb * q_scale
                lo = _lane(qb.shape) < LANES // 2
                qhs.append(jnp.where(lo, qb, 0.0).astype(BF16))
                qhs.append(jnp.where(lo, 0.0, qb).astype(BF16))

            def kv_head(h):
                return h // (H_C // KV_C)

            def scores(h):
                j = kv_head(h)
                if rope:
                    s_s[h % 2, :, 0:n_ctx] = _dot_t(qhs[h], kctx_s[:, j * kw:(j + 1) * kw])
                    s_s[h % 2, :, n_ctx:] = _dot_t(qhs[h], kl[:, j * kw:(j + 1) * kw])
                else:
                    s_s[h % 2] = _dot_t(qhs[h], kl[:, j * kw:(j + 1) * kw])

            def attend(h):
                j = kv_head(h)
                sk = sink_ref[layer, h] * np.float32(LOG2E)
                if rope:
                    sc = s_s[h % 2, :, 0:n_ctx]
                    sl = jnp.where(mask, s_s[h % 2, :, n_ctx:], NEG_INF)
                    mx = jnp.maximum(jnp.max(sl, axis=-1, keepdims=True), jnp.max(sc, axis=-1, keepdims=True))
                else:
                    sl = s_s[h % 2]
                    mx = jnp.max(sl, axis=-1, keepdims=True)
                mx = jnp.maximum(mx, sk)
                o = _dot(jnp.exp2(sl - mx).astype(BF16), vl[:, j * vw:(j + 1) * vw])
                if rope:
                    o = o + _dot(jnp.exp2(sc - mx).astype(BF16), vctx_s[:, j * vw:(j + 1) * vw])
                return o[:, 0:LANES] * (1.0 / (o[:, LANES:2 * LANES] + jnp.exp2(sk - mx)))

            heads = _pipelined(H_C, scores, attend)
            blocks = [_pair_select(heads[2 * m], heads[2 * m + 1]) for m in range(H_C // 2)]
            pg = _dot(hb, win_ref[0, :, g0:g0 + W_C + 3 * W_D])
            c_out = jnp.concatenate(blocks, axis=-1) * _silu(pg[:, 0:W_C])
            u = pg[:, W_C:W_C + W_D]
            vv = pg[:, W_C + W_D:W_C + 2 * W_D]
            g_d = pg[:, W_C + 2 * W_D:W_C + 3 * W_D]

            vvn = _rms(_gelu(vv), gg_ref[0]).astype(BF16)
            chunks = []
            for c in range(ROW_BLOCK // CHUNK):
                cols = []
                for jp in range(G_D // 2):
                    vp = vvn[c * CHUNK:(c + 1) * CHUNK, jp * LANES:(jp + 1) * LANES]
                    o = _dot(ws_ref[0, jp], vp)
                    cols.append(_pair_select(o[0:CHUNK, :], o[CHUNK:2 * CHUNK, :]))
                chunks.append(jnp.concatenate(cols, axis=-1) + bs_ref[0])
            mixed = jnp.concatenate(chunks, axis=0)
            d_out = _gelu(u) * mixed * _silu(g_d)

            y = (_dot(c_out.astype(BF16), wout_ref[0, 0:W_C, :])
                 + _dot(d_out.astype(BF16), wout_ref[0, W_C:W_C + W_D, :]))
            res = x_ref[sq, rows, :] + mod_ref[0, 0, 2:3, :] * y
            if final:
                res = _rms(res, fg_ref[...])
            o_ref[sq, rows, :] = res

        _row_loop(n // ROW_BLOCK, phase3)

    _loop(n_seq, sequence)


def _mod_kernel(c_ref, w_ref, b_ref, o_ref):
    a = _silu(c_ref[...]).astype(BF16)
    o_ref[0] = _dot(a, w_ref[0].astype(BF16)) + b_ref[0]


def _modulation(cvec, ada_w, ada_b):
    rows = cvec.shape[0]
    tn = D_MODEL
    return pl.pallas_call(
        _mod_kernel,
        grid=(DEPTH, 3 * D_MODEL // tn),
        in_specs=[
            pl.BlockSpec((rows, D_MODEL), lambda l, j: (0, 0)),
            pl.BlockSpec((1, D_MODEL, tn), lambda l, j: (l, 0, j)),
            pl.BlockSpec((1, 1, tn), lambda l, j: (l, 0, j)),
        ],
        out_specs=pl.BlockSpec((1, rows, tn), lambda l, j: (l, 0, j)),
        out_shape=jax.ShapeDtypeStruct((DEPTH, rows, 3 * D_MODEL), F32),
        compiler_params=pltpu.CompilerParams(dimension_semantics=("arbitrary", "arbitrary")),
        name="modulation",
    )(cvec, ada_w, ada_b.reshape(DEPTH, 1, 3 * D_MODEL))


def _split_even_kernel(w_ref, head_ref, main_ref):
    x = w_ref[0]
    kv_end = Q_RANK + KV_RANK
    main_ref[0] = x[:, EVEN_HEAD:].astype(BF16)
    head_ref[0, :, 0:kv_end] = x[:, 0:kv_end].astype(BF16)
    blk = x[:, kv_end:kv_end + LANES]
    lane = _lane(blk.shape)
    moved = pltpu.roll(blk, NOPE_A, 1)
    head_ref[0, :, kv_end:kv_end + LANES] = jnp.where((lane >= NOPE_A) & (lane < QK_A), moved, 0.0).astype(BF16)


def _split_even(w_in):
    n_layers, d, cols = w_in.shape
    n_main = cols - EVEN_HEAD
    return pl.pallas_call(
        _split_even_kernel,
        grid=(n_layers, d // ROW_BLOCK),
        in_specs=[pl.BlockSpec((1, ROW_BLOCK, cols), lambda l, r: (l, r, 0))],
        out_specs=[pl.BlockSpec((1, ROW_BLOCK, Q_RANK + KV_RANK + LANES), lambda l, r: (l, r, 0)),
                   pl.BlockSpec((1, ROW_BLOCK, n_main), lambda l, r: (l, r, 0))],
        out_shape=[jax.ShapeDtypeStruct((n_layers, d, Q_RANK + KV_RANK + LANES), BF16),
                   jax.ShapeDtypeStruct((n_layers, d, n_main), BF16)],
        compiler_params=pltpu.CompilerParams(dimension_semantics=("arbitrary", "arbitrary")),
        name="split_even_w_in",
    )(w_in)


def _full(a):
    nd = a.ndim
    return a, pl.BlockSpec(a.shape, lambda b, _nd=nd: (0,) * _nd)


def _per_step(a, n_seq):
    nd = a.ndim
    return a, pl.BlockSpec((n_seq,) + a.shape[1:], lambda b, _nd=nd: (b,) + (0,) * (_nd - 1))


def _of_layer(a, i):
    nd = a.ndim
    return a, pl.BlockSpec((1,) + a.shape[1:], lambda b, _nd=nd, _i=i: (_i,) + (0,) * (_nd - 1))


def _seq_of_layer(a, i):
    nd = a.ndim
    return a, pl.BlockSpec((1, 1) + a.shape[2:], lambda b, _nd=nd, _i=i: (b, _i) + (0,) * (_nd - 2))


def _mod_spec(mod, l, row):
    if row is None:
        return mod, pl.BlockSpec((1, 1, 3, D_MODEL), lambda b, _l=l: (_l, b, 0, 0))
    return mod, pl.BlockSpec((1, 1, 3, D_MODEL), lambda b, _l=l, _r=row: (_l, _r, 0, 0))


def _rope_cs(n, rot_dim):
    rows = n // GRID_W
    r = np.repeat(np.arange(rows, dtype=np.float32), GRID_W)
    col = np.tile(np.arange(GRID_W, dtype=np.float32), rows)
    nf = rot_dim // 4
    inv = np.float32(ROPE_BASE) ** (-np.arange(nf, dtype=np.float32) / np.float32(nf))
    ar = r[:, None] * inv
    ac = col[:, None] * inv
    cos = np.concatenate([np.cos(ar), np.cos(ar), np.cos(ac), np.cos(ac)], axis=-1)
    sin = np.concatenate([-np.sin(ar), np.sin(ar), -np.sin(ac), np.sin(ac)], axis=-1)
    return cos.astype(np.float32), sin.astype(np.float32)


def _layer_call(body, x, n_seq, operands, caches, slot, scratch, name):
    b, n, _ = x.shape
    in_specs = [spec for _, spec in operands] + [pl.BlockSpec(memory_space=pl.ANY) for _ in caches]
    args = [a for a, _ in operands] + list(caches)
    out_shape = [jax.ShapeDtypeStruct(x.shape, F32)] + [jax.ShapeDtypeStruct(c.shape, c.dtype) for c in caches]
    out_specs = [pl.BlockSpec((n_seq, n, D_MODEL), lambda i: (i, 0, 0))] + [
        pl.BlockSpec((n_seq, 1, n, c.shape[-1]), lambda i, _s=slot: (i, _s, 0, 0)) for c in caches]
    return pl.pallas_call(
        body,
        grid=(b // n_seq,),
        in_specs=in_specs,
        out_specs=out_specs,
        out_shape=out_shape,
        input_output_aliases={len(operands) + k: 1 + k for k in range(len(caches))},
        scratch_shapes=scratch,
        compiler_params=pltpu.CompilerParams(dimension_semantics=("arbitrary",), vmem_limit_bytes=VMEM_LIMIT),
        name=name,
    )(*args)


def _seqs_per_step(b, ctx):
    return 1 if ctx is not None else int(np.gcd(b, CTX_SEQS_PER_STEP))


def _even_layer(x, mod_op, i, p, ctx, tab, caches, final_g, name):
    b, n, _ = x.shape
    n_ctx = 0 if ctx is None else ctx[0].shape[2]
    n_seq = _seqs_per_step(b, ctx)
    ops = [_per_step(x, n_seq), mod_op, _of_layer(p["norm_g"], 2 * i),
           _of_layer(p["w_head"], i), _of_layer(p["w_main"], i), _of_layer(p["w_uq"], i), _of_layer(p["w_uk"], i),
           _of_layer(p["w_uv"], i), _of_layer(p["ev_w_out"], i),
           _of_layer(p["gq"], i), _of_layer(p["gkv"], i), _of_layer(p["conv_w"], i), _of_layer(p["conv_b"], i)]
    if ctx is not None:
        ops += [_seq_of_layer(ctx[0], i), _per_step(ctx[1][i], 1), _full(tab)]
    if final_g is not None:
        ops.append(_full(final_g))
    nk = n_ctx + n
    scratch = [pltpu.VMEM((n, D_MODEL), BF16), pltpu.VMEM((n, H_A * LANES), BF16),
               pltpu.VMEM((nk, LANES), BF16), pltpu.VMEM((nk, LANES), BF16),
               pltpu.VMEM((nk, H_A * LANES), BF16), pltpu.VMEM((nk, 2 * W_A), BF16),
               pltpu.VMEM((n + 2 * CONV_PAD, W_B), F32), pltpu.VMEM((2, ROW_BLOCK, nk), F32)]
    body = functools.partial(_even_kernel, n=n, n_ctx=n_ctx, n_seq=n_seq, final=final_g is not None)
    return _layer_call(body, x, n_seq, ops, caches, i, scratch, name)


def _odd_layer(x, mod_op, i, p, ctx, tab, caches, final_g, name):
    b, n, _ = x.shape
    n_ctx = 0 if ctx is None else ctx[0].shape[2]
    n_seq = _seqs_per_step(b, ctx)
    ops = [_per_step(x, n_seq), mod_op, _of_layer(p["norm_g"], 2 * i + 1),
           _of_layer(p["od_w_in"], i), _of_layer(p["od_w_out"], i),
           (p["sink"], pl.BlockSpec(memory_space=pltpu.SMEM)), _of_layer(p["gg"], i), _of_layer(p["ws"], i),
           _of_layer(p["bs"], i)]
    if ctx is not None:
        ops += [_seq_of_layer(ctx[0], i), _seq_of_layer(ctx[1], i), _full(tab)]
    if final_g is not None:
        ops.append(_full(final_g))
    pad = WINDOW if ctx is not None else 0
    n_keys = n_ctx + ROW_BLOCK + 2 * pad if ctx is not None else n
    scratch = [pltpu.VMEM((n, D_MODEL), BF16), pltpu.VMEM((n + 2 * pad, KV_C * LANES), BF16),
               pltpu.VMEM((n + 2 * pad, KV_C * 2 * LANES), BF16), pltpu.VMEM((2, ROW_BLOCK, n_keys), F32)]
    if ctx is not None:
        scratch += [pltpu.VMEM((n_ctx, KV_C * LANES), BF16), pltpu.VMEM((n_ctx, KV_C * 2 * LANES), BF16)]
    body = functools.partial(_odd_kernel, n=n, n_ctx=n_ctx, n_seq=n_seq, final=final_g is not None, layer=i)
    return _layer_call(body, x, n_seq, ops, caches, i, scratch, name)


def kernel(x_prompt, x_sample, cache_mla_ckv, cache_mla_krope, cache_swa_k, cache_swa_v, c, c_ctx, ada_w, ada_b, norm_g, ev_w_in, ev_w_out, mla_gq, mla_gkv, mla_w_uq, mla_w_uk, mla_w_uv, conv_w, conv_b, od_w_in, od_w_out, swa_sink, gmlp_g, gmlp_ws, gmlp_bs, final_g):
    bp, n_p, _ = x_prompt.shape
    bs_, n_s, _ = x_sample.shape
    n_past = cache_mla_ckv.shape[2]
    n_even, n_odd = ev_w_in.shape[0], od_w_in.shape[0]

    n_rows = -(-(bs_ + 1) // 8) * 8
    cvec = jnp.concatenate([c, c_ctx[None, :], jnp.zeros((n_rows - bs_ - 1, D_MODEL), F32)], axis=0)
    mod = _modulation(cvec, ada_w, ada_b).reshape(DEPTH, n_rows, 3, D_MODEL)

    cos_a, sin_a = _rope_cs(n_s, ROPE_A)
    cos_c, sin_c = _rope_cs(n_s, HD_C)
    zero = lambda w: np.zeros((n_s, w), np.float32)
    tab_a = jnp.asarray(np.concatenate([np.ones((n_s, NOPE_A), np.float32), cos_a, zero(LANES - QK_A),
                                        zero(NOPE_A), sin_a, zero(LANES - QK_A)], axis=1))
    tab_c = jnp.asarray(np.concatenate([cos_c, cos_c, sin_c, sin_c], axis=1))

    uq = mla_w_uq.reshape(n_even, Q_RANK, H_A, QK_A)
    uk = mla_w_uk.reshape(n_even, KV_RANK, H_A, NOPE_A)
    w_head, w_main = _split_even(ev_w_in)
    p = {
        "norm_g": norm_g.reshape(DEPTH, 1, D_MODEL),
        "w_head": w_head,
        "w_main": w_main,
        "w_uq": jnp.pad(uq, ((0, 0), (0, 0), (0, 0), (0, LANES - QK_A))).reshape(n_even, Q_RANK, H_A * LANES).astype(BF16),
        "w_uk": jnp.pad(uk, ((0, 0), (0, 0), (0, 0), (0, LANES - NOPE_A))).reshape(n_even, KV_RANK, H_A * LANES).astype(BF16),
        "w_uv": mla_w_uv.astype(BF16),
        "ev_w_out": ev_w_out.astype(BF16),
        "gq": mla_gq.reshape(n_even, 1, Q_RANK),
        "gkv": mla_gkv.reshape(n_even, 1, KV_RANK),
        "conv_w": conv_w,
        "conv_b": conv_b.reshape(n_even, 1, W_B),
        "od_w_in": od_w_in.astype(BF16),
        "od_w_out": od_w_out.astype(BF16),
        "sink": swa_sink,
        "gg": gmlp_g.reshape(n_odd, 1, W_D),
        "ws": gmlp_ws.reshape(n_odd, G_D // 2, 2 * CHUNK, CHUNK).astype(BF16),
        "bs": jnp.repeat(jnp.swapaxes(gmlp_bs, 1, 2), DG_D, axis=2),
    }
    ckr = jnp.pad(cache_mla_krope, ((0, 0), (0, 0), (0, 0), (NOPE_A, LANES - QK_A)))
    ckr = [ckr[:, i] for i in range(n_even)]
    swa_k = cache_swa_k.reshape(bs_, n_odd, n_past, KV_C * HD_C)
    swa_v = cache_swa_v.reshape(bs_, n_odd, n_past, KV_C * HD_C)

    new_ckv = jnp.zeros((bp, n_even, n_p, KV_RANK), F32)
    new_kr = jnp.zeros((bp, n_even, n_p, ROPE_A), F32)
    new_k = jnp.zeros((bp, n_odd, n_p, KV_C * HD_C), F32)
    new_v = jnp.zeros((bp, n_odd, n_p, KV_C * HD_C), F32)

    fg = final_g.reshape(1, D_MODEL)
    xp, xs = x_prompt, x_sample
    for l in range(DEPTH):
        i = l // 2
        last = fg if l == DEPTH - 1 else None
        mod_p = _mod_spec(mod, l, bs_)
        mod_s = _mod_spec(mod, l, None)
        if l % 2 == 0:
            xp, new_ckv, new_kr = _even_layer(xp, mod_p, i, p, None, None, (new_ckv, new_kr), last, f"even{l}_ctx")
            (xs,) = _even_layer(xs, mod_s, i, p, (cache_mla_ckv, ckr), tab_a, (), last, f"even{l}_lat")
        else:
            xp, new_k, new_v = _odd_layer(xp, mod_p, i, p, None, None, (new_k, new_v), last, f"odd{l}_ctx")
            (xs,) = _odd_layer(xs, mod_s, i, p, (swa_k, swa_v), tab_c, (), last, f"odd{l}_lat")

    return (xp, xs, new_ckv, new_kr, new_k.reshape(bp, n_odd, n_p, KV_C, HD_C),
            new_v.reshape(bp, n_odd, n_p, KV_C, HD_C))
```

```python
import functools

import numpy as np
import jax
import jax.numpy as jnp
from jax import lax
from jax.experimental import pallas as pl
from jax.experimental.pallas import tpu as pltpu

D_MODEL = 1024
DEPTH = 4
GRID_W = 64
ROPE_BASE = 10000.0
EPS = 1e-6
NEG_INF = -1e30

H_A = 8
NOPE_A = 64
ROPE_A = 32
V_A = 64
Q_RANK = 256
KV_RANK = 128
QK_A = NOPE_A + ROPE_A
W_A = H_A * V_A
W_B = 512
CONV_W = 3
H_C = 8
KV_C = 2
HD_C = 64
WINDOW = 128
W_C = H_C * HD_C
G_D = 8
W_D = 512
CHUNK = 128
DG_D = W_D // G_D

EVEN_SPLIT = (Q_RANK, KV_RANK, ROPE_A, W_A, W_B, W_B, W_B, W_B)
ODD_SPLIT = (W_C, KV_C * HD_C, KV_C * HD_C, W_C, W_D, W_D, W_D)
EVEN_HEAD = Q_RANK + KV_RANK + ROPE_A

LANES = 128
ROW_BLOCK = 256
CONV_PAD = 8
CTX_SEQS_PER_STEP = 4
VMEM_LIMIT = 60 * 1024 * 1024

F32 = jnp.float32
BF16 = jnp.bfloat16
LOG2E = float(np.log2(np.e))


def _dot(a, b):
    return jnp.dot(a, b, preferred_element_type=F32)


def _dot_t(a, b):
    return lax.dot_general(a, b, (((1,), (1,)), ((), ())), preferred_element_type=F32)


def _rms(x, g):
    return x * lax.rsqrt(jnp.mean(x * x, axis=-1, keepdims=True) + EPS) * g


def _silu(x):
    return x * (1.0 / (1.0 + jnp.exp(-x)))


def _gelu(x):
    c = np.float32(np.sqrt(2.0 / np.pi))
    return 0.5 * x * (1.0 + jnp.tanh(c * (x + 0.044715 * (x * x * x))))


def _lane(shape):
    return lax.broadcasted_iota(jnp.int32, shape, 1)


def _pair_select(lo, hi):
    return jnp.where(_lane(lo.shape) < LANES // 2, lo, hi)


def _rope(x, tab, quarter):
    fwd = pltpu.roll(x, LANES - quarter, 1)
    bwd = pltpu.roll(x, quarter, 1)
    partner = jnp.where((_lane(x.shape) & quarter) == 0, fwd, bwd)
    return x * tab[:, 0:LANES] + partner * tab[:, LANES:2 * LANES]


def _dup_halves(x):
    sw = pltpu.roll(x, LANES // 2, 1)
    lo = _lane(x.shape) < LANES // 2
    return jnp.where(lo, x, sw), jnp.where(lo, sw, x)


def _modulated_norm(x, mod_ref, ng_ref):
    shift = mod_ref[0, 0, 0:1, :]
    scale = mod_ref[0, 0, 1:2, :]
    return _rms(x, ng_ref[0]) * (1.0 + scale) + shift


def _pipelined(n, produce, consume):
    produce(0)
    outs = []
    for i in range(n):
        if i + 1 < n:
            produce(i + 1)
        outs.append(consume(i))
    return outs


def _normalised(o):
    return o[:, 0:LANES] * (1.0 / o[:, LANES:2 * LANES])


def _loop(n, body):
    if n == 1:
        body(0)
    else:
        def step(i, carry):
            body(i)
            return carry
        lax.fori_loop(0, n, step, 0)


def _row_loop(n_blocks, body):
    _loop(n_blocks, lambda i: body(i * ROW_BLOCK if isinstance(i, int) else pl.multiple_of(i * ROW_BLOCK, ROW_BLOCK)))


def _even_kernel(*refs, n, n_ctx, n_seq, final, fresh):
    rope = n_ctx > 0
    it = iter(refs)
    x_ref, mod_ref, ng_ref = next(it), next(it), next(it)
    wh_ref, wm_ref, wuq_ref, wuk_ref, wuv_ref, wout_ref = (next(it) for _ in range(6))
    gq_ref, gkv_ref, cw_ref, cb_ref = next(it), next(it), next(it), next(it)
    if rope:
        cckv_ref, ckr_ref, tab_ref = next(it), next(it), next(it)
    if final:
        fg_ref = next(it)
    if not rope and not fresh:
        next(it), next(it)
    o_ref = next(it)
    if not rope:
        ckv_out, kr_out = next(it), next(it)
    h_s, q_s, ckv_s, kr_s, k_s, v_s, t_s, s_s = (next(it) for _ in range(8))

    nk = n_ctx + n
    q_scale = np.float32(QK_A ** -0.5 * LOG2E)

    zpad = jnp.zeros((CONV_PAD, W_B), F32)
    t_s[0:CONV_PAD, :] = zpad
    t_s[CONV_PAD + n:2 * CONV_PAD + n, :] = zpad

    def sequence(sq):
        if rope:
            ckv_s[0:n_ctx, :] = cckv_ref[0, 0].astype(BF16)
            kr_s[0:n_ctx, :] = ckr_ref[0].astype(BF16)

        def phase1(r0):
            rows = pl.ds(r0, ROW_BLOCK)
            hb = _modulated_norm(x_ref[sq, rows, :], mod_ref, ng_ref).astype(BF16)
            h_s[rows, :] = hb
            ph = _dot(hb, wh_ref[0])
            qn = _rms(ph[:, 0:Q_RANK], gq_ref[0]).astype(BF16)
            qq = _dot(qn, wuq_ref[0])
            ckv = _rms(ph[:, Q_RANK:Q_RANK + KV_RANK], gkv_ref[0])
            kr = ph[:, Q_RANK + KV_RANK:Q_RANK + KV_RANK + LANES]
            if rope:
                tab = tab_ref[rows, :]
                kr = _rope(kr, tab, ROPE_A // 4)
                for h in range(H_A):
                    qh = _rope(qq[:, h * LANES:(h + 1) * LANES], tab, ROPE_A // 4)
                    q_s[rows, h * LANES:(h + 1) * LANES] = (qh * q_scale).astype(BF16)
            else:
                q_s[rows, :] = (qq * q_scale).astype(BF16)
                _store_cache(ckv_out, sq, ckv)
                _store_cache(kr_out, sq, kr.T[NOPE_A:QK_A, :])
            krows = pl.ds(n_ctx + r0, ROW_BLOCK)
            ckv_s[krows, :] = ckv.astype(BF16)
            kr_s[krows, :] = kr.astype(BF16)
            pc = _dot(hb, wm_ref[0, :, 2 * W_B:4 * W_B])
            t_s[pl.ds(CONV_PAD + r0, ROW_BLOCK), :] = pc[:, 0:W_B] * pc[:, W_B:2 * W_B]

        _row_loop(n // ROW_BLOCK, phase1)

        def phase2(r0):
            rows = pl.ds(r0, ROW_BLOCK)
            c = ckv_s[rows, :]
            kn = _dot(c, wuk_ref[0])
            krf = kr_s[rows, :].astype(F32)
            for h in range(H_A):
                k_s[rows, h * LANES:(h + 1) * LANES] = (kn[:, h * LANES:(h + 1) * LANES] + krf).astype(BF16)
            v = _dot(c, wuv_ref[0]).astype(BF16)
            ones = jnp.ones((ROW_BLOCK, LANES), BF16)
            for j in range(H_A // 2):
                v_s[rows, 2 * j * LANES:(2 * j + 1) * LANES] = v[:, j * LANES:(j + 1) * LANES]
                v_s[rows, (2 * j + 1) * LANES:(2 * j + 2) * LANES] = ones

        _row_loop(nk // ROW_BLOCK, phase2)

        def phase3(r0):
            rows = pl.ds(r0, ROW_BLOCK)
            hb = h_s[rows, :]

            def scores(h):
                s_s[h % 2] = _dot_t(q_s[rows, h * LANES:(h + 1) * LANES], k_s[:, h * LANES:(h + 1) * LANES])

            def attend(h):
                s = s_s[h % 2]
                e = jnp.exp2(s - jnp.max(s, axis=-1, keepdims=True)).astype(BF16)
                j = h // 2
                return _normalised(_dot(e, v_s[:, 2 * j * LANES:(2 * j + 2) * LANES]))

            heads = _pipelined(H_A, scores, attend)
            pairs = [_pair_select(heads[2 * j], heads[2 * j + 1]) for j in range(H_A // 2)]
            pa = _dot(hb, wm_ref[0, :, 0:2 * W_B])
            a_out = jnp.concatenate(pairs, axis=-1) * _silu(pa[:, 0:W_A])
            gate_b = pa[:, W_A:W_A + W_B]
            g_b = _dot(hb, wm_ref[0, :, 4 * W_B:5 * W_B])

            t_cur = t_s[pl.ds(CONV_PAD + r0, ROW_BLOCK), :]
            t_up = t_s[pl.ds(r0, CONV_PAD), :]
            t_dn = t_s[pl.ds(CONV_PAD + r0 + ROW_BLOCK, CONV_PAD), :]
            rid = lax.broadcasted_iota(jnp.int32, (ROW_BLOCK, W_B), 0)
            t_prev = jnp.where(rid == 0, t_up[CONV_PAD - 1:CONV_PAD, :], pltpu.roll(t_cur, 1, 0))
            t_next = jnp.where(rid == ROW_BLOCK - 1, t_dn[0:1, :], pltpu.roll(t_cur, ROW_BLOCK - 1, 0))
            conv = cb_ref[0] + t_prev * cw_ref[0, 0:1, :]
            conv = conv + t_cur * cw_ref[0, 1:2, :]
            conv = conv + t_next * cw_ref[0, 2:3, :]
            b_out = gate_b * conv * _silu(g_b)

            y = (_dot(a_out.astype(BF16), wout_ref[0, 0:W_A, :])
                 + _dot(b_out.astype(BF16), wout_ref[0, W_A:W_A + W_B, :]))
            res = x_ref[sq, rows, :] + mod_ref[0, 0, 2:3, :] * y
            if final:
                res = _rms(res, fg_ref[...])
            o_ref[sq, rows, :] = res

        _row_loop(n // ROW_BLOCK, phase3)

    _loop(n_seq, sequence)


def _odd_kernel(*refs, n, n_ctx, n_seq, final, fresh, layer):
    rope = n_ctx > 0
    it = iter(refs)
    x_ref, mod_ref, ng_ref = next(it), next(it), next(it)
    win_ref, wout_ref = next(it), next(it)
    sink_ref, gg_ref, ws_ref, bs_ref = next(it), next(it), next(it), next(it)
    if rope:
        kc_ref, vc_ref, tab_ref = next(it), next(it), next(it)
    if final:
        fg_ref = next(it)
    if not rope and not fresh:
        next(it), next(it)
    o_ref = next(it)
    if not rope:
        k_out, v_out = next(it), next(it)
    h_s, kloc_s, vloc_s, s_s = next(it), next(it), next(it), next(it)
    if rope:
        kctx_s, vctx_s = next(it), next(it)

    pad = WINDOW if rope else 0
    kw, vw = LANES, 2 * LANES
    q0, k0, g0 = 0, W_C, W_C + 2 * KV_C * HD_C
    q_scale = np.float32(HD_C ** -0.5 * LOG2E)

    def store_keys(dst, rows, x):
        a, b = _dup_halves(x)
        dst[rows, 0:LANES] = a.astype(BF16)
        dst[rows, LANES:2 * LANES] = b.astype(BF16)

    def store_values(dst, rows, x):
        a, b = _dup_halves(x)
        ones = jnp.ones(x.shape, BF16)
        dst[rows, 0:LANES] = a.astype(BF16)
        dst[rows, LANES:2 * LANES] = ones
        dst[rows, 2 * LANES:3 * LANES] = b.astype(BF16)
        dst[rows, 3 * LANES:4 * LANES] = ones

    def sequence(sq):
        if rope:
            kloc_s[0:pad, :] = jnp.zeros((pad, KV_C * kw), BF16)
            kloc_s[pad + n:2 * pad + n, :] = jnp.zeros((pad, KV_C * kw), BF16)
            vloc_s[0:pad, :] = jnp.zeros((pad, KV_C * vw), BF16)
            vloc_s[pad + n:2 * pad + n, :] = jnp.zeros((pad, KV_C * vw), BF16)
            store_keys(kctx_s, slice(None), kc_ref[0, 0].T)
            store_values(vctx_s, slice(None), vc_ref[0, 0].T)

        def phase1(r0):
            rows = pl.ds(r0, ROW_BLOCK)
            hb = _modulated_norm(x_ref[sq, rows, :], mod_ref, ng_ref).astype(BF16)
            h_s[rows, :] = hb
            p1 = _dot(hb, win_ref[0, :, k0:g0])
            k = p1[:, 0:LANES]
            v = p1[:, LANES:2 * LANES]
            if rope:
                k = _rope(k, tab_ref[rows, :], HD_C // 4)
            else:
                _store_cache(k_out, sq, k.T)
                _store_cache(v_out, sq, v.T)
            lrows = pl.ds(pad + r0, ROW_BLOCK)
            store_keys(kloc_s, lrows, k)
            store_values(vloc_s, lrows, v)

        _row_loop(n // ROW_BLOCK, phase1)

        def phase3(r0):
            rows = pl.ds(r0, ROW_BLOCK)
            hb = h_s[rows, :]
            pq = _dot(hb, win_ref[0, :, q0:k0])

            if rope:
                tab = tab_ref[rows, :]
                span = ROW_BLOCK + 2 * WINDOW
                kl = kloc_s[pl.ds(r0, span), :]
                vl = vloc_s[pl.ds(r0, span), :]
                ii = lax.broadcasted_iota(jnp.int32, (ROW_BLOCK, span), 0)
                jj = lax.broadcasted_iota(jnp.int32, (ROW_BLOCK, span), 1)
                kpos = jj + (r0 - WINDOW)
                dlt = jj - ii
                mask = (dlt >= 0) & (dlt <= 2 * WINDOW) & (kpos >= 0) & (kpos < n)
            else:
                kl = kloc_s[...]
                vl = vloc_s[...]

            qhs = []
            for m in range(H_C // 2):
                qb = pq[:, m * LANES:(m + 1) * LANES]
                if rope:
                    qb = _rope(qb, tab, HD_C // 4)
                qb = qb * q_scale
                lo = _lane(qb.shape) < LANES // 2
                qhs.append(jnp.where(lo, qb, 0.0).astype(BF16))
                qhs.append(jnp.where(lo, 0.0, qb).astype(BF16))

            def kv_head(h):
                return h // (H_C // KV_C)

            def scores(h):
                j = kv_head(h)
                if rope:
                    s_s[h % 2, :, 0:n_ctx] = _dot_t(qhs[h], kctx_s[:, j * kw:(j + 1) * kw])
                    s_s[h % 2, :, n_ctx:] = _dot_t(qhs[h], kl[:, j * kw:(j + 1) * kw])
                else:
                    s_s[h % 2] = _dot_t(qhs[h], kl[:, j * kw:(j + 1) * kw])

            def attend(h):
                j = kv_head(h)
                sk = sink_ref[layer, h] * np.float32(LOG2E)
                if rope:
                    sc = s_s[h % 2, :, 0:n_ctx]
                    sl = jnp.where(mask, s_s[h % 2, :, n_ctx:], NEG_INF)
                    mx = jnp.maximum(jnp.max(sl, axis=-1, keepdims=True), jnp.max(sc, axis=-1, keepdims=True))
                else:
                    sl = s_s[h % 2]
                    mx = jnp.max(sl, axis=-1, keepdims=True)
                mx = jnp.maximum(mx, sk)
                o = _dot(jnp.exp2(sl - mx).astype(BF16), vl[:, j * vw:(j + 1) * vw])
                if rope:
                    o = o + _dot(jnp.exp2(sc - mx).astype(BF16), vctx_s[:, j * vw:(j + 1) * vw])
                return o[:, 0:LANES] * (1.0 / (o[:, LANES:2 * LANES] + jnp.exp2(sk - mx)))

            heads = _pipelined(H_C, scores, attend)
            blocks = [_pair_select(heads[2 * m], heads[2 * m + 1]) for m in range(H_C // 2)]
            pg = _dot(hb, win_ref[0, :, g0:g0 + W_C + 3 * W_D])
            c_out = jnp.concatenate(blocks, axis=-1) * _silu(pg[:, 0:W_C])
            u = pg[:, W_C:W_C + W_D]
            vv = pg[:, W_C + W_D:W_C + 2 * W_D]
            g_d = pg[:, W_C + 2 * W_D:W_C + 3 * W_D]

            vvn = _rms(_gelu(vv), gg_ref[0]).astype(BF16)
            chunks = []
            for c in range(ROW_BLOCK // CHUNK):
                cols = []
                for jp in range(G_D // 2):
                    vp = vvn[c * CHUNK:(c + 1) * CHUNK, jp * LANES:(jp + 1) * LANES]
                    o = _dot(ws_ref[0, jp], vp)
                    cols.append(_pair_select(o[0:CHUNK, :], o[CHUNK:2 * CHUNK, :]))
                chunks.append(jnp.concatenate(cols, axis=-1) + bs_ref[0])
            mixed = jnp.concatenate(chunks, axis=0)
            d_out = _gelu(u) * mixed * _silu(g_d)

            y = (_dot(c_out.astype(BF16), wout_ref[0, 0:W_C, :])
                 + _dot(d_out.astype(BF16), wout_ref[0, W_C:W_C + W_D, :]))
            res = x_ref[sq, rows, :] + mod_ref[0, 0, 2:3, :] * y
            if final:
                res = _rms(res, fg_ref[...])
            o_ref[sq, rows, :] = res

        _row_loop(n // ROW_BLOCK, phase3)

    _loop(n_seq, sequence)


def _mod_kernel(c_ref, w_ref, b_ref, o_ref):
    a = _silu(c_ref[...]).astype(BF16)
    o_ref[0] = _dot(a, w_ref[0].astype(BF16)) + b_ref[0]


def _modulation(cvec, ada_w, ada_b):
    rows = cvec.shape[0]
    tn = D_MODEL
    return pl.pallas_call(
        _mod_kernel,
        grid=(DEPTH, 3 * D_MODEL // tn),
        in_specs=[
            pl.BlockSpec((rows, D_MODEL), lambda l, j: (0, 0)),
            pl.BlockSpec((1, D_MODEL, tn), lambda l, j: (l, 0, j)),
            pl.BlockSpec((1, 1, tn), lambda l, j: (l, 0, j)),
        ],
        out_specs=pl.BlockSpec((1, rows, tn), lambda l, j: (l, 0, j)),
        out_shape=jax.ShapeDtypeStruct((DEPTH, rows, 3 * D_MODEL), F32),
        compiler_params=pltpu.CompilerParams(dimension_semantics=("arbitrary", "arbitrary")),
        name="modulation",
    )(cvec, ada_w, ada_b.reshape(DEPTH, 1, 3 * D_MODEL))


def _split_even_kernel(wt_ref, head_ref, main_ref):
    kv_end = Q_RANK + KV_RANK

    def put(dst, col, rows_f32):
        dst[0, :, col:col + LANES] = rows_f32.T.astype(BF16)

    for j in range(kv_end // LANES):
        put(head_ref, j * LANES, wt_ref[0, j * LANES:(j + 1) * LANES, :])
    k_r = wt_ref[0, kv_end:EVEN_HEAD, :]
    put(head_ref, kv_end, jnp.concatenate(
        [jnp.zeros((NOPE_A, D_MODEL), F32), k_r, jnp.zeros((LANES - QK_A, D_MODEL), F32)], axis=0))
    for j in range(main_ref.shape[2] // LANES):
        put(main_ref, j * LANES, wt_ref[0, EVEN_HEAD + j * LANES:EVEN_HEAD + (j + 1) * LANES, :])


def _split_even(w_in_t):
    n_layers, cols, d = w_in_t.shape
    n_head, n_main = Q_RANK + KV_RANK + LANES, cols - EVEN_HEAD
    return pl.pallas_call(
        _split_even_kernel,
        grid=(n_layers,),
        in_specs=[pl.BlockSpec((1, cols, d), lambda l: (l, 0, 0))],
        out_specs=[pl.BlockSpec((1, d, n_head), lambda l: (l, 0, 0)),
                   pl.BlockSpec((1, d, n_main), lambda l: (l, 0, 0))],
        out_shape=[jax.ShapeDtypeStruct((n_layers, d, n_head), BF16),
                   jax.ShapeDtypeStruct((n_layers, d, n_main), BF16)],
        compiler_params=pltpu.CompilerParams(dimension_semantics=("arbitrary",), vmem_limit_bytes=VMEM_LIMIT),
        name="split_even_w_in",
    )(w_in_t)


def _full(a):
    nd = a.ndim
    return a, pl.BlockSpec(a.shape, lambda b, _nd=nd: (0,) * _nd)


def _per_step(a, n_seq):
    nd = a.ndim
    return a, pl.BlockSpec((n_seq,) + a.shape[1:], lambda b, _nd=nd: (b,) + (0,) * (_nd - 1))


def _of_layer(a, i):
    nd = a.ndim
    return a, pl.BlockSpec((1,) + a.shape[1:], lambda b, _nd=nd, _i=i: (_i,) + (0,) * (_nd - 1))


def _seq_of_layer(a, i):
    nd = a.ndim
    return a, pl.BlockSpec((1, 1) + a.shape[2:], lambda b, _nd=nd, _i=i: (b, _i) + (0,) * (_nd - 2))


def _mod_spec(mod, l, row):
    if row is None:
        return mod, pl.BlockSpec((1, 1, 3, D_MODEL), lambda b, _l=l: (_l, b, 0, 0))
    return mod, pl.BlockSpec((1, 1, 3, D_MODEL), lambda b, _l=l, _r=row: (_l, _r, 0, 0))


def _rope_cs(n, rot_dim):
    rows = n // GRID_W
    r = np.repeat(np.arange(rows, dtype=np.float32), GRID_W)
    col = np.tile(np.arange(GRID_W, dtype=np.float32), rows)
    nf = rot_dim // 4
    inv = np.float32(ROPE_BASE) ** (-np.arange(nf, dtype=np.float32) / np.float32(nf))
    ar = r[:, None] * inv
    ac = col[:, None] * inv
    cos = np.concatenate([np.cos(ar), np.cos(ar), np.cos(ac), np.cos(ac)], axis=-1)
    sin = np.concatenate([-np.sin(ar), np.sin(ar), -np.sin(ac), np.sin(ac)], axis=-1)
    return cos.astype(np.float32), sin.astype(np.float32)


def _layer_call(body, x, n_seq, operands, caches, slot, scratch, name):
    b, n, _ = x.shape
    fresh = slot == 0
    shapes = [c if fresh else c.shape for c in caches]
    in_specs = [spec for _, spec in operands]
    args = [a for a, _ in operands]
    aliases = {}
    if not fresh:
        in_specs += [pl.BlockSpec(memory_space=pl.ANY) for _ in caches]
        args += list(caches)
        aliases = {len(operands) + k: 1 + k for k in range(len(caches))}
    out_shape = [jax.ShapeDtypeStruct(x.shape, F32)] + [jax.ShapeDtypeStruct(s, F32) for s in shapes]
    out_specs = [pl.BlockSpec((n_seq, n, D_MODEL), lambda i: (i, 0, 0))]
    for s in shapes:
        if fresh:
            out_specs.append(pl.BlockSpec((n_seq,) + tuple(s[1:]), lambda i: (i, 0, 0, 0)))
        else:
            out_specs.append(pl.BlockSpec((n_seq, 1) + tuple(s[2:]), lambda i, _s=slot: (i, _s, 0, 0)))
    return pl.pallas_call(
        body,
        grid=(b // n_seq,),
        in_specs=in_specs,
        out_specs=out_specs,
        out_shape=out_shape,
        input_output_aliases=aliases,
        scratch_shapes=scratch,
        compiler_params=pltpu.CompilerParams(dimension_semantics=("arbitrary",), vmem_limit_bytes=VMEM_LIMIT),
        name=name,
    )(*args)


def _store_cache(ref, sq, value):
    ref[sq, 0] = value
    for l in range(1, ref.shape[1]):
        ref[sq, l] = jnp.zeros(value.shape, value.dtype)


def _seqs_per_step(b, ctx):
    return 1 if ctx is not None else int(np.gcd(b, CTX_SEQS_PER_STEP))


def _even_layer(x, mod_op, i, p, ctx, tab, caches, final_g, name):
    b, n, _ = x.shape
    n_ctx = 0 if ctx is None else ctx[0].shape[2]
    n_seq = _seqs_per_step(b, ctx)
    ops = [_per_step(x, n_seq), mod_op, _of_layer(p["norm_g"], 2 * i),
           _of_layer(p["w_head"], i), _of_layer(p["w_main"], i), _of_layer(p["w_uq"], i), _of_layer(p["w_uk"], i),
           _of_layer(p["w_uv"], i), _of_layer(p["ev_w_out"], i),
           _of_layer(p["gq"], i), _of_layer(p["gkv"], i), _of_layer(p["conv_w"], i), _of_layer(p["conv_b"], i)]
    if ctx is not None:
        ops += [_seq_of_layer(ctx[0], i), _per_step(ctx[1][i], 1), _full(tab)]
    if final_g is not None:
        ops.append(_full(final_g))
    nk = n_ctx + n
    scratch = [pltpu.VMEM((n, D_MODEL), BF16), pltpu.VMEM((n, H_A * LANES), BF16),
               pltpu.VMEM((nk, LANES), BF16), pltpu.VMEM((nk, LANES), BF16),
               pltpu.VMEM((nk, H_A * LANES), BF16), pltpu.VMEM((nk, 2 * W_A), BF16),
               pltpu.VMEM((n + 2 * CONV_PAD, W_B), F32), pltpu.VMEM((2, ROW_BLOCK, nk), F32)]
    assert ctx is not None or n == ROW_BLOCK, "the context pass writes one cache block per sequence"
    body = functools.partial(_even_kernel, n=n, n_ctx=n_ctx, n_seq=n_seq, final=final_g is not None, fresh=i == 0)
    return _layer_call(body, x, n_seq, ops, caches, i, scratch, name)


def _odd_layer(x, mod_op, i, p, ctx, tab, caches, final_g, name):
    b, n, _ = x.shape
    n_ctx = 0 if ctx is None else ctx[0].shape[3]
    n_seq = _seqs_per_step(b, ctx)
    ops = [_per_step(x, n_seq), mod_op, _of_layer(p["norm_g"], 2 * i + 1),
           _of_layer(p["od_w_in"], i), _of_layer(p["od_w_out"], i),
           (p["sink"], pl.BlockSpec(memory_space=pltpu.SMEM)), _of_layer(p["gg"], i), _of_layer(p["ws"], i),
           _of_layer(p["bs"], i)]
    if ctx is not None:
        ops += [_seq_of_layer(ctx[0], i), _seq_of_layer(ctx[1], i), _full(tab)]
    if final_g is not None:
        ops.append(_full(final_g))
    pad = WINDOW if ctx is not None else 0
    n_keys = n_ctx + ROW_BLOCK + 2 * pad if ctx is not None else n
    scratch = [pltpu.VMEM((n, D_MODEL), BF16), pltpu.VMEM((n + 2 * pad, KV_C * LANES), BF16),
               pltpu.VMEM((n + 2 * pad, KV_C * 2 * LANES), BF16), pltpu.VMEM((2, ROW_BLOCK, n_keys), F32)]
    if ctx is not None:
        scratch += [pltpu.VMEM((n_ctx, KV_C * LANES), BF16), pltpu.VMEM((n_ctx, KV_C * 2 * LANES), BF16)]
    assert ctx is not None or n == ROW_BLOCK, "the context pass writes one cache block per sequence"
    body = functools.partial(_odd_kernel, n=n, n_ctx=n_ctx, n_seq=n_seq, final=final_g is not None, fresh=i == 0,
                             layer=i)
    return _layer_call(body, x, n_seq, ops, caches, i, scratch, name)


def kernel(x_prompt, x_sample, cache_mla_ckv, cache_mla_krope, cache_swa_k, cache_swa_v, c, c_ctx, ada_w, ada_b, norm_g, ev_w_in, ev_w_out, mla_gq, mla_gkv, mla_w_uq, mla_w_uk, mla_w_uv, conv_w, conv_b, od_w_in, od_w_out, swa_sink, gmlp_g, gmlp_ws, gmlp_bs, final_g):
    bp, n_p, _ = x_prompt.shape
    bs_, n_s, _ = x_sample.shape
    n_past = cache_mla_ckv.shape[2]
    n_even, n_odd = ev_w_in.shape[0], od_w_in.shape[0]

    n_rows = -(-(bs_ + 1) // 8) * 8
    cvec = jnp.concatenate([c, c_ctx[None, :], jnp.zeros((n_rows - bs_ - 1, D_MODEL), F32)], axis=0)
    mod = _modulation(cvec, ada_w, ada_b).reshape(DEPTH, n_rows, 3, D_MODEL)

    cos_a, sin_a = _rope_cs(n_s, ROPE_A)
    cos_c, sin_c = _rope_cs(n_s, HD_C)
    zero = lambda w: np.zeros((n_s, w), np.float32)
    tab_a = jnp.asarray(np.concatenate([np.ones((n_s, NOPE_A), np.float32), cos_a, zero(LANES - QK_A),
                                        zero(NOPE_A), sin_a, zero(LANES - QK_A)], axis=1))
    tab_c = jnp.asarray(np.concatenate([cos_c, cos_c, sin_c, sin_c], axis=1))

    uq = mla_w_uq.reshape(n_even, Q_RANK, H_A, QK_A)
    uk = mla_w_uk.reshape(n_even, KV_RANK, H_A, NOPE_A)
    w_head, w_main = _split_even(jnp.swapaxes(ev_w_in, 1, 2))
    p = {
        "norm_g": norm_g.reshape(DEPTH, 1, D_MODEL),
        "w_head": w_head,
        "w_main": w_main,
        "w_uq": jnp.pad(uq, ((0, 0), (0, 0), (0, 0), (0, LANES - QK_A))).reshape(n_even, Q_RANK, H_A * LANES).astype(BF16),
        "w_uk": jnp.pad(uk, ((0, 0), (0, 0), (0, 0), (0, LANES - NOPE_A))).reshape(n_even, KV_RANK, H_A * LANES).astype(BF16),
        "w_uv": mla_w_uv.astype(BF16),
        "ev_w_out": ev_w_out.astype(BF16),
        "gq": mla_gq.reshape(n_even, 1, Q_RANK),
        "gkv": mla_gkv.reshape(n_even, 1, KV_RANK),
        "conv_w": conv_w,
        "conv_b": conv_b.reshape(n_even, 1, W_B),
        "od_w_in": od_w_in.astype(BF16),
        "od_w_out": od_w_out.astype(BF16),
        "sink": swa_sink,
        "gg": gmlp_g.reshape(n_odd, 1, W_D),
        "ws": gmlp_ws.reshape(n_odd, G_D // 2, 2 * CHUNK, CHUNK).astype(BF16),
        "bs": jnp.repeat(jnp.swapaxes(gmlp_bs, 1, 2), DG_D, axis=2),
    }
    ckr = jnp.pad(cache_mla_krope, ((0, 0), (0, 0), (0, 0), (NOPE_A, LANES - QK_A)))
    ckr = [ckr[:, i] for i in range(n_even)]
    swa_k = jnp.transpose(cache_swa_k, (0, 1, 3, 4, 2)).reshape(bs_, n_odd, KV_C * HD_C, n_past)
    swa_v = jnp.transpose(cache_swa_v, (0, 1, 3, 4, 2)).reshape(bs_, n_odd, KV_C * HD_C, n_past)

    new_ckv = (bp, n_even, n_p, KV_RANK)
    new_kr = (bp, n_even, ROPE_A, n_p)
    new_k = (bp, n_odd, KV_C * HD_C, n_p)
    new_v = (bp, n_odd, KV_C * HD_C, n_p)

    fg = final_g.reshape(1, D_MODEL)
    xp, xs = x_prompt, x_sample
    for l in range(DEPTH):
        i = l // 2
        last = fg if l == DEPTH - 1 else None
        mod_p = _mod_spec(mod, l, bs_)
        mod_s = _mod_spec(mod, l, None)
        if l % 2 == 0:
            xp, new_ckv, new_kr = _even_layer(xp, mod_p, i, p, None, None, (new_ckv, new_kr), last, f"even{l}_ctx")
            (xs,) = _even_layer(xs, mod_s, i, p, (cache_mla_ckv, ckr), tab_a, (), last, f"even{l}_lat")
        else:
            xp, new_k, new_v = _odd_layer(xp, mod_p, i, p, None, None, (new_k, new_v), last, f"odd{l}_ctx")
            (xs,) = _odd_layer(xs, mod_s, i, p, (swa_k, swa_v), tab_c, (), last, f"odd{l}_lat")

    def heads_last(a):
        return jnp.transpose(a.reshape(bp, n_odd, KV_C, HD_C, n_p), (0, 1, 4, 2, 3))

    return (xp, xs, new_ckv, jnp.swapaxes(new_kr, 2, 3), heads_last(new_k), heads_last(new_v))
```

```python
import functools

import numpy as np
import jax
import jax.numpy as jnp
from jax import lax
from jax.experimental import pallas as pl
from jax.experimental.pallas import tpu as pltpu

D_MODEL = 1024
DEPTH = 4
GRID_W = 64
ROPE_BASE = 10000.0
EPS = 1e-6
NEG_INF = -1e30

H_A = 8
NOPE_A = 64
ROPE_A = 32
V_A = 64
Q_RANK = 256
KV_RANK = 128
QK_A = NOPE_A + ROPE_A
W_A = H_A * V_A
W_B = 512
CONV_W = 3
H_C = 8
KV_C = 2
HD_C = 64
WINDOW = 128
W_C = H_C * HD_C
G_D = 8
W_D = 512
CHUNK = 128
DG_D = W_D // G_D

EVEN_SPLIT = (Q_RANK, KV_RANK, ROPE_A, W_A, W_B, W_B, W_B, W_B)
ODD_SPLIT = (W_C, KV_C * HD_C, KV_C * HD_C, W_C, W_D, W_D, W_D)
EVEN_HEAD = Q_RANK + KV_RANK + ROPE_A

LANES = 128
MAX_ROWS = 512
CONV_PAD = 8
CTX_SEQS_PER_STEP = 4
VMEM_LIMIT = 60 * 1024 * 1024

F32 = jnp.float32
BF16 = jnp.bfloat16
LOG2E = float(np.log2(np.e))


def _dot(a, b):
    return jnp.dot(a, b, preferred_element_type=F32)


def _dot_t(a, b):
    return lax.dot_general(a, b, (((1,), (1,)), ((), ())), preferred_element_type=F32)


def _rms(x, g):
    return x * lax.rsqrt(jnp.mean(x * x, axis=-1, keepdims=True) + EPS) * g


def _silu(x):
    return x * (1.0 / (1.0 + jnp.exp(-x)))


def _gelu(x):
    c = np.float32(np.sqrt(2.0 / np.pi))
    return 0.5 * x * (1.0 + jnp.tanh(c * (x + 0.044715 * (x * x * x))))


def _lane(shape):
    return lax.broadcasted_iota(jnp.int32, shape, 1)


def _pair_select(lo, hi):
    return jnp.where(_lane(lo.shape) < LANES // 2, lo, hi)


def _rope(x, tab, quarter):
    fwd = pltpu.roll(x, LANES - quarter, 1)
    bwd = pltpu.roll(x, quarter, 1)
    partner = jnp.where((_lane(x.shape) & quarter) == 0, fwd, bwd)
    return x * tab[:, 0:LANES] + partner * tab[:, LANES:2 * LANES]


def _dup_halves(x):
    sw = pltpu.roll(x, LANES // 2, 1)
    lo = _lane(x.shape) < LANES // 2
    return jnp.where(lo, x, sw), jnp.where(lo, sw, x)


def _modulated_norm(x, mod_ref, ng_ref):
    shift = mod_ref[0, 0, 0:1, :]
    scale = mod_ref[0, 0, 1:2, :]
    return _rms(x, ng_ref[0]) * (1.0 + scale) + shift


def _pipelined(n, produce, consume):
    produce(0)
    outs = []
    for i in range(n):
        if i + 1 < n:
            produce(i + 1)
        outs.append(consume(i))
    return outs


def _normalised(o):
    return o[:, 0:LANES] * (1.0 / o[:, LANES:2 * LANES])


def _loop(n, body):
    if n == 1:
        body(0)
    else:
        def step(i, carry):
            body(i)
            return carry
        lax.fori_loop(0, n, step, 0)


def _row_block(n):
    return min(n, MAX_ROWS)


def _row_loop(n_rows, rb, body):
    _loop(n_rows // rb, lambda i: body(i * rb if isinstance(i, int) else pl.multiple_of(i * rb, rb)))


def _even_kernel(*refs, n, n_ctx, n_seq, final, fresh):
    rope = n_ctx > 0
    it = iter(refs)
    x_ref, mod_ref, ng_ref = next(it), next(it), next(it)
    wh_ref, wm_ref, wuq_ref, wuk_ref, wuv_ref, wout_ref = (next(it) for _ in range(6))
    gq_ref, gkv_ref, cw_ref, cb_ref = next(it), next(it), next(it), next(it)
    if rope:
        cckv_ref, ckr_ref, tab_ref = next(it), next(it), next(it)
    if final:
        fg_ref = next(it)
    if not rope and not fresh:
        next(it), next(it)
    o_ref = next(it)
    if not rope:
        ckv_out, kr_out = next(it), next(it)
    h_s, q_s, ckv_s, kr_s, k_s, v_s, t_s, s_s = (next(it) for _ in range(8))

    nk = n_ctx + n
    rb = _row_block(n)
    q_scale = np.float32(QK_A ** -0.5 * LOG2E)

    zpad = jnp.zeros((CONV_PAD, W_B), F32)
    t_s[0:CONV_PAD, :] = zpad
    t_s[CONV_PAD + n:2 * CONV_PAD + n, :] = zpad

    def sequence(sq):
        if rope:
            ckv_s[0:n_ctx, :] = cckv_ref[0, 0].astype(BF16)
            kr_s[0:n_ctx, :] = ckr_ref[0].astype(BF16)

        def phase1(r0):
            rows = pl.ds(r0, rb)
            hb = _modulated_norm(x_ref[sq, rows, :], mod_ref, ng_ref).astype(BF16)
            h_s[rows, :] = hb
            ph = _dot(hb, wh_ref[0])
            qn = _rms(ph[:, 0:Q_RANK], gq_ref[0]).astype(BF16)
            qq = _dot(qn, wuq_ref[0])
            ckv = _rms(ph[:, Q_RANK:Q_RANK + KV_RANK], gkv_ref[0])
            kr = ph[:, Q_RANK + KV_RANK:Q_RANK + KV_RANK + LANES]
            if rope:
                tab = tab_ref[rows, :]
                kr = _rope(kr, tab, ROPE_A // 4)
                for h in range(H_A):
                    qh = _rope(qq[:, h * LANES:(h + 1) * LANES], tab, ROPE_A // 4)
                    q_s[rows, h * LANES:(h + 1) * LANES] = (qh * q_scale).astype(BF16)
            else:
                q_s[rows, :] = (qq * q_scale).astype(BF16)
                _store_cache(ckv_out, sq, ckv)
                _store_cache(kr_out, sq, kr.T[NOPE_A:QK_A, :])
            krows = pl.ds(n_ctx + r0, rb)
            ckv_s[krows, :] = ckv.astype(BF16)
            kr_s[krows, :] = kr.astype(BF16)
            pc = _dot(hb, wm_ref[0, :, 2 * W_B:4 * W_B])
            t_s[pl.ds(CONV_PAD + r0, rb), :] = pc[:, 0:W_B] * pc[:, W_B:2 * W_B]

        _row_loop(n, rb, phase1)

        def phase2(r0):
            rows = pl.ds(r0, rb)
            c = ckv_s[rows, :]
            kn = _dot(c, wuk_ref[0])
            krf = kr_s[rows, :].astype(F32)
            for h in range(H_A):
                k_s[rows, h * LANES:(h + 1) * LANES] = (kn[:, h * LANES:(h + 1) * LANES] + krf).astype(BF16)
            v = _dot(c, wuv_ref[0]).astype(BF16)
            ones = jnp.ones((rb, LANES), BF16)
            for j in range(H_A // 2):
                v_s[rows, 2 * j * LANES:(2 * j + 1) * LANES] = v[:, j * LANES:(j + 1) * LANES]
                v_s[rows, (2 * j + 1) * LANES:(2 * j + 2) * LANES] = ones

        _row_loop(nk, rb, phase2)

        def phase3(r0):
            rows = pl.ds(r0, rb)
            hb = h_s[rows, :]

            def scores(h):
                s_s[h % 2] = _dot_t(q_s[rows, h * LANES:(h + 1) * LANES], k_s[:, h * LANES:(h + 1) * LANES])

            def attend(h):
                s = s_s[h % 2]
                e = jnp.exp2(s - jnp.max(s, axis=-1, keepdims=True)).astype(BF16)
                j = h // 2
                return _normalised(_dot(e, v_s[:, 2 * j * LANES:(2 * j + 2) * LANES]))

            heads = _pipelined(H_A, scores, attend)
            pairs = [_pair_select(heads[2 * j], heads[2 * j + 1]) for j in range(H_A // 2)]
            pa = _dot(hb, wm_ref[0, :, 0:2 * W_B])
            a_out = jnp.concatenate(pairs, axis=-1) * _silu(pa[:, 0:W_A])
            gate_b = pa[:, W_A:W_A + W_B]
            g_b = _dot(hb, wm_ref[0, :, 4 * W_B:5 * W_B])

            t_cur = t_s[pl.ds(CONV_PAD + r0, rb), :]
            t_up = t_s[pl.ds(r0, CONV_PAD), :]
            t_dn = t_s[pl.ds(CONV_PAD + r0 + rb, CONV_PAD), :]
            rid = lax.broadcasted_iota(jnp.int32, (rb, W_B), 0)
            t_prev = jnp.where(rid == 0, t_up[CONV_PAD - 1:CONV_PAD, :], pltpu.roll(t_cur, 1, 0))
            t_next = jnp.where(rid == rb - 1, t_dn[0:1, :], pltpu.roll(t_cur, rb - 1, 0))
            conv = cb_ref[0] + t_prev * cw_ref[0, 0:1, :]
            conv = conv + t_cur * cw_ref[0, 1:2, :]
            conv = conv + t_next * cw_ref[0, 2:3, :]
            b_out = gate_b * conv * _silu(g_b)

            y = (_dot(a_out.astype(BF16), wout_ref[0, 0:W_A, :])
                 + _dot(b_out.astype(BF16), wout_ref[0, W_A:W_A + W_B, :]))
            res = x_ref[sq, rows, :] + mod_ref[0, 0, 2:3, :] * y
            if final:
                res = _rms(res, fg_ref[...])
            o_ref[sq, rows, :] = res

        _row_loop(n, rb, phase3)

    _loop(n_seq, sequence)


def _odd_kernel(*refs, n, n_ctx, n_seq, final, fresh, layer):
    rope = n_ctx > 0
    it = iter(refs)
    x_ref, mod_ref, ng_ref = next(it), next(it), next(it)
    win_ref, wout_ref = next(it), next(it)
    sink_ref, gg_ref, ws_ref, bs_ref = next(it), next(it), next(it), next(it)
    if rope:
        kc_ref, vc_ref, tab_ref = next(it), next(it), next(it)
    if final:
        fg_ref = next(it)
    if not rope and not fresh:
        next(it), next(it)
    o_ref = next(it)
    if not rope:
        k_out, v_out = next(it), next(it)
    h_s, kloc_s, vloc_s, s_s = next(it), next(it), next(it), next(it)
    if rope:
        kctx_s, vctx_s = next(it), next(it)

    pad = WINDOW if rope else 0
    rb = _row_block(n)
    kw, vw = LANES, 2 * LANES
    q0, k0, g0 = 0, W_C, W_C + 2 * KV_C * HD_C
    q_scale = np.float32(HD_C ** -0.5 * LOG2E)

    def store_keys(dst, rows, x):
        a, b = _dup_halves(x)
        dst[rows, 0:LANES] = a.astype(BF16)
        dst[rows, LANES:2 * LANES] = b.astype(BF16)

    def store_values(dst, rows, x):
        a, b = _dup_halves(x)
        ones = jnp.ones(x.shape, BF16)
        dst[rows, 0:LANES] = a.astype(BF16)
        dst[rows, LANES:2 * LANES] = ones
        dst[rows, 2 * LANES:3 * LANES] = b.astype(BF16)
        dst[rows, 3 * LANES:4 * LANES] = ones

    def sequence(sq):
        if rope:
            kloc_s[0:pad, :] = jnp.zeros((pad, KV_C * kw), BF16)
            kloc_s[pad + n:2 * pad + n, :] = jnp.zeros((pad, KV_C * kw), BF16)
            vloc_s[0:pad, :] = jnp.zeros((pad, KV_C * vw), BF16)
            vloc_s[pad + n:2 * pad + n, :] = jnp.zeros((pad, KV_C * vw), BF16)
            store_keys(kctx_s, slice(None), kc_ref[0, 0].T)
            store_values(vctx_s, slice(None), vc_ref[0, 0].T)

        def phase1(r0):
            rows = pl.ds(r0, rb)
            hb = _modulated_norm(x_ref[sq, rows, :], mod_ref, ng_ref).astype(BF16)
            h_s[rows, :] = hb
            p1 = _dot(hb, win_ref[0, :, k0:g0])
            k = p1[:, 0:LANES]
            v = p1[:, LANES:2 * LANES]
            if rope:
                k = _rope(k, tab_ref[rows, :], HD_C // 4)
            else:
                _store_cache(k_out, sq, k.T)
                _store_cache(v_out, sq, v.T)
            lrows = pl.ds(pad + r0, rb)
            store_keys(kloc_s, lrows, k)
            store_values(vloc_s, lrows, v)

        _row_loop(n, rb, phase1)

        def phase3(r0):
            rows = pl.ds(r0, rb)
            hb = h_s[rows, :]
            pq = _dot(hb, win_ref[0, :, q0:k0])

            if rope:
                tab = tab_ref[rows, :]
                span = rb + 2 * WINDOW
                kl = kloc_s[pl.ds(r0, span), :]
                vl = vloc_s[pl.ds(r0, span), :]
                ii = lax.broadcasted_iota(jnp.int32, (rb, span), 0)
                jj = lax.broadcasted_iota(jnp.int32, (rb, span), 1)
                kpos = jj + (r0 - WINDOW)
                dlt = jj - ii
                mask = (dlt >= 0) & (dlt <= 2 * WINDOW) & (kpos >= 0) & (kpos < n)
            else:
                kl = kloc_s[...]
                vl = vloc_s[...]

            qhs = []
            for m in range(H_C // 2):
                qb = pq[:, m * LANES:(m + 1) * LANES]
                if rope:
                    qb = _rope(qb, tab, HD_C // 4)
                qb = qb * q_scale
                lo = _lane(qb.shape) < LANES // 2
                qhs.append(jnp.where(lo, qb, 0.0).astype(BF16))
                qhs.append(jnp.where(lo, 0.0, qb).astype(BF16))

            def kv_head(h):
                return h // (H_C // KV_C)

            def scores(h):
                j = kv_head(h)
                if rope:
                    s_s[h % 2, :, 0:n_ctx] = _dot_t(qhs[h], kctx_s[:, j * kw:(j + 1) * kw])
                    s_s[h % 2, :, n_ctx:] = _dot_t(qhs[h], kl[:, j * kw:(j + 1) * kw])
                else:
                    s_s[h % 2] = _dot_t(qhs[h], kl[:, j * kw:(j + 1) * kw])

            def attend(h):
                j = kv_head(h)
                sk = sink_ref[layer, h] * np.float32(LOG2E)
                if rope:
                    sc = s_s[h % 2, :, 0:n_ctx]
                    sl = jnp.where(mask, s_s[h % 2, :, n_ctx:], NEG_INF)
                    mx = jnp.maximum(jnp.max(sl, axis=-1, keepdims=True), jnp.max(sc, axis=-1, keepdims=True))
                else:
                    sl = s_s[h % 2]
                    mx = jnp.max(sl, axis=-1, keepdims=True)
                mx = jnp.maximum(mx, sk)
                o = _dot(jnp.exp2(sl - mx).astype(BF16), vl[:, j * vw:(j + 1) * vw])
                if rope:
                    o = o + _dot(jnp.exp2(sc - mx).astype(BF16), vctx_s[:, j * vw:(j + 1) * vw])
                return o[:, 0:LANES] * (1.0 / (o[:, LANES:2 * LANES] + jnp.exp2(sk - mx)))

            heads = _pipelined(H_C, scores, attend)
            blocks = [_pair_select(heads[2 * m], heads[2 * m + 1]) for m in range(H_C // 2)]
            pg = _dot(hb, win_ref[0, :, g0:g0 + W_C + 3 * W_D])
            c_out = jnp.concatenate(blocks, axis=-1) * _silu(pg[:, 0:W_C])
            u = pg[:, W_C:W_C + W_D]
            vv = pg[:, W_C + W_D:W_C + 2 * W_D]
            g_d = pg[:, W_C + 2 * W_D:W_C + 3 * W_D]

            vvn = _rms(_gelu(vv), gg_ref[0]).astype(BF16)
            chunks = []
            for c in range(rb // CHUNK):
                cols = []
                for jp in range(G_D // 2):
                    vp = vvn[c * CHUNK:(c + 1) * CHUNK, jp * LANES:(jp + 1) * LANES]
                    o = _dot(ws_ref[0, jp], vp)
                    cols.append(_pair_select(o[0:CHUNK, :], o[CHUNK:2 * CHUNK, :]))
                chunks.append(jnp.concatenate(cols, axis=-1) + bs_ref[0])
            mixed = jnp.concatenate(chunks, axis=0)
            d_out = _gelu(u) * mixed * _silu(g_d)

            y = (_dot(c_out.astype(BF16), wout_ref[0, 0:W_C, :])
                 + _dot(d_out.astype(BF16), wout_ref[0, W_C:W_C + W_D, :]))
            res = x_ref[sq, rows, :] + mod_ref[0, 0, 2:3, :] * y
            if final:
                res = _rms(res, fg_ref[...])
            o_ref[sq, rows, :] = res

        _row_loop(n, rb, phase3)

    _loop(n_seq, sequence)


def _mod_kernel(c_ref, w_ref, b_ref, o_ref):
    a = _silu(c_ref[...]).astype(BF16)
    o_ref[0] = _dot(a, w_ref[0].astype(BF16)) + b_ref[0]


def _modulation(cvec, ada_w, ada_b):
    rows = cvec.shape[0]
    tn = D_MODEL
    return pl.pallas_call(
        _mod_kernel,
        grid=(DEPTH, 3 * D_MODEL // tn),
        in_specs=[
            pl.BlockSpec((rows, D_MODEL), lambda l, j: (0, 0)),
            pl.BlockSpec((1, D_MODEL, tn), lambda l, j: (l, 0, j)),
            pl.BlockSpec((1, 1, tn), lambda l, j: (l, 0, j)),
        ],
        out_specs=pl.BlockSpec((1, rows, tn), lambda l, j: (l, 0, j)),
        out_shape=jax.ShapeDtypeStruct((DEPTH, rows, 3 * D_MODEL), F32),
        compiler_params=pltpu.CompilerParams(dimension_semantics=("arbitrary", "arbitrary")),
        name="modulation",
    )(cvec, ada_w, ada_b.reshape(DEPTH, 1, 3 * D_MODEL))


def _split_even_kernel(wt_ref, head_ref, main_ref):
    kv_end = Q_RANK + KV_RANK

    def put(dst, col, rows_f32):
        dst[0, :, col:col + LANES] = rows_f32.T.astype(BF16)

    for j in range(kv_end // LANES):
        put(head_ref, j * LANES, wt_ref[0, j * LANES:(j + 1) * LANES, :])
    k_r = wt_ref[0, kv_end:EVEN_HEAD, :]
    put(head_ref, kv_end, jnp.concatenate(
        [jnp.zeros((NOPE_A, D_MODEL), F32), k_r, jnp.zeros((LANES - QK_A, D_MODEL), F32)], axis=0))
    for j in range(main_ref.shape[2] // LANES):
        put(main_ref, j * LANES, wt_ref[0, EVEN_HEAD + j * LANES:EVEN_HEAD + (j + 1) * LANES, :])


def _split_even(w_in_t):
    n_layers, cols, d = w_in_t.shape
    n_head, n_main = Q_RANK + KV_RANK + LANES, cols - EVEN_HEAD
    return pl.pallas_call(
        _split_even_kernel,
        grid=(n_layers,),
        in_specs=[pl.BlockSpec((1, cols, d), lambda l: (l, 0, 0))],
        out_specs=[pl.BlockSpec((1, d, n_head), lambda l: (l, 0, 0)),
                   pl.BlockSpec((1, d, n_main), lambda l: (l, 0, 0))],
        out_shape=[jax.ShapeDtypeStruct((n_layers, d, n_head), BF16),
                   jax.ShapeDtypeStruct((n_layers, d, n_main), BF16)],
        compiler_params=pltpu.CompilerParams(dimension_semantics=("arbitrary",), vmem_limit_bytes=VMEM_LIMIT),
        name="split_even_w_in",
    )(w_in_t)


def _full(a):
    nd = a.ndim
    return a, pl.BlockSpec(a.shape, lambda b, _nd=nd: (0,) * _nd)


def _per_step(a, n_seq):
    nd = a.ndim
    return a, pl.BlockSpec((n_seq,) + a.shape[1:], lambda b, _nd=nd: (b,) + (0,) * (_nd - 1))


def _of_layer(a, i):
    nd = a.ndim
    return a, pl.BlockSpec((1,) + a.shape[1:], lambda b, _nd=nd, _i=i: (_i,) + (0,) * (_nd - 1))


def _seq_of_layer(a, i):
    nd = a.ndim
    return a, pl.BlockSpec((1, 1) + a.shape[2:], lambda b, _nd=nd, _i=i: (b, _i) + (0,) * (_nd - 2))


def _mod_spec(mod, l, row):
    if row is None:
        return mod, pl.BlockSpec((1, 1, 3, D_MODEL), lambda b, _l=l: (_l, b, 0, 0))
    return mod, pl.BlockSpec((1, 1, 3, D_MODEL), lambda b, _l=l, _r=row: (_l, _r, 0, 0))


def _rope_cs(n, rot_dim):
    rows = n // GRID_W
    r = np.repeat(np.arange(rows, dtype=np.float32), GRID_W)
    col = np.tile(np.arange(GRID_W, dtype=np.float32), rows)
    nf = rot_dim // 4
    inv = np.float32(ROPE_BASE) ** (-np.arange(nf, dtype=np.float32) / np.float32(nf))
    ar = r[:, None] * inv
    ac = col[:, None] * inv
    cos = np.concatenate([np.cos(ar), np.cos(ar), np.cos(ac), np.cos(ac)], axis=-1)
    sin = np.concatenate([-np.sin(ar), np.sin(ar), -np.sin(ac), np.sin(ac)], axis=-1)
    return cos.astype(np.float32), sin.astype(np.float32)


def _layer_call(body, x, n_seq, operands, caches, slot, scratch, name):
    b, n, _ = x.shape
    fresh = slot == 0
    shapes = [c if fresh else c.shape for c in caches]
    in_specs = [spec for _, spec in operands]
    args = [a for a, _ in operands]
    aliases = {}
    if not fresh:
        in_specs += [pl.BlockSpec(memory_space=pl.ANY) for _ in caches]
        args += list(caches)
        aliases = {len(operands) + k: 1 + k for k in range(len(caches))}
    out_shape = [jax.ShapeDtypeStruct(x.shape, F32)] + [jax.ShapeDtypeStruct(s, F32) for s in shapes]
    out_specs = [pl.BlockSpec((n_seq, n, D_MODEL), lambda i: (i, 0, 0))]
    for s in shapes:
        if fresh:
            out_specs.append(pl.BlockSpec((n_seq,) + tuple(s[1:]), lambda i: (i, 0, 0, 0)))
        else:
            out_specs.append(pl.BlockSpec((n_seq, 1) + tuple(s[2:]), lambda i, _s=slot: (i, _s, 0, 0)))
    return pl.pallas_call(
        body,
        grid=(b // n_seq,),
        in_specs=in_specs,
        out_specs=out_specs,
        out_shape=out_shape,
        input_output_aliases=aliases,
        scratch_shapes=scratch,
        compiler_params=pltpu.CompilerParams(dimension_semantics=("arbitrary",), vmem_limit_bytes=VMEM_LIMIT),
        name=name,
    )(*args)


def _store_cache(ref, sq, value):
    ref[sq, 0] = value
    for l in range(1, ref.shape[1]):
        ref[sq, l] = jnp.zeros(value.shape, value.dtype)


def _seqs_per_step(b, ctx):
    return 1 if ctx is not None else int(np.gcd(b, CTX_SEQS_PER_STEP))


def _even_layer(x, mod_op, i, p, ctx, tab, caches, final_g, name):
    b, n, _ = x.shape
    n_ctx = 0 if ctx is None else ctx[0].shape[2]
    n_seq = _seqs_per_step(b, ctx)
    ops = [_per_step(x, n_seq), mod_op, _of_layer(p["norm_g"], 2 * i),
           _of_layer(p["w_head"], i), _of_layer(p["w_main"], i), _of_layer(p["w_uq"], i), _of_layer(p["w_uk"], i),
           _of_layer(p["w_uv"], i), _of_layer(p["ev_w_out"], i),
           _of_layer(p["gq"], i), _of_layer(p["gkv"], i), _of_layer(p["conv_w"], i), _of_layer(p["conv_b"], i)]
    if ctx is not None:
        ops += [_seq_of_layer(ctx[0], i), _per_step(ctx[1][i], 1), _full(tab)]
    if final_g is not None:
        ops.append(_full(final_g))
    nk = n_ctx + n
    scratch = [pltpu.VMEM((n, D_MODEL), BF16), pltpu.VMEM((n, H_A * LANES), BF16),
               pltpu.VMEM((nk, LANES), BF16), pltpu.VMEM((nk, LANES), BF16),
               pltpu.VMEM((nk, H_A * LANES), BF16), pltpu.VMEM((nk, 2 * W_A), BF16),
               pltpu.VMEM((n + 2 * CONV_PAD, W_B), F32), pltpu.VMEM((2, _row_block(n), nk), F32)]
    assert ctx is not None or n == _row_block(n), "the context pass writes one cache block per sequence"
    body = functools.partial(_even_kernel, n=n, n_ctx=n_ctx, n_seq=n_seq, final=final_g is not None, fresh=i == 0)
    return _layer_call(body, x, n_seq, ops, caches, i, scratch, name)


def _odd_layer(x, mod_op, i, p, ctx, tab, caches, final_g, name):
    b, n, _ = x.shape
    n_ctx = 0 if ctx is None else ctx[0].shape[3]
    n_seq = _seqs_per_step(b, ctx)
    ops = [_per_step(x, n_seq), mod_op, _of_layer(p["norm_g"], 2 * i + 1),
           _of_layer(p["od_w_in"], i), _of_layer(p["od_w_out"], i),
           (p["sink"], pl.BlockSpec(memory_space=pltpu.SMEM)), _of_layer(p["gg"], i), _of_layer(p["ws"], i),
           _of_layer(p["bs"], i)]
    if ctx is not None:
        ops += [_seq_of_layer(ctx[0], i), _seq_of_layer(ctx[1], i), _full(tab)]
    if final_g is not None:
        ops.append(_full(final_g))
    pad = WINDOW if ctx is not None else 0
    n_keys = n_ctx + _row_block(n) + 2 * pad if ctx is not None else n
    scratch = [pltpu.VMEM((n, D_MODEL), BF16), pltpu.VMEM((n + 2 * pad, KV_C * LANES), BF16),
               pltpu.VMEM((n + 2 * pad, KV_C * 2 * LANES), BF16), pltpu.VMEM((2, _row_block(n), n_keys), F32)]
    if ctx is not None:
        scratch += [pltpu.VMEM((n_ctx, KV_C * LANES), BF16), pltpu.VMEM((n_ctx, KV_C * 2 * LANES), BF16)]
    assert ctx is not None or n == _row_block(n), "the context pass writes one cache block per sequence"
    body = functools.partial(_odd_kernel, n=n, n_ctx=n_ctx, n_seq=n_seq, final=final_g is not None, fresh=i == 0,
                             layer=i)
    return _layer_call(body, x, n_seq, ops, caches, i, scratch, name)


def kernel(x_prompt, x_sample, cache_mla_ckv, cache_mla_krope, cache_swa_k, cache_swa_v, c, c_ctx, ada_w, ada_b, norm_g, ev_w_in, ev_w_out, mla_gq, mla_gkv, mla_w_uq, mla_w_uk, mla_w_uv, conv_w, conv_b, od_w_in, od_w_out, swa_sink, gmlp_g, gmlp_ws, gmlp_bs, final_g):
    bp, n_p, _ = x_prompt.shape
    bs_, n_s, _ = x_sample.shape
    n_past = cache_mla_ckv.shape[2]
    n_even, n_odd = ev_w_in.shape[0], od_w_in.shape[0]

    n_rows = -(-(bs_ + 1) // 8) * 8
    cvec = jnp.concatenate([c, c_ctx[None, :], jnp.zeros((n_rows - bs_ - 1, D_MODEL), F32)], axis=0)
    mod = _modulation(cvec, ada_w, ada_b).reshape(DEPTH, n_rows, 3, D_MODEL)

    cos_a, sin_a = _rope_cs(n_s, ROPE_A)
    cos_c, sin_c = _rope_cs(n_s, HD_C)
    zero = lambda w: np.zeros((n_s, w), np.float32)
    tab_a = jnp.asarray(np.concatenate([np.ones((n_s, NOPE_A), np.float32), cos_a, zero(LANES - QK_A),
                                        zero(NOPE_A), sin_a, zero(LANES - QK_A)], axis=1))
    tab_c = jnp.asarray(np.concatenate([cos_c, cos_c, sin_c, sin_c], axis=1))

    uq = mla_w_uq.reshape(n_even, Q_RANK, H_A, QK_A)
    uk = mla_w_uk.reshape(n_even, KV_RANK, H_A, NOPE_A)
    w_head, w_main = _split_even(jnp.swapaxes(ev_w_in, 1, 2))
    p = {
        "norm_g": norm_g.reshape(DEPTH, 1, D_MODEL),
        "w_head": w_head,
        "w_main": w_main,
        "w_uq": jnp.pad(uq, ((0, 0), (0, 0), (0, 0), (0, LANES - QK_A))).reshape(n_even, Q_RANK, H_A * LANES).astype(BF16),
        "w_uk": jnp.pad(uk, ((0, 0), (0, 0), (0, 0), (0, LANES - NOPE_A))).reshape(n_even, KV_RANK, H_A * LANES).astype(BF16),
        "w_uv": mla_w_uv.astype(BF16),
        "ev_w_out": ev_w_out.astype(BF16),
        "gq": mla_gq.reshape(n_even, 1, Q_RANK),
        "gkv": mla_gkv.reshape(n_even, 1, KV_RANK),
        "conv_w": conv_w,
        "conv_b": conv_b.reshape(n_even, 1, W_B),
        "od_w_in": od_w_in.astype(BF16),
        "od_w_out": od_w_out.astype(BF16),
        "sink": swa_sink,
        "gg": gmlp_g.reshape(n_odd, 1, W_D),
        "ws": gmlp_ws.reshape(n_odd, G_D // 2, 2 * CHUNK, CHUNK).astype(BF16),
        "bs": jnp.repeat(jnp.swapaxes(gmlp_bs, 1, 2), DG_D, axis=2),
    }
    ckr = jnp.pad(cache_mla_krope, ((0, 0), (0, 0), (0, 0), (NOPE_A, LANES - QK_A)))
    ckr = [ckr[:, i] for i in range(n_even)]
    swa_k = jnp.transpose(cache_swa_k, (0, 1, 3, 4, 2)).reshape(bs_, n_odd, KV_C * HD_C, n_past)
    swa_v = jnp.transpose(cache_swa_v, (0, 1, 3, 4, 2)).reshape(bs_, n_odd, KV_C * HD_C, n_past)

    new_ckv = (bp, n_even, n_p, KV_RANK)
    new_kr = (bp, n_even, ROPE_A, n_p)
    new_k = (bp, n_odd, KV_C * HD_C, n_p)
    new_v = (bp, n_odd, KV_C * HD_C, n_p)

    fg = final_g.reshape(1, D_MODEL)
    xp, xs = x_prompt, x_sample
    for l in range(DEPTH):
        i = l // 2
        last = fg if l == DEPTH - 1 else None
        mod_p = _mod_spec(mod, l, bs_)
        mod_s = _mod_spec(mod, l, None)
        if l % 2 == 0:
            xp, new_ckv, new_kr = _even_layer(xp, mod_p, i, p, None, None, (new_ckv, new_kr), last, f"even{l}_ctx")
            (xs,) = _even_layer(xs, mod_s, i, p, (cache_mla_ckv, ckr), tab_a, (), last, f"even{l}_lat")
        else:
            xp, new_k, new_v = _odd_layer(xp, mod_p, i, p, None, None, (new_k, new_v), last, f"odd{l}_ctx")
            (xs,) = _odd_layer(xs, mod_s, i, p, (swa_k, swa_v), tab_c, (), last, f"odd{l}_lat")

    def heads_last(a):
        return jnp.transpose(a.reshape(bp, n_odd, KV_C, HD_C, n_p), (0, 1, 4, 2, 3))

    return (xp, xs, new_ckv, jnp.swapaxes(new_kr, 2, 3), heads_last(new_k), heads_last(new_v))
```

```python
import functools

import numpy as np
import jax
import jax.numpy as jnp
from jax import lax
from jax.experimental import pallas as pl
from jax.experimental.pallas import tpu as pltpu

D_MODEL = 1024
DEPTH = 4
GRID_W = 64
ROPE_BASE = 10000.0
EPS = 1e-6
NEG_INF = -1e30

H_A = 8
NOPE_A = 64
ROPE_A = 32
V_A = 64
Q_RANK = 256
KV_RANK = 128
QK_A = NOPE_A + ROPE_A
W_A = H_A * V_A
W_B = 512
CONV_W = 3
H_C = 8
KV_C = 2
HD_C = 64
WINDOW = 128
W_C = H_C * HD_C
G_D = 8
W_D = 512
CHUNK = 128
DG_D = W_D // G_D

EVEN_SPLIT = (Q_RANK, KV_RANK, ROPE_A, W_A, W_B, W_B, W_B, W_B)
ODD_SPLIT = (W_C, KV_C * HD_C, KV_C * HD_C, W_C, W_D, W_D, W_D)
EVEN_HEAD = Q_RANK + KV_RANK + ROPE_A

LANES = 128
MAX_ROWS = 512
WINDOW_ROWS = 256
CONV_PAD = 8
CTX_SEQS_PER_STEP = 4
VMEM_LIMIT = 60 * 1024 * 1024

F32 = jnp.float32
BF16 = jnp.bfloat16
LOG2E = float(np.log2(np.e))


def _dot(a, b):
    return jnp.dot(a, b, preferred_element_type=F32)


def _dot_t(a, b):
    return lax.dot_general(a, b, (((1,), (1,)), ((), ())), preferred_element_type=F32)


def _rms(x, g):
    return x * lax.rsqrt(jnp.mean(x * x, axis=-1, keepdims=True) + EPS) * g


def _silu(x):
    return x * (1.0 / (1.0 + jnp.exp(-x)))


def _gelu(x):
    c = np.float32(np.sqrt(2.0 / np.pi))
    return 0.5 * x * (1.0 + jnp.tanh(c * (x + 0.044715 * (x * x * x))))


def _lane(shape):
    return lax.broadcasted_iota(jnp.int32, shape, 1)


def _pair_select(lo, hi):
    return jnp.where(_lane(lo.shape) < LANES // 2, lo, hi)


def _rope(x, tab, quarter):
    fwd = pltpu.roll(x, LANES - quarter, 1)
    bwd = pltpu.roll(x, quarter, 1)
    partner = jnp.where((_lane(x.shape) & quarter) == 0, fwd, bwd)
    return x * tab[:, 0:LANES] + partner * tab[:, LANES:2 * LANES]


def _dup_halves(x):
    sw = pltpu.roll(x, LANES // 2, 1)
    lo = _lane(x.shape) < LANES // 2
    return jnp.where(lo, x, sw), jnp.where(lo, sw, x)


def _modulated_norm(x, mod_ref, ng_ref):
    shift = mod_ref[0, 0, 0:1, :]
    scale = mod_ref[0, 0, 1:2, :]
    return _rms(x, ng_ref[0]) * (1.0 + scale) + shift


def _pipelined(n, produce, consume):
    produce(0)
    outs = []
    for i in range(n):
        if i + 1 < n:
            produce(i + 1)
        outs.append(consume(i))
    return outs


def _normalised(o):
    return o[:, 0:LANES] * (1.0 / o[:, LANES:2 * LANES])


def _loop(n, body):
    if n == 1:
        body(0)
    else:
        def step(i, carry):
            body(i)
            return carry
        lax.fori_loop(0, n, step, 0)


def _row_block(n):
    return min(n, MAX_ROWS)


def _row_loop(n_rows, rb, body):
    _loop(n_rows // rb, lambda i: body(i * rb if isinstance(i, int) else pl.multiple_of(i * rb, rb)))


def _even_kernel(*refs, n, n_ctx, n_seq, final, fresh):
    rope = n_ctx > 0
    it = iter(refs)
    x_ref, mod_ref, ng_ref = next(it), next(it), next(it)
    wh_ref, wm_ref, wuq_ref, wuk_ref, wuv_ref, wout_ref = (next(it) for _ in range(6))
    gq_ref, gkv_ref, cw_ref, cb_ref = next(it), next(it), next(it), next(it)
    if rope:
        cckv_ref, ckr_ref, tab_ref = next(it), next(it), next(it)
    if final:
        fg_ref = next(it)
    if not rope and not fresh:
        next(it), next(it)
    o_ref = next(it)
    if not rope:
        ckv_out, kr_out = next(it), next(it)
    h_s, q_s, ckv_s, kr_s, k_s, v_s, t_s, s_s = (next(it) for _ in range(8))

    nk = n_ctx + n
    rb = _row_block(n)
    q_scale = np.float32(QK_A ** -0.5 * LOG2E)

    zpad = jnp.zeros((CONV_PAD, W_B), F32)
    t_s[0:CONV_PAD, :] = zpad
    t_s[CONV_PAD + n:2 * CONV_PAD + n, :] = zpad

    def sequence(sq):
        if rope:
            ckv_s[0:n_ctx, :] = cckv_ref[0, 0].astype(BF16)
            kr_s[0:n_ctx, :] = ckr_ref[0].astype(BF16)

        def phase1(r0):
            rows = pl.ds(r0, rb)
            hb = _modulated_norm(x_ref[sq, rows, :], mod_ref, ng_ref).astype(BF16)
            h_s[rows, :] = hb
            ph = _dot(hb, wh_ref[0])
            qn = _rms(ph[:, 0:Q_RANK], gq_ref[0]).astype(BF16)
            qq = _dot(qn, wuq_ref[0])
            ckv = _rms(ph[:, Q_RANK:Q_RANK + KV_RANK], gkv_ref[0])
            kr = ph[:, Q_RANK + KV_RANK:Q_RANK + KV_RANK + LANES]
            if rope:
                tab = tab_ref[rows, :]
                kr = _rope(kr, tab, ROPE_A // 4)
                for h in range(H_A):
                    qh = _rope(qq[:, h * LANES:(h + 1) * LANES], tab, ROPE_A // 4)
                    q_s[rows, h * LANES:(h + 1) * LANES] = (qh * q_scale).astype(BF16)
            else:
                q_s[rows, :] = (qq * q_scale).astype(BF16)
                _store_cache(ckv_out, sq, ckv)
                _store_cache(kr_out, sq, kr.T[NOPE_A:QK_A, :])
            krows = pl.ds(n_ctx + r0, rb)
            ckv_s[krows, :] = ckv.astype(BF16)
            kr_s[krows, :] = kr.astype(BF16)
            pc = _dot(hb, wm_ref[0, :, 2 * W_B:4 * W_B])
            t_s[pl.ds(CONV_PAD + r0, rb), :] = pc[:, 0:W_B] * pc[:, W_B:2 * W_B]

        _row_loop(n, rb, phase1)

        def phase2(r0):
            rows = pl.ds(r0, rb)
            c = ckv_s[rows, :]
            kn = _dot(c, wuk_ref[0])
            krf = kr_s[rows, :].astype(F32)
            for h in range(H_A):
                k_s[rows, h * LANES:(h + 1) * LANES] = (kn[:, h * LANES:(h + 1) * LANES] + krf).astype(BF16)
            v = _dot(c, wuv_ref[0]).astype(BF16)
            ones = jnp.ones((rb, LANES), BF16)
            for j in range(H_A // 2):
                v_s[rows, 2 * j * LANES:(2 * j + 1) * LANES] = v[:, j * LANES:(j + 1) * LANES]
                v_s[rows, (2 * j + 1) * LANES:(2 * j + 2) * LANES] = ones

        _row_loop(nk, rb, phase2)

        def phase3(r0):
            rows = pl.ds(r0, rb)
            hb = h_s[rows, :]

            def scores(h):
                s_s[h % 2] = _dot_t(q_s[rows, h * LANES:(h + 1) * LANES], k_s[:, h * LANES:(h + 1) * LANES])

            def attend(h):
                s = s_s[h % 2]
                e = jnp.exp2(s - jnp.max(s, axis=-1, keepdims=True)).astype(BF16)
                j = h // 2
                return _normalised(_dot(e, v_s[:, 2 * j * LANES:(2 * j + 2) * LANES]))

            heads = _pipelined(H_A, scores, attend)
            pairs = [_pair_select(heads[2 * j], heads[2 * j + 1]) for j in range(H_A // 2)]
            pa = _dot(hb, wm_ref[0, :, 0:2 * W_B])
            a_out = jnp.concatenate(pairs, axis=-1) * _silu(pa[:, 0:W_A])
            gate_b = pa[:, W_A:W_A + W_B]
            g_b = _dot(hb, wm_ref[0, :, 4 * W_B:5 * W_B])

            t_cur = t_s[pl.ds(CONV_PAD + r0, rb), :]
            t_up = t_s[pl.ds(r0, CONV_PAD), :]
            t_dn = t_s[pl.ds(CONV_PAD + r0 + rb, CONV_PAD), :]
            rid = lax.broadcasted_iota(jnp.int32, (rb, W_B), 0)
            t_prev = jnp.where(rid == 0, t_up[CONV_PAD - 1:CONV_PAD, :], pltpu.roll(t_cur, 1, 0))
            t_next = jnp.where(rid == rb - 1, t_dn[0:1, :], pltpu.roll(t_cur, rb - 1, 0))
            conv = cb_ref[0] + t_prev * cw_ref[0, 0:1, :]
            conv = conv + t_cur * cw_ref[0, 1:2, :]
            conv = conv + t_next * cw_ref[0, 2:3, :]
            b_out = gate_b * conv * _silu(g_b)

            y = (_dot(a_out.astype(BF16), wout_ref[0, 0:W_A, :])
                 + _dot(b_out.astype(BF16), wout_ref[0, W_A:W_A + W_B, :]))
            res = x_ref[sq, rows, :] + mod_ref[0, 0, 2:3, :] * y
            if final:
                res = _rms(res, fg_ref[...])
            o_ref[sq, rows, :] = res

        _row_loop(n, rb, phase3)

    _loop(n_seq, sequence)


def _odd_kernel(*refs, n, n_ctx, n_seq, final, fresh, layer):
    rope = n_ctx > 0
    it = iter(refs)
    x_ref, mod_ref, ng_ref = next(it), next(it), next(it)
    win_ref, wout_ref = next(it), next(it)
    sink_ref, gg_ref, ws_ref, bs_ref = next(it), next(it), next(it), next(it)
    if rope:
        kc_ref, vc_ref, tab_ref = next(it), next(it), next(it)
    if final:
        fg_ref = next(it)
    if not rope and not fresh:
        next(it), next(it)
    o_ref = next(it)
    if not rope:
        k_out, v_out = next(it), next(it)
    h_s, kloc_s, vloc_s, s_s = next(it), next(it), next(it), next(it)
    if rope:
        kctx_s, vctx_s = next(it), next(it)

    pad = WINDOW if rope else 0
    rb = _row_block(n)
    tq = min(rb, WINDOW_ROWS)
    span = tq + 2 * pad
    kw, vw = LANES, 2 * LANES
    q0, k0, g0 = 0, W_C, W_C + 2 * KV_C * HD_C
    q_scale = np.float32(HD_C ** -0.5 * LOG2E)

    def store_keys(dst, rows, x):
        a, b = _dup_halves(x)
        dst[rows, 0:LANES] = a.astype(BF16)
        dst[rows, LANES:2 * LANES] = b.astype(BF16)

    def store_values(dst, rows, x):
        a, b = _dup_halves(x)
        ones = jnp.ones(x.shape, BF16)
        dst[rows, 0:LANES] = a.astype(BF16)
        dst[rows, LANES:2 * LANES] = ones
        dst[rows, 2 * LANES:3 * LANES] = b.astype(BF16)
        dst[rows, 3 * LANES:4 * LANES] = ones

    def sequence(sq):
        if rope:
            kloc_s[0:pad, :] = jnp.zeros((pad, KV_C * kw), BF16)
            kloc_s[pad + n:2 * pad + n, :] = jnp.zeros((pad, KV_C * kw), BF16)
            vloc_s[0:pad, :] = jnp.zeros((pad, KV_C * vw), BF16)
            vloc_s[pad + n:2 * pad + n, :] = jnp.zeros((pad, KV_C * vw), BF16)
            store_keys(kctx_s, slice(None), kc_ref[0, 0].T)
            store_values(vctx_s, slice(None), vc_ref[0, 0].T)

        def phase1(r0):
            rows = pl.ds(r0, rb)
            hb = _modulated_norm(x_ref[sq, rows, :], mod_ref, ng_ref).astype(BF16)
            h_s[rows, :] = hb
            p1 = _dot(hb, win_ref[0, :, k0:g0])
            k = p1[:, 0:LANES]
            v = p1[:, LANES:2 * LANES]
            if rope:
                k = _rope(k, tab_ref[rows, :], HD_C // 4)
            else:
                _store_cache(k_out, sq, k.T)
                _store_cache(v_out, sq, v.T)
            lrows = pl.ds(pad + r0, rb)
            store_keys(kloc_s, lrows, k)
            store_values(vloc_s, lrows, v)

        _row_loop(n, rb, phase1)

        def phase3(r0):
            rows = pl.ds(r0, rb)
            hb = h_s[rows, :]
            pq = _dot(hb, win_ref[0, :, q0:k0])

            tiles = []
            for t in range(rb // tq):
                t0 = r0 + t * tq
                pqt = pq[t * tq:(t + 1) * tq, :]
                if rope:
                    tab = tab_ref[pl.ds(t0, tq), :]
                    kl = kloc_s[pl.ds(t0, span), :]
                    vl = vloc_s[pl.ds(t0, span), :]
                    ii = lax.broadcasted_iota(jnp.int32, (tq, span), 0)
                    jj = lax.broadcasted_iota(jnp.int32, (tq, span), 1)
                    kpos = jj + (t0 - WINDOW)
                    dlt = jj - ii
                    mask = (dlt >= 0) & (dlt <= 2 * WINDOW) & (kpos >= 0) & (kpos < n)
                else:
                    kl, vl, mask = kloc_s[...], vloc_s[...], None
                qhs = []
                for m in range(H_C // 2):
                    qb = pqt[:, m * LANES:(m + 1) * LANES]
                    if rope:
                        qb = _rope(qb, tab, HD_C // 4)
                    qb = qb * q_scale
                    lo = _lane(qb.shape) < LANES // 2
                    qhs.append(jnp.where(lo, qb, 0.0).astype(BF16))
                    qhs.append(jnp.where(lo, 0.0, qb).astype(BF16))
                tiles.append((kl, vl, mask, qhs))

            def kv_head(h):
                return h // (H_C // KV_C)

            def scores(i):
                (kl, _, _, qhs), h = tiles[i // H_C], i % H_C
                j = kv_head(h)
                if rope:
                    s_s[i % 2, :, 0:n_ctx] = _dot_t(qhs[h], kctx_s[:, j * kw:(j + 1) * kw])
                    s_s[i % 2, :, n_ctx:] = _dot_t(qhs[h], kl[:, j * kw:(j + 1) * kw])
                else:
                    s_s[i % 2] = _dot_t(qhs[h], kl[:, j * kw:(j + 1) * kw])

            def attend(i):
                (_, vl, mask, _), h = tiles[i // H_C], i % H_C
                j = kv_head(h)
                sk = sink_ref[layer, h] * np.float32(LOG2E)
                if rope:
                    sc = s_s[i % 2, :, 0:n_ctx]
                    sl = jnp.where(mask, s_s[i % 2, :, n_ctx:], NEG_INF)
                    mx = jnp.maximum(jnp.max(sl, axis=-1, keepdims=True), jnp.max(sc, axis=-1, keepdims=True))
                else:
                    sl = s_s[i % 2]
                    mx = jnp.max(sl, axis=-1, keepdims=True)
                mx = jnp.maximum(mx, sk)
                o = _dot(jnp.exp2(sl - mx).astype(BF16), vl[:, j * vw:(j + 1) * vw])
                if rope:
                    o = o + _dot(jnp.exp2(sc - mx).astype(BF16), vctx_s[:, j * vw:(j + 1) * vw])
                return o[:, 0:LANES] * (1.0 / (o[:, LANES:2 * LANES] + jnp.exp2(sk - mx)))

            heads = _pipelined(len(tiles) * H_C, scores, attend)
            attn = jnp.concatenate(
                [jnp.concatenate([_pair_select(heads[t * H_C + 2 * m], heads[t * H_C + 2 * m + 1])
                                  for m in range(H_C // 2)], axis=-1) for t in range(len(tiles))], axis=0)
            pg = _dot(hb, win_ref[0, :, g0:g0 + W_C + 3 * W_D])
            c_out = attn * _silu(pg[:, 0:W_C])
            u = pg[:, W_C:W_C + W_D]
            vv = pg[:, W_C + W_D:W_C + 2 * W_D]
            g_d = pg[:, W_C + 2 * W_D:W_C + 3 * W_D]

            vvn = _rms(_gelu(vv), gg_ref[0]).astype(BF16)
            chunks = []
            for c in range(rb // CHUNK):
                cols = []
                for jp in range(G_D // 2):
                    vp = vvn[c * CHUNK:(c + 1) * CHUNK, jp * LANES:(jp + 1) * LANES]
                    o = _dot(ws_ref[0, jp], vp)
                    cols.append(_pair_select(o[0:CHUNK, :], o[CHUNK:2 * CHUNK, :]))
                chunks.append(jnp.concatenate(cols, axis=-1) + bs_ref[0])
            mixed = jnp.concatenate(chunks, axis=0)
            d_out = _gelu(u) * mixed * _silu(g_d)

            y = (_dot(c_out.astype(BF16), wout_ref[0, 0:W_C, :])
                 + _dot(d_out.astype(BF16), wout_ref[0, W_C:W_C + W_D, :]))
            res = x_ref[sq, rows, :] + mod_ref[0, 0, 2:3, :] * y
            if final:
                res = _rms(res, fg_ref[...])
            o_ref[sq, rows, :] = res

        _row_loop(n, rb, phase3)

    _loop(n_seq, sequence)


def _mod_kernel(c_ref, w_ref, b_ref, o_ref):
    a = _silu(c_ref[...]).astype(BF16)
    o_ref[0] = _dot(a, w_ref[0].astype(BF16)) + b_ref[0]


def _modulation(cvec, ada_w, ada_b):
    rows = cvec.shape[0]
    tn = D_MODEL
    return pl.pallas_call(
        _mod_kernel,
        grid=(DEPTH, 3 * D_MODEL // tn),
        in_specs=[
            pl.BlockSpec((rows, D_MODEL), lambda l, j: (0, 0)),
            pl.BlockSpec((1, D_MODEL, tn), lambda l, j: (l, 0, j)),
            pl.BlockSpec((1, 1, tn), lambda l, j: (l, 0, j)),
        ],
        out_specs=pl.BlockSpec((1, rows, tn), lambda l, j: (l, 0, j)),
        out_shape=jax.ShapeDtypeStruct((DEPTH, rows, 3 * D_MODEL), F32),
        compiler_params=pltpu.CompilerParams(dimension_semantics=("arbitrary", "arbitrary")),
        name="modulation",
    )(cvec, ada_w, ada_b.reshape(DEPTH, 1, 3 * D_MODEL))


def _split_even_kernel(wt_ref, head_ref, main_ref):
    kv_end = Q_RANK + KV_RANK

    def put(dst, col, rows_f32):
        dst[0, :, col:col + LANES] = rows_f32.T.astype(BF16)

    for j in range(kv_end // LANES):
        put(head_ref, j * LANES, wt_ref[0, j * LANES:(j + 1) * LANES, :])
    k_r = wt_ref[0, kv_end:EVEN_HEAD, :]
    put(head_ref, kv_end, jnp.concatenate(
        [jnp.zeros((NOPE_A, D_MODEL), F32), k_r, jnp.zeros((LANES - QK_A, D_MODEL), F32)], axis=0))
    for j in range(main_ref.shape[2] // LANES):
        put(main_ref, j * LANES, wt_ref[0, EVEN_HEAD + j * LANES:EVEN_HEAD + (j + 1) * LANES, :])


def _split_even(w_in_t):
    n_layers, cols, d = w_in_t.shape
    n_head, n_main = Q_RANK + KV_RANK + LANES, cols - EVEN_HEAD
    return pl.pallas_call(
        _split_even_kernel,
        grid=(n_layers,),
        in_specs=[pl.BlockSpec((1, cols, d), lambda l: (l, 0, 0))],
        out_specs=[pl.BlockSpec((1, d, n_head), lambda l: (l, 0, 0)),
                   pl.BlockSpec((1, d, n_main), lambda l: (l, 0, 0))],
        out_shape=[jax.ShapeDtypeStruct((n_layers, d, n_head), BF16),
                   jax.ShapeDtypeStruct((n_layers, d, n_main), BF16)],
        compiler_params=pltpu.CompilerParams(dimension_semantics=("arbitrary",), vmem_limit_bytes=VMEM_LIMIT),
        name="split_even_w_in",
    )(w_in_t)


def _full(a):
    nd = a.ndim
    return a, pl.BlockSpec(a.shape, lambda b, _nd=nd: (0,) * _nd)


def _per_step(a, n_seq):
    nd = a.ndim
    return a, pl.BlockSpec((n_seq,) + a.shape[1:], lambda b, _nd=nd: (b,) + (0,) * (_nd - 1))


def _of_layer(a, i):
    nd = a.ndim
    return a, pl.BlockSpec((1,) + a.shape[1:], lambda b, _nd=nd, _i=i: (_i,) + (0,) * (_nd - 1))


def _seq_of_layer(a, i):
    nd = a.ndim
    return a, pl.BlockSpec((1, 1) + a.shape[2:], lambda b, _nd=nd, _i=i: (b, _i) + (0,) * (_nd - 2))


def _mod_spec(mod, l, row):
    if row is None:
        return mod, pl.BlockSpec((1, 1, 3, D_MODEL), lambda b, _l=l: (_l, b, 0, 0))
    return mod, pl.BlockSpec((1, 1, 3, D_MODEL), lambda b, _l=l, _r=row: (_l, _r, 0, 0))


def _rope_cs(n, rot_dim):
    rows = n // GRID_W
    r = np.repeat(np.arange(rows, dtype=np.float32), GRID_W)
    col = np.tile(np.arange(GRID_W, dtype=np.float32), rows)
    nf = rot_dim // 4
    inv = np.float32(ROPE_BASE) ** (-np.arange(nf, dtype=np.float32) / np.float32(nf))
    ar = r[:, None] * inv
    ac = col[:, None] * inv
    cos = np.concatenate([np.cos(ar), np.cos(ar), np.cos(ac), np.cos(ac)], axis=-1)
    sin = np.concatenate([-np.sin(ar), np.sin(ar), -np.sin(ac), np.sin(ac)], axis=-1)
    return cos.astype(np.float32), sin.astype(np.float32)


def _layer_call(body, x, n_seq, operands, caches, slot, scratch, name):
    b, n, _ = x.shape
    fresh = slot == 0
    shapes = [c if fresh else c.shape for c in caches]
    in_specs = [spec for _, spec in operands]
    args = [a for a, _ in operands]
    aliases = {}
    if not fresh:
        in_specs += [pl.BlockSpec(memory_space=pl.ANY) for _ in caches]
        args += list(caches)
        aliases = {len(operands) + k: 1 + k for k in range(len(caches))}
    out_shape = [jax.ShapeDtypeStruct(x.shape, F32)] + [jax.ShapeDtypeStruct(s, F32) for s in shapes]
    out_specs = [pl.BlockSpec((n_seq, n, D_MODEL), lambda i: (i, 0, 0))]
    for s in shapes:
        if fresh:
            out_specs.append(pl.BlockSpec((n_seq,) + tuple(s[1:]), lambda i: (i, 0, 0, 0)))
        else:
            out_specs.append(pl.BlockSpec((n_seq, 1) + tuple(s[2:]), lambda i, _s=slot: (i, _s, 0, 0)))
    return pl.pallas_call(
        body,
        grid=(b // n_seq,),
        in_specs=in_specs,
        out_specs=out_specs,
        out_shape=out_shape,
        input_output_aliases=aliases,
        scratch_shapes=scratch,
        compiler_params=pltpu.CompilerParams(dimension_semantics=("arbitrary",), vmem_limit_bytes=VMEM_LIMIT),
        name=name,
    )(*args)


def _store_cache(ref, sq, value):
    ref[sq, 0] = value
    for l in range(1, ref.shape[1]):
        ref[sq, l] = jnp.zeros(value.shape, value.dtype)


def _seqs_per_step(b, ctx):
    return 1 if ctx is not None else int(np.gcd(b, CTX_SEQS_PER_STEP))


def _even_layer(x, mod_op, i, p, ctx, tab, caches, final_g, name):
    b, n, _ = x.shape
    n_ctx = 0 if ctx is None else ctx[0].shape[2]
    n_seq = _seqs_per_step(b, ctx)
    ops = [_per_step(x, n_seq), mod_op, _of_layer(p["norm_g"], 2 * i),
           _of_layer(p["w_head"], i), _of_layer(p["w_main"], i), _of_layer(p["w_uq"], i), _of_layer(p["w_uk"], i),
           _of_layer(p["w_uv"], i), _of_layer(p["ev_w_out"], i),
           _of_layer(p["gq"], i), _of_layer(p["gkv"], i), _of_layer(p["conv_w"], i), _of_layer(p["conv_b"], i)]
    if ctx is not None:
        ops += [_seq_of_layer(ctx[0], i), _per_step(ctx[1][i], 1), _full(tab)]
    if final_g is not None:
        ops.append(_full(final_g))
    nk = n_ctx + n
    scratch = [pltpu.VMEM((n, D_MODEL), BF16), pltpu.VMEM((n, H_A * LANES), BF16),
               pltpu.VMEM((nk, LANES), BF16), pltpu.VMEM((nk, LANES), BF16),
               pltpu.VMEM((nk, H_A * LANES), BF16), pltpu.VMEM((nk, 2 * W_A), BF16),
               pltpu.VMEM((n + 2 * CONV_PAD, W_B), F32), pltpu.VMEM((2, _row_block(n), nk), F32)]
    assert ctx is not None or n == _row_block(n), "the context pass writes one cache block per sequence"
    body = functools.partial(_even_kernel, n=n, n_ctx=n_ctx, n_seq=n_seq, final=final_g is not None, fresh=i == 0)
    return _layer_call(body, x, n_seq, ops, caches, i, scratch, name)


def _odd_layer(x, mod_op, i, p, ctx, tab, caches, final_g, name):
    b, n, _ = x.shape
    n_ctx = 0 if ctx is None else ctx[0].shape[3]
    n_seq = _seqs_per_step(b, ctx)
    ops = [_per_step(x, n_seq), mod_op, _of_layer(p["norm_g"], 2 * i + 1),
           _of_layer(p["od_w_in"], i), _of_layer(p["od_w_out"], i),
           (p["sink"], pl.BlockSpec(memory_space=pltpu.SMEM)), _of_layer(p["gg"], i), _of_layer(p["ws"], i),
           _of_layer(p["bs"], i)]
    if ctx is not None:
        ops += [_seq_of_layer(ctx[0], i), _seq_of_layer(ctx[1], i), _full(tab)]
    if final_g is not None:
        ops.append(_full(final_g))
    pad = WINDOW if ctx is not None else 0
    tq = min(_row_block(n), WINDOW_ROWS)
    n_keys = n_ctx + tq + 2 * pad if ctx is not None else n
    scratch = [pltpu.VMEM((n, D_MODEL), BF16), pltpu.VMEM((n + 2 * pad, KV_C * LANES), BF16),
               pltpu.VMEM((n + 2 * pad, KV_C * 2 * LANES), BF16), pltpu.VMEM((2, tq, n_keys), F32)]
    if ctx is not None:
        scratch += [pltpu.VMEM((n_ctx, KV_C * LANES), BF16), pltpu.VMEM((n_ctx, KV_C * 2 * LANES), BF16)]
    assert ctx is not None or n == _row_block(n), "the context pass writes one cache block per sequence"
    body = functools.partial(_odd_kernel, n=n, n_ctx=n_ctx, n_seq=n_seq, final=final_g is not None, fresh=i == 0,
                             layer=i)
    return _layer_call(body, x, n_seq, ops, caches, i, scratch, name)


def kernel(x_prompt, x_sample, cache_mla_ckv, cache_mla_krope, cache_swa_k, cache_swa_v, c, c_ctx, ada_w, ada_b, norm_g, ev_w_in, ev_w_out, mla_gq, mla_gkv, mla_w_uq, mla_w_uk, mla_w_uv, conv_w, conv_b, od_w_in, od_w_out, swa_sink, gmlp_g, gmlp_ws, gmlp_bs, final_g):
    bp, n_p, _ = x_prompt.shape
    bs_, n_s, _ = x_sample.shape
    n_past = cache_mla_ckv.shape[2]
    n_even, n_odd = ev_w_in.shape[0], od_w_in.shape[0]

    n_rows = -(-(bs_ + 1) // 8) * 8
    cvec = jnp.concatenate([c, c_ctx[None, :], jnp.zeros((n_rows - bs_ - 1, D_MODEL), F32)], axis=0)
    mod = _modulation(cvec, ada_w, ada_b).reshape(DEPTH, n_rows, 3, D_MODEL)

    cos_a, sin_a = _rope_cs(n_s, ROPE_A)
    cos_c, sin_c = _rope_cs(n_s, HD_C)
    zero = lambda w: np.zeros((n_s, w), np.float32)
    tab_a = jnp.asarray(np.concatenate([np.ones((n_s, NOPE_A), np.float32), cos_a, zero(LANES - QK_A),
                                        zero(NOPE_A), sin_a, zero(LANES - QK_A)], axis=1))
    tab_c = jnp.asarray(np.concatenate([cos_c, cos_c, sin_c, sin_c], axis=1))

    uq = mla_w_uq.reshape(n_even, Q_RANK, H_A, QK_A)
    uk = mla_w_uk.reshape(n_even, KV_RANK, H_A, NOPE_A)
    w_head, w_main = _split_even(jnp.swapaxes(ev_w_in, 1, 2))
    p = {
        "norm_g": norm_g.reshape(DEPTH, 1, D_MODEL),
        "w_head": w_head,
        "w_main": w_main,
        "w_uq": jnp.pad(uq, ((0, 0), (0, 0), (0, 0), (0, LANES - QK_A))).reshape(n_even, Q_RANK, H_A * LANES).astype(BF16),
        "w_uk": jnp.pad(uk, ((0, 0), (0, 0), (0, 0), (0, LANES - NOPE_A))).reshape(n_even, KV_RANK, H_A * LANES).astype(BF16),
        "w_uv": mla_w_uv.astype(BF16),
        "ev_w_out": ev_w_out.astype(BF16),
        "gq": mla_gq.reshape(n_even, 1, Q_RANK),
        "gkv": mla_gkv.reshape(n_even, 1, KV_RANK),
        "conv_w": conv_w,
        "conv_b": conv_b.reshape(n_even, 1, W_B),
        "od_w_in": od_w_in.astype(BF16),
        "od_w_out": od_w_out.astype(BF16),
        "sink": swa_sink,
        "gg": gmlp_g.reshape(n_odd, 1, W_D),
        "ws": gmlp_ws.reshape(n_odd, G_D // 2, 2 * CHUNK, CHUNK).astype(BF16),
        "bs": jnp.repeat(jnp.swapaxes(gmlp_bs, 1, 2), DG_D, axis=2),
    }
    ckr = jnp.pad(cache_mla_krope, ((0, 0), (0, 0), (0, 0), (NOPE_A, LANES - QK_A)))
    ckr = [ckr[:, i] for i in range(n_even)]
    swa_k = jnp.transpose(cache_swa_k, (0, 1, 3, 4, 2)).reshape(bs_, n_odd, KV_C * HD_C, n_past)
    swa_v = jnp.transpose(cache_swa_v, (0, 1, 3, 4, 2)).reshape(bs_, n_odd, KV_C * HD_C, n_past)

    new_ckv = (bp, n_even, n_p, KV_RANK)
    new_kr = (bp, n_even, ROPE_A, n_p)
    new_k = (bp, n_odd, KV_C * HD_C, n_p)
    new_v = (bp, n_odd, KV_C * HD_C, n_p)

    fg = final_g.reshape(1, D_MODEL)
    xp, xs = x_prompt, x_sample
    for l in range(DEPTH):
        i = l // 2
        last = fg if l == DEPTH - 1 else None
        mod_p = _mod_spec(mod, l, bs_)
        mod_s = _mod_spec(mod, l, None)
        if l % 2 == 0:
            xp, new_ckv, new_kr = _even_layer(xp, mod_p, i, p, None, None, (new_ckv, new_kr), last, f"even{l}_ctx")
            (xs,) = _even_layer(xs, mod_s, i, p, (cache_mla_ckv, ckr), tab_a, (), last, f"even{l}_lat")
        else:
            xp, new_k, new_v = _odd_layer(xp, mod_p, i, p, None, None, (new_k, new_v), last, f"odd{l}_ctx")
            (xs,) = _odd_layer(xs, mod_s, i, p, (swa_k, swa_v), tab_c, (), last, f"odd{l}_lat")

    def heads_last(a):
        return jnp.transpose(a.reshape(bp, n_odd, KV_C, HD_C, n_p), (0, 1, 4, 2, 3))

    return (xp, xs, new_ckv, jnp.swapaxes(new_kr, 2, 3), heads_last(new_k), heads_last(new_v))
```

```python
import functools

import numpy as np
import jax
import jax.numpy as jnp
from jax import lax
from jax.experimental import pallas as pl
from jax.experimental.pallas import tpu as pltpu

D_MODEL = 1024
DEPTH = 4
GRID_W = 64
ROPE_BASE = 10000.0
EPS = 1e-6
NEG_INF = -1e30

H_A = 8
NOPE_A = 64
ROPE_A = 32
V_A = 64
Q_RANK = 256
KV_RANK = 128
QK_A = NOPE_A + ROPE_A
W_A = H_A * V_A
W_B = 512
CONV_W = 3
H_C = 8
KV_C = 2
HD_C = 64
WINDOW = 128
W_C = H_C * HD_C
G_D = 8
W_D = 512
CHUNK = 128
DG_D = W_D // G_D

EVEN_SPLIT = (Q_RANK, KV_RANK, ROPE_A, W_A, W_B, W_B, W_B, W_B)
ODD_SPLIT = (W_C, KV_C * HD_C, KV_C * HD_C, W_C, W_D, W_D, W_D)
EVEN_HEAD = Q_RANK + KV_RANK + ROPE_A

LANES = 128
MAX_ROWS = 512
WINDOW_ROWS = 256
CONV_PAD = 8
CTX_SEQS_PER_STEP = 4
VMEM_LIMIT = 60 * 1024 * 1024

F32 = jnp.float32
BF16 = jnp.bfloat16
LOG2E = float(np.log2(np.e))


def _dot(a, b):
    return jnp.dot(a, b, preferred_element_type=F32)


def _dot_t(a, b):
    return lax.dot_general(a, b, (((1,), (1,)), ((), ())), preferred_element_type=F32)


def _rms(x, g):
    return x * lax.rsqrt(jnp.mean(x * x, axis=-1, keepdims=True) + EPS) * g


def _silu(x):
    return x * (1.0 / (1.0 + jnp.exp(-x)))


def _gelu(x):
    c = np.float32(np.sqrt(2.0 / np.pi))
    return 0.5 * x * (1.0 + jnp.tanh(c * (x + 0.044715 * (x * x * x))))


def _lane(shape):
    return lax.broadcasted_iota(jnp.int32, shape, 1)


def _pair_select(lo, hi):
    return jnp.where(_lane(lo.shape) < LANES // 2, lo, hi)


def _rope(x, tab, quarter):
    fwd = pltpu.roll(x, LANES - quarter, 1)
    bwd = pltpu.roll(x, quarter, 1)
    partner = jnp.where((_lane(x.shape) & quarter) == 0, fwd, bwd)
    return x * tab[:, 0:LANES] + partner * tab[:, LANES:2 * LANES]


def _dup_halves(x):
    sw = pltpu.roll(x, LANES // 2, 1)
    lo = _lane(x.shape) < LANES // 2
    return jnp.where(lo, x, sw), jnp.where(lo, sw, x)


def _modulated_norm(x, mod_ref, ng_ref):
    shift = mod_ref[0, 0, 0:1, :]
    scale = mod_ref[0, 0, 1:2, :]
    return _rms(x, ng_ref[0]) * (1.0 + scale) + shift


def _pipelined(n, produce, consume):
    produce(0)
    outs = []
    for i in range(n):
        if i + 1 < n:
            produce(i + 1)
        outs.append(consume(i))
    return outs


def _normalised(o):
    return o[:, 0:LANES] * (1.0 / o[:, LANES:2 * LANES])


def _loop(n, body):
    if n == 1:
        body(0)
    else:
        def step(i, carry):
            body(i)
            return carry
        lax.fori_loop(0, n, step, 0)


def _grouping(n, n_seq):
    gs = max(1, min(n_seq, MAX_ROWS // n))
    rows_g = gs * n
    return gs, rows_g, min(rows_g, MAX_ROWS)


def _group_rows(ref, g, gs, r0, rb):
    if gs == 1:
        return ref[g, pl.ds(r0, rb), :]
    return ref[pl.ds(g * gs, gs), :, :].reshape(rb, ref.shape[-1])


def _set_group_rows(ref, g, gs, r0, rb, value):
    if gs == 1:
        ref[g, pl.ds(r0, rb), :] = value
    else:
        ref[pl.ds(g * gs, gs), :, :] = value.reshape(gs, rb // gs, ref.shape[-1])


def _row_loop(n_rows, rb, body):
    _loop(n_rows // rb, lambda i: body(i * rb if isinstance(i, int) else pl.multiple_of(i * rb, rb)))


def _even_kernel(*refs, n, n_ctx, n_seq, final, fresh):
    rope = n_ctx > 0
    it = iter(refs)
    x_ref, mod_ref, ng_ref = next(it), next(it), next(it)
    wh_ref, wm_ref, wuq_ref, wuk_ref, wuv_ref, wout_ref = (next(it) for _ in range(6))
    gq_ref, gkv_ref, cw_ref, cb_ref = next(it), next(it), next(it), next(it)
    if rope:
        cckv_ref, ckr_ref, tab_ref = next(it), next(it), next(it)
    if final:
        fg_ref = next(it)
    if not rope and not fresh:
        next(it), next(it)
    o_ref = next(it)
    if not rope:
        ckv_out, kr_out = next(it), next(it)
    h_s, q_s, ckv_s, kr_s, k_s, v_s, t_s, s_s = (next(it) for _ in range(8))

    gs, rows_g, rb = _grouping(n, n_seq)
    nk = n_ctx + n
    tq = min(rb, n)
    q_scale = np.float32(QK_A ** -0.5 * LOG2E)

    zpad = jnp.zeros((CONV_PAD, W_B), F32)
    t_s[0:CONV_PAD, :] = zpad
    t_s[CONV_PAD + rows_g:2 * CONV_PAD + rows_g, :] = zpad

    def group(g):
        if rope:
            ckv_s[0:n_ctx, :] = cckv_ref[0, 0].astype(BF16)
            kr_s[0:n_ctx, :] = ckr_ref[0].astype(BF16)

        def phase1(r0):
            rows = pl.ds(r0, rb)
            hb = _modulated_norm(_group_rows(x_ref, g, gs, r0, rb), mod_ref, ng_ref).astype(BF16)
            h_s[rows, :] = hb
            ph = _dot(hb, wh_ref[0])
            qn = _rms(ph[:, 0:Q_RANK], gq_ref[0]).astype(BF16)
            qq = _dot(qn, wuq_ref[0])
            ckv = _rms(ph[:, Q_RANK:Q_RANK + KV_RANK], gkv_ref[0])
            kr = ph[:, Q_RANK + KV_RANK:Q_RANK + KV_RANK + LANES]
            if rope:
                tab = tab_ref[rows, :]
                kr = _rope(kr, tab, ROPE_A // 4)
                for h in range(H_A):
                    qh = _rope(qq[:, h * LANES:(h + 1) * LANES], tab, ROPE_A // 4)
                    q_s[rows, h * LANES:(h + 1) * LANES] = (qh * q_scale).astype(BF16)
            else:
                q_s[rows, :] = (qq * q_scale).astype(BF16)
                for s in range(gs):
                    _store_cache(ckv_out, g * gs + s, ckv[s * n:(s + 1) * n, :])
                    _store_cache(kr_out, g * gs + s, kr[s * n:(s + 1) * n, :].T[NOPE_A:QK_A, :])
            krows = pl.ds(n_ctx + r0, rb)
            ckv_s[krows, :] = ckv.astype(BF16)
            kr_s[krows, :] = kr.astype(BF16)
            pc = _dot(hb, wm_ref[0, :, 2 * W_B:4 * W_B])
            t_s[pl.ds(CONV_PAD + r0, rb), :] = pc[:, 0:W_B] * pc[:, W_B:2 * W_B]

        _row_loop(rows_g, rb, phase1)

        def phase2(r0):
            rows = pl.ds(r0, rb)
            c = ckv_s[rows, :]
            kn = _dot(c, wuk_ref[0])
            krf = kr_s[rows, :].astype(F32)
            for h in range(H_A):
                k_s[rows, h * LANES:(h + 1) * LANES] = (kn[:, h * LANES:(h + 1) * LANES] + krf).astype(BF16)
            v = _dot(c, wuv_ref[0]).astype(BF16)
            ones = jnp.ones((rb, LANES), BF16)
            for j in range(H_A // 2):
                v_s[rows, 2 * j * LANES:(2 * j + 1) * LANES] = v[:, j * LANES:(j + 1) * LANES]
                v_s[rows, (2 * j + 1) * LANES:(2 * j + 2) * LANES] = ones

        _row_loop(n_ctx + rows_g, rb, phase2)

        def phase3(r0):
            rows = pl.ds(r0, rb)
            hb = h_s[rows, :]

            def keys_of(t):
                return slice(None) if gs == 1 else slice(t * n, (t + 1) * n)

            def scores(i):
                t, h = divmod(i, H_A)
                s_s[i % 2] = _dot_t(q_s[pl.ds(r0 + t * tq, tq), h * LANES:(h + 1) * LANES],
                                    k_s[keys_of(t), h * LANES:(h + 1) * LANES])

            def attend(i):
                t, h = divmod(i, H_A)
                s = s_s[i % 2]
                e = jnp.exp2(s - jnp.max(s, axis=-1, keepdims=True)).astype(BF16)
                j = h // 2
                return _normalised(_dot(e, v_s[keys_of(t), 2 * j * LANES:(2 * j + 2) * LANES]))

            heads = _pipelined((rb // tq) * H_A, scores, attend)
            attn = jnp.concatenate(
                [jnp.concatenate([_pair_select(heads[t * H_A + 2 * j], heads[t * H_A + 2 * j + 1])
                                  for j in range(H_A // 2)], axis=-1) for t in range(rb // tq)], axis=0)
            pa = _dot(hb, wm_ref[0, :, 0:2 * W_B])
            a_out = attn * _silu(pa[:, 0:W_A])
            gate_b = pa[:, W_A:W_A + W_B]
            g_b = _dot(hb, wm_ref[0, :, 4 * W_B:5 * W_B])

            t_cur = t_s[pl.ds(CONV_PAD + r0, rb), :]
            rid = lax.broadcasted_iota(jnp.int32, (rb, W_B), 0)
            if n <= rb:
                t_prev = jnp.where(rid % n == 0, 0.0, pltpu.roll(t_cur, 1, 0))
                t_next = jnp.where(rid % n == n - 1, 0.0, pltpu.roll(t_cur, rb - 1, 0))
            else:
                t_up = t_s[pl.ds(r0, CONV_PAD), :]
                t_dn = t_s[pl.ds(CONV_PAD + r0 + rb, CONV_PAD), :]
                t_prev = jnp.where(rid == 0, t_up[CONV_PAD - 1:CONV_PAD, :], pltpu.roll(t_cur, 1, 0))
                t_next = jnp.where(rid == rb - 1, t_dn[0:1, :], pltpu.roll(t_cur, rb - 1, 0))
            conv = cb_ref[0] + t_prev * cw_ref[0, 0:1, :]
            conv = conv + t_cur * cw_ref[0, 1:2, :]
            conv = conv + t_next * cw_ref[0, 2:3, :]
            b_out = gate_b * conv * _silu(g_b)

            y = (_dot(a_out.astype(BF16), wout_ref[0, 0:W_A, :])
                 + _dot(b_out.astype(BF16), wout_ref[0, W_A:W_A + W_B, :]))
            res = _group_rows(x_ref, g, gs, r0, rb) + mod_ref[0, 0, 2:3, :] * y
            if final:
                res = _rms(res, fg_ref[...])
            _set_group_rows(o_ref, g, gs, r0, rb, res)

        _row_loop(rows_g, rb, phase3)

    _loop(n_seq // gs, group)


def _odd_kernel(*refs, n, n_ctx, n_seq, final, fresh, layer):
    rope = n_ctx > 0
    it = iter(refs)
    x_ref, mod_ref, ng_ref = next(it), next(it), next(it)
    win_ref, wout_ref = next(it), next(it)
    sink_ref, gg_ref, ws_ref, bs_ref = next(it), next(it), next(it), next(it)
    if rope:
        kc_ref, vc_ref, tab_ref = next(it), next(it), next(it)
    if final:
        fg_ref = next(it)
    if not rope and not fresh:
        next(it), next(it)
    o_ref = next(it)
    if not rope:
        k_out, v_out = next(it), next(it)
    h_s, kloc_s, vloc_s, s_s = next(it), next(it), next(it), next(it)
    if rope:
        kctx_s, vctx_s = next(it), next(it)

    pad = WINDOW if rope else 0
    gs, rows_g, rb = _grouping(n, n_seq)
    tq = min(rb, n, WINDOW_ROWS)
    span = tq + 2 * pad
    kw, vw = LANES, 2 * LANES
    q0, k0, g0 = 0, W_C, W_C + 2 * KV_C * HD_C
    q_scale = np.float32(HD_C ** -0.5 * LOG2E)

    def store_keys(dst, rows, x):
        a, b = _dup_halves(x)
        dst[rows, 0:LANES] = a.astype(BF16)
        dst[rows, LANES:2 * LANES] = b.astype(BF16)

    def store_values(dst, rows, x):
        a, b = _dup_halves(x)
        ones = jnp.ones(x.shape, BF16)
        dst[rows, 0:LANES] = a.astype(BF16)
        dst[rows, LANES:2 * LANES] = ones
        dst[rows, 2 * LANES:3 * LANES] = b.astype(BF16)
        dst[rows, 3 * LANES:4 * LANES] = ones

    def group(g):
        if rope:
            kloc_s[0:pad, :] = jnp.zeros((pad, KV_C * kw), BF16)
            kloc_s[pad + n:2 * pad + n, :] = jnp.zeros((pad, KV_C * kw), BF16)
            vloc_s[0:pad, :] = jnp.zeros((pad, KV_C * vw), BF16)
            vloc_s[pad + n:2 * pad + n, :] = jnp.zeros((pad, KV_C * vw), BF16)
            store_keys(kctx_s, slice(None), kc_ref[0, 0].T)
            store_values(vctx_s, slice(None), vc_ref[0, 0].T)

        def phase1(r0):
            rows = pl.ds(r0, rb)
            hb = _modulated_norm(_group_rows(x_ref, g, gs, r0, rb), mod_ref, ng_ref).astype(BF16)
            h_s[rows, :] = hb
            p1 = _dot(hb, win_ref[0, :, k0:g0])
            k = p1[:, 0:LANES]
            v = p1[:, LANES:2 * LANES]
            if rope:
                k = _rope(k, tab_ref[rows, :], HD_C // 4)
            else:
                for s in range(gs):
                    _store_cache(k_out, g * gs + s, k[s * n:(s + 1) * n, :].T)
                    _store_cache(v_out, g * gs + s, v[s * n:(s + 1) * n, :].T)
            lrows = pl.ds(pad + r0, rb)
            store_keys(kloc_s, lrows, k)
            store_values(vloc_s, lrows, v)

        _row_loop(rows_g, rb, phase1)

        def phase3(r0):
            rows = pl.ds(r0, rb)
            hb = h_s[rows, :]
            pq = _dot(hb, win_ref[0, :, q0:k0])

            tiles = []
            for t in range(rb // tq):
                t0 = r0 + t * tq
                pqt = pq[t * tq:(t + 1) * tq, :]
                if rope:
                    tab = tab_ref[pl.ds(t0, tq), :]
                    kl = kloc_s[pl.ds(t0, span), :]
                    vl = vloc_s[pl.ds(t0, span), :]
                    ii = lax.broadcasted_iota(jnp.int32, (tq, span), 0)
                    jj = lax.broadcasted_iota(jnp.int32, (tq, span), 1)
                    kpos = jj + (t0 - WINDOW)
                    dlt = jj - ii
                    mask = (dlt >= 0) & (dlt <= 2 * WINDOW) & (kpos >= 0) & (kpos < n)
                else:
                    kl, vl, mask = kloc_s[t * n:(t + 1) * n, :], vloc_s[t * n:(t + 1) * n, :], None
                qhs = []
                for m in range(H_C // 2):
                    qb = pqt[:, m * LANES:(m + 1) * LANES]
                    if rope:
                        qb = _rope(qb, tab, HD_C // 4)
                    qb = qb * q_scale
                    lo = _lane(qb.shape) < LANES // 2
                    qhs.append(jnp.where(lo, qb, 0.0).astype(BF16))
                    qhs.append(jnp.where(lo, 0.0, qb).astype(BF16))
                tiles.append((kl, vl, mask, qhs))

            def kv_head(h):
                return h // (H_C // KV_C)

            def scores(i):
                (kl, _, _, qhs), h = tiles[i // H_C], i % H_C
                j = kv_head(h)
                if rope:
                    s_s[i % 2, :, 0:n_ctx] = _dot_t(qhs[h], kctx_s[:, j * kw:(j + 1) * kw])
                    s_s[i % 2, :, n_ctx:] = _dot_t(qhs[h], kl[:, j * kw:(j + 1) * kw])
                else:
                    s_s[i % 2] = _dot_t(qhs[h], kl[:, j * kw:(j + 1) * kw])

            def attend(i):
                (_, vl, mask, _), h = tiles[i // H_C], i % H_C
                j = kv_head(h)
                sk = sink_ref[layer, h] * np.float32(LOG2E)
                if rope:
                    sc = s_s[i % 2, :, 0:n_ctx]
                    sl = jnp.where(mask, s_s[i % 2, :, n_ctx:], NEG_INF)
                    mx = jnp.maximum(jnp.max(sl, axis=-1, keepdims=True), jnp.max(sc, axis=-1, keepdims=True))
                else:
                    sl = s_s[i % 2]
                    mx = jnp.max(sl, axis=-1, keepdims=True)
                mx = jnp.maximum(mx, sk)
                o = _dot(jnp.exp2(sl - mx).astype(BF16), vl[:, j * vw:(j + 1) * vw])
                if rope:
                    o = o + _dot(jnp.exp2(sc - mx).astype(BF16), vctx_s[:, j * vw:(j + 1) * vw])
                return o[:, 0:LANES] * (1.0 / (o[:, LANES:2 * LANES] + jnp.exp2(sk - mx)))

            heads = _pipelined(len(tiles) * H_C, scores, attend)
            attn = jnp.concatenate(
                [jnp.concatenate([_pair_select(heads[t * H_C + 2 * m], heads[t * H_C + 2 * m + 1])
                                  for m in range(H_C // 2)], axis=-1) for t in range(len(tiles))], axis=0)
            pg = _dot(hb, win_ref[0, :, g0:g0 + W_C + 3 * W_D])
            c_out = attn * _silu(pg[:, 0:W_C])
            u = pg[:, W_C:W_C + W_D]
            vv = pg[:, W_C + W_D:W_C + 2 * W_D]
            g_d = pg[:, W_C + 2 * W_D:W_C + 3 * W_D]

            vvn = _rms(_gelu(vv), gg_ref[0]).astype(BF16)
            chunks = []
            for c in range(rb // CHUNK):
                cols = []
                for jp in range(G_D // 2):
                    vp = vvn[c * CHUNK:(c + 1) * CHUNK, jp * LANES:(jp + 1) * LANES]
                    o = _dot(ws_ref[0, jp], vp)
                    cols.append(_pair_select(o[0:CHUNK, :], o[CHUNK:2 * CHUNK, :]))
                chunks.append(jnp.concatenate(cols, axis=-1) + bs_ref[0])
            mixed = jnp.concatenate(chunks, axis=0)
            d_out = _gelu(u) * mixed * _silu(g_d)

            y = (_dot(c_out.astype(BF16), wout_ref[0, 0:W_C, :])
                 + _dot(d_out.astype(BF16), wout_ref[0, W_C:W_C + W_D, :]))
            res = _group_rows(x_ref, g, gs, r0, rb) + mod_ref[0, 0, 2:3, :] * y
            if final:
                res = _rms(res, fg_ref[...])
            _set_group_rows(o_ref, g, gs, r0, rb, res)

        _row_loop(rows_g, rb, phase3)

    _loop(n_seq // gs, group)


def _mod_kernel(c_ref, w_ref, b_ref, o_ref):
    a = _silu(c_ref[...]).astype(BF16)
    o_ref[0] = _dot(a, w_ref[0].astype(BF16)) + b_ref[0]


def _modulation(cvec, ada_w, ada_b):
    rows = cvec.shape[0]
    tn = D_MODEL
    return pl.pallas_call(
        _mod_kernel,
        grid=(DEPTH, 3 * D_MODEL // tn),
        in_specs=[
            pl.BlockSpec((rows, D_MODEL), lambda l, j: (0, 0)),
            pl.BlockSpec((1, D_MODEL, tn), lambda l, j: (l, 0, j)),
            pl.BlockSpec((1, 1, tn), lambda l, j: (l, 0, j)),
        ],
        out_specs=pl.BlockSpec((1, rows, tn), lambda l, j: (l, 0, j)),
        out_shape=jax.ShapeDtypeStruct((DEPTH, rows, 3 * D_MODEL), F32),
        compiler_params=pltpu.CompilerParams(dimension_semantics=("arbitrary", "arbitrary")),
        name="modulation",
    )(cvec, ada_w, ada_b.reshape(DEPTH, 1, 3 * D_MODEL))


def _split_even_kernel(wt_ref, head_ref, main_ref):
    kv_end = Q_RANK + KV_RANK

    def put(dst, col, rows_f32):
        dst[0, :, col:col + LANES] = rows_f32.T.astype(BF16)

    for j in range(kv_end // LANES):
        put(head_ref, j * LANES, wt_ref[0, j * LANES:(j + 1) * LANES, :])
    k_r = wt_ref[0, kv_end:EVEN_HEAD, :]
    put(head_ref, kv_end, jnp.concatenate(
        [jnp.zeros((NOPE_A, D_MODEL), F32), k_r, jnp.zeros((LANES - QK_A, D_MODEL), F32)], axis=0))
    for j in range(main_ref.shape[2] // LANES):
        put(main_ref, j * LANES, wt_ref[0, EVEN_HEAD + j * LANES:EVEN_HEAD + (j + 1) * LANES, :])


def _split_even(w_in_t):
    n_layers, cols, d = w_in_t.shape
    n_head, n_main = Q_RANK + KV_RANK + LANES, cols - EVEN_HEAD
    return pl.pallas_call(
        _split_even_kernel,
        grid=(n_layers,),
        in_specs=[pl.BlockSpec((1, cols, d), lambda l: (l, 0, 0))],
        out_specs=[pl.BlockSpec((1, d, n_head), lambda l: (l, 0, 0)),
                   pl.BlockSpec((1, d, n_main), lambda l: (l, 0, 0))],
        out_shape=[jax.ShapeDtypeStruct((n_layers, d, n_head), BF16),
                   jax.ShapeDtypeStruct((n_layers, d, n_main), BF16)],
        compiler_params=pltpu.CompilerParams(dimension_semantics=("arbitrary",), vmem_limit_bytes=VMEM_LIMIT),
        name="split_even_w_in",
    )(w_in_t)


def _full(a):
    nd = a.ndim
    return a, pl.BlockSpec(a.shape, lambda b, _nd=nd: (0,) * _nd)


def _per_step(a, n_seq):
    nd = a.ndim
    return a, pl.BlockSpec((n_seq,) + a.shape[1:], lambda b, _nd=nd: (b,) + (0,) * (_nd - 1))


def _of_layer(a, i):
    nd = a.ndim
    return a, pl.BlockSpec((1,) + a.shape[1:], lambda b, _nd=nd, _i=i: (_i,) + (0,) * (_nd - 1))


def _seq_of_layer(a, i):
    nd = a.ndim
    return a, pl.BlockSpec((1, 1) + a.shape[2:], lambda b, _nd=nd, _i=i: (b, _i) + (0,) * (_nd - 2))


def _mod_spec(mod, l, row):
    if row is None:
        return mod, pl.BlockSpec((1, 1, 3, D_MODEL), lambda b, _l=l: (_l, b, 0, 0))
    return mod, pl.BlockSpec((1, 1, 3, D_MODEL), lambda b, _l=l, _r=row: (_l, _r, 0, 0))


def _rope_cs(n, rot_dim):
    rows = n // GRID_W
    r = np.repeat(np.arange(rows, dtype=np.float32), GRID_W)
    col = np.tile(np.arange(GRID_W, dtype=np.float32), rows)
    nf = rot_dim // 4
    inv = np.float32(ROPE_BASE) ** (-np.arange(nf, dtype=np.float32) / np.float32(nf))
    ar = r[:, None] * inv
    ac = col[:, None] * inv
    cos = np.concatenate([np.cos(ar), np.cos(ar), np.cos(ac), np.cos(ac)], axis=-1)
    sin = np.concatenate([-np.sin(ar), np.sin(ar), -np.sin(ac), np.sin(ac)], axis=-1)
    return cos.astype(np.float32), sin.astype(np.float32)


def _layer_call(body, x, n_seq, operands, caches, slot, scratch, name):
    b, n, _ = x.shape
    fresh = slot == 0
    shapes = [c if fresh else c.shape for c in caches]
    in_specs = [spec for _, spec in operands]
    args = [a for a, _ in operands]
    aliases = {}
    if not fresh:
        in_specs += [pl.BlockSpec(memory_space=pl.ANY) for _ in caches]
        args += list(caches)
        aliases = {len(operands) + k: 1 + k for k in range(len(caches))}
    out_shape = [jax.ShapeDtypeStruct(x.shape, F32)] + [jax.ShapeDtypeStruct(s, F32) for s in shapes]
    out_specs = [pl.BlockSpec((n_seq, n, D_MODEL), lambda i: (i, 0, 0))]
    for s in shapes:
        if fresh:
            out_specs.append(pl.BlockSpec((n_seq,) + tuple(s[1:]), lambda i: (i, 0, 0, 0)))
        else:
            out_specs.append(pl.BlockSpec((n_seq, 1) + tuple(s[2:]), lambda i, _s=slot: (i, _s, 0, 0)))
    return pl.pallas_call(
        body,
        grid=(b // n_seq,),
        in_specs=in_specs,
        out_specs=out_specs,
        out_shape=out_shape,
        input_output_aliases=aliases,
        scratch_shapes=scratch,
        compiler_params=pltpu.CompilerParams(dimension_semantics=("arbitrary",), vmem_limit_bytes=VMEM_LIMIT),
        name=name,
    )(*args)


def _store_cache(ref, sq, value):
    ref[sq, 0] = value
    for l in range(1, ref.shape[1]):
        ref[sq, l] = jnp.zeros(value.shape, value.dtype)


def _seqs_per_step(b, ctx):
    return 1 if ctx is not None else int(np.gcd(b, CTX_SEQS_PER_STEP))


def _even_layer(x, mod_op, i, p, ctx, tab, caches, final_g, name):
    b, n, _ = x.shape
    n_ctx = 0 if ctx is None else ctx[0].shape[2]
    n_seq = _seqs_per_step(b, ctx)
    ops = [_per_step(x, n_seq), mod_op, _of_layer(p["norm_g"], 2 * i),
           _of_layer(p["w_head"], i), _of_layer(p["w_main"], i), _of_layer(p["w_uq"], i), _of_layer(p["w_uk"], i),
           _of_layer(p["w_uv"], i), _of_layer(p["ev_w_out"], i),
           _of_layer(p["gq"], i), _of_layer(p["gkv"], i), _of_layer(p["conv_w"], i), _of_layer(p["conv_b"], i)]
    if ctx is not None:
        ops += [_seq_of_layer(ctx[0], i), _per_step(ctx[1][i], 1), _full(tab)]
    if final_g is not None:
        ops.append(_full(final_g))
    _, rows_g, rb = _grouping(n, n_seq)
    nk_g = n_ctx + rows_g
    scratch = [pltpu.VMEM((rows_g, D_MODEL), BF16), pltpu.VMEM((rows_g, H_A * LANES), BF16),
               pltpu.VMEM((nk_g, LANES), BF16), pltpu.VMEM((nk_g, LANES), BF16),
               pltpu.VMEM((nk_g, H_A * LANES), BF16), pltpu.VMEM((nk_g, 2 * W_A), BF16),
               pltpu.VMEM((rows_g + 2 * CONV_PAD, W_B), F32), pltpu.VMEM((2, min(rb, n), n_ctx + n), F32)]
    assert ctx is not None or n <= rb, "the context pass writes one cache block per sequence"
    body = functools.partial(_even_kernel, n=n, n_ctx=n_ctx, n_seq=n_seq, final=final_g is not None, fresh=i == 0)
    return _layer_call(body, x, n_seq, ops, caches, i, scratch, name)


def _odd_layer(x, mod_op, i, p, ctx, tab, caches, final_g, name):
    b, n, _ = x.shape
    n_ctx = 0 if ctx is None else ctx[0].shape[3]
    n_seq = _seqs_per_step(b, ctx)
    ops = [_per_step(x, n_seq), mod_op, _of_layer(p["norm_g"], 2 * i + 1),
           _of_layer(p["od_w_in"], i), _of_layer(p["od_w_out"], i),
           (p["sink"], pl.BlockSpec(memory_space=pltpu.SMEM)), _of_layer(p["gg"], i), _of_layer(p["ws"], i),
           _of_layer(p["bs"], i)]
    if ctx is not None:
        ops += [_seq_of_layer(ctx[0], i), _seq_of_layer(ctx[1], i), _full(tab)]
    if final_g is not None:
        ops.append(_full(final_g))
    pad = WINDOW if ctx is not None else 0
    _, rows_g, rb = _grouping(n, n_seq)
    tq = min(rb, n, WINDOW_ROWS)
    n_keys = n_ctx + tq + 2 * pad if ctx is not None else n
    scratch = [pltpu.VMEM((rows_g, D_MODEL), BF16), pltpu.VMEM((rows_g + 2 * pad, KV_C * LANES), BF16),
               pltpu.VMEM((rows_g + 2 * pad, KV_C * 2 * LANES), BF16), pltpu.VMEM((2, tq, n_keys), F32)]
    if ctx is not None:
        scratch += [pltpu.VMEM((n_ctx, KV_C * LANES), BF16), pltpu.VMEM((n_ctx, KV_C * 2 * LANES), BF16)]
    assert ctx is not None or n <= rb, "the context pass writes one cache block per sequence"
    body = functools.partial(_odd_kernel, n=n, n_ctx=n_ctx, n_seq=n_seq, final=final_g is not None, fresh=i == 0,
                             layer=i)
    return _layer_call(body, x, n_seq, ops, caches, i, scratch, name)


def kernel(x_prompt, x_sample, cache_mla_ckv, cache_mla_krope, cache_swa_k, cache_swa_v, c, c_ctx, ada_w, ada_b, norm_g, ev_w_in, ev_w_out, mla_gq, mla_gkv, mla_w_uq, mla_w_uk, mla_w_uv, conv_w, conv_b, od_w_in, od_w_out, swa_sink, gmlp_g, gmlp_ws, gmlp_bs, final_g):
    bp, n_p, _ = x_prompt.shape
    bs_, n_s, _ = x_sample.shape
    n_past = cache_mla_ckv.shape[2]
    n_even, n_odd = ev_w_in.shape[0], od_w_in.shape[0]

    n_rows = -(-(bs_ + 1) // 8) * 8
    cvec = jnp.concatenate([c, c_ctx[None, :], jnp.zeros((n_rows - bs_ - 1, D_MODEL), F32)], axis=0)
    mod = _modulation(cvec, ada_w, ada_b).reshape(DEPTH, n_rows, 3, D_MODEL)

    cos_a, sin_a = _rope_cs(n_s, ROPE_A)
    cos_c, sin_c = _rope_cs(n_s, HD_C)
    zero = lambda w: np.zeros((n_s, w), np.float32)
    tab_a = jnp.asarray(np.concatenate([np.ones((n_s, NOPE_A), np.float32), cos_a, zero(LANES - QK_A),
                                        zero(NOPE_A), sin_a, zero(LANES - QK_A)], axis=1))
    tab_c = jnp.asarray(np.concatenate([cos_c, cos_c, sin_c, sin_c], axis=1))

    uq = mla_w_uq.reshape(n_even, Q_RANK, H_A, QK_A)
    uk = mla_w_uk.reshape(n_even, KV_RANK, H_A, NOPE_A)
    w_head, w_main = _split_even(jnp.swapaxes(ev_w_in, 1, 2))
    p = {
        "norm_g": norm_g.reshape(DEPTH, 1, D_MODEL),
        "w_head": w_head,
        "w_main": w_main,
        "w_uq": jnp.pad(uq, ((0, 0), (0, 0), (0, 0), (0, LANES - QK_A))).reshape(n_even, Q_RANK, H_A * LANES).astype(BF16),
        "w_uk": jnp.pad(uk, ((0, 0), (0, 0), (0, 0), (0, LANES - NOPE_A))).reshape(n_even, KV_RANK, H_A * LANES).astype(BF16),
        "w_uv": mla_w_uv.astype(BF16),
        "ev_w_out": ev_w_out.astype(BF16),
        "gq": mla_gq.reshape(n_even, 1, Q_RANK),
        "gkv": mla_gkv.reshape(n_even, 1, KV_RANK),
        "conv_w": conv_w,
        "conv_b": conv_b.reshape(n_even, 1, W_B),
        "od_w_in": od_w_in.astype(BF16),
        "od_w_out": od_w_out.astype(BF16),
        "sink": swa_sink,
        "gg": gmlp_g.reshape(n_odd, 1, W_D),
        "ws": gmlp_ws.reshape(n_odd, G_D // 2, 2 * CHUNK, CHUNK).astype(BF16),
        "bs": jnp.repeat(jnp.swapaxes(gmlp_bs, 1, 2), DG_D, axis=2),
    }
    ckr = jnp.pad(cache_mla_krope, ((0, 0), (0, 0), (0, 0), (NOPE_A, LANES - QK_A)))
    ckr = [ckr[:, i] for i in range(n_even)]
    swa_k = jnp.transpose(cache_swa_k, (0, 1, 3, 4, 2)).reshape(bs_, n_odd, KV_C * HD_C, n_past)
    swa_v = jnp.transpose(cache_swa_v, (0, 1, 3, 4, 2)).reshape(bs_, n_odd, KV_C * HD_C, n_past)

    new_ckv = (bp, n_even, n_p, KV_RANK)
    new_kr = (bp, n_even, ROPE_A, n_p)
    new_k = (bp, n_odd, KV_C * HD_C, n_p)
    new_v = (bp, n_odd, KV_C * HD_C, n_p)

    fg = final_g.reshape(1, D_MODEL)
    xp, xs = x_prompt, x_sample
    for l in range(DEPTH):
        i = l // 2
        last = fg if l == DEPTH - 1 else None
        mod_p = _mod_spec(mod, l, bs_)
        mod_s = _mod_spec(mod, l, None)
        if l % 2 == 0:
            xp, new_ckv, new_kr = _even_layer(xp, mod_p, i, p, None, None, (new_ckv, new_kr), last, f"even{l}_ctx")
            (xs,) = _even_layer(xs, mod_s, i, p, (cache_mla_ckv, ckr), tab_a, (), last, f"even{l}_lat")
        else:
            xp, new_k, new_v = _odd_layer(xp, mod_p, i, p, None, None, (new_k, new_v), last, f"odd{l}_ctx")
            (xs,) = _odd_layer(xs, mod_s, i, p, (swa_k, swa_v), tab_c, (), last, f"odd{l}_lat")

    def heads_last(a):
        return jnp.transpose(a.reshape(bp, n_odd, KV_C, HD_C, n_p), (0, 1, 4, 2, 3))

    return (xp, xs, new_ckv, jnp.swapaxes(new_kr, 2, 3), heads_last(new_k), heads_last(new_v))
```

```python
import functools

import numpy as np
import jax
import jax.numpy as jnp
from jax import lax
from jax.experimental import pallas as pl
from jax.experimental.pallas import tpu as pltpu

D_MODEL = 1024
DEPTH = 4
GRID_W = 64
ROPE_BASE = 10000.0
EPS = 1e-6
NEG_INF = -1e30

H_A = 8
NOPE_A = 64
ROPE_A = 32
V_A = 64
Q_RANK = 256
KV_RANK = 128
QK_A = NOPE_A + ROPE_A
W_A = H_A * V_A
W_B = 512
CONV_W = 3
H_C = 8
KV_C = 2
HD_C = 64
WINDOW = 128
W_C = H_C * HD_C
G_D = 8
W_D = 512
CHUNK = 128
DG_D = W_D // G_D

EVEN_SPLIT = (Q_RANK, KV_RANK, ROPE_A, W_A, W_B, W_B, W_B, W_B)
ODD_SPLIT = (W_C, KV_C * HD_C, KV_C * HD_C, W_C, W_D, W_D, W_D)
EVEN_HEAD = Q_RANK + KV_RANK + ROPE_A

LANES = 128
MAX_ROWS = 1024
DENSE_ROWS = 512
WINDOW_ROWS = 256
CONV_PAD = 8
CTX_SEQS_PER_STEP = 4
VMEM_LIMIT = 60 * 1024 * 1024

F32 = jnp.float32
BF16 = jnp.bfloat16
LOG2E = float(np.log2(np.e))


def _dot(a, b):
    return jnp.dot(a, b, preferred_element_type=F32)


def _dot_t(a, b):
    return lax.dot_general(a, b, (((1,), (1,)), ((), ())), preferred_element_type=F32)


def _rms(x, g):
    return x * lax.rsqrt(jnp.mean(x * x, axis=-1, keepdims=True) + EPS) * g


def _silu(x):
    return x * (1.0 / (1.0 + jnp.exp(-x)))


def _gelu(x):
    c = np.float32(np.sqrt(2.0 / np.pi))
    return 0.5 * x * (1.0 + jnp.tanh(c * (x + 0.044715 * (x * x * x))))


def _lane(shape):
    return lax.broadcasted_iota(jnp.int32, shape, 1)


def _pair_select(lo, hi):
    return jnp.where(_lane(lo.shape) < LANES // 2, lo, hi)


def _rope(x, tab, quarter):
    fwd = pltpu.roll(x, LANES - quarter, 1)
    bwd = pltpu.roll(x, quarter, 1)
    partner = jnp.where((_lane(x.shape) & quarter) == 0, fwd, bwd)
    return x * tab[:, 0:LANES] + partner * tab[:, LANES:2 * LANES]


def _dup_halves(x):
    sw = pltpu.roll(x, LANES // 2, 1)
    lo = _lane(x.shape) < LANES // 2
    return jnp.where(lo, x, sw), jnp.where(lo, sw, x)


def _modulated_norm(x, mod_ref, ng_ref):
    shift = mod_ref[0, 0, 0:1, :]
    scale = mod_ref[0, 0, 1:2, :]
    return _rms(x, ng_ref[0]) * (1.0 + scale) + shift


def _pipelined(n, produce, consume):
    produce(0)
    outs = []
    for i in range(n):
        if i + 1 < n:
            produce(i + 1)
        outs.append(consume(i))
    return outs


def _normalised(o):
    return o[:, 0:LANES] * (1.0 / o[:, LANES:2 * LANES])


def _loop(n, body):
    if n == 1:
        body(0)
    else:
        def step(i, carry):
            body(i)
            return carry
        lax.fori_loop(0, n, step, 0)


def _grouping(n, n_seq):
    gs = max(1, min(n_seq, MAX_ROWS // n))
    rows_g = gs * n
    return gs, rows_g, min(rows_g, MAX_ROWS)


def _group_rows(ref, g, gs, r0, rb):
    if gs == 1:
        return ref[g, pl.ds(r0, rb), :]
    return ref[pl.ds(g * gs, gs), :, :].reshape(rb, ref.shape[-1])


def _set_group_rows(ref, g, gs, r0, rb, value):
    if gs == 1:
        ref[g, pl.ds(r0, rb), :] = value
    else:
        ref[pl.ds(g * gs, gs), :, :] = value.reshape(gs, rb // gs, ref.shape[-1])


def _row_loop(n_rows, rb, body):
    _loop(n_rows // rb, lambda i: body(i * rb if isinstance(i, int) else pl.multiple_of(i * rb, rb)))


def _even_kernel(*refs, n, n_ctx, n_seq, final, fresh):
    rope = n_ctx > 0
    it = iter(refs)
    x_ref, mod_ref, ng_ref = next(it), next(it), next(it)
    wh_ref, wm_ref, wuq_ref, wuk_ref, wuv_ref, wout_ref = (next(it) for _ in range(6))
    gq_ref, gkv_ref, cw_ref, cb_ref = next(it), next(it), next(it), next(it)
    if rope:
        cckv_ref, ckr_ref, tab_ref = next(it), next(it), next(it)
    if final:
        fg_ref = next(it)
    if not rope and not fresh:
        next(it), next(it)
    o_ref = next(it)
    if not rope:
        ckv_out, kr_out = next(it), next(it)
    h_s, q_s, ckv_s, kr_s, k_s, v_s, t_s, s_s = (next(it) for _ in range(8))

    gs, rows_g, rb = _grouping(n, n_seq)
    nk = n_ctx + n
    tq = min(rb, n, DENSE_ROWS)
    q_scale = np.float32(QK_A ** -0.5 * LOG2E)

    zpad = jnp.zeros((CONV_PAD, W_B), F32)
    t_s[0:CONV_PAD, :] = zpad
    t_s[CONV_PAD + rows_g:2 * CONV_PAD + rows_g, :] = zpad

    def group(g):
        if rope:
            ckv_s[0:n_ctx, :] = cckv_ref[0, 0].astype(BF16)
            kr_s[0:n_ctx, :] = ckr_ref[0].astype(BF16)

        def phase1(r0):
            rows = pl.ds(r0, rb)
            hb = _modulated_norm(_group_rows(x_ref, g, gs, r0, rb), mod_ref, ng_ref).astype(BF16)
            h_s[rows, :] = hb
            ph = _dot(hb, wh_ref[0])
            qn = _rms(ph[:, 0:Q_RANK], gq_ref[0]).astype(BF16)
            qq = _dot(qn, wuq_ref[0])
            ckv = _rms(ph[:, Q_RANK:Q_RANK + KV_RANK], gkv_ref[0])
            kr = ph[:, Q_RANK + KV_RANK:Q_RANK + KV_RANK + LANES]
            if rope:
                tab = tab_ref[rows, :]
                kr = _rope(kr, tab, ROPE_A // 4)
                for h in range(H_A):
                    qh = _rope(qq[:, h * LANES:(h + 1) * LANES], tab, ROPE_A // 4)
                    q_s[rows, h * LANES:(h + 1) * LANES] = (qh * q_scale).astype(BF16)
            else:
                q_s[rows, :] = (qq * q_scale).astype(BF16)
                for s in range(gs):
                    _store_cache(ckv_out, g * gs + s, ckv[s * n:(s + 1) * n, :])
                    _store_cache(kr_out, g * gs + s, kr[s * n:(s + 1) * n, :].T[NOPE_A:QK_A, :])
            krows = pl.ds(n_ctx + r0, rb)
            ckv_s[krows, :] = ckv.astype(BF16)
            kr_s[krows, :] = kr.astype(BF16)
            pc = _dot(hb, wm_ref[0, :, 2 * W_B:4 * W_B])
            t_s[pl.ds(CONV_PAD + r0, rb), :] = pc[:, 0:W_B] * pc[:, W_B:2 * W_B]

        _row_loop(rows_g, rb, phase1)

        rb2 = int(np.gcd(rb, n_ctx + rows_g))

        def phase2(r0):
            rows = pl.ds(r0, rb2)
            c = ckv_s[rows, :]
            kn = _dot(c, wuk_ref[0])
            krf = kr_s[rows, :].astype(F32)
            for h in range(H_A):
                k_s[rows, h * LANES:(h + 1) * LANES] = (kn[:, h * LANES:(h + 1) * LANES] + krf).astype(BF16)
            v = _dot(c, wuv_ref[0]).astype(BF16)
            ones = jnp.ones((rb2, LANES), BF16)
            for j in range(H_A // 2):
                v_s[rows, 2 * j * LANES:(2 * j + 1) * LANES] = v[:, j * LANES:(j + 1) * LANES]
                v_s[rows, (2 * j + 1) * LANES:(2 * j + 2) * LANES] = ones

        _row_loop(n_ctx + rows_g, rb2, phase2)

        def phase3(r0):
            rows = pl.ds(r0, rb)
            hb = h_s[rows, :]

            def keys_of(t):
                return slice(None) if gs == 1 else slice(t * n, (t + 1) * n)

            def scores(i):
                t, h = divmod(i, H_A)
                s_s[i % 2] = _dot_t(q_s[pl.ds(r0 + t * tq, tq), h * LANES:(h + 1) * LANES],
                                    k_s[keys_of(t), h * LANES:(h + 1) * LANES])

            def attend(i):
                t, h = divmod(i, H_A)
                s = s_s[i % 2]
                e = jnp.exp2(s - jnp.max(s, axis=-1, keepdims=True)).astype(BF16)
                j = h // 2
                return _normalised(_dot(e, v_s[keys_of(t), 2 * j * LANES:(2 * j + 2) * LANES]))

            heads = _pipelined((rb // tq) * H_A, scores, attend)
            attn = jnp.concatenate(
                [jnp.concatenate([_pair_select(heads[t * H_A + 2 * j], heads[t * H_A + 2 * j + 1])
                                  for j in range(H_A // 2)], axis=-1) for t in range(rb // tq)], axis=0)
            pa = _dot(hb, wm_ref[0, :, 0:2 * W_B])
            a_out = attn * _silu(pa[:, 0:W_A])
            gate_b = pa[:, W_A:W_A + W_B]
            g_b = _dot(hb, wm_ref[0, :, 4 * W_B:5 * W_B])

            t_cur = t_s[pl.ds(CONV_PAD + r0, rb), :]
            rid = lax.broadcasted_iota(jnp.int32, (rb, W_B), 0)
            if n <= rb:
                t_prev = jnp.where(rid % n == 0, 0.0, pltpu.roll(t_cur, 1, 0))
                t_next = jnp.where(rid % n == n - 1, 0.0, pltpu.roll(t_cur, rb - 1, 0))
            else:
                t_up = t_s[pl.ds(r0, CONV_PAD), :]
                t_dn = t_s[pl.ds(CONV_PAD + r0 + rb, CONV_PAD), :]
                t_prev = jnp.where(rid == 0, t_up[CONV_PAD - 1:CONV_PAD, :], pltpu.roll(t_cur, 1, 0))
                t_next = jnp.where(rid == rb - 1, t_dn[0:1, :], pltpu.roll(t_cur, rb - 1, 0))
            conv = cb_ref[0] + t_prev * cw_ref[0, 0:1, :]
            conv = conv + t_cur * cw_ref[0, 1:2, :]
            conv = conv + t_next * cw_ref[0, 2:3, :]
            b_out = gate_b * conv * _silu(g_b)

            y = (_dot(a_out.astype(BF16), wout_ref[0, 0:W_A, :])
                 + _dot(b_out.astype(BF16), wout_ref[0, W_A:W_A + W_B, :]))
            res = _group_rows(x_ref, g, gs, r0, rb) + mod_ref[0, 0, 2:3, :] * y
            if final:
                res = _rms(res, fg_ref[...])
            _set_group_rows(o_ref, g, gs, r0, rb, res)

        _row_loop(rows_g, rb, phase3)

    _loop(n_seq // gs, group)


def _odd_kernel(*refs, n, n_ctx, n_seq, final, fresh, layer):
    rope = n_ctx > 0
    it = iter(refs)
    x_ref, mod_ref, ng_ref = next(it), next(it), next(it)
    win_ref, wout_ref = next(it), next(it)
    sink_ref, gg_ref, ws_ref, bs_ref = next(it), next(it), next(it), next(it)
    if rope:
        kc_ref, vc_ref, tab_ref = next(it), next(it), next(it)
    if final:
        fg_ref = next(it)
    if not rope and not fresh:
        next(it), next(it)
    o_ref = next(it)
    if not rope:
        k_out, v_out = next(it), next(it)
    h_s, kloc_s, vloc_s, s_s = next(it), next(it), next(it), next(it)
    if rope:
        kctx_s, vctx_s = next(it), next(it)

    pad = WINDOW if rope else 0
    gs, rows_g, rb = _grouping(n, n_seq)
    tq = min(rb, n, WINDOW_ROWS)
    span = tq + 2 * pad
    kw, vw = LANES, 2 * LANES
    q0, k0, g0 = 0, W_C, W_C + 2 * KV_C * HD_C
    q_scale = np.float32(HD_C ** -0.5 * LOG2E)

    def store_keys(dst, rows, x):
        a, b = _dup_halves(x)
        dst[rows, 0:LANES] = a.astype(BF16)
        dst[rows, LANES:2 * LANES] = b.astype(BF16)

    def store_values(dst, rows, x):
        a, b = _dup_halves(x)
        ones = jnp.ones(x.shape, BF16)
        dst[rows, 0:LANES] = a.astype(BF16)
        dst[rows, LANES:2 * LANES] = ones
        dst[rows, 2 * LANES:3 * LANES] = b.astype(BF16)
        dst[rows, 3 * LANES:4 * LANES] = ones

    def group(g):
        if rope:
            kloc_s[0:pad, :] = jnp.zeros((pad, KV_C * kw), BF16)
            kloc_s[pad + n:2 * pad + n, :] = jnp.zeros((pad, KV_C * kw), BF16)
            vloc_s[0:pad, :] = jnp.zeros((pad, KV_C * vw), BF16)
            vloc_s[pad + n:2 * pad + n, :] = jnp.zeros((pad, KV_C * vw), BF16)
            store_keys(kctx_s, slice(None), kc_ref[0, 0].T)
            store_values(vctx_s, slice(None), vc_ref[0, 0].T)

        def phase1(r0):
            rows = pl.ds(r0, rb)
            hb = _modulated_norm(_group_rows(x_ref, g, gs, r0, rb), mod_ref, ng_ref).astype(BF16)
            h_s[rows, :] = hb
            p1 = _dot(hb, win_ref[0, :, k0:g0])
            k = p1[:, 0:LANES]
            v = p1[:, LANES:2 * LANES]
            if rope:
                k = _rope(k, tab_ref[rows, :], HD_C // 4)
            else:
                for s in range(gs):
                    _store_cache(k_out, g * gs + s, k[s * n:(s + 1) * n, :].T)
                    _store_cache(v_out, g * gs + s, v[s * n:(s + 1) * n, :].T)
            lrows = pl.ds(pad + r0, rb)
            store_keys(kloc_s, lrows, k)
            store_values(vloc_s, lrows, v)

        _row_loop(rows_g, rb, phase1)

        def phase3(r0):
            rows = pl.ds(r0, rb)
            hb = h_s[rows, :]
            pq = _dot(hb, win_ref[0, :, q0:k0])

            tiles = []
            for t in range(rb // tq):
                t0 = r0 + t * tq
                pqt = pq[t * tq:(t + 1) * tq, :]
                if rope:
                    tab = tab_ref[pl.ds(t0, tq), :]
                    kl = kloc_s[pl.ds(t0, span), :]
                    vl = vloc_s[pl.ds(t0, span), :]
                    ii = lax.broadcasted_iota(jnp.int32, (tq, span), 0)
                    jj = lax.broadcasted_iota(jnp.int32, (tq, span), 1)
                    kpos = jj + (t0 - WINDOW)
                    dlt = jj - ii
                    mask = (dlt >= 0) & (dlt <= 2 * WINDOW) & (kpos >= 0) & (kpos < n)
                else:
                    kl, vl, mask = kloc_s[t * n:(t + 1) * n, :], vloc_s[t * n:(t + 1) * n, :], None
                qhs = []
                for m in range(H_C // 2):
                    qb = pqt[:, m * LANES:(m + 1) * LANES]
                    if rope:
                        qb = _rope(qb, tab, HD_C // 4)
                    qb = qb * q_scale
                    lo = _lane(qb.shape) < LANES // 2
                    qhs.append(jnp.where(lo, qb, 0.0).astype(BF16))
                    qhs.append(jnp.where(lo, 0.0, qb).astype(BF16))
                tiles.append((kl, vl, mask, qhs))

            def kv_head(h):
                return h // (H_C // KV_C)

            def scores(i):
                (kl, _, _, qhs), h = tiles[i // H_C], i % H_C
                j = kv_head(h)
                if rope:
                    s_s[i % 2, :, 0:n_ctx] = _dot_t(qhs[h], kctx_s[:, j * kw:(j + 1) * kw])
                    s_s[i % 2, :, n_ctx:] = _dot_t(qhs[h], kl[:, j * kw:(j + 1) * kw])
                else:
                    s_s[i % 2] = _dot_t(qhs[h], kl[:, j * kw:(j + 1) * kw])

            def attend(i):
                (_, vl, mask, _), h = tiles[i // H_C], i % H_C
                j = kv_head(h)
                sk = sink_ref[layer, h] * np.float32(LOG2E)
                if rope:
                    sc = s_s[i % 2, :, 0:n_ctx]
                    sl = jnp.where(mask, s_s[i % 2, :, n_ctx:], NEG_INF)
                    mx = jnp.maximum(jnp.max(sl, axis=-1, keepdims=True), jnp.max(sc, axis=-1, keepdims=True))
                else:
                    sl = s_s[i % 2]
                    mx = jnp.max(sl, axis=-1, keepdims=True)
                mx = jnp.maximum(mx, sk)
                o = _dot(jnp.exp2(sl - mx).astype(BF16), vl[:, j * vw:(j + 1) * vw])
                if rope:
                    o = o + _dot(jnp.exp2(sc - mx).astype(BF16), vctx_s[:, j * vw:(j + 1) * vw])
                return o[:, 0:LANES] * (1.0 / (o[:, LANES:2 * LANES] + jnp.exp2(sk - mx)))

            heads = _pipelined(len(tiles) * H_C, scores, attend)
            attn = jnp.concatenate(
                [jnp.concatenate([_pair_select(heads[t * H_C + 2 * m], heads[t * H_C + 2 * m + 1])
                                  for m in range(H_C // 2)], axis=-1) for t in range(len(tiles))], axis=0)
            pg = _dot(hb, win_ref[0, :, g0:g0 + W_C + 3 * W_D])
            c_out = attn * _silu(pg[:, 0:W_C])
            u = pg[:, W_C:W_C + W_D]
            vv = pg[:, W_C + W_D:W_C + 2 * W_D]
            g_d = pg[:, W_C + 2 * W_D:W_C + 3 * W_D]

            vvn = _rms(_gelu(vv), gg_ref[0]).astype(BF16)
            chunks = []
            for c in range(rb // CHUNK):
                cols = []
                for jp in range(G_D // 2):
                    vp = vvn[c * CHUNK:(c + 1) * CHUNK, jp * LANES:(jp + 1) * LANES]
                    o = _dot(ws_ref[0, jp], vp)
                    cols.append(_pair_select(o[0:CHUNK, :], o[CHUNK:2 * CHUNK, :]))
                chunks.append(jnp.concatenate(cols, axis=-1) + bs_ref[0])
            mixed = jnp.concatenate(chunks, axis=0)
            d_out = _gelu(u) * mixed * _silu(g_d)

            y = (_dot(c_out.astype(BF16), wout_ref[0, 0:W_C, :])
                 + _dot(d_out.astype(BF16), wout_ref[0, W_C:W_C + W_D, :]))
            res = _group_rows(x_ref, g, gs, r0, rb) + mod_ref[0, 0, 2:3, :] * y
            if final:
                res = _rms(res, fg_ref[...])
            _set_group_rows(o_ref, g, gs, r0, rb, res)

        _row_loop(rows_g, rb, phase3)

    _loop(n_seq // gs, group)


def _mod_kernel(c_ref, w_ref, b_ref, o_ref):
    a = _silu(c_ref[...]).astype(BF16)
    o_ref[0] = _dot(a, w_ref[0].astype(BF16)) + b_ref[0]


def _modulation(cvec, ada_w, ada_b):
    rows = cvec.shape[0]
    tn = D_MODEL
    return pl.pallas_call(
        _mod_kernel,
        grid=(DEPTH, 3 * D_MODEL // tn),
        in_specs=[
            pl.BlockSpec((rows, D_MODEL), lambda l, j: (0, 0)),
            pl.BlockSpec((1, D_MODEL, tn), lambda l, j: (l, 0, j)),
            pl.BlockSpec((1, 1, tn), lambda l, j: (l, 0, j)),
        ],
        out_specs=pl.BlockSpec((1, rows, tn), lambda l, j: (l, 0, j)),
        out_shape=jax.ShapeDtypeStruct((DEPTH, rows, 3 * D_MODEL), F32),
        compiler_params=pltpu.CompilerParams(dimension_semantics=("arbitrary", "arbitrary")),
        name="modulation",
    )(cvec, ada_w, ada_b.reshape(DEPTH, 1, 3 * D_MODEL))


def _split_even_kernel(wt_ref, head_ref, main_ref):
    kv_end = Q_RANK + KV_RANK

    def put(dst, col, rows_f32):
        dst[0, :, col:col + LANES] = rows_f32.T.astype(BF16)

    for j in range(kv_end // LANES):
        put(head_ref, j * LANES, wt_ref[0, j * LANES:(j + 1) * LANES, :])
    k_r = wt_ref[0, kv_end:EVEN_HEAD, :]
    put(head_ref, kv_end, jnp.concatenate(
        [jnp.zeros((NOPE_A, D_MODEL), F32), k_r, jnp.zeros((LANES - QK_A, D_MODEL), F32)], axis=0))
    for j in range(main_ref.shape[2] // LANES):
        put(main_ref, j * LANES, wt_ref[0, EVEN_HEAD + j * LANES:EVEN_HEAD + (j + 1) * LANES, :])


def _split_even(w_in_t):
    n_layers, cols, d = w_in_t.shape
    n_head, n_main = Q_RANK + KV_RANK + LANES, cols - EVEN_HEAD
    return pl.pallas_call(
        _split_even_kernel,
        grid=(n_layers,),
        in_specs=[pl.BlockSpec((1, cols, d), lambda l: (l, 0, 0))],
        out_specs=[pl.BlockSpec((1, d, n_head), lambda l: (l, 0, 0)),
                   pl.BlockSpec((1, d, n_main), lambda l: (l, 0, 0))],
        out_shape=[jax.ShapeDtypeStruct((n_layers, d, n_head), BF16),
                   jax.ShapeDtypeStruct((n_layers, d, n_main), BF16)],
        compiler_params=pltpu.CompilerParams(dimension_semantics=("arbitrary",), vmem_limit_bytes=VMEM_LIMIT),
        name="split_even_w_in",
    )(w_in_t)


def _full(a):
    nd = a.ndim
    return a, pl.BlockSpec(a.shape, lambda b, _nd=nd: (0,) * _nd)


def _per_step(a, n_seq):
    nd = a.ndim
    return a, pl.BlockSpec((n_seq,) + a.shape[1:], lambda b, _nd=nd: (b,) + (0,) * (_nd - 1))


def _of_layer(a, i):
    nd = a.ndim
    return a, pl.BlockSpec((1,) + a.shape[1:], lambda b, _nd=nd, _i=i: (_i,) + (0,) * (_nd - 1))


def _seq_of_layer(a, i):
    nd = a.ndim
    return a, pl.BlockSpec((1, 1) + a.shape[2:], lambda b, _nd=nd, _i=i: (b, _i) + (0,) * (_nd - 2))


def _mod_spec(mod, l, row):
    if row is None:
        return mod, pl.BlockSpec((1, 1, 3, D_MODEL), lambda b, _l=l: (_l, b, 0, 0))
    return mod, pl.BlockSpec((1, 1, 3, D_MODEL), lambda b, _l=l, _r=row: (_l, _r, 0, 0))


def _rope_cs(n, rot_dim):
    rows = n // GRID_W
    r = np.repeat(np.arange(rows, dtype=np.float32), GRID_W)
    col = np.tile(np.arange(GRID_W, dtype=np.float32), rows)
    nf = rot_dim // 4
    inv = np.float32(ROPE_BASE) ** (-np.arange(nf, dtype=np.float32) / np.float32(nf))
    ar = r[:, None] * inv
    ac = col[:, None] * inv
    cos = np.concatenate([np.cos(ar), np.cos(ar), np.cos(ac), np.cos(ac)], axis=-1)
    sin = np.concatenate([-np.sin(ar), np.sin(ar), -np.sin(ac), np.sin(ac)], axis=-1)
    return cos.astype(np.float32), sin.astype(np.float32)


def _layer_call(body, x, n_seq, operands, caches, slot, scratch, name):
    b, n, _ = x.shape
    fresh = slot == 0
    shapes = [c if fresh else c.shape for c in caches]
    in_specs = [spec for _, spec in operands]
    args = [a for a, _ in operands]
    aliases = {}
    if not fresh:
        in_specs += [pl.BlockSpec(memory_space=pl.ANY) for _ in caches]
        args += list(caches)
        aliases = {len(operands) + k: 1 + k for k in range(len(caches))}
    out_shape = [jax.ShapeDtypeStruct(x.shape, F32)] + [jax.ShapeDtypeStruct(s, F32) for s in shapes]
    out_specs = [pl.BlockSpec((n_seq, n, D_MODEL), lambda i: (i, 0, 0))]
    for s in shapes:
        if fresh:
            out_specs.append(pl.BlockSpec((n_seq,) + tuple(s[1:]), lambda i: (i, 0, 0, 0)))
        else:
            out_specs.append(pl.BlockSpec((n_seq, 1) + tuple(s[2:]), lambda i, _s=slot: (i, _s, 0, 0)))
    return pl.pallas_call(
        body,
        grid=(b // n_seq,),
        in_specs=in_specs,
        out_specs=out_specs,
        out_shape=out_shape,
        input_output_aliases=aliases,
        scratch_shapes=scratch,
        compiler_params=pltpu.CompilerParams(dimension_semantics=("arbitrary",), vmem_limit_bytes=VMEM_LIMIT),
        name=name,
    )(*args)


def _store_cache(ref, sq, value):
    ref[sq, 0] = value
    for l in range(1, ref.shape[1]):
        ref[sq, l] = jnp.zeros(value.shape, value.dtype)


def _seqs_per_step(b, ctx):
    return 1 if ctx is not None else int(np.gcd(b, CTX_SEQS_PER_STEP))


def _even_layer(x, mod_op, i, p, ctx, tab, caches, final_g, name):
    b, n, _ = x.shape
    n_ctx = 0 if ctx is None else ctx[0].shape[2]
    n_seq = _seqs_per_step(b, ctx)
    ops = [_per_step(x, n_seq), mod_op, _of_layer(p["norm_g"], 2 * i),
           _of_layer(p["w_head"], i), _of_layer(p["w_main"], i), _of_layer(p["w_uq"], i), _of_layer(p["w_uk"], i),
           _of_layer(p["w_uv"], i), _of_layer(p["ev_w_out"], i),
           _of_layer(p["gq"], i), _of_layer(p["gkv"], i), _of_layer(p["conv_w"], i), _of_layer(p["conv_b"], i)]
    if ctx is not None:
        ops += [_seq_of_layer(ctx[0], i), _per_step(ctx[1][i], 1), _full(tab)]
    if final_g is not None:
        ops.append(_full(final_g))
    _, rows_g, rb = _grouping(n, n_seq)
    nk_g = n_ctx + rows_g
    scratch = [pltpu.VMEM((rows_g, D_MODEL), BF16), pltpu.VMEM((rows_g, H_A * LANES), BF16),
               pltpu.VMEM((nk_g, LANES), BF16), pltpu.VMEM((nk_g, LANES), BF16),
               pltpu.VMEM((nk_g, H_A * LANES), BF16), pltpu.VMEM((nk_g, 2 * W_A), BF16),
               pltpu.VMEM((rows_g + 2 * CONV_PAD, W_B), F32),
               pltpu.VMEM((2, min(rb, n, DENSE_ROWS), n_ctx + n), F32)]
    assert ctx is not None or n <= rb, "the context pass writes one cache block per sequence"
    body = functools.partial(_even_kernel, n=n, n_ctx=n_ctx, n_seq=n_seq, final=final_g is not None, fresh=i == 0)
    return _layer_call(body, x, n_seq, ops, caches, i, scratch, name)


def _odd_layer(x, mod_op, i, p, ctx, tab, caches, final_g, name):
    b, n, _ = x.shape
    n_ctx = 0 if ctx is None else ctx[0].shape[3]
    n_seq = _seqs_per_step(b, ctx)
    ops = [_per_step(x, n_seq), mod_op, _of_layer(p["norm_g"], 2 * i + 1),
           _of_layer(p["od_w_in"], i), _of_layer(p["od_w_out"], i),
           (p["sink"], pl.BlockSpec(memory_space=pltpu.SMEM)), _of_layer(p["gg"], i), _of_layer(p["ws"], i),
           _of_layer(p["bs"], i)]
    if ctx is not None:
        ops += [_seq_of_layer(ctx[0], i), _seq_of_layer(ctx[1], i), _full(tab)]
    if final_g is not None:
        ops.append(_full(final_g))
    pad = WINDOW if ctx is not None else 0
    _, rows_g, rb = _grouping(n, n_seq)
    tq = min(rb, n, WINDOW_ROWS)
    n_keys = n_ctx + tq + 2 * pad if ctx is not None else n
    scratch = [pltpu.VMEM((rows_g, D_MODEL), BF16), pltpu.VMEM((rows_g + 2 * pad, KV_C * LANES), BF16),
               pltpu.VMEM((rows_g + 2 * pad, KV_C * 2 * LANES), BF16), pltpu.VMEM((2, tq, n_keys), F32)]
    if ctx is not None:
        scratch += [pltpu.VMEM((n_ctx, KV_C * LANES), BF16), pltpu.VMEM((n_ctx, KV_C * 2 * LANES), BF16)]
    assert ctx is not None or n <= rb, "the context pass writes one cache block per sequence"
    body = functools.partial(_odd_kernel, n=n, n_ctx=n_ctx, n_seq=n_seq, final=final_g is not None, fresh=i == 0,
                             layer=i)
    return _layer_call(body, x, n_seq, ops, caches, i, scratch, name)


def kernel(x_prompt, x_sample, cache_mla_ckv, cache_mla_krope, cache_swa_k, cache_swa_v, c, c_ctx, ada_w, ada_b, norm_g, ev_w_in, ev_w_out, mla_gq, mla_gkv, mla_w_uq, mla_w_uk, mla_w_uv, conv_w, conv_b, od_w_in, od_w_out, swa_sink, gmlp_g, gmlp_ws, gmlp_bs, final_g):
    bp, n_p, _ = x_prompt.shape
    bs_, n_s, _ = x_sample.shape
    n_past = cache_mla_ckv.shape[2]
    n_even, n_odd = ev_w_in.shape[0], od_w_in.shape[0]

    n_rows = -(-(bs_ + 1) // 8) * 8
    cvec = jnp.concatenate([c, c_ctx[None, :], jnp.zeros((n_rows - bs_ - 1, D_MODEL), F32)], axis=0)
    mod = _modulation(cvec, ada_w, ada_b).reshape(DEPTH, n_rows, 3, D_MODEL)

    cos_a, sin_a = _rope_cs(n_s, ROPE_A)
    cos_c, sin_c = _rope_cs(n_s, HD_C)
    zero = lambda w: np.zeros((n_s, w), np.float32)
    tab_a = jnp.asarray(np.concatenate([np.ones((n_s, NOPE_A), np.float32), cos_a, zero(LANES - QK_A),
                                        zero(NOPE_A), sin_a, zero(LANES - QK_A)], axis=1))
    tab_c = jnp.asarray(np.concatenate([cos_c, cos_c, sin_c, sin_c], axis=1))

    uq = mla_w_uq.reshape(n_even, Q_RANK, H_A, QK_A)
    uk = mla_w_uk.reshape(n_even, KV_RANK, H_A, NOPE_A)
    w_head, w_main = _split_even(jnp.swapaxes(ev_w_in, 1, 2))
    p = {
        "norm_g": norm_g.reshape(DEPTH, 1, D_MODEL),
        "w_head": w_head,
        "w_main": w_main,
        "w_uq": jnp.pad(uq, ((0, 0), (0, 0), (0, 0), (0, LANES - QK_A))).reshape(n_even, Q_RANK, H_A * LANES).astype(BF16),
        "w_uk": jnp.pad(uk, ((0, 0), (0, 0), (0, 0), (0, LANES - NOPE_A))).reshape(n_even, KV_RANK, H_A * LANES).astype(BF16),
        "w_uv": mla_w_uv.astype(BF16),
        "ev_w_out": ev_w_out.astype(BF16),
        "gq": mla_gq.reshape(n_even, 1, Q_RANK),
        "gkv": mla_gkv.reshape(n_even, 1, KV_RANK),
        "conv_w": conv_w,
        "conv_b": conv_b.reshape(n_even, 1, W_B),
        "od_w_in": od_w_in.astype(BF16),
        "od_w_out": od_w_out.astype(BF16),
        "sink": swa_sink,
        "gg": gmlp_g.reshape(n_odd, 1, W_D),
        "ws": gmlp_ws.reshape(n_odd, G_D // 2, 2 * CHUNK, CHUNK).astype(BF16),
        "bs": jnp.repeat(jnp.swapaxes(gmlp_bs, 1, 2), DG_D, axis=2),
    }
    ckr = jnp.pad(cache_mla_krope, ((0, 0), (0, 0), (0, 0), (NOPE_A, LANES - QK_A)))
    ckr = [ckr[:, i] for i in range(n_even)]
    swa_k = jnp.transpose(cache_swa_k, (0, 1, 3, 4, 2)).reshape(bs_, n_odd, KV_C * HD_C, n_past)
    swa_v = jnp.transpose(cache_swa_v, (0, 1, 3, 4, 2)).reshape(bs_, n_odd, KV_C * HD_C, n_past)

    new_ckv = (bp, n_even, n_p, KV_RANK)
    new_kr = (bp, n_even, ROPE_A, n_p)
    new_k = (bp, n_odd, KV_C * HD_C, n_p)
    new_v = (bp, n_odd, KV_C * HD_C, n_p)

    fg = final_g.reshape(1, D_MODEL)
    xp, xs = x_prompt, x_sample
    for l in range(DEPTH):
        i = l // 2
        last = fg if l == DEPTH - 1 else None
        mod_p = _mod_spec(mod, l, bs_)
        mod_s = _mod_spec(mod, l, None)
        if l % 2 == 0:
            xp, new_ckv, new_kr = _even_layer(xp, mod_p, i, p, None, None, (new_ckv, new_kr), last, f"even{l}_ctx")
            (xs,) = _even_layer(xs, mod_s, i, p, (cache_mla_ckv, ckr), tab_a, (), last, f"even{l}_lat")
        else:
            xp, new_k, new_v = _odd_layer(xp, mod_p, i, p, None, None, (new_k, new_v), last, f"odd{l}_ctx")
            (xs,) = _odd_layer(xs, mod_s, i, p, (swa_k, swa_v), tab_c, (), last, f"odd{l}_lat")

    def heads_last(a):
        return jnp.transpose(a.reshape(bp, n_odd, KV_C, HD_C, n_p), (0, 1, 4, 2, 3))

    return (xp, xs, new_ckv, jnp.swapaxes(new_kr, 2, 3), heads_last(new_k), heads_last(new_v))
```

```python
import functools

import numpy as np
import jax
import jax.numpy as jnp
from jax import lax
from jax.experimental import pallas as pl
from jax.experimental.pallas import tpu as pltpu

D_MODEL = 1024
DEPTH = 4
GRID_W = 64
ROPE_BASE = 10000.0
EPS = 1e-6
NEG_INF = -1e30

H_A = 8
NOPE_A = 64
ROPE_A = 32
V_A = 64
Q_RANK = 256
KV_RANK = 128
QK_A = NOPE_A + ROPE_A
W_A = H_A * V_A
W_B = 512
CONV_W = 3
H_C = 8
KV_C = 2
HD_C = 64
WINDOW = 128
W_C = H_C * HD_C
G_D = 8
W_D = 512
CHUNK = 128
DG_D = W_D // G_D

EVEN_SPLIT = (Q_RANK, KV_RANK, ROPE_A, W_A, W_B, W_B, W_B, W_B)
ODD_SPLIT = (W_C, KV_C * HD_C, KV_C * HD_C, W_C, W_D, W_D, W_D)
EVEN_HEAD = Q_RANK + KV_RANK + ROPE_A

LANES = 128
EVEN_MAX_ROWS = 512
ODD_MAX_ROWS = 1024
DENSE_ROWS = 512
WINDOW_ROWS = 256
CONV_PAD = 8
CTX_SEQS_PER_STEP = 4
VMEM_LIMIT = 60 * 1024 * 1024

F32 = jnp.float32
BF16 = jnp.bfloat16
LOG2E = float(np.log2(np.e))


def _dot(a, b):
    return jnp.dot(a, b, preferred_element_type=F32)


def _dot_t(a, b):
    return lax.dot_general(a, b, (((1,), (1,)), ((), ())), preferred_element_type=F32)


def _rms(x, g):
    return x * lax.rsqrt(jnp.mean(x * x, axis=-1, keepdims=True) + EPS) * g


def _silu(x):
    return x * (1.0 / (1.0 + jnp.exp(-x)))


def _gelu(x):
    c = np.float32(np.sqrt(2.0 / np.pi))
    return 0.5 * x * (1.0 + jnp.tanh(c * (x + 0.044715 * (x * x * x))))


def _lane(shape):
    return lax.broadcasted_iota(jnp.int32, shape, 1)


def _pair_select(lo, hi):
    return jnp.where(_lane(lo.shape) < LANES // 2, lo, hi)


def _rope(x, tab, quarter):
    fwd = pltpu.roll(x, LANES - quarter, 1)
    bwd = pltpu.roll(x, quarter, 1)
    partner = jnp.where((_lane(x.shape) & quarter) == 0, fwd, bwd)
    return x * tab[:, 0:LANES] + partner * tab[:, LANES:2 * LANES]


def _dup_halves(x):
    sw = pltpu.roll(x, LANES // 2, 1)
    lo = _lane(x.shape) < LANES // 2
    return jnp.where(lo, x, sw), jnp.where(lo, sw, x)


def _modulated_norm(x, mod_ref, ng_ref):
    shift = mod_ref[0, 0, 0:1, :]
    scale = mod_ref[0, 0, 1:2, :]
    return _rms(x, ng_ref[0]) * (1.0 + scale) + shift


def _pipelined(n, produce, consume):
    produce(0)
    outs = []
    for i in range(n):
        if i + 1 < n:
            produce(i + 1)
        outs.append(consume(i))
    return outs


def _normalised(o):
    return o[:, 0:LANES] * (1.0 / o[:, LANES:2 * LANES])


def _loop(n, body):
    if n == 1:
        body(0)
    else:
        def step(i, carry):
            body(i)
            return carry
        lax.fori_loop(0, n, step, 0)


def _grouping(n, n_seq, max_rows):
    gs = max(1, min(n_seq, max_rows // n))
    rows_g = gs * n
    return gs, rows_g, min(rows_g, max_rows)


def _group_rows(ref, g, gs, r0, rb):
    if gs == 1:
        return ref[g, pl.ds(r0, rb), :]
    return ref[pl.ds(g * gs, gs), :, :].reshape(rb, ref.shape[-1])


def _set_group_rows(ref, g, gs, r0, rb, value):
    if gs == 1:
        ref[g, pl.ds(r0, rb), :] = value
    else:
        ref[pl.ds(g * gs, gs), :, :] = value.reshape(gs, rb // gs, ref.shape[-1])


def _row_loop(n_rows, rb, body):
    _loop(n_rows // rb, lambda i: body(i * rb if isinstance(i, int) else pl.multiple_of(i * rb, rb)))


def _even_kernel(*refs, n, n_ctx, n_seq, final, fresh):
    rope = n_ctx > 0
    it = iter(refs)
    x_ref, mod_ref, ng_ref = next(it), next(it), next(it)
    wh_ref, wm_ref, wuq_ref, wuk_ref, wuv_ref, wout_ref = (next(it) for _ in range(6))
    gq_ref, gkv_ref, cw_ref, cb_ref = next(it), next(it), next(it), next(it)
    if rope:
        cckv_ref, ckr_ref, tab_ref = next(it), next(it), next(it)
    if final:
        fg_ref = next(it)
    if not rope and not fresh:
        next(it), next(it)
    o_ref = next(it)
    if not rope:
        ckv_out, kr_out = next(it), next(it)
    h_s, q_s, ckv_s, kr_s, k_s, v_s, t_s, s_s = (next(it) for _ in range(8))

    gs, rows_g, rb = _grouping(n, n_seq, EVEN_MAX_ROWS)
    nk = n_ctx + n
    tq = min(rb, n, DENSE_ROWS)
    q_scale = np.float32(QK_A ** -0.5 * LOG2E)

    zpad = jnp.zeros((CONV_PAD, W_B), F32)
    t_s[0:CONV_PAD, :] = zpad
    t_s[CONV_PAD + rows_g:2 * CONV_PAD + rows_g, :] = zpad

    def group(g):
        if rope:
            ckv_s[0:n_ctx, :] = cckv_ref[0, 0].astype(BF16)
            kr_s[0:n_ctx, :] = ckr_ref[0].astype(BF16)

        def phase1(r0):
            rows = pl.ds(r0, rb)
            hb = _modulated_norm(_group_rows(x_ref, g, gs, r0, rb), mod_ref, ng_ref).astype(BF16)
            h_s[rows, :] = hb
            ph = _dot(hb, wh_ref[0])
            qn = _rms(ph[:, 0:Q_RANK], gq_ref[0]).astype(BF16)
            qq = _dot(qn, wuq_ref[0])
            ckv = _rms(ph[:, Q_RANK:Q_RANK + KV_RANK], gkv_ref[0])
            kr = ph[:, Q_RANK + KV_RANK:Q_RANK + KV_RANK + LANES]
            if rope:
                tab = tab_ref[rows, :]
                kr = _rope(kr, tab, ROPE_A // 4)
                for h in range(H_A):
                    qh = _rope(qq[:, h * LANES:(h + 1) * LANES], tab, ROPE_A // 4)
                    q_s[rows, h * LANES:(h + 1) * LANES] = (qh * q_scale).astype(BF16)
            else:
                q_s[rows, :] = (qq * q_scale).astype(BF16)
                for s in range(gs):
                    _store_cache(ckv_out, g * gs + s, ckv[s * n:(s + 1) * n, :])
                    _store_cache(kr_out, g * gs + s, kr[s * n:(s + 1) * n, :].T[NOPE_A:QK_A, :])
            krows = pl.ds(n_ctx + r0, rb)
            ckv_s[krows, :] = ckv.astype(BF16)
            kr_s[krows, :] = kr.astype(BF16)
            pc = _dot(hb, wm_ref[0, :, 2 * W_B:4 * W_B])
            t_s[pl.ds(CONV_PAD + r0, rb), :] = pc[:, 0:W_B] * pc[:, W_B:2 * W_B]

        _row_loop(rows_g, rb, phase1)

        rb2 = int(np.gcd(rb, n_ctx + rows_g))

        def phase2(r0):
            rows = pl.ds(r0, rb2)
            c = ckv_s[rows, :]
            kn = _dot(c, wuk_ref[0])
            krf = kr_s[rows, :].astype(F32)
            for h in range(H_A):
                k_s[rows, h * LANES:(h + 1) * LANES] = (kn[:, h * LANES:(h + 1) * LANES] + krf).astype(BF16)
            v = _dot(c, wuv_ref[0]).astype(BF16)
            ones = jnp.ones((rb2, LANES), BF16)
            for j in range(H_A // 2):
                v_s[rows, 2 * j * LANES:(2 * j + 1) * LANES] = v[:, j * LANES:(j + 1) * LANES]
                v_s[rows, (2 * j + 1) * LANES:(2 * j + 2) * LANES] = ones

        _row_loop(n_ctx + rows_g, rb2, phase2)

        def phase3(r0):
            rows = pl.ds(r0, rb)
            hb = h_s[rows, :]

            def keys_of(t):
                return slice(None) if gs == 1 else slice(t * n, (t + 1) * n)

            def scores(i):
                t, h = divmod(i, H_A)
                s_s[i % 2] = _dot_t(q_s[pl.ds(r0 + t * tq, tq), h * LANES:(h + 1) * LANES],
                                    k_s[keys_of(t), h * LANES:(h + 1) * LANES])

            def attend(i):
                t, h = divmod(i, H_A)
                s = s_s[i % 2]
                e = jnp.exp2(s - jnp.max(s, axis=-1, keepdims=True)).astype(BF16)
                j = h // 2
                return _normalised(_dot(e, v_s[keys_of(t), 2 * j * LANES:(2 * j + 2) * LANES]))

            heads = _pipelined((rb // tq) * H_A, scores, attend)
            attn = jnp.concatenate(
                [jnp.concatenate([_pair_select(heads[t * H_A + 2 * j], heads[t * H_A + 2 * j + 1])
                                  for j in range(H_A // 2)], axis=-1) for t in range(rb // tq)], axis=0)
            pa = _dot(hb, wm_ref[0, :, 0:2 * W_B])
            a_out = attn * _silu(pa[:, 0:W_A])
            gate_b = pa[:, W_A:W_A + W_B]
            g_b = _dot(hb, wm_ref[0, :, 4 * W_B:5 * W_B])

            t_cur = t_s[pl.ds(CONV_PAD + r0, rb), :]
            rid = lax.broadcasted_iota(jnp.int32, (rb, W_B), 0)
            if n <= rb:
                t_prev = jnp.where(rid % n == 0, 0.0, pltpu.roll(t_cur, 1, 0))
                t_next = jnp.where(rid % n == n - 1, 0.0, pltpu.roll(t_cur, rb - 1, 0))
            else:
                t_up = t_s[pl.ds(r0, CONV_PAD), :]
                t_dn = t_s[pl.ds(CONV_PAD + r0 + rb, CONV_PAD), :]
                t_prev = jnp.where(rid == 0, t_up[CONV_PAD - 1:CONV_PAD, :], pltpu.roll(t_cur, 1, 0))
                t_next = jnp.where(rid == rb - 1, t_dn[0:1, :], pltpu.roll(t_cur, rb - 1, 0))
            conv = cb_ref[0] + t_prev * cw_ref[0, 0:1, :]
            conv = conv + t_cur * cw_ref[0, 1:2, :]
            conv = conv + t_next * cw_ref[0, 2:3, :]
            b_out = gate_b * conv * _silu(g_b)

            y = (_dot(a_out.astype(BF16), wout_ref[0, 0:W_A, :])
                 + _dot(b_out.astype(BF16), wout_ref[0, W_A:W_A + W_B, :]))
            res = _group_rows(x_ref, g, gs, r0, rb) + mod_ref[0, 0, 2:3, :] * y
            if final:
                res = _rms(res, fg_ref[...])
            _set_group_rows(o_ref, g, gs, r0, rb, res)

        _row_loop(rows_g, rb, phase3)

    _loop(n_seq // gs, group)


def _odd_kernel(*refs, n, n_ctx, n_seq, final, fresh, layer):
    rope = n_ctx > 0
    it = iter(refs)
    x_ref, mod_ref, ng_ref = next(it), next(it), next(it)
    win_ref, wout_ref = next(it), next(it)
    sink_ref, gg_ref, ws_ref, bs_ref = next(it), next(it), next(it), next(it)
    if rope:
        kc_ref, vc_ref, tab_ref = next(it), next(it), next(it)
    if final:
        fg_ref = next(it)
    if not rope and not fresh:
        next(it), next(it)
    o_ref = next(it)
    if not rope:
        k_out, v_out = next(it), next(it)
    h_s, kloc_s, vloc_s, s_s = next(it), next(it), next(it), next(it)
    if rope:
        kctx_s, vctx_s = next(it), next(it)

    pad = WINDOW if rope else 0
    gs, rows_g, rb = _grouping(n, n_seq, ODD_MAX_ROWS)
    tq = min(rb, n, WINDOW_ROWS)
    span = tq + 2 * pad
    kw, vw = LANES, 2 * LANES
    q0, k0, g0 = 0, W_C, W_C + 2 * KV_C * HD_C
    q_scale = np.float32(HD_C ** -0.5 * LOG2E)

    def store_keys(dst, rows, x):
        a, b = _dup_halves(x)
        dst[rows, 0:LANES] = a.astype(BF16)
        dst[rows, LANES:2 * LANES] = b.astype(BF16)

    def store_values(dst, rows, x):
        a, b = _dup_halves(x)
        ones = jnp.ones(x.shape, BF16)
        dst[rows, 0:LANES] = a.astype(BF16)
        dst[rows, LANES:2 * LANES] = ones
        dst[rows, 2 * LANES:3 * LANES] = b.astype(BF16)
        dst[rows, 3 * LANES:4 * LANES] = ones

    def group(g):
        if rope:
            kloc_s[0:pad, :] = jnp.zeros((pad, KV_C * kw), BF16)
            kloc_s[pad + n:2 * pad + n, :] = jnp.zeros((pad, KV_C * kw), BF16)
            vloc_s[0:pad, :] = jnp.zeros((pad, KV_C * vw), BF16)
            vloc_s[pad + n:2 * pad + n, :] = jnp.zeros((pad, KV_C * vw), BF16)
            store_keys(kctx_s, slice(None), kc_ref[0, 0].T)
            store_values(vctx_s, slice(None), vc_ref[0, 0].T)

        def phase1(r0):
            rows = pl.ds(r0, rb)
            hb = _modulated_norm(_group_rows(x_ref, g, gs, r0, rb), mod_ref, ng_ref).astype(BF16)
            h_s[rows, :] = hb
            p1 = _dot(hb, win_ref[0, :, k0:g0])
            k = p1[:, 0:LANES]
            v = p1[:, LANES:2 * LANES]
            if rope:
                k = _rope(k, tab_ref[rows, :], HD_C // 4)
            else:
                for s in range(gs):
                    _store_cache(k_out, g * gs + s, k[s * n:(s + 1) * n, :].T)
                    _store_cache(v_out, g * gs + s, v[s * n:(s + 1) * n, :].T)
            lrows = pl.ds(pad + r0, rb)
            store_keys(kloc_s, lrows, k)
            store_values(vloc_s, lrows, v)

        _row_loop(rows_g, rb, phase1)

        def phase3(r0):
            rows = pl.ds(r0, rb)
            hb = h_s[rows, :]
            pq = _dot(hb, win_ref[0, :, q0:k0])

            tiles = []
            for t in range(rb // tq):
                t0 = r0 + t * tq
                pqt = pq[t * tq:(t + 1) * tq, :]
                if rope:
                    tab = tab_ref[pl.ds(t0, tq), :]
                    kl = kloc_s[pl.ds(t0, span), :]
                    vl = vloc_s[pl.ds(t0, span), :]
                    ii = lax.broadcasted_iota(jnp.int32, (tq, span), 0)
                    jj = lax.broadcasted_iota(jnp.int32, (tq, span), 1)
                    kpos = jj + (t0 - WINDOW)
                    dlt = jj - ii
                    mask = (dlt >= 0) & (dlt <= 2 * WINDOW) & (kpos >= 0) & (kpos < n)
                else:
                    kl, vl, mask = kloc_s[t * n:(t + 1) * n, :], vloc_s[t * n:(t + 1) * n, :], None
                qhs = []
                for m in range(H_C // 2):
                    qb = pqt[:, m * LANES:(m + 1) * LANES]
                    if rope:
                        qb = _rope(qb, tab, HD_C // 4)
                    qb = qb * q_scale
                    lo = _lane(qb.shape) < LANES // 2
                    qhs.append(jnp.where(lo, qb, 0.0).astype(BF16))
                    qhs.append(jnp.where(lo, 0.0, qb).astype(BF16))
                tiles.append((kl, vl, mask, qhs))

            def kv_head(h):
                return h // (H_C // KV_C)

            def scores(i):
                (kl, _, _, qhs), h = tiles[i // H_C], i % H_C
                j = kv_head(h)
                if rope:
                    s_s[i % 2, :, 0:n_ctx] = _dot_t(qhs[h], kctx_s[:, j * kw:(j + 1) * kw])
                    s_s[i % 2, :, n_ctx:] = _dot_t(qhs[h], kl[:, j * kw:(j + 1) * kw])
                else:
                    s_s[i % 2] = _dot_t(qhs[h], kl[:, j * kw:(j + 1) * kw])

            def attend(i):
                (_, vl, mask, _), h = tiles[i // H_C], i % H_C
                j = kv_head(h)
                sk = sink_ref[layer, h] * np.float32(LOG2E)
                if rope:
                    sc = s_s[i % 2, :, 0:n_ctx]
                    sl = jnp.where(mask, s_s[i % 2, :, n_ctx:], NEG_INF)
                    mx = jnp.maximum(jnp.max(sl, axis=-1, keepdims=True), jnp.max(sc, axis=-1, keepdims=True))
                else:
                    sl = s_s[i % 2]
                    mx = jnp.max(sl, axis=-1, keepdims=True)
                mx = jnp.maximum(mx, sk)
                o = _dot(jnp.exp2(sl - mx).astype(BF16), vl[:, j * vw:(j + 1) * vw])
                if rope:
                    o = o + _dot(jnp.exp2(sc - mx).astype(BF16), vctx_s[:, j * vw:(j + 1) * vw])
                return o[:, 0:LANES] * (1.0 / (o[:, LANES:2 * LANES] + jnp.exp2(sk - mx)))

            heads = _pipelined(len(tiles) * H_C, scores, attend)
            attn = jnp.concatenate(
                [jnp.concatenate([_pair_select(heads[t * H_C + 2 * m], heads[t * H_C + 2 * m + 1])
                                  for m in range(H_C // 2)], axis=-1) for t in range(len(tiles))], axis=0)
            pg = _dot(hb, win_ref[0, :, g0:g0 + W_C + 3 * W_D])
            c_out = attn * _silu(pg[:, 0:W_C])
            u = pg[:, W_C:W_C + W_D]
            vv = pg[:, W_C + W_D:W_C + 2 * W_D]
            g_d = pg[:, W_C + 2 * W_D:W_C + 3 * W_D]

            vvn = _rms(_gelu(vv), gg_ref[0]).astype(BF16)
            chunks = []
            for c in range(rb // CHUNK):
                cols = []
                for jp in range(G_D // 2):
                    vp = vvn[c * CHUNK:(c + 1) * CHUNK, jp * LANES:(jp + 1) * LANES]
                    o = _dot(ws_ref[0, jp], vp)
                    cols.append(_pair_select(o[0:CHUNK, :], o[CHUNK:2 * CHUNK, :]))
                chunks.append(jnp.concatenate(cols, axis=-1) + bs_ref[0])
            mixed = jnp.concatenate(chunks, axis=0)
            d_out = _gelu(u) * mixed * _silu(g_d)

            y = (_dot(c_out.astype(BF16), wout_ref[0, 0:W_C, :])
                 + _dot(d_out.astype(BF16), wout_ref[0, W_C:W_C + W_D, :]))
            res = _group_rows(x_ref, g, gs, r0, rb) + mod_ref[0, 0, 2:3, :] * y
            if final:
                res = _rms(res, fg_ref[...])
            _set_group_rows(o_ref, g, gs, r0, rb, res)

        _row_loop(rows_g, rb, phase3)

    _loop(n_seq // gs, group)


def _mod_kernel(c_ref, w_ref, b_ref, o_ref):
    a = _silu(c_ref[...]).astype(BF16)
    o_ref[0] = _dot(a, w_ref[0].astype(BF16)) + b_ref[0]


def _modulation(cvec, ada_w, ada_b):
    rows = cvec.shape[0]
    tn = D_MODEL
    return pl.pallas_call(
        _mod_kernel,
        grid=(DEPTH, 3 * D_MODEL // tn),
        in_specs=[
            pl.BlockSpec((rows, D_MODEL), lambda l, j: (0, 0)),
            pl.BlockSpec((1, D_MODEL, tn), lambda l, j: (l, 0, j)),
            pl.BlockSpec((1, 1, tn), lambda l, j: (l, 0, j)),
        ],
        out_specs=pl.BlockSpec((1, rows, tn), lambda l, j: (l, 0, j)),
        out_shape=jax.ShapeDtypeStruct((DEPTH, rows, 3 * D_MODEL), F32),
        compiler_params=pltpu.CompilerParams(dimension_semantics=("arbitrary", "arbitrary")),
        name="modulation",
    )(cvec, ada_w, ada_b.reshape(DEPTH, 1, 3 * D_MODEL))


def _split_even_kernel(wt_ref, head_ref, main_ref):
    kv_end = Q_RANK + KV_RANK

    def put(dst, col, rows_f32):
        dst[0, :, col:col + LANES] = rows_f32.T.astype(BF16)

    for j in range(kv_end // LANES):
        put(head_ref, j * LANES, wt_ref[0, j * LANES:(j + 1) * LANES, :])
    k_r = wt_ref[0, kv_end:EVEN_HEAD, :]
    put(head_ref, kv_end, jnp.concatenate(
        [jnp.zeros((NOPE_A, D_MODEL), F32), k_r, jnp.zeros((LANES - QK_A, D_MODEL), F32)], axis=0))
    for j in range(main_ref.shape[2] // LANES):
        put(main_ref, j * LANES, wt_ref[0, EVEN_HEAD + j * LANES:EVEN_HEAD + (j + 1) * LANES, :])


def _split_even(w_in_t):
    n_layers, cols, d = w_in_t.shape
    n_head, n_main = Q_RANK + KV_RANK + LANES, cols - EVEN_HEAD
    return pl.pallas_call(
        _split_even_kernel,
        grid=(n_layers,),
        in_specs=[pl.BlockSpec((1, cols, d), lambda l: (l, 0, 0))],
        out_specs=[pl.BlockSpec((1, d, n_head), lambda l: (l, 0, 0)),
                   pl.BlockSpec((1, d, n_main), lambda l: (l, 0, 0))],
        out_shape=[jax.ShapeDtypeStruct((n_layers, d, n_head), BF16),
                   jax.ShapeDtypeStruct((n_layers, d, n_main), BF16)],
        compiler_params=pltpu.CompilerParams(dimension_semantics=("arbitrary",), vmem_limit_bytes=VMEM_LIMIT),
        name="split_even_w_in",
    )(w_in_t)


def _full(a):
    nd = a.ndim
    return a, pl.BlockSpec(a.shape, lambda b, _nd=nd: (0,) * _nd)


def _per_step(a, n_seq):
    nd = a.ndim
    return a, pl.BlockSpec((n_seq,) + a.shape[1:], lambda b, _nd=nd: (b,) + (0,) * (_nd - 1))


def _of_layer(a, i):
    nd = a.ndim
    return a, pl.BlockSpec((1,) + a.shape[1:], lambda b, _nd=nd, _i=i: (_i,) + (0,) * (_nd - 1))


def _seq_of_layer(a, i):
    nd = a.ndim
    return a, pl.BlockSpec((1, 1) + a.shape[2:], lambda b, _nd=nd, _i=i: (b, _i) + (0,) * (_nd - 2))


def _mod_spec(mod, l, row):
    if row is None:
        return mod, pl.BlockSpec((1, 1, 3, D_MODEL), lambda b, _l=l: (_l, b, 0, 0))
    return mod, pl.BlockSpec((1, 1, 3, D_MODEL), lambda b, _l=l, _r=row: (_l, _r, 0, 0))


def _rope_cs(n, rot_dim):
    rows = n // GRID_W
    r = np.repeat(np.arange(rows, dtype=np.float32), GRID_W)
    col = np.tile(np.arange(GRID_W, dtype=np.float32), rows)
    nf = rot_dim // 4
    inv = np.float32(ROPE_BASE) ** (-np.arange(nf, dtype=np.float32) / np.float32(nf))
    ar = r[:, None] * inv
    ac = col[:, None] * inv
    cos = np.concatenate([np.cos(ar), np.cos(ar), np.cos(ac), np.cos(ac)], axis=-1)
    sin = np.concatenate([-np.sin(ar), np.sin(ar), -np.sin(ac), np.sin(ac)], axis=-1)
    return cos.astype(np.float32), sin.astype(np.float32)


def _layer_call(body, x, n_seq, operands, caches, slot, scratch, name):
    b, n, _ = x.shape
    fresh = slot == 0
    shapes = [c if fresh else c.shape for c in caches]
    in_specs = [spec for _, spec in operands]
    args = [a for a, _ in operands]
    aliases = {}
    if not fresh:
        in_specs += [pl.BlockSpec(memory_space=pl.ANY) for _ in caches]
        args += list(caches)
        aliases = {len(operands) + k: 1 + k for k in range(len(caches))}
    out_shape = [jax.ShapeDtypeStruct(x.shape, F32)] + [jax.ShapeDtypeStruct(s, F32) for s in shapes]
    out_specs = [pl.BlockSpec((n_seq, n, D_MODEL), lambda i: (i, 0, 0))]
    for s in shapes:
        if fresh:
            out_specs.append(pl.BlockSpec((n_seq,) + tuple(s[1:]), lambda i: (i, 0, 0, 0)))
        else:
            out_specs.append(pl.BlockSpec((n_seq, 1) + tuple(s[2:]), lambda i, _s=slot: (i, _s, 0, 0)))
    return pl.pallas_call(
        body,
        grid=(b // n_seq,),
        in_specs=in_specs,
        out_specs=out_specs,
        out_shape=out_shape,
        input_output_aliases=aliases,
        scratch_shapes=scratch,
        compiler_params=pltpu.CompilerParams(dimension_semantics=("arbitrary",), vmem_limit_bytes=VMEM_LIMIT),
        name=name,
    )(*args)


def _store_cache(ref, sq, value):
    ref[sq, 0] = value
    for l in range(1, ref.shape[1]):
        ref[sq, l] = jnp.zeros(value.shape, value.dtype)


def _seqs_per_step(b, ctx):
    return 1 if ctx is not None else int(np.gcd(b, CTX_SEQS_PER_STEP))


def _even_layer(x, mod_op, i, p, ctx, tab, caches, final_g, name):
    b, n, _ = x.shape
    n_ctx = 0 if ctx is None else ctx[0].shape[2]
    n_seq = _seqs_per_step(b, ctx)
    ops = [_per_step(x, n_seq), mod_op, _of_layer(p["norm_g"], 2 * i),
           _of_layer(p["w_head"], i), _of_layer(p["w_main"], i), _of_layer(p["w_uq"], i), _of_layer(p["w_uk"], i),
           _of_layer(p["w_uv"], i), _of_layer(p["ev_w_out"], i),
           _of_layer(p["gq"], i), _of_layer(p["gkv"], i), _of_layer(p["conv_w"], i), _of_layer(p["conv_b"], i)]
    if ctx is not None:
        ops += [_seq_of_layer(ctx[0], i), _per_step(ctx[1][i], 1), _full(tab)]
    if final_g is not None:
        ops.append(_full(final_g))
    _, rows_g, rb = _grouping(n, n_seq, EVEN_MAX_ROWS)
    nk_g = n_ctx + rows_g
    scratch = [pltpu.VMEM((rows_g, D_MODEL), BF16), pltpu.VMEM((rows_g, H_A * LANES), BF16),
               pltpu.VMEM((nk_g, LANES), BF16), pltpu.VMEM((nk_g, LANES), BF16),
               pltpu.VMEM((nk_g, H_A * LANES), BF16), pltpu.VMEM((nk_g, 2 * W_A), BF16),
               pltpu.VMEM((rows_g + 2 * CONV_PAD, W_B), F32),
               pltpu.VMEM((2, min(rb, n, DENSE_ROWS), n_ctx + n), F32)]
    assert ctx is not None or n <= rb, "the context pass writes one cache block per sequence"
    body = functools.partial(_even_kernel, n=n, n_ctx=n_ctx, n_seq=n_seq, final=final_g is not None, fresh=i == 0)
    return _layer_call(body, x, n_seq, ops, caches, i, scratch, name)


def _odd_layer(x, mod_op, i, p, ctx, tab, caches, final_g, name):
    b, n, _ = x.shape
    n_ctx = 0 if ctx is None else ctx[0].shape[3]
    n_seq = _seqs_per_step(b, ctx)
    ops = [_per_step(x, n_seq), mod_op, _of_layer(p["norm_g"], 2 * i + 1),
           _of_layer(p["od_w_in"], i), _of_layer(p["od_w_out"], i),
           (p["sink"], pl.BlockSpec(memory_space=pltpu.SMEM)), _of_layer(p["gg"], i), _of_layer(p["ws"], i),
           _of_layer(p["bs"], i)]
    if ctx is not None:
        ops += [_seq_of_layer(ctx[0], i), _seq_of_layer(ctx[1], i), _full(tab)]
    if final_g is not None:
        ops.append(_full(final_g))
    pad = WINDOW if ctx is not None else 0
    _, rows_g, rb = _grouping(n, n_seq, ODD_MAX_ROWS)
    tq = min(rb, n, WINDOW_ROWS)
    n_keys = n_ctx + tq + 2 * pad if ctx is not None else n
    scratch = [pltpu.VMEM((rows_g, D_MODEL), BF16), pltpu.VMEM((rows_g + 2 * pad, KV_C * LANES), BF16),
               pltpu.VMEM((rows_g + 2 * pad, KV_C * 2 * LANES), BF16), pltpu.VMEM((2, tq, n_keys), F32)]
    if ctx is not None:
        scratch += [pltpu.VMEM((n_ctx, KV_C * LANES), BF16), pltpu.VMEM((n_ctx, KV_C * 2 * LANES), BF16)]
    assert ctx is not None or n <= rb, "the context pass writes one cache block per sequence"
    body = functools.partial(_odd_kernel, n=n, n_ctx=n_ctx, n_seq=n_seq, final=final_g is not None, fresh=i == 0,
                             layer=i)
    return _layer_call(body, x, n_seq, ops, caches, i, scratch, name)


def kernel(x_prompt, x_sample, cache_mla_ckv, cache_mla_krope, cache_swa_k, cache_swa_v, c, c_ctx, ada_w, ada_b, norm_g, ev_w_in, ev_w_out, mla_gq, mla_gkv, mla_w_uq, mla_w_uk, mla_w_uv, conv_w, conv_b, od_w_in, od_w_out, swa_sink, gmlp_g, gmlp_ws, gmlp_bs, final_g):
    bp, n_p, _ = x_prompt.shape
    bs_, n_s, _ = x_sample.shape
    n_past = cache_mla_ckv.shape[2]
    n_even, n_odd = ev_w_in.shape[0], od_w_in.shape[0]

    n_rows = -(-(bs_ + 1) // 8) * 8
    cvec = jnp.concatenate([c, c_ctx[None, :], jnp.zeros((n_rows - bs_ - 1, D_MODEL), F32)], axis=0)
    mod = _modulation(cvec, ada_w, ada_b).reshape(DEPTH, n_rows, 3, D_MODEL)

    cos_a, sin_a = _rope_cs(n_s, ROPE_A)
    cos_c, sin_c = _rope_cs(n_s, HD_C)
    zero = lambda w: np.zeros((n_s, w), np.float32)
    tab_a = jnp.asarray(np.concatenate([np.ones((n_s, NOPE_A), np.float32), cos_a, zero(LANES - QK_A),
                                        zero(NOPE_A), sin_a, zero(LANES - QK_A)], axis=1))
    tab_c = jnp.asarray(np.concatenate([cos_c, cos_c, sin_c, sin_c], axis=1))

    uq = mla_w_uq.reshape(n_even, Q_RANK, H_A, QK_A)
    uk = mla_w_uk.reshape(n_even, KV_RANK, H_A, NOPE_A)
    w_head, w_main = _split_even(jnp.swapaxes(ev_w_in, 1, 2))
    p = {
        "norm_g": norm_g.reshape(DEPTH, 1, D_MODEL),
        "w_head": w_head,
        "w_main": w_main,
        "w_uq": jnp.pad(uq, ((0, 0), (0, 0), (0, 0), (0, LANES - QK_A))).reshape(n_even, Q_RANK, H_A * LANES).astype(BF16),
        "w_uk": jnp.pad(uk, ((0, 0), (0, 0), (0, 0), (0, LANES - NOPE_A))).reshape(n_even, KV_RANK, H_A * LANES).astype(BF16),
        "w_uv": mla_w_uv.astype(BF16),
        "ev_w_out": ev_w_out.astype(BF16),
        "gq": mla_gq.reshape(n_even, 1, Q_RANK),
        "gkv": mla_gkv.reshape(n_even, 1, KV_RANK),
        "conv_w": conv_w,
        "conv_b": conv_b.reshape(n_even, 1, W_B),
        "od_w_in": od_w_in.astype(BF16),
        "od_w_out": od_w_out.astype(BF16),
        "sink": swa_sink,
        "gg": gmlp_g.reshape(n_odd, 1, W_D),
        "ws": gmlp_ws.reshape(n_odd, G_D // 2, 2 * CHUNK, CHUNK).astype(BF16),
        "bs": jnp.repeat(jnp.swapaxes(gmlp_bs, 1, 2), DG_D, axis=2),
    }
    ckr = jnp.pad(cache_mla_krope, ((0, 0), (0, 0), (0, 0), (NOPE_A, LANES - QK_A)))
    ckr = [ckr[:, i] for i in range(n_even)]
    swa_k = jnp.transpose(cache_swa_k, (0, 1, 3, 4, 2)).reshape(bs_, n_odd, KV_C * HD_C, n_past)
    swa_v = jnp.transpose(cache_swa_v, (0, 1, 3, 4, 2)).reshape(bs_, n_odd, KV_C * HD_C, n_past)

    new_ckv = (bp, n_even, n_p, KV_RANK)
    new_kr = (bp, n_even, ROPE_A, n_p)
    new_k = (bp, n_odd, KV_C * HD_C, n_p)
    new_v = (bp, n_odd, KV_C * HD_C, n_p)

    fg = final_g.reshape(1, D_MODEL)
    xp, xs = x_prompt, x_sample
    for l in range(DEPTH):
        i = l // 2
        last = fg if l == DEPTH - 1 else None
        mod_p = _mod_spec(mod, l, bs_)
        mod_s = _mod_spec(mod, l, None)
        if l % 2 == 0:
            xp, new_ckv, new_kr = _even_layer(xp, mod_p, i, p, None, None, (new_ckv, new_kr), last, f"even{l}_ctx")
            (xs,) = _even_layer(xs, mod_s, i, p, (cache_mla_ckv, ckr), tab_a, (), last, f"even{l}_lat")
        else:
            xp, new_k, new_v = _odd_layer(xp, mod_p, i, p, None, None, (new_k, new_v), last, f"odd{l}_ctx")
            (xs,) = _odd_layer(xs, mod_s, i, p, (swa_k, swa_v), tab_c, (), last, f"odd{l}_lat")

    def heads_last(a):
        return jnp.transpose(a.reshape(bp, n_odd, KV_C, HD_C, n_p), (0, 1, 4, 2, 3))

    return (xp, xs, new_ckv, jnp.swapaxes(new_kr, 2, 3), heads_last(new_k), heads_last(new_v))
```

```python
import functools

import numpy as np
import jax
import jax.numpy as jnp
from jax import lax
from jax.experimental import pallas as pl
from jax.experimental.pallas import tpu as pltpu

D_MODEL = 1024
DEPTH = 4
GRID_W = 64
ROPE_BASE = 10000.0
EPS = 1e-6
NEG_INF = -1e30

H_A = 8
NOPE_A = 64
ROPE_A = 32
V_A = 64
Q_RANK = 256
KV_RANK = 128
QK_A = NOPE_A + ROPE_A
W_A = H_A * V_A
W_B = 512
CONV_W = 3
H_C = 8
KV_C = 2
HD_C = 64
WINDOW = 128
W_C = H_C * HD_C
G_D = 8
W_D = 512
CHUNK = 128
DG_D = W_D // G_D

EVEN_SPLIT = (Q_RANK, KV_RANK, ROPE_A, W_A, W_B, W_B, W_B, W_B)
ODD_SPLIT = (W_C, KV_C * HD_C, KV_C * HD_C, W_C, W_D, W_D, W_D)
EVEN_HEAD = Q_RANK + KV_RANK + ROPE_A

LANES = 128
EVEN_MAX_ROWS = 512
ODD_MAX_ROWS = 1024
DENSE_ROWS = 512
WINDOW_ROWS = 256
CONV_PAD = 8
CTX_SEQS_PER_STEP = 2
VMEM_LIMIT = 60 * 1024 * 1024

F32 = jnp.float32
BF16 = jnp.bfloat16
LOG2E = float(np.log2(np.e))


def _dot(a, b):
    return jnp.dot(a, b, preferred_element_type=F32)


def _dot_t(a, b):
    return lax.dot_general(a, b, (((1,), (1,)), ((), ())), preferred_element_type=F32)


def _rms(x, g):
    return x * lax.rsqrt(jnp.mean(x * x, axis=-1, keepdims=True) + EPS) * g


def _silu(x):
    return x * (1.0 / (1.0 + jnp.exp(-x)))


def _gelu(x):
    c = np.float32(np.sqrt(2.0 / np.pi))
    return 0.5 * x * (1.0 + jnp.tanh(c * (x + 0.044715 * (x * x * x))))


def _lane(shape):
    return lax.broadcasted_iota(jnp.int32, shape, 1)


def _pair_select(lo, hi):
    return jnp.where(_lane(lo.shape) < LANES // 2, lo, hi)


def _rope(x, tab, quarter):
    fwd = pltpu.roll(x, LANES - quarter, 1)
    bwd = pltpu.roll(x, quarter, 1)
    partner = jnp.where((_lane(x.shape) & quarter) == 0, fwd, bwd)
    return x * tab[:, 0:LANES] + partner * tab[:, LANES:2 * LANES]


def _dup_halves(x):
    sw = pltpu.roll(x, LANES // 2, 1)
    lo = _lane(x.shape) < LANES // 2
    return jnp.where(lo, x, sw), jnp.where(lo, sw, x)


def _modulated_norm(x, mod_ref, ng_ref):
    shift = mod_ref[0, 0, 0:1, :]
    scale = mod_ref[0, 0, 1:2, :]
    return _rms(x, ng_ref[0]) * (1.0 + scale) + shift


def _pipelined(n, produce, consume):
    produce(0)
    outs = []
    for i in range(n):
        if i + 1 < n:
            produce(i + 1)
        outs.append(consume(i))
    return outs


def _normalised(o):
    return o[:, 0:LANES] * (1.0 / o[:, LANES:2 * LANES])


def _loop(n, body):
    if n == 1:
        body(0)
    else:
        def step(i, carry):
            body(i)
            return carry
        lax.fori_loop(0, n, step, 0)


def _grouping(n, n_seq, max_rows):
    gs = max(1, min(n_seq, max_rows // n))
    rows_g = gs * n
    return gs, rows_g, min(rows_g, max_rows)


def _group_rows(ref, g, gs, r0, rb):
    if gs == 1:
        return ref[g, pl.ds(r0, rb), :]
    return ref[pl.ds(g * gs, gs), :, :].reshape(rb, ref.shape[-1])


def _set_group_rows(ref, g, gs, r0, rb, value):
    if gs == 1:
        ref[g, pl.ds(r0, rb), :] = value
    else:
        ref[pl.ds(g * gs, gs), :, :] = value.reshape(gs, rb // gs, ref.shape[-1])


def _row_loop(n_rows, rb, body):
    _loop(n_rows // rb, lambda i: body(i * rb if isinstance(i, int) else pl.multiple_of(i * rb, rb)))


def _even_kernel(*refs, n, n_ctx, n_seq, final, fresh):
    rope = n_ctx > 0
    it = iter(refs)
    x_ref, mod_ref, ng_ref = next(it), next(it), next(it)
    wh_ref, wm_ref, wuq_ref, wuk_ref, wuv_ref, wout_ref = (next(it) for _ in range(6))
    gq_ref, gkv_ref, cw_ref, cb_ref = next(it), next(it), next(it), next(it)
    if rope:
        cckv_ref, ckr_ref, tab_ref = next(it), next(it), next(it)
    if final:
        fg_ref = next(it)
    if not rope and not fresh:
        next(it), next(it)
    o_ref = next(it)
    if not rope:
        ckv_out, kr_out = next(it), next(it)
    h_s, q_s, ckv_s, kr_s, k_s, v_s, t_s, s_s = (next(it) for _ in range(8))

    gs, rows_g, rb = _grouping(n, n_seq, EVEN_MAX_ROWS)
    nk = n_ctx + n
    tq = min(rb, n, DENSE_ROWS)
    q_scale = np.float32(QK_A ** -0.5 * LOG2E)

    zpad = jnp.zeros((CONV_PAD, W_B), F32)
    t_s[0:CONV_PAD, :] = zpad
    t_s[CONV_PAD + rows_g:2 * CONV_PAD + rows_g, :] = zpad

    def group(g):
        if rope:
            ckv_s[0:n_ctx, :] = cckv_ref[0, 0].astype(BF16)
            kr_s[0:n_ctx, :] = ckr_ref[0].astype(BF16)

        def phase1(r0):
            rows = pl.ds(r0, rb)
            hb = _modulated_norm(_group_rows(x_ref, g, gs, r0, rb), mod_ref, ng_ref).astype(BF16)
            h_s[rows, :] = hb
            ph = _dot(hb, wh_ref[0])
            qn = _rms(ph[:, 0:Q_RANK], gq_ref[0]).astype(BF16)
            qq = _dot(qn, wuq_ref[0])
            ckv = _rms(ph[:, Q_RANK:Q_RANK + KV_RANK], gkv_ref[0])
            kr = ph[:, Q_RANK + KV_RANK:Q_RANK + KV_RANK + LANES]
            if rope:
                tab = tab_ref[rows, :]
                kr = _rope(kr, tab, ROPE_A // 4)
                for h in range(H_A):
                    qh = _rope(qq[:, h * LANES:(h + 1) * LANES], tab, ROPE_A // 4)
                    q_s[rows, h * LANES:(h + 1) * LANES] = (qh * q_scale).astype(BF16)
            else:
                q_s[rows, :] = (qq * q_scale).astype(BF16)
                for s in range(gs):
                    _store_cache(ckv_out, g * gs + s, ckv[s * n:(s + 1) * n, :])
                    _store_cache(kr_out, g * gs + s, kr[s * n:(s + 1) * n, :].T[NOPE_A:QK_A, :])
            krows = pl.ds(n_ctx + r0, rb)
            ckv_s[krows, :] = ckv.astype(BF16)
            kr_s[krows, :] = kr.astype(BF16)
            pc = _dot(hb, wm_ref[0, :, 2 * W_B:4 * W_B])
            t_s[pl.ds(CONV_PAD + r0, rb), :] = pc[:, 0:W_B] * pc[:, W_B:2 * W_B]

        _row_loop(rows_g, rb, phase1)

        rb2 = int(np.gcd(rb, n_ctx + rows_g))

        def phase2(r0):
            rows = pl.ds(r0, rb2)
            c = ckv_s[rows, :]
            kn = _dot(c, wuk_ref[0])
            krf = kr_s[rows, :].astype(F32)
            for h in range(H_A):
                k_s[rows, h * LANES:(h + 1) * LANES] = (kn[:, h * LANES:(h + 1) * LANES] + krf).astype(BF16)
            v = _dot(c, wuv_ref[0]).astype(BF16)
            ones = jnp.ones((rb2, LANES), BF16)
            for j in range(H_A // 2):
                v_s[rows, 2 * j * LANES:(2 * j + 1) * LANES] = v[:, j * LANES:(j + 1) * LANES]
                v_s[rows, (2 * j + 1) * LANES:(2 * j + 2) * LANES] = ones

        _row_loop(n_ctx + rows_g, rb2, phase2)

        def phase3(r0):
            rows = pl.ds(r0, rb)
            hb = h_s[rows, :]

            def keys_of(t):
                return slice(None) if gs == 1 else slice(t * n, (t + 1) * n)

            def scores(i):
                t, h = divmod(i, H_A)
                s_s[i % 2] = _dot_t(q_s[pl.ds(r0 + t * tq, tq), h * LANES:(h + 1) * LANES],
                                    k_s[keys_of(t), h * LANES:(h + 1) * LANES])

            def attend(i):
                t, h = divmod(i, H_A)
                s = s_s[i % 2]
                e = jnp.exp2(s - jnp.max(s, axis=-1, keepdims=True)).astype(BF16)
                j = h // 2
                return _normalised(_dot(e, v_s[keys_of(t), 2 * j * LANES:(2 * j + 2) * LANES]))

            heads = _pipelined((rb // tq) * H_A, scores, attend)
            attn = jnp.concatenate(
                [jnp.concatenate([_pair_select(heads[t * H_A + 2 * j], heads[t * H_A + 2 * j + 1])
                                  for j in range(H_A // 2)], axis=-1) for t in range(rb // tq)], axis=0)
            pa = _dot(hb, wm_ref[0, :, 0:2 * W_B])
            a_out = attn * _silu(pa[:, 0:W_A])
            gate_b = pa[:, W_A:W_A + W_B]
            g_b = _dot(hb, wm_ref[0, :, 4 * W_B:5 * W_B])

            t_cur = t_s[pl.ds(CONV_PAD + r0, rb), :]
            rid = lax.broadcasted_iota(jnp.int32, (rb, W_B), 0)
            if n <= rb:
                t_prev = jnp.where(rid % n == 0, 0.0, pltpu.roll(t_cur, 1, 0))
                t_next = jnp.where(rid % n == n - 1, 0.0, pltpu.roll(t_cur, rb - 1, 0))
            else:
                t_up = t_s[pl.ds(r0, CONV_PAD), :]
                t_dn = t_s[pl.ds(CONV_PAD + r0 + rb, CONV_PAD), :]
                t_prev = jnp.where(rid == 0, t_up[CONV_PAD - 1:CONV_PAD, :], pltpu.roll(t_cur, 1, 0))
                t_next = jnp.where(rid == rb - 1, t_dn[0:1, :], pltpu.roll(t_cur, rb - 1, 0))
            conv = cb_ref[0] + t_prev * cw_ref[0, 0:1, :]
            conv = conv + t_cur * cw_ref[0, 1:2, :]
            conv = conv + t_next * cw_ref[0, 2:3, :]
            b_out = gate_b * conv * _silu(g_b)

            y = (_dot(a_out.astype(BF16), wout_ref[0, 0:W_A, :])
                 + _dot(b_out.astype(BF16), wout_ref[0, W_A:W_A + W_B, :]))
            res = _group_rows(x_ref, g, gs, r0, rb) + mod_ref[0, 0, 2:3, :] * y
            if final:
                res = _rms(res, fg_ref[...])
            _set_group_rows(o_ref, g, gs, r0, rb, res)

        _row_loop(rows_g, rb, phase3)

    _loop(n_seq // gs, group)


def _odd_kernel(*refs, n, n_ctx, n_seq, final, fresh, layer):
    rope = n_ctx > 0
    it = iter(refs)
    x_ref, mod_ref, ng_ref = next(it), next(it), next(it)
    win_ref, wout_ref = next(it), next(it)
    sink_ref, gg_ref, ws_ref, bs_ref = next(it), next(it), next(it), next(it)
    if rope:
        kc_ref, vc_ref, tab_ref = next(it), next(it), next(it)
    if final:
        fg_ref = next(it)
    if not rope and not fresh:
        next(it), next(it)
    o_ref = next(it)
    if not rope:
        k_out, v_out = next(it), next(it)
    h_s, kloc_s, vloc_s, s_s = next(it), next(it), next(it), next(it)
    if rope:
        kctx_s, vctx_s = next(it), next(it)

    pad = WINDOW if rope else 0
    gs, rows_g, rb = _grouping(n, n_seq, ODD_MAX_ROWS)
    tq = min(rb, n, WINDOW_ROWS)
    span = tq + 2 * pad
    kw, vw = LANES, 2 * LANES
    q0, k0, g0 = 0, W_C, W_C + 2 * KV_C * HD_C
    q_scale = np.float32(HD_C ** -0.5 * LOG2E)

    def store_keys(dst, rows, x):
        a, b = _dup_halves(x)
        dst[rows, 0:LANES] = a.astype(BF16)
        dst[rows, LANES:2 * LANES] = b.astype(BF16)

    def store_values(dst, rows, x):
        a, b = _dup_halves(x)
        ones = jnp.ones(x.shape, BF16)
        dst[rows, 0:LANES] = a.astype(BF16)
        dst[rows, LANES:2 * LANES] = ones
        dst[rows, 2 * LANES:3 * LANES] = b.astype(BF16)
        dst[rows, 3 * LANES:4 * LANES] = ones

    def group(g):
        if rope:
            kloc_s[0:pad, :] = jnp.zeros((pad, KV_C * kw), BF16)
            kloc_s[pad + n:2 * pad + n, :] = jnp.zeros((pad, KV_C * kw), BF16)
            vloc_s[0:pad, :] = jnp.zeros((pad, KV_C * vw), BF16)
            vloc_s[pad + n:2 * pad + n, :] = jnp.zeros((pad, KV_C * vw), BF16)
            store_keys(kctx_s, slice(None), kc_ref[0, 0].T)
            store_values(vctx_s, slice(None), vc_ref[0, 0].T)

        def phase1(r0):
            rows = pl.ds(r0, rb)
            hb = _modulated_norm(_group_rows(x_ref, g, gs, r0, rb), mod_ref, ng_ref).astype(BF16)
            h_s[rows, :] = hb
            p1 = _dot(hb, win_ref[0, :, k0:g0])
            k = p1[:, 0:LANES]
            v = p1[:, LANES:2 * LANES]
            if rope:
                k = _rope(k, tab_ref[rows, :], HD_C // 4)
            else:
                for s in range(gs):
                    _store_cache(k_out, g * gs + s, k[s * n:(s + 1) * n, :].T)
                    _store_cache(v_out, g * gs + s, v[s * n:(s + 1) * n, :].T)
            lrows = pl.ds(pad + r0, rb)
            store_keys(kloc_s, lrows, k)
            store_values(vloc_s, lrows, v)

        _row_loop(rows_g, rb, phase1)

        def phase3(r0):
            rows = pl.ds(r0, rb)
            hb = h_s[rows, :]
            pq = _dot(hb, win_ref[0, :, q0:k0])

            tiles = []
            for t in range(rb // tq):
                t0 = r0 + t * tq
                pqt = pq[t * tq:(t + 1) * tq, :]
                if rope:
                    tab = tab_ref[pl.ds(t0, tq), :]
                    kl = kloc_s[pl.ds(t0, span), :]
                    vl = vloc_s[pl.ds(t0, span), :]
                    ii = lax.broadcasted_iota(jnp.int32, (tq, span), 0)
                    jj = lax.broadcasted_iota(jnp.int32, (tq, span), 1)
                    kpos = jj + (t0 - WINDOW)
                    dlt = jj - ii
                    mask = (dlt >= 0) & (dlt <= 2 * WINDOW) & (kpos >= 0) & (kpos < n)
                else:
                    kl, vl, mask = kloc_s[t * n:(t + 1) * n, :], vloc_s[t * n:(t + 1) * n, :], None
                qhs = []
                for m in range(H_C // 2):
                    qb = pqt[:, m * LANES:(m + 1) * LANES]
                    if rope:
                        qb = _rope(qb, tab, HD_C // 4)
                    qb = qb * q_scale
                    lo = _lane(qb.shape) < LANES // 2
                    qhs.append(jnp.where(lo, qb, 0.0).astype(BF16))
                    qhs.append(jnp.where(lo, 0.0, qb).astype(BF16))
                tiles.append((kl, vl, mask, qhs))

            def kv_head(h):
                return h // (H_C // KV_C)

            def scores(i):
                (kl, _, _, qhs), h = tiles[i // H_C], i % H_C
                j = kv_head(h)
                if rope:
                    s_s[i % 2, :, 0:n_ctx] = _dot_t(qhs[h], kctx_s[:, j * kw:(j + 1) * kw])
                    s_s[i % 2, :, n_ctx:] = _dot_t(qhs[h], kl[:, j * kw:(j + 1) * kw])
                else:
                    s_s[i % 2] = _dot_t(qhs[h], kl[:, j * kw:(j + 1) * kw])

            def attend(i):
                (_, vl, mask, _), h = tiles[i // H_C], i % H_C
                j = kv_head(h)
                sk = sink_ref[layer, h] * np.float32(LOG2E)
                if rope:
                    sc = s_s[i % 2, :, 0:n_ctx]
                    sl = jnp.where(mask, s_s[i % 2, :, n_ctx:], NEG_INF)
                    mx = jnp.maximum(jnp.max(sl, axis=-1, keepdims=True), jnp.max(sc, axis=-1, keepdims=True))
                else:
                    sl = s_s[i % 2]
                    mx = jnp.max(sl, axis=-1, keepdims=True)
                mx = jnp.maximum(mx, sk)
                o = _dot(jnp.exp2(sl - mx).astype(BF16), vl[:, j * vw:(j + 1) * vw])
                if rope:
                    o = o + _dot(jnp.exp2(sc - mx).astype(BF16), vctx_s[:, j * vw:(j + 1) * vw])
                return o[:, 0:LANES] * (1.0 / (o[:, LANES:2 * LANES] + jnp.exp2(sk - mx)))

            heads = _pipelined(len(tiles) * H_C, scores, attend)
            attn = jnp.concatenate(
                [jnp.concatenate([_pair_select(heads[t * H_C + 2 * m], heads[t * H_C + 2 * m + 1])
                                  for m in range(H_C // 2)], axis=-1) for t in range(len(tiles))], axis=0)
            pg = _dot(hb, win_ref[0, :, g0:g0 + W_C + 3 * W_D])
            c_out = attn * _silu(pg[:, 0:W_C])
            u = pg[:, W_C:W_C + W_D]
            vv = pg[:, W_C + W_D:W_C + 2 * W_D]
            g_d = pg[:, W_C + 2 * W_D:W_C + 3 * W_D]

            vvn = _rms(_gelu(vv), gg_ref[0]).astype(BF16)
            chunks = []
            for c in range(rb // CHUNK):
                cols = []
                for jp in range(G_D // 2):
                    vp = vvn[c * CHUNK:(c + 1) * CHUNK, jp * LANES:(jp + 1) * LANES]
                    lo = _lane(vp.shape) < LANES // 2
                    zero = jnp.zeros_like(vp)
                    rhs = jnp.concatenate([jnp.where(lo, vp, zero), jnp.where(lo, zero, vp)], axis=0)
                    cols.append(_dot(ws_ref[0, jp], rhs))
                chunks.append(jnp.concatenate(cols, axis=-1) + bs_ref[0])
            mixed = jnp.concatenate(chunks, axis=0)
            d_out = _gelu(u) * mixed * _silu(g_d)

            y = (_dot(c_out.astype(BF16), wout_ref[0, 0:W_C, :])
                 + _dot(d_out.astype(BF16), wout_ref[0, W_C:W_C + W_D, :]))
            res = _group_rows(x_ref, g, gs, r0, rb) + mod_ref[0, 0, 2:3, :] * y
            if final:
                res = _rms(res, fg_ref[...])
            _set_group_rows(o_ref, g, gs, r0, rb, res)

        _row_loop(rows_g, rb, phase3)

    _loop(n_seq // gs, group)


def _mod_kernel(c_ref, w_ref, b_ref, o_ref):
    a = _silu(c_ref[...]).astype(BF16)
    o_ref[0] = _dot(a, w_ref[0].astype(BF16)) + b_ref[0]


def _modulation(cvec, ada_w, ada_b):
    rows = cvec.shape[0]
    tn = 3 * D_MODEL
    return pl.pallas_call(
        _mod_kernel,
        grid=(DEPTH, 3 * D_MODEL // tn),
        in_specs=[
            pl.BlockSpec((rows, D_MODEL), lambda l, j: (0, 0)),
            pl.BlockSpec((1, D_MODEL, tn), lambda l, j: (l, 0, j)),
            pl.BlockSpec((1, 1, tn), lambda l, j: (l, 0, j)),
        ],
        out_specs=pl.BlockSpec((1, rows, tn), lambda l, j: (l, 0, j)),
        out_shape=jax.ShapeDtypeStruct((DEPTH, rows, 3 * D_MODEL), F32),
        compiler_params=pltpu.CompilerParams(dimension_semantics=("arbitrary", "arbitrary"),
                                             vmem_limit_bytes=VMEM_LIMIT),
        name="modulation",
    )(cvec, ada_w, ada_b.reshape(DEPTH, 1, 3 * D_MODEL))


def _split_even_kernel(wt_ref, head_ref, main_ref):
    kv_end = Q_RANK + KV_RANK

    def put(dst, col, rows_f32):
        dst[0, :, col:col + LANES] = rows_f32.T.astype(BF16)

    for j in range(kv_end // LANES):
        put(head_ref, j * LANES, wt_ref[0, j * LANES:(j + 1) * LANES, :])
    k_r = wt_ref[0, kv_end:EVEN_HEAD, :]
    put(head_ref, kv_end, jnp.concatenate(
        [jnp.zeros((NOPE_A, D_MODEL), F32), k_r, jnp.zeros((LANES - QK_A, D_MODEL), F32)], axis=0))
    for j in range(main_ref.shape[2] // LANES):
        put(main_ref, j * LANES, wt_ref[0, EVEN_HEAD + j * LANES:EVEN_HEAD + (j + 1) * LANES, :])


def _split_even(w_in_t):
    n_layers, cols, d = w_in_t.shape
    n_head, n_main = Q_RANK + KV_RANK + LANES, cols - EVEN_HEAD
    return pl.pallas_call(
        _split_even_kernel,
        grid=(n_layers,),
        in_specs=[pl.BlockSpec((1, cols, d), lambda l: (l, 0, 0))],
        out_specs=[pl.BlockSpec((1, d, n_head), lambda l: (l, 0, 0)),
                   pl.BlockSpec((1, d, n_main), lambda l: (l, 0, 0))],
        out_shape=[jax.ShapeDtypeStruct((n_layers, d, n_head), BF16),
                   jax.ShapeDtypeStruct((n_layers, d, n_main), BF16)],
        compiler_params=pltpu.CompilerParams(dimension_semantics=("arbitrary",), vmem_limit_bytes=VMEM_LIMIT),
        name="split_even_w_in",
    )(w_in_t)


def _full(a):
    nd = a.ndim
    return a, pl.BlockSpec(a.shape, lambda b, _nd=nd: (0,) * _nd)


def _per_step(a, n_seq):
    nd = a.ndim
    return a, pl.BlockSpec((n_seq,) + a.shape[1:], lambda b, _nd=nd: (b,) + (0,) * (_nd - 1))


def _of_layer(a, i):
    nd = a.ndim
    return a, pl.BlockSpec((1,) + a.shape[1:], lambda b, _nd=nd, _i=i: (_i,) + (0,) * (_nd - 1))


def _seq_of_layer(a, i):
    nd = a.ndim
    return a, pl.BlockSpec((1, 1) + a.shape[2:], lambda b, _nd=nd, _i=i: (b, _i) + (0,) * (_nd - 2))


def _mod_spec(mod, l, row):
    if row is None:
        return mod, pl.BlockSpec((1, 1, 3, D_MODEL), lambda b, _l=l: (_l, b, 0, 0))
    return mod, pl.BlockSpec((1, 1, 3, D_MODEL), lambda b, _l=l, _r=row: (_l, _r, 0, 0))


def _rope_cs(n, rot_dim):
    rows = n // GRID_W
    r = np.repeat(np.arange(rows, dtype=np.float32), GRID_W)
    col = np.tile(np.arange(GRID_W, dtype=np.float32), rows)
    nf = rot_dim // 4
    inv = np.float32(ROPE_BASE) ** (-np.arange(nf, dtype=np.float32) / np.float32(nf))
    ar = r[:, None] * inv
    ac = col[:, None] * inv
    cos = np.concatenate([np.cos(ar), np.cos(ar), np.cos(ac), np.cos(ac)], axis=-1)
    sin = np.concatenate([-np.sin(ar), np.sin(ar), -np.sin(ac), np.sin(ac)], axis=-1)
    return cos.astype(np.float32), sin.astype(np.float32)


def _layer_call(body, x, n_seq, operands, caches, slot, scratch, name):
    b, n, _ = x.shape
    fresh = slot == 0
    shapes = [c if fresh else c.shape for c in caches]
    in_specs = [spec for _, spec in operands]
    args = [a for a, _ in operands]
    aliases = {}
    if not fresh:
        in_specs += [pl.BlockSpec(memory_space=pl.ANY) for _ in caches]
        args += list(caches)
        aliases = {len(operands) + k: 1 + k for k in range(len(caches))}
    out_shape = [jax.ShapeDtypeStruct(x.shape, F32)] + [jax.ShapeDtypeStruct(s, F32) for s in shapes]
    out_specs = [pl.BlockSpec((n_seq, n, D_MODEL), lambda i: (i, 0, 0))]
    for s in shapes:
        if fresh:
            out_specs.append(pl.BlockSpec((n_seq,) + tuple(s[1:]), lambda i: (i, 0, 0, 0)))
        else:
            out_specs.append(pl.BlockSpec((n_seq, 1) + tuple(s[2:]), lambda i, _s=slot: (i, _s, 0, 0)))
    return pl.pallas_call(
        body,
        grid=(b // n_seq,),
        in_specs=in_specs,
        out_specs=out_specs,
        out_shape=out_shape,
        input_output_aliases=aliases,
        scratch_shapes=scratch,
        compiler_params=pltpu.CompilerParams(dimension_semantics=("arbitrary",), vmem_limit_bytes=VMEM_LIMIT),
        name=name,
    )(*args)


def _store_cache(ref, sq, value):
    ref[sq, 0] = value
    for l in range(1, ref.shape[1]):
        ref[sq, l] = jnp.zeros(value.shape, value.dtype)


def _seqs_per_step(b, ctx):
    return 1 if ctx is not None else int(np.gcd(b, CTX_SEQS_PER_STEP))


def _even_layer(x, mod_op, i, p, ctx, tab, caches, final_g, name):
    b, n, _ = x.shape
    n_ctx = 0 if ctx is None else ctx[0].shape[2]
    n_seq = _seqs_per_step(b, ctx)
    ops = [_per_step(x, n_seq), mod_op, _of_layer(p["norm_g"], 2 * i),
           _of_layer(p["w_head"], i), _of_layer(p["w_main"], i), _of_layer(p["w_uq"], i), _of_layer(p["w_uk"], i),
           _of_layer(p["w_uv"], i), _of_layer(p["ev_w_out"], i),
           _of_layer(p["gq"], i), _of_layer(p["gkv"], i), _of_layer(p["conv_w"], i), _of_layer(p["conv_b"], i)]
    if ctx is not None:
        ops += [_seq_of_layer(ctx[0], i), _per_step(ctx[1][i], 1), _full(tab)]
    if final_g is not None:
        ops.append(_full(final_g))
    _, rows_g, rb = _grouping(n, n_seq, EVEN_MAX_ROWS)
    nk_g = n_ctx + rows_g
    scratch = [pltpu.VMEM((rows_g, D_MODEL), BF16), pltpu.VMEM((rows_g, H_A * LANES), BF16),
               pltpu.VMEM((nk_g, LANES), BF16), pltpu.VMEM((nk_g, LANES), BF16),
               pltpu.VMEM((nk_g, H_A * LANES), BF16), pltpu.VMEM((nk_g, 2 * W_A), BF16),
               pltpu.VMEM((rows_g + 2 * CONV_PAD, W_B), F32),
               pltpu.VMEM((2, min(rb, n, DENSE_ROWS), n_ctx + n), F32)]
    assert ctx is not None or n <= rb, "the context pass writes one cache block per sequence"
    body = functools.partial(_even_kernel, n=n, n_ctx=n_ctx, n_seq=n_seq, final=final_g is not None, fresh=i == 0)
    return _layer_call(body, x, n_seq, ops, caches, i, scratch, name)


def _odd_layer(x, mod_op, i, p, ctx, tab, caches, final_g, name):
    b, n, _ = x.shape
    n_ctx = 0 if ctx is None else ctx[0].shape[3]
    n_seq = _seqs_per_step(b, ctx)
    ops = [_per_step(x, n_seq), mod_op, _of_layer(p["norm_g"], 2 * i + 1),
           _of_layer(p["od_w_in"], i), _of_layer(p["od_w_out"], i),
           (p["sink"], pl.BlockSpec(memory_space=pltpu.SMEM)), _of_layer(p["gg"], i), _of_layer(p["ws"], i),
           _of_layer(p["bs"], i)]
    if ctx is not None:
        ops += [_seq_of_layer(ctx[0], i), _seq_of_layer(ctx[1], i), _full(tab)]
    if final_g is not None:
        ops.append(_full(final_g))
    pad = WINDOW if ctx is not None else 0
    _, rows_g, rb = _grouping(n, n_seq, ODD_MAX_ROWS)
    tq = min(rb, n, WINDOW_ROWS)
    n_keys = n_ctx + tq + 2 * pad if ctx is not None else n
    scratch = [pltpu.VMEM((rows_g, D_MODEL), BF16), pltpu.VMEM((rows_g + 2 * pad, KV_C * LANES), BF16),
               pltpu.VMEM((rows_g + 2 * pad, KV_C * 2 * LANES), BF16), pltpu.VMEM((2, tq, n_keys), F32)]
    if ctx is not None:
        scratch += [pltpu.VMEM((n_ctx, KV_C * LANES), BF16), pltpu.VMEM((n_ctx, KV_C * 2 * LANES), BF16)]
    assert ctx is not None or n <= rb, "the context pass writes one cache block per sequence"
    body = functools.partial(_odd_kernel, n=n, n_ctx=n_ctx, n_seq=n_seq, final=final_g is not None, fresh=i == 0,
                             layer=i)
    return _layer_call(body, x, n_seq, ops, caches, i, scratch, name)


def kernel(x_prompt, x_sample, cache_mla_ckv, cache_mla_krope, cache_swa_k, cache_swa_v, c, c_ctx, ada_w, ada_b, norm_g, ev_w_in, ev_w_out, mla_gq, mla_gkv, mla_w_uq, mla_w_uk, mla_w_uv, conv_w, conv_b, od_w_in, od_w_out, swa_sink, gmlp_g, gmlp_ws, gmlp_bs, final_g):
    bp, n_p, _ = x_prompt.shape
    bs_, n_s, _ = x_sample.shape
    n_past = cache_mla_ckv.shape[2]
    n_even, n_odd = ev_w_in.shape[0], od_w_in.shape[0]

    n_rows = -(-(bs_ + 1) // 8) * 8
    cvec = jnp.concatenate([c, c_ctx[None, :], jnp.zeros((n_rows - bs_ - 1, D_MODEL), F32)], axis=0)
    mod = _modulation(cvec, ada_w, ada_b).reshape(DEPTH, n_rows, 3, D_MODEL)

    cos_a, sin_a = _rope_cs(n_s, ROPE_A)
    cos_c, sin_c = _rope_cs(n_s, HD_C)
    zero = lambda w: np.zeros((n_s, w), np.float32)
    tab_a = jnp.asarray(np.concatenate([np.ones((n_s, NOPE_A), np.float32), cos_a, zero(LANES - QK_A),
                                        zero(NOPE_A), sin_a, zero(LANES - QK_A)], axis=1))
    tab_c = jnp.asarray(np.concatenate([cos_c, cos_c, sin_c, sin_c], axis=1))

    uq = mla_w_uq.reshape(n_even, Q_RANK, H_A, QK_A)
    uk = mla_w_uk.reshape(n_even, KV_RANK, H_A, NOPE_A)
    w_head, w_main = _split_even(jnp.swapaxes(ev_w_in, 1, 2))
    p = {
        "norm_g": norm_g.reshape(DEPTH, 1, D_MODEL),
        "w_head": w_head,
        "w_main": w_main,
        "w_uq": jnp.pad(uq, ((0, 0), (0, 0), (0, 0), (0, LANES - QK_A))).reshape(n_even, Q_RANK, H_A * LANES).astype(BF16),
        "w_uk": jnp.pad(uk, ((0, 0), (0, 0), (0, 0), (0, LANES - NOPE_A))).reshape(n_even, KV_RANK, H_A * LANES).astype(BF16),
        "w_uv": mla_w_uv.astype(BF16),
        "ev_w_out": ev_w_out.astype(BF16),
        "gq": mla_gq.reshape(n_even, 1, Q_RANK),
        "gkv": mla_gkv.reshape(n_even, 1, KV_RANK),
        "conv_w": conv_w,
        "conv_b": conv_b.reshape(n_even, 1, W_B),
        "od_w_in": od_w_in.astype(BF16),
        "od_w_out": od_w_out.astype(BF16),
        "sink": swa_sink,
        "gg": gmlp_g.reshape(n_odd, 1, W_D),
        "ws": jnp.swapaxes(gmlp_ws.reshape(n_odd, G_D // 2, 2, CHUNK, CHUNK), 2, 3)
                 .reshape(n_odd, G_D // 2, CHUNK, 2 * CHUNK).astype(BF16),
        "bs": jnp.repeat(jnp.swapaxes(gmlp_bs, 1, 2), DG_D, axis=2),
    }
    ckr = jnp.pad(cache_mla_krope, ((0, 0), (0, 0), (0, 0), (NOPE_A, LANES - QK_A)))
    ckr = [ckr[:, i] for i in range(n_even)]
    swa_k = jnp.transpose(cache_swa_k, (0, 1, 3, 4, 2)).reshape(bs_, n_odd, KV_C * HD_C, n_past)
    swa_v = jnp.transpose(cache_swa_v, (0, 1, 3, 4, 2)).reshape(bs_, n_odd, KV_C * HD_C, n_past)

    new_ckv = (bp, n_even, n_p, KV_RANK)
    new_kr = (bp, n_even, ROPE_A, n_p)
    new_k = (bp, n_odd, KV_C * HD_C, n_p)
    new_v = (bp, n_odd, KV_C * HD_C, n_p)

    fg = final_g.reshape(1, D_MODEL)
    xp, xs = x_prompt, x_sample
    for l in range(DEPTH):
        i = l // 2
        last = fg if l == DEPTH - 1 else None
        mod_p = _mod_spec(mod, l, bs_)
        mod_s = _mod_spec(mod, l, None)
        if l % 2 == 0:
            xp, new_ckv, new_kr = _even_layer(xp, mod_p, i, p, None, None, (new_ckv, new_kr), last, f"even{l}_ctx")
            (xs,) = _even_layer(xs, mod_s, i, p, (cache_mla_ckv, ckr), tab_a, (), last, f"even{l}_lat")
        else:
            xp, new_k, new_v = _odd_layer(xp, mod_p, i, p, None, None, (new_k, new_v), last, f"odd{l}_ctx")
            (xs,) = _odd_layer(xs, mod_s, i, p, (swa_k, swa_v), tab_c, (), last, f"odd{l}_lat")

    def heads_last(a):
        return jnp.transpose(a.reshape(bp, n_odd, KV_C, HD_C, n_p), (0, 1, 4, 2, 3))

    return (xp, xs, new_ckv, jnp.swapaxes(new_kr, 2, 3), heads_last(new_k), heads_last(new_v))
```

```python
import functools

import numpy as np
import jax
import jax.numpy as jnp
from jax import lax
from jax.experimental import pallas as pl
from jax.experimental.pallas import tpu as pltpu

D_MODEL = 1024
DEPTH = 4
GRID_W = 64
ROPE_BASE = 10000.0
EPS = 1e-6
NEG_INF = -1e30

H_A = 8
NOPE_A = 64
ROPE_A = 32
V_A = 64
Q_RANK = 256
KV_RANK = 128
QK_A = NOPE_A + ROPE_A
W_A = H_A * V_A
W_B = 512
CONV_W = 3
H_C = 8
KV_C = 2
HD_C = 64
WINDOW = 128
W_C = H_C * HD_C
G_D = 8
W_D = 512
CHUNK = 128
DG_D = W_D // G_D

EVEN_SPLIT = (Q_RANK, KV_RANK, ROPE_A, W_A, W_B, W_B, W_B, W_B)
ODD_SPLIT = (W_C, KV_C * HD_C, KV_C * HD_C, W_C, W_D, W_D, W_D)
EVEN_HEAD = Q_RANK + KV_RANK + ROPE_A

LANES = 128
EVEN_MAX_ROWS = 512
ODD_MAX_ROWS = 1024
DENSE_ROWS = 512
WINDOW_ROWS = 256
CONV_PAD = 8
CTX_SEQS_PER_STEP = 4
VMEM_LIMIT = 60 * 1024 * 1024

F32 = jnp.float32
BF16 = jnp.bfloat16
LOG2E = float(np.log2(np.e))


def _dot(a, b):
    return jnp.dot(a, b, preferred_element_type=F32)


def _dot_t(a, b):
    return lax.dot_general(a, b, (((1,), (1,)), ((), ())), preferred_element_type=F32)


def _rms(x, g):
    return x * lax.rsqrt(jnp.mean(x * x, axis=-1, keepdims=True) + EPS) * g


def _silu(x):
    return x * (1.0 / (1.0 + jnp.exp(-x)))


def _gelu(x):
    c = np.float32(np.sqrt(2.0 / np.pi))
    return 0.5 * x * (1.0 + jnp.tanh(c * (x + 0.044715 * (x * x * x))))


def _lane(shape):
    return lax.broadcasted_iota(jnp.int32, shape, 1)


def _pair_select(lo, hi):
    return jnp.where(_lane(lo.shape) < LANES // 2, lo, hi)


def _rope(x, tab, quarter):
    fwd = pltpu.roll(x, LANES - quarter, 1)
    bwd = pltpu.roll(x, quarter, 1)
    partner = jnp.where((_lane(x.shape) & quarter) == 0, fwd, bwd)
    return x * tab[:, 0:LANES] + partner * tab[:, LANES:2 * LANES]


def _dup_halves(x):
    sw = pltpu.roll(x, LANES // 2, 1)
    lo = _lane(x.shape) < LANES // 2
    return jnp.where(lo, x, sw), jnp.where(lo, sw, x)


def _modulated_norm(x, mod_ref, ng_ref):
    shift = mod_ref[0, 0, 0:1, :]
    scale = mod_ref[0, 0, 1:2, :]
    return _rms(x, ng_ref[0]) * (1.0 + scale) + shift


def _pipelined(n, produce, consume):
    produce(0)
    outs = []
    for i in range(n):
        if i + 1 < n:
            produce(i + 1)
        outs.append(consume(i))
    return outs


def _normalised(o):
    return o[:, 0:LANES] * (1.0 / o[:, LANES:2 * LANES])


def _loop(n, body):
    if n == 1:
        body(0)
    else:
        def step(i, carry):
            body(i)
            return carry
        lax.fori_loop(0, n, step, 0)


def _grouping(n, n_seq, max_rows):
    gs = max(1, min(n_seq, max_rows // n))
    rows_g = gs * n
    return gs, rows_g, min(rows_g, max_rows)


def _group_rows(ref, g, gs, r0, rb):
    if gs == 1:
        return ref[g, pl.ds(r0, rb), :]
    return ref[pl.ds(g * gs, gs), :, :].reshape(rb, ref.shape[-1])


def _set_group_rows(ref, g, gs, r0, rb, value):
    if gs == 1:
        ref[g, pl.ds(r0, rb), :] = value
    else:
        ref[pl.ds(g * gs, gs), :, :] = value.reshape(gs, rb // gs, ref.shape[-1])


def _row_loop(n_rows, rb, body):
    _loop(n_rows // rb, lambda i: body(i * rb if isinstance(i, int) else pl.multiple_of(i * rb, rb)))


def _even_kernel(*refs, n, n_ctx, n_seq, final, fresh):
    rope = n_ctx > 0
    it = iter(refs)
    x_ref, mod_ref, ng_ref = next(it), next(it), next(it)
    wh_ref, wm_ref, wuq_ref, wuk_ref, wuv_ref, wout_ref = (next(it) for _ in range(6))
    gq_ref, gkv_ref, cw_ref, cb_ref = next(it), next(it), next(it), next(it)
    if rope:
        cckv_ref, ckr_ref, tab_ref = next(it), next(it), next(it)
    if final:
        fg_ref = next(it)
    if not rope and not fresh:
        next(it), next(it)
    o_ref = next(it)
    if not rope:
        ckv_out, kr_out = next(it), next(it)
    h_s, q_s, ckv_s, kr_s, k_s, v_s, t_s, s_s = (next(it) for _ in range(8))

    gs, rows_g, rb = _grouping(n, n_seq, EVEN_MAX_ROWS)
    nk = n_ctx + n
    tq = min(rb, n, DENSE_ROWS)
    q_scale = np.float32(QK_A ** -0.5 * LOG2E)

    zpad = jnp.zeros((CONV_PAD, W_B), F32)
    t_s[0:CONV_PAD, :] = zpad
    t_s[CONV_PAD + rows_g:2 * CONV_PAD + rows_g, :] = zpad

    def group(g):
        if rope:
            ckv_s[0:n_ctx, :] = cckv_ref[0, 0].astype(BF16)
            kr_s[0:n_ctx, :] = ckr_ref[0].astype(BF16)

        def phase1(r0):
            rows = pl.ds(r0, rb)
            hb = _modulated_norm(_group_rows(x_ref, g, gs, r0, rb), mod_ref, ng_ref).astype(BF16)
            h_s[rows, :] = hb
            ph = _dot(hb, wh_ref[0])
            qn = _rms(ph[:, 0:Q_RANK], gq_ref[0]).astype(BF16)
            qq = _dot(qn, wuq_ref[0])
            ckv = _rms(ph[:, Q_RANK:Q_RANK + KV_RANK], gkv_ref[0])
            kr = ph[:, Q_RANK + KV_RANK:Q_RANK + KV_RANK + LANES]
            if rope:
                tab = tab_ref[rows, :]
                kr = _rope(kr, tab, ROPE_A // 4)
                for h in range(H_A):
                    qh = _rope(qq[:, h * LANES:(h + 1) * LANES], tab, ROPE_A // 4)
                    q_s[rows, h * LANES:(h + 1) * LANES] = (qh * q_scale).astype(BF16)
            else:
                q_s[rows, :] = (qq * q_scale).astype(BF16)
                for s in range(gs):
                    _store_cache(ckv_out, g * gs + s, ckv[s * n:(s + 1) * n, :])
                    _store_cache(kr_out, g * gs + s, kr[s * n:(s + 1) * n, :].T[NOPE_A:QK_A, :])
            krows = pl.ds(n_ctx + r0, rb)
            ckv_s[krows, :] = ckv.astype(BF16)
            kr_s[krows, :] = kr.astype(BF16)
            pc = _dot(hb, wm_ref[0, :, 2 * W_B:4 * W_B])
            t_s[pl.ds(CONV_PAD + r0, rb), :] = pc[:, 0:W_B] * pc[:, W_B:2 * W_B]

        _row_loop(rows_g, rb, phase1)

        rb2 = int(np.gcd(rb, n_ctx + rows_g))

        def phase2(r0):
            rows = pl.ds(r0, rb2)
            c = ckv_s[rows, :]
            kn = _dot(c, wuk_ref[0])
            krf = kr_s[rows, :].astype(F32)
            for h in range(H_A):
                k_s[rows, h * LANES:(h + 1) * LANES] = (kn[:, h * LANES:(h + 1) * LANES] + krf).astype(BF16)
            v = _dot(c, wuv_ref[0]).astype(BF16)
            ones = jnp.ones((rb2, LANES), BF16)
            for j in range(H_A // 2):
                v_s[rows, 2 * j * LANES:(2 * j + 1) * LANES] = v[:, j * LANES:(j + 1) * LANES]
                v_s[rows, (2 * j + 1) * LANES:(2 * j + 2) * LANES] = ones

        _row_loop(n_ctx + rows_g, rb2, phase2)

        def phase3(r0):
            rows = pl.ds(r0, rb)
            hb = h_s[rows, :]

            def keys_of(t):
                return slice(None) if gs == 1 else slice(t * n, (t + 1) * n)

            def scores(i):
                t, h = divmod(i, H_A)
                s_s[i % 2] = _dot_t(q_s[pl.ds(r0 + t * tq, tq), h * LANES:(h + 1) * LANES],
                                    k_s[keys_of(t), h * LANES:(h + 1) * LANES])

            def attend(i):
                t, h = divmod(i, H_A)
                s = s_s[i % 2]
                e = jnp.exp2(s - jnp.max(s, axis=-1, keepdims=True)).astype(BF16)
                j = h // 2
                return _normalised(_dot(e, v_s[keys_of(t), 2 * j * LANES:(2 * j + 2) * LANES]))

            heads = _pipelined((rb // tq) * H_A, scores, attend)
            attn = jnp.concatenate(
                [jnp.concatenate([_pair_select(heads[t * H_A + 2 * j], heads[t * H_A + 2 * j + 1])
                                  for j in range(H_A // 2)], axis=-1) for t in range(rb // tq)], axis=0)
            pa = _dot(hb, wm_ref[0, :, 0:2 * W_B])
            a_out = attn * _silu(pa[:, 0:W_A])
            gate_b = pa[:, W_A:W_A + W_B]
            g_b = _dot(hb, wm_ref[0, :, 4 * W_B:5 * W_B])

            t_cur = t_s[pl.ds(CONV_PAD + r0, rb), :]
            rid = lax.broadcasted_iota(jnp.int32, (rb, W_B), 0)
            if n <= rb:
                t_prev = jnp.where(rid % n == 0, 0.0, pltpu.roll(t_cur, 1, 0))
                t_next = jnp.where(rid % n == n - 1, 0.0, pltpu.roll(t_cur, rb - 1, 0))
            else:
                t_up = t_s[pl.ds(r0, CONV_PAD), :]
                t_dn = t_s[pl.ds(CONV_PAD + r0 + rb, CONV_PAD), :]
                t_prev = jnp.where(rid == 0, t_up[CONV_PAD - 1:CONV_PAD, :], pltpu.roll(t_cur, 1, 0))
                t_next = jnp.where(rid == rb - 1, t_dn[0:1, :], pltpu.roll(t_cur, rb - 1, 0))
            conv = cb_ref[0] + t_prev * cw_ref[0, 0:1, :]
            conv = conv + t_cur * cw_ref[0, 1:2, :]
            conv = conv + t_next * cw_ref[0, 2:3, :]
            b_out = gate_b * conv * _silu(g_b)

            y = (_dot(a_out.astype(BF16), wout_ref[0, 0:W_A, :])
                 + _dot(b_out.astype(BF16), wout_ref[0, W_A:W_A + W_B, :]))
            res = _group_rows(x_ref, g, gs, r0, rb) + mod_ref[0, 0, 2:3, :] * y
            if final:
                res = _rms(res, fg_ref[...])
            _set_group_rows(o_ref, g, gs, r0, rb, res)

        _row_loop(rows_g, rb, phase3)

    _loop(n_seq // gs, group)


def _odd_kernel(*refs, n, n_ctx, n_seq, final, fresh, layer):
    rope = n_ctx > 0
    it = iter(refs)
    x_ref, mod_ref, ng_ref = next(it), next(it), next(it)
    win_ref, wout_ref = next(it), next(it)
    sink_ref, gg_ref, ws_ref, bs_ref = next(it), next(it), next(it), next(it)
    if rope:
        kc_ref, vc_ref, tab_ref = next(it), next(it), next(it)
    if final:
        fg_ref = next(it)
    if not rope and not fresh:
        next(it), next(it)
    o_ref = next(it)
    if not rope:
        k_out, v_out = next(it), next(it)
    h_s, kloc_s, vloc_s, s_s = next(it), next(it), next(it), next(it)
    if rope:
        kctx_s, vctx_s = next(it), next(it)

    pad = WINDOW if rope else 0
    gs, rows_g, rb = _grouping(n, n_seq, ODD_MAX_ROWS)
    tq = min(rb, n, WINDOW_ROWS)
    span = tq + 2 * pad
    kw, vw = LANES, 2 * LANES
    q0, k0, g0 = 0, W_C, W_C + 2 * KV_C * HD_C
    q_scale = np.float32(HD_C ** -0.5 * LOG2E)

    def store_keys(dst, rows, x):
        a, b = _dup_halves(x)
        dst[rows, 0:LANES] = a.astype(BF16)
        dst[rows, LANES:2 * LANES] = b.astype(BF16)

    def store_values(dst, rows, x):
        a, b = _dup_halves(x)
        ones = jnp.ones(x.shape, BF16)
        dst[rows, 0:LANES] = a.astype(BF16)
        dst[rows, LANES:2 * LANES] = ones
        dst[rows, 2 * LANES:3 * LANES] = b.astype(BF16)
        dst[rows, 3 * LANES:4 * LANES] = ones

    def group(g):
        if rope:
            kloc_s[0:pad, :] = jnp.zeros((pad, KV_C * kw), BF16)
            kloc_s[pad + n:2 * pad + n, :] = jnp.zeros((pad, KV_C * kw), BF16)
            vloc_s[0:pad, :] = jnp.zeros((pad, KV_C * vw), BF16)
            vloc_s[pad + n:2 * pad + n, :] = jnp.zeros((pad, KV_C * vw), BF16)
            store_keys(kctx_s, slice(None), kc_ref[0, 0].T)
            store_values(vctx_s, slice(None), vc_ref[0, 0].T)

        def phase1(r0):
            rows = pl.ds(r0, rb)
            hb = _modulated_norm(_group_rows(x_ref, g, gs, r0, rb), mod_ref, ng_ref).astype(BF16)
            h_s[rows, :] = hb
            p1 = _dot(hb, win_ref[0, :, k0:g0])
            k = p1[:, 0:LANES]
            v = p1[:, LANES:2 * LANES]
            if rope:
                k = _rope(k, tab_ref[rows, :], HD_C // 4)
            else:
                for s in range(gs):
                    _store_cache(k_out, g * gs + s, k[s * n:(s + 1) * n, :].T)
                    _store_cache(v_out, g * gs + s, v[s * n:(s + 1) * n, :].T)
            lrows = pl.ds(pad + r0, rb)
            store_keys(kloc_s, lrows, k)
            store_values(vloc_s, lrows, v)

        _row_loop(rows_g, rb, phase1)

        def phase3(r0):
            rows = pl.ds(r0, rb)
            hb = h_s[rows, :]
            pq = _dot(hb, win_ref[0, :, q0:k0])

            tiles = []
            for t in range(rb // tq):
                t0 = r0 + t * tq
                pqt = pq[t * tq:(t + 1) * tq, :]
                if rope:
                    tab = tab_ref[pl.ds(t0, tq), :]
                    kl = kloc_s[pl.ds(t0, span), :]
                    vl = vloc_s[pl.ds(t0, span), :]
                    ii = lax.broadcasted_iota(jnp.int32, (tq, span), 0)
                    jj = lax.broadcasted_iota(jnp.int32, (tq, span), 1)
                    kpos = jj + (t0 - WINDOW)
                    dlt = jj - ii
                    mask = (dlt >= 0) & (dlt <= 2 * WINDOW) & (kpos >= 0) & (kpos < n)
                else:
                    kl, vl, mask = kloc_s[t * n:(t + 1) * n, :], vloc_s[t * n:(t + 1) * n, :], None
                qhs = []
                for m in range(H_C // 2):
                    qb = pqt[:, m * LANES:(m + 1) * LANES]
                    if rope:
                        qb = _rope(qb, tab, HD_C // 4)
                    qb = qb * q_scale
                    lo = _lane(qb.shape) < LANES // 2
                    qhs.append(jnp.where(lo, qb, 0.0).astype(BF16))
                    qhs.append(jnp.where(lo, 0.0, qb).astype(BF16))
                tiles.append((kl, vl, mask, qhs))

            def kv_head(h):
                return h // (H_C // KV_C)

            def scores(i):
                (kl, _, _, qhs), h = tiles[i // H_C], i % H_C
                j = kv_head(h)
                if rope:
                    s_s[i % 2, :, 0:n_ctx] = _dot_t(qhs[h], kctx_s[:, j * kw:(j + 1) * kw])
                    s_s[i % 2, :, n_ctx:] = _dot_t(qhs[h], kl[:, j * kw:(j + 1) * kw])
                else:
                    s_s[i % 2] = _dot_t(qhs[h], kl[:, j * kw:(j + 1) * kw])

            def attend(i):
                (_, vl, mask, _), h = tiles[i // H_C], i % H_C
                j = kv_head(h)
                sk = sink_ref[layer, h] * np.float32(LOG2E)
                if rope:
                    sc = s_s[i % 2, :, 0:n_ctx]
                    sl = jnp.where(mask, s_s[i % 2, :, n_ctx:], NEG_INF)
                    mx = jnp.maximum(jnp.max(sl, axis=-1, keepdims=True), jnp.max(sc, axis=-1, keepdims=True))
                else:
                    sl = s_s[i % 2]
                    mx = jnp.max(sl, axis=-1, keepdims=True)
                mx = jnp.maximum(mx, sk)
                o = _dot(jnp.exp2(sl - mx).astype(BF16), vl[:, j * vw:(j + 1) * vw])
                if rope:
                    o = o + _dot(jnp.exp2(sc - mx).astype(BF16), vctx_s[:, j * vw:(j + 1) * vw])
                return o[:, 0:LANES] * (1.0 / (o[:, LANES:2 * LANES] + jnp.exp2(sk - mx)))

            heads = _pipelined(len(tiles) * H_C, scores, attend)
            attn = jnp.concatenate(
                [jnp.concatenate([_pair_select(heads[t * H_C + 2 * m], heads[t * H_C + 2 * m + 1])
                                  for m in range(H_C // 2)], axis=-1) for t in range(len(tiles))], axis=0)
            pg = _dot(hb, win_ref[0, :, g0:g0 + W_C + 3 * W_D])
            c_out = attn * _silu(pg[:, 0:W_C])
            u = pg[:, W_C:W_C + W_D]
            vv = pg[:, W_C + W_D:W_C + 2 * W_D]
            g_d = pg[:, W_C + 2 * W_D:W_C + 3 * W_D]

            vvn = _rms(_gelu(vv), gg_ref[0]).astype(BF16)
            chunks = []
            for c in range(rb // CHUNK):
                cols = []
                for jp in range(G_D // 2):
                    vp = vvn[c * CHUNK:(c + 1) * CHUNK, jp * LANES:(jp + 1) * LANES]
                    lo = _lane(vp.shape) < LANES // 2
                    zero = jnp.zeros_like(vp)
                    rhs = jnp.concatenate([jnp.where(lo, vp, zero), jnp.where(lo, zero, vp)], axis=0)
                    cols.append(_dot(ws_ref[0, jp], rhs))
                chunks.append(jnp.concatenate(cols, axis=-1) + bs_ref[0])
            mixed = jnp.concatenate(chunks, axis=0)
            d_out = _gelu(u) * mixed * _silu(g_d)

            y = (_dot(c_out.astype(BF16), wout_ref[0, 0:W_C, :])
                 + _dot(d_out.astype(BF16), wout_ref[0, W_C:W_C + W_D, :]))
            res = _group_rows(x_ref, g, gs, r0, rb) + mod_ref[0, 0, 2:3, :] * y
            if final:
                res = _rms(res, fg_ref[...])
            _set_group_rows(o_ref, g, gs, r0, rb, res)

        _row_loop(rows_g, rb, phase3)

    _loop(n_seq // gs, group)


def _mod_kernel(c_ref, w0_ref, w1_ref, b_ref, o_ref):
    a = _silu(c_ref[...]).astype(BF16)
    half = w0_ref.shape[1]
    o_ref[0] = (_dot(a[:, 0:half], w0_ref[0].astype(BF16)) + _dot(a[:, half:], w1_ref[0].astype(BF16))) + b_ref[0]


def _modulation(cvec, ada_w, ada_b):
    rows = cvec.shape[0]
    tn = D_MODEL
    return pl.pallas_call(
        _mod_kernel,
        grid=(DEPTH, 3 * D_MODEL // tn),
        in_specs=[
            pl.BlockSpec((rows, D_MODEL), lambda l, j: (0, 0)),
            pl.BlockSpec((1, D_MODEL // 2, tn), lambda l, j: (l, 0, j)),
            pl.BlockSpec((1, D_MODEL // 2, tn), lambda l, j: (l, 1, j)),
            pl.BlockSpec((1, 1, tn), lambda l, j: (l, 0, j)),
        ],
        out_specs=pl.BlockSpec((1, rows, tn), lambda l, j: (l, 0, j)),
        out_shape=jax.ShapeDtypeStruct((DEPTH, rows, 3 * D_MODEL), F32),
        compiler_params=pltpu.CompilerParams(dimension_semantics=("arbitrary", "arbitrary")),
        name="modulation",
    )(cvec, ada_w, ada_w, ada_b.reshape(DEPTH, 1, 3 * D_MODEL))


def _split_even_kernel(wt_ref, head_ref, main_ref):
    kv_end = Q_RANK + KV_RANK

    def put(dst, col, rows_f32):
        dst[0, :, col:col + LANES] = rows_f32.T.astype(BF16)

    for j in range(kv_end // LANES):
        put(head_ref, j * LANES, wt_ref[0, j * LANES:(j + 1) * LANES, :])
    k_r = wt_ref[0, kv_end:EVEN_HEAD, :]
    put(head_ref, kv_end, jnp.concatenate(
        [jnp.zeros((NOPE_A, D_MODEL), F32), k_r, jnp.zeros((LANES - QK_A, D_MODEL), F32)], axis=0))
    for j in range(main_ref.shape[2] // LANES):
        put(main_ref, j * LANES, wt_ref[0, EVEN_HEAD + j * LANES:EVEN_HEAD + (j + 1) * LANES, :])


def _split_even(w_in_t):
    n_layers, cols, d = w_in_t.shape
    n_head, n_main = Q_RANK + KV_RANK + LANES, cols - EVEN_HEAD
    return pl.pallas_call(
        _split_even_kernel,
        grid=(n_layers,),
        in_specs=[pl.BlockSpec((1, cols, d), lambda l: (l, 0, 0))],
        out_specs=[pl.BlockSpec((1, d, n_head), lambda l: (l, 0, 0)),
                   pl.BlockSpec((1, d, n_main), lambda l: (l, 0, 0))],
        out_shape=[jax.ShapeDtypeStruct((n_layers, d, n_head), BF16),
                   jax.ShapeDtypeStruct((n_layers, d, n_main), BF16)],
        compiler_params=pltpu.CompilerParams(dimension_semantics=("arbitrary",), vmem_limit_bytes=VMEM_LIMIT),
        name="split_even_w_in",
    )(w_in_t)


def _full(a):
    nd = a.ndim
    return a, pl.BlockSpec(a.shape, lambda b, _nd=nd: (0,) * _nd)


def _per_step(a, n_seq):
    nd = a.ndim
    return a, pl.BlockSpec((n_seq,) + a.shape[1:], lambda b, _nd=nd: (b,) + (0,) * (_nd - 1))


def _of_layer(a, i):
    nd = a.ndim
    return a, pl.BlockSpec((1,) + a.shape[1:], lambda b, _nd=nd, _i=i: (_i,) + (0,) * (_nd - 1))


def _seq_of_layer(a, i):
    nd = a.ndim
    return a, pl.BlockSpec((1, 1) + a.shape[2:], lambda b, _nd=nd, _i=i: (b, _i) + (0,) * (_nd - 2))


def _mod_spec(mod, l, row):
    if row is None:
        return mod, pl.BlockSpec((1, 1, 3, D_MODEL), lambda b, _l=l: (_l, b, 0, 0))
    return mod, pl.BlockSpec((1, 1, 3, D_MODEL), lambda b, _l=l, _r=row: (_l, _r, 0, 0))


def _rope_cs(n, rot_dim):
    rows = n // GRID_W
    r = np.repeat(np.arange(rows, dtype=np.float32), GRID_W)
    col = np.tile(np.arange(GRID_W, dtype=np.float32), rows)
    nf = rot_dim // 4
    inv = np.float32(ROPE_BASE) ** (-np.arange(nf, dtype=np.float32) / np.float32(nf))
    ar = r[:, None] * inv
    ac = col[:, None] * inv
    cos = np.concatenate([np.cos(ar), np.cos(ar), np.cos(ac), np.cos(ac)], axis=-1)
    sin = np.concatenate([-np.sin(ar), np.sin(ar), -np.sin(ac), np.sin(ac)], axis=-1)
    return cos.astype(np.float32), sin.astype(np.float32)


def _layer_call(body, x, n_seq, operands, caches, slot, scratch, name):
    b, n, _ = x.shape
    fresh = slot == 0
    shapes = [c if fresh else c.shape for c in caches]
    in_specs = [spec for _, spec in operands]
    args = [a for a, _ in operands]
    aliases = {}
    if not fresh:
        in_specs += [pl.BlockSpec(memory_space=pl.ANY) for _ in caches]
        args += list(caches)
        aliases = {len(operands) + k: 1 + k for k in range(len(caches))}
    out_shape = [jax.ShapeDtypeStruct(x.shape, F32)] + [jax.ShapeDtypeStruct(s, F32) for s in shapes]
    out_specs = [pl.BlockSpec((n_seq, n, D_MODEL), lambda i: (i, 0, 0))]
    for s in shapes:
        if fresh:
            out_specs.append(pl.BlockSpec((n_seq,) + tuple(s[1:]), lambda i: (i, 0, 0, 0)))
        else:
            out_specs.append(pl.BlockSpec((n_seq, 1) + tuple(s[2:]), lambda i, _s=slot: (i, _s, 0, 0)))
    return pl.pallas_call(
        body,
        grid=(b // n_seq,),
        in_specs=in_specs,
        out_specs=out_specs,
        out_shape=out_shape,
        input_output_aliases=aliases,
        scratch_shapes=scratch,
        compiler_params=pltpu.CompilerParams(dimension_semantics=("arbitrary",), vmem_limit_bytes=VMEM_LIMIT),
        name=name,
    )(*args)


def _store_cache(ref, sq, value):
    ref[sq, 0] = value
    for l in range(1, ref.shape[1]):
        ref[sq, l] = jnp.zeros(value.shape, value.dtype)


def _seqs_per_step(b, ctx):
    return 1 if ctx is not None else int(np.gcd(b, CTX_SEQS_PER_STEP))


def _even_layer(x, mod_op, i, p, ctx, tab, caches, final_g, name):
    b, n, _ = x.shape
    n_ctx = 0 if ctx is None else ctx[0].shape[2]
    n_seq = _seqs_per_step(b, ctx)
    ops = [_per_step(x, n_seq), mod_op, _of_layer(p["norm_g"], 2 * i),
           _of_layer(p["w_head"], i), _of_layer(p["w_main"], i), _of_layer(p["w_uq"], i), _of_layer(p["w_uk"], i),
           _of_layer(p["w_uv"], i), _of_layer(p["ev_w_out"], i),
           _of_layer(p["gq"], i), _of_layer(p["gkv"], i), _of_layer(p["conv_w"], i), _of_layer(p["conv_b"], i)]
    if ctx is not None:
        ops += [_seq_of_layer(ctx[0], i), _per_step(ctx[1][i], 1), _full(tab)]
    if final_g is not None:
        ops.append(_full(final_g))
    _, rows_g, rb = _grouping(n, n_seq, EVEN_MAX_ROWS)
    nk_g = n_ctx + rows_g
    scratch = [pltpu.VMEM((rows_g, D_MODEL), BF16), pltpu.VMEM((rows_g, H_A * LANES), BF16),
               pltpu.VMEM((nk_g, LANES), BF16), pltpu.VMEM((nk_g, LANES), BF16),
               pltpu.VMEM((nk_g, H_A * LANES), BF16), pltpu.VMEM((nk_g, 2 * W_A), BF16),
               pltpu.VMEM((rows_g + 2 * CONV_PAD, W_B), F32),
               pltpu.VMEM((2, min(rb, n, DENSE_ROWS), n_ctx + n), F32)]
    assert ctx is not None or n <= rb, "the context pass writes one cache block per sequence"
    body = functools.partial(_even_kernel, n=n, n_ctx=n_ctx, n_seq=n_seq, final=final_g is not None, fresh=i == 0)
    return _layer_call(body, x, n_seq, ops, caches, i, scratch, name)


def _odd_layer(x, mod_op, i, p, ctx, tab, caches, final_g, name):
    b, n, _ = x.shape
    n_ctx = 0 if ctx is None else ctx[0].shape[3]
    n_seq = _seqs_per_step(b, ctx)
    ops = [_per_step(x, n_seq), mod_op, _of_layer(p["norm_g"], 2 * i + 1),
           _of_layer(p["od_w_in"], i), _of_layer(p["od_w_out"], i),
           (p["sink"], pl.BlockSpec(memory_space=pltpu.SMEM)), _of_layer(p["gg"], i), _of_layer(p["ws"], i),
           _of_layer(p["bs"], i)]
    if ctx is not None:
        ops += [_seq_of_layer(ctx[0], i), _seq_of_layer(ctx[1], i), _full(tab)]
    if final_g is not None:
        ops.append(_full(final_g))
    pad = WINDOW if ctx is not None else 0
    _, rows_g, rb = _grouping(n, n_seq, ODD_MAX_ROWS)
    tq = min(rb, n, WINDOW_ROWS)
    n_keys = n_ctx + tq + 2 * pad if ctx is not None else n
    scratch = [pltpu.VMEM((rows_g, D_MODEL), BF16), pltpu.VMEM((rows_g + 2 * pad, KV_C * LANES), BF16),
               pltpu.VMEM((rows_g + 2 * pad, KV_C * 2 * LANES), BF16), pltpu.VMEM((2, tq, n_keys), F32)]
    if ctx is not None:
        scratch += [pltpu.VMEM((n_ctx, KV_C * LANES), BF16), pltpu.VMEM((n_ctx, KV_C * 2 * LANES), BF16)]
    assert ctx is not None or n <= rb, "the context pass writes one cache block per sequence"
    body = functools.partial(_odd_kernel, n=n, n_ctx=n_ctx, n_seq=n_seq, final=final_g is not None, fresh=i == 0,
                             layer=i)
    return _layer_call(body, x, n_seq, ops, caches, i, scratch, name)


def kernel(x_prompt, x_sample, cache_mla_ckv, cache_mla_krope, cache_swa_k, cache_swa_v, c, c_ctx, ada_w, ada_b, norm_g, ev_w_in, ev_w_out, mla_gq, mla_gkv, mla_w_uq, mla_w_uk, mla_w_uv, conv_w, conv_b, od_w_in, od_w_out, swa_sink, gmlp_g, gmlp_ws, gmlp_bs, final_g):
    bp, n_p, _ = x_prompt.shape
    bs_, n_s, _ = x_sample.shape
    n_past = cache_mla_ckv.shape[2]
    n_even, n_odd = ev_w_in.shape[0], od_w_in.shape[0]

    n_rows = -(-(bs_ + 1) // 8) * 8
    cvec = jnp.concatenate([c, c_ctx[None, :], jnp.zeros((n_rows - bs_ - 1, D_MODEL), F32)], axis=0)
    mod = _modulation(cvec, ada_w, ada_b).reshape(DEPTH, n_rows, 3, D_MODEL)

    cos_a, sin_a = _rope_cs(n_s, ROPE_A)
    cos_c, sin_c = _rope_cs(n_s, HD_C)
    zero = lambda w: np.zeros((n_s, w), np.float32)
    tab_a = jnp.asarray(np.concatenate([np.ones((n_s, NOPE_A), np.float32), cos_a, zero(LANES - QK_A),
                                        zero(NOPE_A), sin_a, zero(LANES - QK_A)], axis=1))
    tab_c = jnp.asarray(np.concatenate([cos_c, cos_c, sin_c, sin_c], axis=1))

    uq = mla_w_uq.reshape(n_even, Q_RANK, H_A, QK_A)
    uk = mla_w_uk.reshape(n_even, KV_RANK, H_A, NOPE_A)
    w_head, w_main = _split_even(jnp.swapaxes(ev_w_in, 1, 2))
    p = {
        "norm_g": norm_g.reshape(DEPTH, 1, D_MODEL),
        "w_head": w_head,
        "w_main": w_main,
        "w_uq": jnp.pad(uq, ((0, 0), (0, 0), (0, 0), (0, LANES - QK_A))).reshape(n_even, Q_RANK, H_A * LANES).astype(BF16),
        "w_uk": jnp.pad(uk, ((0, 0), (0, 0), (0, 0), (0, LANES - NOPE_A))).reshape(n_even, KV_RANK, H_A * LANES).astype(BF16),
        "w_uv": mla_w_uv.astype(BF16),
        "ev_w_out": ev_w_out.astype(BF16),
        "gq": mla_gq.reshape(n_even, 1, Q_RANK),
        "gkv": mla_gkv.reshape(n_even, 1, KV_RANK),
        "conv_w": conv_w,
        "conv_b": conv_b.reshape(n_even, 1, W_B),
        "od_w_in": od_w_in.astype(BF16),
        "od_w_out": od_w_out.astype(BF16),
        "sink": swa_sink,
        "gg": gmlp_g.reshape(n_odd, 1, W_D),
        "ws": jnp.swapaxes(gmlp_ws.reshape(n_odd, G_D // 2, 2, CHUNK, CHUNK), 2, 3)
                 .reshape(n_odd, G_D // 2, CHUNK, 2 * CHUNK).astype(BF16),
        "bs": jnp.repeat(jnp.swapaxes(gmlp_bs, 1, 2), DG_D, axis=2),
    }
    ckr = jnp.pad(cache_mla_krope, ((0, 0), (0, 0), (0, 0), (NOPE_A, LANES - QK_A)))
    ckr = [ckr[:, i] for i in range(n_even)]
    swa_k = jnp.transpose(cache_swa_k, (0, 1, 3, 4, 2)).reshape(bs_, n_odd, KV_C * HD_C, n_past)
    swa_v = jnp.transpose(cache_swa_v, (0, 1, 3, 4, 2)).reshape(bs_, n_odd, KV_C * HD_C, n_past)

    new_ckv = (bp, n_even, n_p, KV_RANK)
    new_kr = (bp, n_even, ROPE_A, n_p)
    new_k = (bp, n_odd, KV_C * HD_C, n_p)
    new_v = (bp, n_odd, KV_C * HD_C, n_p)

    fg = final_g.reshape(1, D_MODEL)
    xp, xs = x_prompt, x_sample
    for l in range(DEPTH):
        i = l // 2
        last = fg if l == DEPTH - 1 else None
        mod_p = _mod_spec(mod, l, bs_)
        mod_s = _mod_spec(mod, l, None)
        if l % 2 == 0:
            xp, new_ckv, new_kr = _even_layer(xp, mod_p, i, p, None, None, (new_ckv, new_kr), last, f"even{l}_ctx")
            (xs,) = _even_layer(xs, mod_s, i, p, (cache_mla_ckv, ckr), tab_a, (), last, f"even{l}_lat")
        else:
            xp, new_k, new_v = _odd_layer(xp, mod_p, i, p, None, None, (new_k, new_v), last, f"odd{l}_ctx")
            (xs,) = _odd_layer(xs, mod_s, i, p, (swa_k, swa_v), tab_c, (), last, f"odd{l}_lat")

    def heads_last(a):
        return jnp.transpose(a.reshape(bp, n_odd, KV_C, HD_C, n_p), (0, 1, 4, 2, 3))

    return (xp, xs, new_ckv, jnp.swapaxes(new_kr, 2, 3), heads_last(new_k), heads_last(new_v))
```

```python
import functools

import numpy as np
import jax
import jax.numpy as jnp
from jax import lax
from jax.experimental import pallas as pl
from jax.experimental.pallas import tpu as pltpu

D_MODEL = 1024
DEPTH = 4
GRID_W = 64
ROPE_BASE = 10000.0
EPS = 1e-6
NEG_INF = -1e30

H_A = 8
NOPE_A = 64
ROPE_A = 32
V_A = 64
Q_RANK = 256
KV_RANK = 128
QK_A = NOPE_A + ROPE_A
W_A = H_A * V_A
W_B = 512
CONV_W = 3
H_C = 8
KV_C = 2
HD_C = 64
WINDOW = 128
W_C = H_C * HD_C
G_D = 8
W_D = 512
CHUNK = 128
DG_D = W_D // G_D

EVEN_SPLIT = (Q_RANK, KV_RANK, ROPE_A, W_A, W_B, W_B, W_B, W_B)
ODD_SPLIT = (W_C, KV_C * HD_C, KV_C * HD_C, W_C, W_D, W_D, W_D)
EVEN_HEAD = Q_RANK + KV_RANK + ROPE_A

LANES = 128
EVEN_MAX_ROWS = 512
ODD_MAX_ROWS = 1024
DENSE_ROWS = 512
WINDOW_ROWS = 256
CONV_PAD = 8
CTX_SEQS_PER_STEP = 4
VMEM_LIMIT = 60 * 1024 * 1024

F32 = jnp.float32
BF16 = jnp.bfloat16
LOG2E = float(np.log2(np.e))


def _dot(a, b):
    return jnp.dot(a, b, preferred_element_type=F32)


def _dot_t(a, b):
    return lax.dot_general(a, b, (((1,), (1,)), ((), ())), preferred_element_type=F32)


def _rms(x, g):
    return x * lax.rsqrt(jnp.mean(x * x, axis=-1, keepdims=True) + EPS) * g


def _silu(x):
    return x * (1.0 / (1.0 + jnp.exp(-x)))


def _gelu(x):
    c = np.float32(np.sqrt(2.0 / np.pi))
    return 0.5 * x * (1.0 + jnp.tanh(c * (x + 0.044715 * (x * x * x))))


def _lane(shape):
    return lax.broadcasted_iota(jnp.int32, shape, 1)


def _pair_select(lo, hi):
    return jnp.where(_lane(lo.shape) < LANES // 2, lo, hi)


def _rope(x, tab, quarter):
    fwd = pltpu.roll(x, LANES - quarter, 1)
    bwd = pltpu.roll(x, quarter, 1)
    partner = jnp.where((_lane(x.shape) & quarter) == 0, fwd, bwd)
    return x * tab[:, 0:LANES] + partner * tab[:, LANES:2 * LANES]


def _dup_halves(x):
    sw = pltpu.roll(x, LANES // 2, 1)
    lo = _lane(x.shape) < LANES // 2
    return jnp.where(lo, x, sw), jnp.where(lo, sw, x)


def _modulated_norm(x, mod_ref, ng_ref):
    shift = mod_ref[0, 0, 0:1, :]
    scale = mod_ref[0, 0, 1:2, :]
    return _rms(x, ng_ref[0]) * (1.0 + scale) + shift


def _pipelined(n, produce, consume):
    produce(0)
    outs = []
    for i in range(n):
        if i + 1 < n:
            produce(i + 1)
        outs.append(consume(i))
    return outs


def _normalised(o):
    return o[:, 0:LANES] * (1.0 / o[:, LANES:2 * LANES])


def _loop(n, body):
    if n == 1:
        body(0)
    else:
        def step(i, carry):
            body(i)
            return carry
        lax.fori_loop(0, n, step, 0)


def _grouping(n, n_seq, max_rows):
    gs = max(1, min(n_seq, max_rows // n))
    rows_g = gs * n
    return gs, rows_g, min(rows_g, max_rows)


def _group_rows(ref, g, gs, r0, rb):
    if gs == 1:
        return ref[g, pl.ds(r0, rb), :]
    return ref[pl.ds(g * gs, gs), :, :].reshape(rb, ref.shape[-1])


def _set_group_rows(ref, g, gs, r0, rb, value):
    if gs == 1:
        ref[g, pl.ds(r0, rb), :] = value
    else:
        ref[pl.ds(g * gs, gs), :, :] = value.reshape(gs, rb // gs, ref.shape[-1])


def _row_loop(n_rows, rb, body):
    _loop(n_rows // rb, lambda i: body(i * rb if isinstance(i, int) else pl.multiple_of(i * rb, rb)))


def _even_kernel(*refs, n, n_ctx, n_seq, final, fresh):
    rope = n_ctx > 0
    it = iter(refs)
    x_ref, mod_ref, ng_ref = next(it), next(it), next(it)
    wh_ref, wm_ref, wuq_ref, wuk_ref, wuv_ref, wout_ref = (next(it) for _ in range(6))
    gq_ref, gkv_ref, cw_ref, cb_ref = next(it), next(it), next(it), next(it)
    if rope:
        cckv_ref, ckr_ref, tab_ref = next(it), next(it), next(it)
    if final:
        fg_ref = next(it)
    if not rope and not fresh:
        next(it), next(it)
    o_ref = next(it)
    if not rope:
        ckv_out, kr_out = next(it), next(it)
    h_s, q_s, ckv_s, kr_s, k_s, v_s, t_s, s_s = (next(it) for _ in range(8))

    gs, rows_g, rb = _grouping(n, n_seq, EVEN_MAX_ROWS)
    nk = n_ctx + n
    tq = min(rb, n, DENSE_ROWS)
    q_scale = np.float32(QK_A ** -0.5 * LOG2E)

    zpad = jnp.zeros((CONV_PAD, W_B), F32)
    t_s[0:CONV_PAD, :] = zpad
    t_s[CONV_PAD + rows_g:2 * CONV_PAD + rows_g, :] = zpad

    def group(g):
        if rope:
            ckv_s[0:n_ctx, :] = cckv_ref[0, 0].astype(BF16)
            kr_s[0:n_ctx, :] = ckr_ref[0].astype(BF16)

        def phase1(r0):
            rows = pl.ds(r0, rb)
            hb = _modulated_norm(_group_rows(x_ref, g, gs, r0, rb), mod_ref, ng_ref).astype(BF16)
            h_s[rows, :] = hb
            ph = _dot(hb, wh_ref[0])
            qn = _rms(ph[:, 0:Q_RANK], gq_ref[0]).astype(BF16)
            qq = _dot(qn, wuq_ref[0])
            ckv = _rms(ph[:, Q_RANK:Q_RANK + KV_RANK], gkv_ref[0])
            kr = ph[:, Q_RANK + KV_RANK:Q_RANK + KV_RANK + LANES]
            if rope:
                tab = tab_ref[rows, :]
                kr = _rope(kr, tab, ROPE_A // 4)
                for h in range(H_A):
                    qh = _rope(qq[:, h * LANES:(h + 1) * LANES], tab, ROPE_A // 4)
                    q_s[rows, h * LANES:(h + 1) * LANES] = (qh * q_scale).astype(BF16)
            else:
                q_s[rows, :] = (qq * q_scale).astype(BF16)
                for s in range(gs):
                    _store_cache(ckv_out, g * gs + s, ckv[s * n:(s + 1) * n, :])
                    _store_cache(kr_out, g * gs + s, kr[s * n:(s + 1) * n, :].T[NOPE_A:QK_A, :])
            krows = pl.ds(n_ctx + r0, rb)
            ckv_s[krows, :] = ckv.astype(BF16)
            kr_s[krows, :] = kr.astype(BF16)
            pc = _dot(hb, wm_ref[0, :, 2 * W_B:4 * W_B])
            t_s[pl.ds(CONV_PAD + r0, rb), :] = pc[:, 0:W_B] * pc[:, W_B:2 * W_B]

        _row_loop(rows_g, rb, phase1)

        rb2 = int(np.gcd(rb, n_ctx + rows_g))

        def phase2(r0):
            rows = pl.ds(r0, rb2)
            c = ckv_s[rows, :]
            kn = _dot(c, wuk_ref[0])
            krf = kr_s[rows, :].astype(F32)
            for h in range(H_A):
                k_s[rows, h * LANES:(h + 1) * LANES] = (kn[:, h * LANES:(h + 1) * LANES] + krf).astype(BF16)
            v = _dot(c, wuv_ref[0]).astype(BF16)
            ones = jnp.ones((rb2, LANES), BF16)
            for j in range(H_A // 2):
                v_s[rows, 2 * j * LANES:(2 * j + 1) * LANES] = v[:, j * LANES:(j + 1) * LANES]
                v_s[rows, (2 * j + 1) * LANES:(2 * j + 2) * LANES] = ones

        _row_loop(n_ctx + rows_g, rb2, phase2)

        def phase3(r0):
            rows = pl.ds(r0, rb)
            hb = h_s[rows, :]

            def keys_of(t):
                return slice(None) if gs == 1 else slice(t * n, (t + 1) * n)

            def scores(i):
                t, h = divmod(i, H_A)
                s_s[i % 2] = _dot_t(q_s[pl.ds(r0 + t * tq, tq), h * LANES:(h + 1) * LANES],
                                    k_s[keys_of(t), h * LANES:(h + 1) * LANES])

            def attend(i):
                t, h = divmod(i, H_A)
                s = s_s[i % 2]
                e = jnp.exp2(s - jnp.max(s, axis=-1, keepdims=True)).astype(BF16)
                j = h // 2
                return _normalised(_dot(e, v_s[keys_of(t), 2 * j * LANES:(2 * j + 2) * LANES]))

            heads = _pipelined((rb // tq) * H_A, scores, attend)

            def head_cols(j0, j1):
                return jnp.concatenate(
                    [jnp.concatenate([_pair_select(heads[t * H_A + 2 * j], heads[t * H_A + 2 * j + 1])
                                      for j in range(j0, j1)], axis=-1) for t in range(rb // tq)], axis=0)

            pa = _dot(hb, wm_ref[0, :, 0:2 * W_B])
            wa = W_A // 2
            a_lo = (head_cols(0, H_A // 4) * _silu(pa[:, 0:wa])).astype(BF16)
            a_hi = (head_cols(H_A // 4, H_A // 2) * _silu(pa[:, wa:W_A])).astype(BF16)
            gate_b = pa[:, W_A:W_A + W_B]
            g_b = _dot(hb, wm_ref[0, :, 4 * W_B:5 * W_B])

            t_cur = t_s[pl.ds(CONV_PAD + r0, rb), :]
            rid = lax.broadcasted_iota(jnp.int32, (rb, W_B), 0)
            if n <= rb:
                t_prev = jnp.where(rid % n == 0, 0.0, pltpu.roll(t_cur, 1, 0))
                t_next = jnp.where(rid % n == n - 1, 0.0, pltpu.roll(t_cur, rb - 1, 0))
            else:
                t_up = t_s[pl.ds(r0, CONV_PAD), :]
                t_dn = t_s[pl.ds(CONV_PAD + r0 + rb, CONV_PAD), :]
                t_prev = jnp.where(rid == 0, t_up[CONV_PAD - 1:CONV_PAD, :], pltpu.roll(t_cur, 1, 0))
                t_next = jnp.where(rid == rb - 1, t_dn[0:1, :], pltpu.roll(t_cur, rb - 1, 0))
            conv = cb_ref[0] + t_prev * cw_ref[0, 0:1, :]
            conv = conv + t_cur * cw_ref[0, 1:2, :]
            conv = conv + t_next * cw_ref[0, 2:3, :]
            b_out = gate_b * conv * _silu(g_b)

            y = (_dot(b_out.astype(BF16), wout_ref[0, W_A:W_A + W_B, :]) + _dot(a_lo, wout_ref[0, 0:wa, :])
                 + _dot(a_hi, wout_ref[0, wa:W_A, :]))
            res = _group_rows(x_ref, g, gs, r0, rb) + mod_ref[0, 0, 2:3, :] * y
            if final:
                res = _rms(res, fg_ref[...])
            _set_group_rows(o_ref, g, gs, r0, rb, res)

        _row_loop(rows_g, rb, phase3)

    _loop(n_seq // gs, group)


def _odd_kernel(*refs, n, n_ctx, n_seq, final, fresh, layer):
    rope = n_ctx > 0
    it = iter(refs)
    x_ref, mod_ref, ng_ref = next(it), next(it), next(it)
    win_ref, wout_ref = next(it), next(it)
    sink_ref, gg_ref, ws_ref, bs_ref = next(it), next(it), next(it), next(it)
    if rope:
        kc_ref, vc_ref, tab_ref = next(it), next(it), next(it)
    if final:
        fg_ref = next(it)
    if not rope and not fresh:
        next(it), next(it)
    o_ref = next(it)
    if not rope:
        k_out, v_out = next(it), next(it)
    h_s, kloc_s, vloc_s, s_s = next(it), next(it), next(it), next(it)
    if rope:
        kctx_s, vctx_s = next(it), next(it)

    pad = WINDOW if rope else 0
    gs, rows_g, rb = _grouping(n, n_seq, ODD_MAX_ROWS)
    tq = min(rb, n, WINDOW_ROWS)
    span = tq + 2 * pad
    kw, vw = LANES, 2 * LANES
    q0, k0, g0 = 0, W_C, W_C + 2 * KV_C * HD_C
    q_scale = np.float32(HD_C ** -0.5 * LOG2E)

    def store_keys(dst, rows, x):
        a, b = _dup_halves(x)
        dst[rows, 0:LANES] = a.astype(BF16)
        dst[rows, LANES:2 * LANES] = b.astype(BF16)

    def store_values(dst, rows, x):
        a, b = _dup_halves(x)
        ones = jnp.ones(x.shape, BF16)
        dst[rows, 0:LANES] = a.astype(BF16)
        dst[rows, LANES:2 * LANES] = ones
        dst[rows, 2 * LANES:3 * LANES] = b.astype(BF16)
        dst[rows, 3 * LANES:4 * LANES] = ones

    def group(g):
        if rope:
            kloc_s[0:pad, :] = jnp.zeros((pad, KV_C * kw), BF16)
            kloc_s[pad + n:2 * pad + n, :] = jnp.zeros((pad, KV_C * kw), BF16)
            vloc_s[0:pad, :] = jnp.zeros((pad, KV_C * vw), BF16)
            vloc_s[pad + n:2 * pad + n, :] = jnp.zeros((pad, KV_C * vw), BF16)
            store_keys(kctx_s, slice(None), kc_ref[0, 0].T)
            store_values(vctx_s, slice(None), vc_ref[0, 0].T)

        def phase1(r0):
            rows = pl.ds(r0, rb)
            hb = _modulated_norm(_group_rows(x_ref, g, gs, r0, rb), mod_ref, ng_ref).astype(BF16)
            h_s[rows, :] = hb
            p1 = _dot(hb, win_ref[0, :, k0:g0])
            k = p1[:, 0:LANES]
            v = p1[:, LANES:2 * LANES]
            if rope:
                k = _rope(k, tab_ref[rows, :], HD_C // 4)
            else:
                for s in range(gs):
                    _store_cache(k_out, g * gs + s, k[s * n:(s + 1) * n, :].T)
                    _store_cache(v_out, g * gs + s, v[s * n:(s + 1) * n, :].T)
            lrows = pl.ds(pad + r0, rb)
            store_keys(kloc_s, lrows, k)
            store_values(vloc_s, lrows, v)

        _row_loop(rows_g, rb, phase1)

        def phase3(r0):
            rows = pl.ds(r0, rb)
            hb = h_s[rows, :]
            pq = _dot(hb, win_ref[0, :, q0:k0])

            tiles = []
            for t in range(rb // tq):
                t0 = r0 + t * tq
                pqt = pq[t * tq:(t + 1) * tq, :]
                if rope:
                    tab = tab_ref[pl.ds(t0, tq), :]
                    kl = kloc_s[pl.ds(t0, span), :]
                    vl = vloc_s[pl.ds(t0, span), :]
                    ii = lax.broadcasted_iota(jnp.int32, (tq, span), 0)
                    jj = lax.broadcasted_iota(jnp.int32, (tq, span), 1)
                    kpos = jj + (t0 - WINDOW)
                    dlt = jj - ii
                    mask = (dlt >= 0) & (dlt <= 2 * WINDOW) & (kpos >= 0) & (kpos < n)
                else:
                    kl, vl, mask = kloc_s[t * n:(t + 1) * n, :], vloc_s[t * n:(t + 1) * n, :], None
                qhs = []
                for m in range(H_C // 2):
                    qb = pqt[:, m * LANES:(m + 1) * LANES]
                    if rope:
                        qb = _rope(qb, tab, HD_C // 4)
                    qb = qb * q_scale
                    lo = _lane(qb.shape) < LANES // 2
                    qhs.append(jnp.where(lo, qb, 0.0).astype(BF16))
                    qhs.append(jnp.where(lo, 0.0, qb).astype(BF16))
                tiles.append((kl, vl, mask, qhs))

            def kv_head(h):
                return h // (H_C // KV_C)

            def scores(i):
                (kl, _, _, qhs), h = tiles[i // H_C], i % H_C
                j = kv_head(h)
                if rope:
                    s_s[i % 2, :, 0:n_ctx] = _dot_t(qhs[h], kctx_s[:, j * kw:(j + 1) * kw])
                    s_s[i % 2, :, n_ctx:] = _dot_t(qhs[h], kl[:, j * kw:(j + 1) * kw])
                else:
                    s_s[i % 2] = _dot_t(qhs[h], kl[:, j * kw:(j + 1) * kw])

            def attend(i):
                (_, vl, mask, _), h = tiles[i // H_C], i % H_C
                j = kv_head(h)
                sk = sink_ref[layer, h] * np.float32(LOG2E)
                if rope:
                    sc = s_s[i % 2, :, 0:n_ctx]
                    sl = jnp.where(mask, s_s[i % 2, :, n_ctx:], NEG_INF)
                    mx = jnp.maximum(jnp.max(sl, axis=-1, keepdims=True), jnp.max(sc, axis=-1, keepdims=True))
                else:
                    sl = s_s[i % 2]
                    mx = jnp.max(sl, axis=-1, keepdims=True)
                mx = jnp.maximum(mx, sk)
                o = _dot(jnp.exp2(sl - mx).astype(BF16), vl[:, j * vw:(j + 1) * vw])
                if rope:
                    o = o + _dot(jnp.exp2(sc - mx).astype(BF16), vctx_s[:, j * vw:(j + 1) * vw])
                return o[:, 0:LANES] * (1.0 / (o[:, LANES:2 * LANES] + jnp.exp2(sk - mx)))

            heads = _pipelined(len(tiles) * H_C, scores, attend)
            pg = _dot(hb, win_ref[0, :, g0:g0 + W_C + 3 * W_D])
            yc = []
            for t in range(len(tiles)):
                attn_t = jnp.concatenate([_pair_select(heads[t * H_C + 2 * m], heads[t * H_C + 2 * m + 1])
                                          for m in range(H_C // 2)], axis=-1)
                c_out = attn_t * _silu(pg[t * tq:(t + 1) * tq, 0:W_C])
                yc.append(_dot(c_out.astype(BF16), wout_ref[0, 0:W_C, :]))
            u = pg[:, W_C:W_C + W_D]
            vv = pg[:, W_C + W_D:W_C + 2 * W_D]
            g_d = pg[:, W_C + 2 * W_D:W_C + 3 * W_D]

            vvn = _rms(_gelu(vv), gg_ref[0]).astype(BF16)
            chunks = []
            for c in range(rb // CHUNK):
                cols = []
                for jp in range(G_D // 2):
                    vp = vvn[c * CHUNK:(c + 1) * CHUNK, jp * LANES:(jp + 1) * LANES]
                    lo = _lane(vp.shape) < LANES // 2
                    zero = jnp.zeros_like(vp)
                    rhs = jnp.concatenate([jnp.where(lo, vp, zero), jnp.where(lo, zero, vp)], axis=0)
                    cols.append(_dot(ws_ref[0, jp], rhs))
                chunks.append(jnp.concatenate(cols, axis=-1) + bs_ref[0])
            mixed = jnp.concatenate(chunks, axis=0)
            d_out = _gelu(u) * mixed * _silu(g_d)

            y = jnp.concatenate(yc, axis=0) + _dot(d_out.astype(BF16), wout_ref[0, W_C:W_C + W_D, :])
            res = _group_rows(x_ref, g, gs, r0, rb) + mod_ref[0, 0, 2:3, :] * y
            if final:
                res = _rms(res, fg_ref[...])
            _set_group_rows(o_ref, g, gs, r0, rb, res)

        _row_loop(rows_g, rb, phase3)

    _loop(n_seq // gs, group)


def _mod_kernel(c_ref, w0_ref, w1_ref, b_ref, o_ref):
    a = _silu(c_ref[...]).astype(BF16)
    half = w0_ref.shape[1]
    o_ref[0] = (_dot(a[:, 0:half], w0_ref[0].astype(BF16)) + _dot(a[:, half:], w1_ref[0].astype(BF16))) + b_ref[0]


def _modulation(cvec, ada_w, ada_b):
    rows = cvec.shape[0]
    tn = D_MODEL
    return pl.pallas_call(
        _mod_kernel,
        grid=(DEPTH, 3 * D_MODEL // tn),
        in_specs=[
            pl.BlockSpec((rows, D_MODEL), lambda l, j: (0, 0)),
            pl.BlockSpec((1, D_MODEL // 2, tn), lambda l, j: (l, 0, j)),
            pl.BlockSpec((1, D_MODEL // 2, tn), lambda l, j: (l, 1, j)),
            pl.BlockSpec((1, 1, tn), lambda l, j: (l, 0, j)),
        ],
        out_specs=pl.BlockSpec((1, rows, tn), lambda l, j: (l, 0, j)),
        out_shape=jax.ShapeDtypeStruct((DEPTH, rows, 3 * D_MODEL), F32),
        compiler_params=pltpu.CompilerParams(dimension_semantics=("arbitrary", "arbitrary")),
        name="modulation",
    )(cvec, ada_w, ada_w, ada_b.reshape(DEPTH, 1, 3 * D_MODEL))


def _split_even_kernel(wt_ref, head_ref, main_ref):
    kv_end = Q_RANK + KV_RANK

    def put(dst, col, rows_f32):
        dst[0, :, col:col + LANES] = rows_f32.T.astype(BF16)

    for j in range(kv_end // LANES):
        put(head_ref, j * LANES, wt_ref[0, j * LANES:(j + 1) * LANES, :])
    k_r = wt_ref[0, kv_end:EVEN_HEAD, :]
    put(head_ref, kv_end, jnp.concatenate(
        [jnp.zeros((NOPE_A, D_MODEL), F32), k_r, jnp.zeros((LANES - QK_A, D_MODEL), F32)], axis=0))
    for j in range(main_ref.shape[2] // LANES):
        put(main_ref, j * LANES, wt_ref[0, EVEN_HEAD + j * LANES:EVEN_HEAD + (j + 1) * LANES, :])


def _split_even(w_in_t):
    n_layers, cols, d = w_in_t.shape
    n_head, n_main = Q_RANK + KV_RANK + LANES, cols - EVEN_HEAD
    return pl.pallas_call(
        _split_even_kernel,
        grid=(n_layers,),
        in_specs=[pl.BlockSpec((1, cols, d), lambda l: (l, 0, 0))],
        out_specs=[pl.BlockSpec((1, d, n_head), lambda l: (l, 0, 0)),
                   pl.BlockSpec((1, d, n_main), lambda l: (l, 0, 0))],
        out_shape=[jax.ShapeDtypeStruct((n_layers, d, n_head), BF16),
                   jax.ShapeDtypeStruct((n_layers, d, n_main), BF16)],
        compiler_params=pltpu.CompilerParams(dimension_semantics=("arbitrary",), vmem_limit_bytes=VMEM_LIMIT),
        name="split_even_w_in",
    )(w_in_t)


def _full(a):
    nd = a.ndim
    return a, pl.BlockSpec(a.shape, lambda b, _nd=nd: (0,) * _nd)


def _per_step(a, n_seq):
    nd = a.ndim
    return a, pl.BlockSpec((n_seq,) + a.shape[1:], lambda b, _nd=nd: (b,) + (0,) * (_nd - 1))


def _of_layer(a, i):
    nd = a.ndim
    return a, pl.BlockSpec((1,) + a.shape[1:], lambda b, _nd=nd, _i=i: (_i,) + (0,) * (_nd - 1))


def _seq_of_layer(a, i):
    nd = a.ndim
    return a, pl.BlockSpec((1, 1) + a.shape[2:], lambda b, _nd=nd, _i=i: (b, _i) + (0,) * (_nd - 2))


def _mod_spec(mod, l, row):
    if row is None:
        return mod, pl.BlockSpec((1, 1, 3, D_MODEL), lambda b, _l=l: (_l, b, 0, 0))
    return mod, pl.BlockSpec((1, 1, 3, D_MODEL), lambda b, _l=l, _r=row: (_l, _r, 0, 0))


def _rope_cs(n, rot_dim):
    rows = n // GRID_W
    r = np.repeat(np.arange(rows, dtype=np.float32), GRID_W)
    col = np.tile(np.arange(GRID_W, dtype=np.float32), rows)
    nf = rot_dim // 4
    inv = np.float32(ROPE_BASE) ** (-np.arange(nf, dtype=np.float32) / np.float32(nf))
    ar = r[:, None] * inv
    ac = col[:, None] * inv
    cos = np.concatenate([np.cos(ar), np.cos(ar), np.cos(ac), np.cos(ac)], axis=-1)
    sin = np.concatenate([-np.sin(ar), np.sin(ar), -np.sin(ac), np.sin(ac)], axis=-1)
    return cos.astype(np.float32), sin.astype(np.float32)


def _layer_call(body, x, n_seq, operands, caches, slot, scratch, name):
    b, n, _ = x.shape
    fresh = slot == 0
    shapes = [c if fresh else c.shape for c in caches]
    in_specs = [spec for _, spec in operands]
    args = [a for a, _ in operands]
    aliases = {}
    if not fresh:
        in_specs += [pl.BlockSpec(memory_space=pl.ANY) for _ in caches]
        args += list(caches)
        aliases = {len(operands) + k: 1 + k for k in range(len(caches))}
    out_shape = [jax.ShapeDtypeStruct(x.shape, F32)] + [jax.ShapeDtypeStruct(s, F32) for s in shapes]
    out_specs = [pl.BlockSpec((n_seq, n, D_MODEL), lambda i: (i, 0, 0))]
    for s in shapes:
        if fresh:
            out_specs.append(pl.BlockSpec((n_seq,) + tuple(s[1:]), lambda i: (i, 0, 0, 0)))
        else:
            out_specs.append(pl.BlockSpec((n_seq, 1) + tuple(s[2:]), lambda i, _s=slot: (i, _s, 0, 0)))
    return pl.pallas_call(
        body,
        grid=(b // n_seq,),
        in_specs=in_specs,
        out_specs=out_specs,
        out_shape=out_shape,
        input_output_aliases=aliases,
        scratch_shapes=scratch,
        compiler_params=pltpu.CompilerParams(dimension_semantics=("arbitrary",), vmem_limit_bytes=VMEM_LIMIT),
        name=name,
    )(*args)


def _store_cache(ref, sq, value):
    ref[sq, 0] = value
    for l in range(1, ref.shape[1]):
        ref[sq, l] = jnp.zeros(value.shape, value.dtype)


def _seqs_per_step(b, ctx):
    return 1 if ctx is not None else int(np.gcd(b, CTX_SEQS_PER_STEP))


def _even_layer(x, mod_op, i, p, ctx, tab, caches, final_g, name):
    b, n, _ = x.shape
    n_ctx = 0 if ctx is None else ctx[0].shape[2]
    n_seq = _seqs_per_step(b, ctx)
    ops = [_per_step(x, n_seq), mod_op, _of_layer(p["norm_g"], 2 * i),
           _of_layer(p["w_head"], i), _of_layer(p["w_main"], i), _of_layer(p["w_uq"], i), _of_layer(p["w_uk"], i),
           _of_layer(p["w_uv"], i), _of_layer(p["ev_w_out"], i),
           _of_layer(p["gq"], i), _of_layer(p["gkv"], i), _of_layer(p["conv_w"], i), _of_layer(p["conv_b"], i)]
    if ctx is not None:
        ops += [_seq_of_layer(ctx[0], i), _per_step(ctx[1][i], 1), _full(tab)]
    if final_g is not None:
        ops.append(_full(final_g))
    _, rows_g, rb = _grouping(n, n_seq, EVEN_MAX_ROWS)
    nk_g = n_ctx + rows_g
    scratch = [pltpu.VMEM((rows_g, D_MODEL), BF16), pltpu.VMEM((rows_g, H_A * LANES), BF16),
               pltpu.VMEM((nk_g, LANES), BF16), pltpu.VMEM((nk_g, LANES), BF16),
               pltpu.VMEM((nk_g, H_A * LANES), BF16), pltpu.VMEM((nk_g, 2 * W_A), BF16),
               pltpu.VMEM((rows_g + 2 * CONV_PAD, W_B), F32),
               pltpu.VMEM((2, min(rb, n, DENSE_ROWS), n_ctx + n), F32)]
    assert ctx is not None or n <= rb, "the context pass writes one cache block per sequence"
    body = functools.partial(_even_kernel, n=n, n_ctx=n_ctx, n_seq=n_seq, final=final_g is not None, fresh=i == 0)
    return _layer_call(body, x, n_seq, ops, caches, i, scratch, name)


def _odd_layer(x, mod_op, i, p, ctx, tab, caches, final_g, name):
    b, n, _ = x.shape
    n_ctx = 0 if ctx is None else ctx[0].shape[3]
    n_seq = _seqs_per_step(b, ctx)
    ops = [_per_step(x, n_seq), mod_op, _of_layer(p["norm_g"], 2 * i + 1),
           _of_layer(p["od_w_in"], i), _of_layer(p["od_w_out"], i),
           (p["sink"], pl.BlockSpec(memory_space=pltpu.SMEM)), _of_layer(p["gg"], i), _of_layer(p["ws"], i),
           _of_layer(p["bs"], i)]
    if ctx is not None:
        ops += [_seq_of_layer(ctx[0], i), _seq_of_layer(ctx[1], i), _full(tab)]
    if final_g is not None:
        ops.append(_full(final_g))
    pad = WINDOW if ctx is not None else 0
    _, rows_g, rb = _grouping(n, n_seq, ODD_MAX_ROWS)
    tq = min(rb, n, WINDOW_ROWS)
    n_keys = n_ctx + tq + 2 * pad if ctx is not None else n
    scratch = [pltpu.VMEM((rows_g, D_MODEL), BF16), pltpu.VMEM((rows_g + 2 * pad, KV_C * LANES), BF16),
               pltpu.VMEM((rows_g + 2 * pad, KV_C * 2 * LANES), BF16), pltpu.VMEM((2, tq, n_keys), F32)]
    if ctx is not None:
        scratch += [pltpu.VMEM((n_ctx, KV_C * LANES), BF16), pltpu.VMEM((n_ctx, KV_C * 2 * LANES), BF16)]
    assert ctx is not None or n <= rb, "the context pass writes one cache block per sequence"
    body = functools.partial(_odd_kernel, n=n, n_ctx=n_ctx, n_seq=n_seq, final=final_g is not None, fresh=i == 0,
                             layer=i)
    return _layer_call(body, x, n_seq, ops, caches, i, scratch, name)


def kernel(x_prompt, x_sample, cache_mla_ckv, cache_mla_krope, cache_swa_k, cache_swa_v, c, c_ctx, ada_w, ada_b, norm_g, ev_w_in, ev_w_out, mla_gq, mla_gkv, mla_w_uq, mla_w_uk, mla_w_uv, conv_w, conv_b, od_w_in, od_w_out, swa_sink, gmlp_g, gmlp_ws, gmlp_bs, final_g):
    bp, n_p, _ = x_prompt.shape
    bs_, n_s, _ = x_sample.shape
    n_past = cache_mla_ckv.shape[2]
    n_even, n_odd = ev_w_in.shape[0], od_w_in.shape[0]

    n_rows = -(-(bs_ + 1) // 8) * 8
    cvec = jnp.concatenate([c, c_ctx[None, :], jnp.zeros((n_rows - bs_ - 1, D_MODEL), F32)], axis=0)
    mod = _modulation(cvec, ada_w, ada_b).reshape(DEPTH, n_rows, 3, D_MODEL)

    cos_a, sin_a = _rope_cs(n_s, ROPE_A)
    cos_c, sin_c = _rope_cs(n_s, HD_C)
    zero = lambda w: np.zeros((n_s, w), np.float32)
    tab_a = jnp.asarray(np.concatenate([np.ones((n_s, NOPE_A), np.float32), cos_a, zero(LANES - QK_A),
                                        zero(NOPE_A), sin_a, zero(LANES - QK_A)], axis=1))
    tab_c = jnp.asarray(np.concatenate([cos_c, cos_c, sin_c, sin_c], axis=1))

    uq = mla_w_uq.reshape(n_even, Q_RANK, H_A, QK_A)
    uk = mla_w_uk.reshape(n_even, KV_RANK, H_A, NOPE_A)
    w_head, w_main = _split_even(jnp.swapaxes(ev_w_in, 1, 2))
    p = {
        "norm_g": norm_g.reshape(DEPTH, 1, D_MODEL),
        "w_head": w_head,
        "w_main": w_main,
        "w_uq": jnp.pad(uq, ((0, 0), (0, 0), (0, 0), (0, LANES - QK_A))).reshape(n_even, Q_RANK, H_A * LANES).astype(BF16),
        "w_uk": jnp.pad(uk, ((0, 0), (0, 0), (0, 0), (0, LANES - NOPE_A))).reshape(n_even, KV_RANK, H_A * LANES).astype(BF16),
        "w_uv": mla_w_uv.astype(BF16),
        "ev_w_out": ev_w_out.astype(BF16),
        "gq": mla_gq.reshape(n_even, 1, Q_RANK),
        "gkv": mla_gkv.reshape(n_even, 1, KV_RANK),
        "conv_w": conv_w,
        "conv_b": conv_b.reshape(n_even, 1, W_B),
        "od_w_in": od_w_in.astype(BF16),
        "od_w_out": od_w_out.astype(BF16),
        "sink": swa_sink,
        "gg": gmlp_g.reshape(n_odd, 1, W_D),
        "ws": jnp.swapaxes(gmlp_ws.reshape(n_odd, G_D // 2, 2, CHUNK, CHUNK), 2, 3)
                 .reshape(n_odd, G_D // 2, CHUNK, 2 * CHUNK).astype(BF16),
        "bs": jnp.repeat(jnp.swapaxes(gmlp_bs, 1, 2), DG_D, axis=2),
    }
    ckr = jnp.pad(cache_mla_krope, ((0, 0), (0, 0), (0, 0), (NOPE_A, LANES - QK_A)))
    ckr = [ckr[:, i] for i in range(n_even)]
    swa_k = jnp.transpose(cache_swa_k, (0, 1, 3, 4, 2)).reshape(bs_, n_odd, KV_C * HD_C, n_past)
    swa_v = jnp.transpose(cache_swa_v, (0, 1, 3, 4, 2)).reshape(bs_, n_odd, KV_C * HD_C, n_past)

    new_ckv = (bp, n_even, n_p, KV_RANK)
    new_kr = (bp, n_even, ROPE_A, n_p)
    new_k = (bp, n_odd, KV_C * HD_C, n_p)
    new_v = (bp, n_odd, KV_C * HD_C, n_p)

    fg = final_g.reshape(1, D_MODEL)
    xp, xs = x_prompt, x_sample
    for l in range(DEPTH):
        i = l // 2
        last = fg if l == DEPTH - 1 else None
        mod_p = _mod_spec(mod, l, bs_)
        mod_s = _mod_spec(mod, l, None)
        if l % 2 == 0:
            xp, new_ckv, new_kr = _even_layer(xp, mod_p, i, p, None, None, (new_ckv, new_kr), last, f"even{l}_ctx")
            (xs,) = _even_layer(xs, mod_s, i, p, (cache_mla_ckv, ckr), tab_a, (), last, f"even{l}_lat")
        else:
            xp, new_k, new_v = _odd_layer(xp, mod_p, i, p, None, None, (new_k, new_v), last, f"odd{l}_ctx")
            (xs,) = _odd_layer(xs, mod_s, i, p, (swa_k, swa_v), tab_c, (), last, f"odd{l}_lat")

    def heads_last(a):
        return jnp.transpose(a.reshape(bp, n_odd, KV_C, HD_C, n_p), (0, 1, 4, 2, 3))

    return (xp, xs, new_ckv, jnp.swapaxes(new_kr, 2, 3), heads_last(new_k), heads_last(new_v))
```

```python
import functools

import numpy as np
import jax
import jax.numpy as jnp
from jax import lax
from jax.experimental import pallas as pl
from jax.experimental.pallas import tpu as pltpu

D_MODEL = 1024
DEPTH = 4
GRID_W = 64
ROPE_BASE = 10000.0
EPS = 1e-6
NEG_INF = -1e30

H_A = 8
NOPE_A = 64
ROPE_A = 32
V_A = 64
Q_RANK = 256
KV_RANK = 128
QK_A = NOPE_A + ROPE_A
W_A = H_A * V_A
W_B = 512
CONV_W = 3
H_C = 8
KV_C = 2
HD_C = 64
WINDOW = 128
W_C = H_C * HD_C
G_D = 8
W_D = 512
CHUNK = 128
DG_D = W_D // G_D

EVEN_SPLIT = (Q_RANK, KV_RANK, ROPE_A, W_A, W_B, W_B, W_B, W_B)
ODD_SPLIT = (W_C, KV_C * HD_C, KV_C * HD_C, W_C, W_D, W_D, W_D)
EVEN_HEAD = Q_RANK + KV_RANK + ROPE_A

LANES = 128
EVEN_MAX_ROWS = 512
ODD_MAX_ROWS = 1024
DENSE_ROWS = 512
WINDOW_ROWS = 256
CONV_PAD = 8
EVEN_CTX_SEQS = 4
ODD_CTX_SEQS = 2
VMEM_LIMIT = 60 * 1024 * 1024

F32 = jnp.float32
BF16 = jnp.bfloat16
LOG2E = float(np.log2(np.e))


def _dot(a, b):
    return jnp.dot(a, b, preferred_element_type=F32)


def _dot_t(a, b):
    return lax.dot_general(a, b, (((1,), (1,)), ((), ())), preferred_element_type=F32)


def _rms(x, g):
    return x * lax.rsqrt(jnp.mean(x * x, axis=-1, keepdims=True) + EPS) * g


def _silu(x):
    return x * (1.0 / (1.0 + jnp.exp(-x)))


def _gelu(x):
    c = np.float32(np.sqrt(2.0 / np.pi))
    return 0.5 * x * (1.0 + jnp.tanh(c * (x + 0.044715 * (x * x * x))))


def _lane(shape):
    return lax.broadcasted_iota(jnp.int32, shape, 1)


def _pair_select(lo, hi):
    return jnp.where(_lane(lo.shape) < LANES // 2, lo, hi)


def _rope(x, tab, quarter):
    fwd = pltpu.roll(x, LANES - quarter, 1)
    bwd = pltpu.roll(x, quarter, 1)
    partner = jnp.where((_lane(x.shape) & quarter) == 0, fwd, bwd)
    return x * tab[:, 0:LANES] + partner * tab[:, LANES:2 * LANES]


def _dup_halves(x):
    sw = pltpu.roll(x, LANES // 2, 1)
    lo = _lane(x.shape) < LANES // 2
    return jnp.where(lo, x, sw), jnp.where(lo, sw, x)


def _modulated_norm(x, mod_ref, ng_ref):
    shift = mod_ref[0, 0, 0:1, :]
    scale = mod_ref[0, 0, 1:2, :]
    return _rms(x, ng_ref[0]) * (1.0 + scale) + shift


def _pipelined(n, produce, consume):
    produce(0)
    outs = []
    for i in range(n):
        if i + 1 < n:
            produce(i + 1)
        outs.append(consume(i))
    return outs


def _normalised(o):
    return o[:, 0:LANES] * (1.0 / o[:, LANES:2 * LANES])


def _loop(n, body):
    if n == 1:
        body(0)
    else:
        def step(i, carry):
            body(i)
            return carry
        lax.fori_loop(0, n, step, 0)


def _grouping(n, n_seq, max_rows):
    gs = max(1, min(n_seq, max_rows // n))
    rows_g = gs * n
    return gs, rows_g, min(rows_g, max_rows)


def _group_rows(ref, g, gs, r0, rb):
    if gs == 1:
        return ref[g, pl.ds(r0, rb), :]
    return ref[pl.ds(g * gs, gs), :, :].reshape(rb, ref.shape[-1])


def _set_group_rows(ref, g, gs, r0, rb, value):
    if gs == 1:
        ref[g, pl.ds(r0, rb), :] = value
    else:
        ref[pl.ds(g * gs, gs), :, :] = value.reshape(gs, rb // gs, ref.shape[-1])


def _row_loop(n_rows, rb, body):
    _loop(n_rows // rb, lambda i: body(i * rb if isinstance(i, int) else pl.multiple_of(i * rb, rb)))


def _even_kernel(*refs, n, n_ctx, n_seq, final, fresh):
    rope = n_ctx > 0
    it = iter(refs)
    x_ref, mod_ref, ng_ref = next(it), next(it), next(it)
    wh_ref, wm_ref, wuq_ref, wuk_ref, wuv_ref, wout_ref = (next(it) for _ in range(6))
    gq_ref, gkv_ref, cw_ref, cb_ref = next(it), next(it), next(it), next(it)
    if rope:
        cckv_ref, ckr_ref, tab_ref = next(it), next(it), next(it)
    if final:
        fg_ref = next(it)
    if not rope and not fresh:
        next(it), next(it)
    o_ref = next(it)
    if not rope:
        ckv_out, kr_out = next(it), next(it)
    h_s, q_s, ckv_s, kr_s, k_s, v_s, t_s, s_s = (next(it) for _ in range(8))

    gs, rows_g, rb = _grouping(n, n_seq, EVEN_MAX_ROWS)
    nk = n_ctx + n
    tq = min(rb, n, DENSE_ROWS)
    q_scale = np.float32(QK_A ** -0.5 * LOG2E)

    zpad = jnp.zeros((CONV_PAD, W_B), F32)
    t_s[0:CONV_PAD, :] = zpad
    t_s[CONV_PAD + rows_g:2 * CONV_PAD + rows_g, :] = zpad

    def group(g):
        if rope:
            ckv_s[0:n_ctx, :] = cckv_ref[0, 0].astype(BF16)
            kr_s[0:n_ctx, :] = ckr_ref[0].astype(BF16)

        def phase1(r0):
            rows = pl.ds(r0, rb)
            hb = _modulated_norm(_group_rows(x_ref, g, gs, r0, rb), mod_ref, ng_ref).astype(BF16)
            h_s[rows, :] = hb
            ph = _dot(hb, wh_ref[0])
            qn = _rms(ph[:, 0:Q_RANK], gq_ref[0]).astype(BF16)
            qq = _dot(qn, wuq_ref[0])
            ckv = _rms(ph[:, Q_RANK:Q_RANK + KV_RANK], gkv_ref[0])
            kr = ph[:, Q_RANK + KV_RANK:Q_RANK + KV_RANK + LANES]
            if rope:
                tab = tab_ref[rows, :]
                kr = _rope(kr, tab, ROPE_A // 4)
                for h in range(H_A):
                    qh = _rope(qq[:, h * LANES:(h + 1) * LANES], tab, ROPE_A // 4)
                    q_s[rows, h * LANES:(h + 1) * LANES] = (qh * q_scale).astype(BF16)
            else:
                q_s[rows, :] = (qq * q_scale).astype(BF16)
                for s in range(gs):
                    _store_cache(ckv_out, g * gs + s, ckv[s * n:(s + 1) * n, :])
                    _store_cache(kr_out, g * gs + s, kr[s * n:(s + 1) * n, :].T[NOPE_A:QK_A, :])
            krows = pl.ds(n_ctx + r0, rb)
            ckv_s[krows, :] = ckv.astype(BF16)
            kr_s[krows, :] = kr.astype(BF16)
            pc = _dot(hb, wm_ref[0, :, 2 * W_B:4 * W_B])
            t_s[pl.ds(CONV_PAD + r0, rb), :] = pc[:, 0:W_B] * pc[:, W_B:2 * W_B]

        _row_loop(rows_g, rb, phase1)

        rb2 = int(np.gcd(rb, n_ctx + rows_g))

        def phase2(r0):
            rows = pl.ds(r0, rb2)
            c = ckv_s[rows, :]
            kn = _dot(c, wuk_ref[0])
            krf = kr_s[rows, :].astype(F32)
            for h in range(H_A):
                k_s[rows, h * LANES:(h + 1) * LANES] = (kn[:, h * LANES:(h + 1) * LANES] + krf).astype(BF16)
            v = _dot(c, wuv_ref[0]).astype(BF16)
            ones = jnp.ones((rb2, LANES), BF16)
            for j in range(H_A // 2):
                v_s[rows, 2 * j * LANES:(2 * j + 1) * LANES] = v[:, j * LANES:(j + 1) * LANES]
                v_s[rows, (2 * j + 1) * LANES:(2 * j + 2) * LANES] = ones

        _row_loop(n_ctx + rows_g, rb2, phase2)

        def phase3(r0):
            rows = pl.ds(r0, rb)
            hb = h_s[rows, :]

            def keys_of(t):
                return slice(None) if gs == 1 else slice(t * n, (t + 1) * n)

            def scores(i):
                t, h = divmod(i, H_A)
                s_s[i % 2] = _dot_t(q_s[pl.ds(r0 + t * tq, tq), h * LANES:(h + 1) * LANES],
                                    k_s[keys_of(t), h * LANES:(h + 1) * LANES])

            def attend(i):
                t, h = divmod(i, H_A)
                s = s_s[i % 2]
                e = jnp.exp2(s - jnp.max(s, axis=-1, keepdims=True)).astype(BF16)
                j = h // 2
                return _normalised(_dot(e, v_s[keys_of(t), 2 * j * LANES:(2 * j + 2) * LANES]))

            heads = _pipelined((rb // tq) * H_A, scores, attend)

            def head_cols(j0, j1):
                return jnp.concatenate(
                    [jnp.concatenate([_pair_select(heads[t * H_A + 2 * j], heads[t * H_A + 2 * j + 1])
                                      for j in range(j0, j1)], axis=-1) for t in range(rb // tq)], axis=0)

            pa = _dot(hb, wm_ref[0, :, 0:2 * W_B])
            wa = W_A // 2
            a_lo = (head_cols(0, H_A // 4) * _silu(pa[:, 0:wa])).astype(BF16)
            a_hi = (head_cols(H_A // 4, H_A // 2) * _silu(pa[:, wa:W_A])).astype(BF16)
            gate_b = pa[:, W_A:W_A + W_B]
            g_b = _dot(hb, wm_ref[0, :, 4 * W_B:5 * W_B])

            t_cur = t_s[pl.ds(CONV_PAD + r0, rb), :]
            rid = lax.broadcasted_iota(jnp.int32, (rb, W_B), 0)
            if n <= rb:
                t_prev = jnp.where(rid % n == 0, 0.0, pltpu.roll(t_cur, 1, 0))
                t_next = jnp.where(rid % n == n - 1, 0.0, pltpu.roll(t_cur, rb - 1, 0))
            else:
                t_up = t_s[pl.ds(r0, CONV_PAD), :]
                t_dn = t_s[pl.ds(CONV_PAD + r0 + rb, CONV_PAD), :]
                t_prev = jnp.where(rid == 0, t_up[CONV_PAD - 1:CONV_PAD, :], pltpu.roll(t_cur, 1, 0))
                t_next = jnp.where(rid == rb - 1, t_dn[0:1, :], pltpu.roll(t_cur, rb - 1, 0))
            conv = cb_ref[0] + t_prev * cw_ref[0, 0:1, :]
            conv = conv + t_cur * cw_ref[0, 1:2, :]
            conv = conv + t_next * cw_ref[0, 2:3, :]
            b_out = gate_b * conv * _silu(g_b)

            y = (_dot(b_out.astype(BF16), wout_ref[0, W_A:W_A + W_B, :]) + _dot(a_lo, wout_ref[0, 0:wa, :])
                 + _dot(a_hi, wout_ref[0, wa:W_A, :]))
            res = _group_rows(x_ref, g, gs, r0, rb) + mod_ref[0, 0, 2:3, :] * y
            if final:
                res = _rms(res, fg_ref[...])
            _set_group_rows(o_ref, g, gs, r0, rb, res)

        _row_loop(rows_g, rb, phase3)

    _loop(n_seq // gs, group)


def _odd_kernel(*refs, n, n_ctx, n_seq, final, fresh, layer):
    rope = n_ctx > 0
    it = iter(refs)
    x_ref, mod_ref, ng_ref = next(it), next(it), next(it)
    win_ref, wout_ref = next(it), next(it)
    sink_ref, gg_ref, ws_ref, bs_ref = next(it), next(it), next(it), next(it)
    if rope:
        kc_ref, vc_ref, tab_ref = next(it), next(it), next(it)
    if final:
        fg_ref = next(it)
    if not rope and not fresh:
        next(it), next(it)
    o_ref = next(it)
    if not rope:
        k_out, v_out = next(it), next(it)
    h_s, kloc_s, vloc_s, s_s = next(it), next(it), next(it), next(it)
    if rope:
        kctx_s, vctx_s = next(it), next(it)

    pad = WINDOW if rope else 0
    gs, rows_g, rb = _grouping(n, n_seq, ODD_MAX_ROWS)
    tq = min(rb, n, WINDOW_ROWS)
    span = tq + 2 * pad
    kw, vw = LANES, 2 * LANES
    q0, k0, g0 = 0, W_C, W_C + 2 * KV_C * HD_C
    q_scale = np.float32(HD_C ** -0.5 * LOG2E)

    def store_keys(dst, rows, x):
        a, b = _dup_halves(x)
        dst[rows, 0:LANES] = a.astype(BF16)
        dst[rows, LANES:2 * LANES] = b.astype(BF16)

    def store_values(dst, rows, x):
        a, b = _dup_halves(x)
        ones = jnp.ones(x.shape, BF16)
        dst[rows, 0:LANES] = a.astype(BF16)
        dst[rows, LANES:2 * LANES] = ones
        dst[rows, 2 * LANES:3 * LANES] = b.astype(BF16)
        dst[rows, 3 * LANES:4 * LANES] = ones

    def group(g):
        if rope:
            kloc_s[0:pad, :] = jnp.zeros((pad, KV_C * kw), BF16)
            kloc_s[pad + n:2 * pad + n, :] = jnp.zeros((pad, KV_C * kw), BF16)
            vloc_s[0:pad, :] = jnp.zeros((pad, KV_C * vw), BF16)
            vloc_s[pad + n:2 * pad + n, :] = jnp.zeros((pad, KV_C * vw), BF16)
            store_keys(kctx_s, slice(None), kc_ref[0, 0].T)
            store_values(vctx_s, slice(None), vc_ref[0, 0].T)

        def phase1(r0):
            rows = pl.ds(r0, rb)
            hb = _modulated_norm(_group_rows(x_ref, g, gs, r0, rb), mod_ref, ng_ref).astype(BF16)
            h_s[rows, :] = hb
            p1 = _dot(hb, win_ref[0, :, k0:g0])
            k = p1[:, 0:LANES]
            v = p1[:, LANES:2 * LANES]
            if rope:
                k = _rope(k, tab_ref[rows, :], HD_C // 4)
            else:
                for s in range(gs):
                    _store_cache(k_out, g * gs + s, k[s * n:(s + 1) * n, :].T)
                    _store_cache(v_out, g * gs + s, v[s * n:(s + 1) * n, :].T)
            lrows = pl.ds(pad + r0, rb)
            store_keys(kloc_s, lrows, k)
            store_values(vloc_s, lrows, v)

        _row_loop(rows_g, rb, phase1)

        def phase3(r0):
            rows = pl.ds(r0, rb)
            hb = h_s[rows, :]
            pq = _dot(hb, win_ref[0, :, q0:k0])

            tiles = []
            for t in range(rb // tq):
                t0 = r0 + t * tq
                pqt = pq[t * tq:(t + 1) * tq, :]
                if rope:
                    tab = tab_ref[pl.ds(t0, tq), :]
                    kl = kloc_s[pl.ds(t0, span), :]
                    vl = vloc_s[pl.ds(t0, span), :]
                    ii = lax.broadcasted_iota(jnp.int32, (tq, span), 0)
                    jj = lax.broadcasted_iota(jnp.int32, (tq, span), 1)
                    kpos = jj + (t0 - WINDOW)
                    dlt = jj - ii
                    mask = (dlt >= 0) & (dlt <= 2 * WINDOW) & (kpos >= 0) & (kpos < n)
                else:
                    kl, vl, mask = kloc_s[t * n:(t + 1) * n, :], vloc_s[t * n:(t + 1) * n, :], None
                qhs = []
                for m in range(H_C // 2):
                    qb = pqt[:, m * LANES:(m + 1) * LANES]
                    if rope:
                        qb = _rope(qb, tab, HD_C // 4)
                    qb = qb * q_scale
                    lo = _lane(qb.shape) < LANES // 2
                    qhs.append(jnp.where(lo, qb, 0.0).astype(BF16))
                    qhs.append(jnp.where(lo, 0.0, qb).astype(BF16))
                tiles.append((kl, vl, mask, qhs))

            def kv_head(h):
                return h // (H_C // KV_C)

            def scores(i):
                (kl, _, _, qhs), h = tiles[i // H_C], i % H_C
                j = kv_head(h)
                if rope:
                    s_s[i % 2, :, 0:n_ctx] = _dot_t(qhs[h], kctx_s[:, j * kw:(j + 1) * kw])
                    s_s[i % 2, :, n_ctx:] = _dot_t(qhs[h], kl[:, j * kw:(j + 1) * kw])
                else:
                    s_s[i % 2] = _dot_t(qhs[h], kl[:, j * kw:(j + 1) * kw])

            def attend(i):
                (_, vl, mask, _), h = tiles[i // H_C], i % H_C
                j = kv_head(h)
                sk = sink_ref[layer, h] * np.float32(LOG2E)
                if rope:
                    sc = s_s[i % 2, :, 0:n_ctx]
                    sl = jnp.where(mask, s_s[i % 2, :, n_ctx:], NEG_INF)
                    mx = jnp.maximum(jnp.max(sl, axis=-1, keepdims=True), jnp.max(sc, axis=-1, keepdims=True))
                else:
                    sl = s_s[i % 2]
                    mx = jnp.max(sl, axis=-1, keepdims=True)
                mx = jnp.maximum(mx, sk)
                o = _dot(jnp.exp2(sl - mx).astype(BF16), vl[:, j * vw:(j + 1) * vw])
                if rope:
                    o = o + _dot(jnp.exp2(sc - mx).astype(BF16), vctx_s[:, j * vw:(j + 1) * vw])
                return o[:, 0:LANES] * (1.0 / (o[:, LANES:2 * LANES] + jnp.exp2(sk - mx)))

            heads = _pipelined(len(tiles) * H_C, scores, attend)
            pg = _dot(hb, win_ref[0, :, g0:g0 + W_C + 3 * W_D])
            yc = []
            for t in range(len(tiles)):
                attn_t = jnp.concatenate([_pair_select(heads[t * H_C + 2 * m], heads[t * H_C + 2 * m + 1])
                                          for m in range(H_C // 2)], axis=-1)
                c_out = attn_t * _silu(pg[t * tq:(t + 1) * tq, 0:W_C])
                yc.append(_dot(c_out.astype(BF16), wout_ref[0, 0:W_C, :]))
            u = pg[:, W_C:W_C + W_D]
            vv = pg[:, W_C + W_D:W_C + 2 * W_D]
            g_d = pg[:, W_C + 2 * W_D:W_C + 3 * W_D]

            vvn = _rms(_gelu(vv), gg_ref[0]).astype(BF16)
            chunks = []
            for c in range(rb // CHUNK):
                cols = []
                for jp in range(G_D // 2):
                    vp = vvn[c * CHUNK:(c + 1) * CHUNK, jp * LANES:(jp + 1) * LANES]
                    lo = _lane(vp.shape) < LANES // 2
                    zero = jnp.zeros_like(vp)
                    rhs = jnp.concatenate([jnp.where(lo, vp, zero), jnp.where(lo, zero, vp)], axis=0)
                    cols.append(_dot(ws_ref[0, jp], rhs))
                chunks.append(jnp.concatenate(cols, axis=-1) + bs_ref[0])
            mixed = jnp.concatenate(chunks, axis=0)
            d_out = _gelu(u) * mixed * _silu(g_d)

            y = jnp.concatenate(yc, axis=0) + _dot(d_out.astype(BF16), wout_ref[0, W_C:W_C + W_D, :])
            res = _group_rows(x_ref, g, gs, r0, rb) + mod_ref[0, 0, 2:3, :] * y
            if final:
                res = _rms(res, fg_ref[...])
            _set_group_rows(o_ref, g, gs, r0, rb, res)

        _row_loop(rows_g, rb, phase3)

    _loop(n_seq // gs, group)


def _mod_kernel(c_ref, w_ref, b_ref, o_ref):
    a = _silu(c_ref[...]).astype(BF16)
    o_ref[0] = _dot(a, w_ref[0].astype(BF16)) + b_ref[0]


def _modulation(cvec, ada_w, ada_b):
    rows = cvec.shape[0]
    tn = D_MODEL
    return pl.pallas_call(
        _mod_kernel,
        grid=(DEPTH, 3 * D_MODEL // tn),
        in_specs=[
            pl.BlockSpec((rows, D_MODEL), lambda l, j: (0, 0)),
            pl.BlockSpec((1, D_MODEL, tn), lambda l, j: (l, 0, j)),
            pl.BlockSpec((1, 1, tn), lambda l, j: (l, 0, j)),
        ],
        out_specs=pl.BlockSpec((1, rows, tn), lambda l, j: (l, 0, j)),
        out_shape=jax.ShapeDtypeStruct((DEPTH, rows, 3 * D_MODEL), F32),
        compiler_params=pltpu.CompilerParams(dimension_semantics=("arbitrary", "arbitrary")),
        name="modulation",
    )(cvec, ada_w, ada_b.reshape(DEPTH, 1, 3 * D_MODEL))


def _split_even_kernel(wt_ref, head_ref, main_ref):
    kv_end = Q_RANK + KV_RANK

    def put(dst, col, rows_f32):
        dst[0, :, col:col + LANES] = rows_f32.T.astype(BF16)

    for j in range(kv_end // LANES):
        put(head_ref, j * LANES, wt_ref[0, j * LANES:(j + 1) * LANES, :])
    k_r = wt_ref[0, kv_end:EVEN_HEAD, :]
    put(head_ref, kv_end, jnp.concatenate(
        [jnp.zeros((NOPE_A, D_MODEL), F32), k_r, jnp.zeros((LANES - QK_A, D_MODEL), F32)], axis=0))
    for j in range(main_ref.shape[2] // LANES):
        put(main_ref, j * LANES, wt_ref[0, EVEN_HEAD + j * LANES:EVEN_HEAD + (j + 1) * LANES, :])


def _split_even(w_in_t):
    n_layers, cols, d = w_in_t.shape
    n_head, n_main = Q_RANK + KV_RANK + LANES, cols - EVEN_HEAD
    return pl.pallas_call(
        _split_even_kernel,
        grid=(n_layers,),
        in_specs=[pl.BlockSpec((1, cols, d), lambda l: (l, 0, 0))],
        out_specs=[pl.BlockSpec((1, d, n_head), lambda l: (l, 0, 0)),
                   pl.BlockSpec((1, d, n_main), lambda l: (l, 0, 0))],
        out_shape=[jax.ShapeDtypeStruct((n_layers, d, n_head), BF16),
                   jax.ShapeDtypeStruct((n_layers, d, n_main), BF16)],
        compiler_params=pltpu.CompilerParams(dimension_semantics=("arbitrary",), vmem_limit_bytes=VMEM_LIMIT),
        name="split_even_w_in",
    )(w_in_t)


def _full(a):
    nd = a.ndim
    return a, pl.BlockSpec(a.shape, lambda b, _nd=nd: (0,) * _nd)


def _per_step(a, n_seq):
    nd = a.ndim
    return a, pl.BlockSpec((n_seq,) + a.shape[1:], lambda b, _nd=nd: (b,) + (0,) * (_nd - 1))


def _of_layer(a, i):
    nd = a.ndim
    return a, pl.BlockSpec((1,) + a.shape[1:], lambda b, _nd=nd, _i=i: (_i,) + (0,) * (_nd - 1))


def _seq_of_layer(a, i):
    nd = a.ndim
    return a, pl.BlockSpec((1, 1) + a.shape[2:], lambda b, _nd=nd, _i=i: (b, _i) + (0,) * (_nd - 2))


def _mod_spec(mod, l, row):
    if row is None:
        return mod, pl.BlockSpec((1, 1, 3, D_MODEL), lambda b, _l=l: (_l, b, 0, 0))
    return mod, pl.BlockSpec((1, 1, 3, D_MODEL), lambda b, _l=l, _r=row: (_l, _r, 0, 0))


def _rope_cs(n, rot_dim):
    rows = n // GRID_W
    r = np.repeat(np.arange(rows, dtype=np.float32), GRID_W)
    col = np.tile(np.arange(GRID_W, dtype=np.float32), rows)
    nf = rot_dim // 4
    inv = np.float32(ROPE_BASE) ** (-np.arange(nf, dtype=np.float32) / np.float32(nf))
    ar = r[:, None] * inv
    ac = col[:, None] * inv
    cos = np.concatenate([np.cos(ar), np.cos(ar), np.cos(ac), np.cos(ac)], axis=-1)
    sin = np.concatenate([-np.sin(ar), np.sin(ar), -np.sin(ac), np.sin(ac)], axis=-1)
    return cos.astype(np.float32), sin.astype(np.float32)


def _layer_call(body, x, n_seq, operands, caches, slot, scratch, name):
    b, n, _ = x.shape
    fresh = slot == 0
    shapes = [c if fresh else c.shape for c in caches]
    in_specs = [spec for _, spec in operands]
    args = [a for a, _ in operands]
    aliases = {}
    if not fresh:
        in_specs += [pl.BlockSpec(memory_space=pl.ANY) for _ in caches]
        args += list(caches)
        aliases = {len(operands) + k: 1 + k for k in range(len(caches))}
    out_shape = [jax.ShapeDtypeStruct(x.shape, F32)] + [jax.ShapeDtypeStruct(s, F32) for s in shapes]
    out_specs = [pl.BlockSpec((n_seq, n, D_MODEL), lambda i: (i, 0, 0))]
    for s in shapes:
        if fresh:
            out_specs.append(pl.BlockSpec((n_seq,) + tuple(s[1:]), lambda i: (i, 0, 0, 0)))
        else:
            out_specs.append(pl.BlockSpec((n_seq, 1) + tuple(s[2:]), lambda i, _s=slot: (i, _s, 0, 0)))
    return pl.pallas_call(
        body,
        grid=(b // n_seq,),
        in_specs=in_specs,
        out_specs=out_specs,
        out_shape=out_shape,
        input_output_aliases=aliases,
        scratch_shapes=scratch,
        compiler_params=pltpu.CompilerParams(dimension_semantics=("arbitrary",), vmem_limit_bytes=VMEM_LIMIT),
        name=name,
    )(*args)


def _store_cache(ref, sq, value):
    ref[sq, 0] = value
    for l in range(1, ref.shape[1]):
        ref[sq, l] = jnp.zeros(value.shape, value.dtype)


def _seqs_per_step(b, ctx, ctx_seqs):
    return 1 if ctx is not None else int(np.gcd(b, ctx_seqs))


def _even_layer(x, mod_op, i, p, ctx, tab, caches, final_g, name):
    b, n, _ = x.shape
    n_ctx = 0 if ctx is None else ctx[0].shape[2]
    n_seq = _seqs_per_step(b, ctx, EVEN_CTX_SEQS)
    ops = [_per_step(x, n_seq), mod_op, _of_layer(p["norm_g"], 2 * i),
           _of_layer(p["w_head"], i), _of_layer(p["w_main"], i), _of_layer(p["w_uq"], i), _of_layer(p["w_uk"], i),
           _of_layer(p["w_uv"], i), _of_layer(p["ev_w_out"], i),
           _of_layer(p["gq"], i), _of_layer(p["gkv"], i), _of_layer(p["conv_w"], i), _of_layer(p["conv_b"], i)]
    if ctx is not None:
        ops += [_seq_of_layer(ctx[0], i), _per_step(ctx[1][i], 1), _full(tab)]
    if final_g is not None:
        ops.append(_full(final_g))
    _, rows_g, rb = _grouping(n, n_seq, EVEN_MAX_ROWS)
    nk_g = n_ctx + rows_g
    scratch = [pltpu.VMEM((rows_g, D_MODEL), BF16), pltpu.VMEM((rows_g, H_A * LANES), BF16),
               pltpu.VMEM((nk_g, LANES), BF16), pltpu.VMEM((nk_g, LANES), BF16),
               pltpu.VMEM((nk_g, H_A * LANES), BF16), pltpu.VMEM((nk_g, 2 * W_A), BF16),
               pltpu.VMEM((rows_g + 2 * CONV_PAD, W_B), F32),
               pltpu.VMEM((2, min(rb, n, DENSE_ROWS), n_ctx + n), F32)]
    assert ctx is not None or n <= rb, "the context pass writes one cache block per sequence"
    body = functools.partial(_even_kernel, n=n, n_ctx=n_ctx, n_seq=n_seq, final=final_g is not None, fresh=i == 0)
    return _layer_call(body, x, n_seq, ops, caches, i, scratch, name)


def _odd_layer(x, mod_op, i, p, ctx, tab, caches, final_g, name):
    b, n, _ = x.shape
    n_ctx = 0 if ctx is None else ctx[0].shape[3]
    n_seq = _seqs_per_step(b, ctx, ODD_CTX_SEQS)
    ops = [_per_step(x, n_seq), mod_op, _of_layer(p["norm_g"], 2 * i + 1),
           _of_layer(p["od_w_in"], i), _of_layer(p["od_w_out"], i),
           (p["sink"], pl.BlockSpec(memory_space=pltpu.SMEM)), _of_layer(p["gg"], i), _of_layer(p["ws"], i),
           _of_layer(p["bs"], i)]
    if ctx is not None:
        ops += [_seq_of_layer(ctx[0], i), _seq_of_layer(ctx[1], i), _full(tab)]
    if final_g is not None:
        ops.append(_full(final_g))
    pad = WINDOW if ctx is not None else 0
    _, rows_g, rb = _grouping(n, n_seq, ODD_MAX_ROWS)
    tq = min(rb, n, WINDOW_ROWS)
    n_keys = n_ctx + tq + 2 * pad if ctx is not None else n
    scratch = [pltpu.VMEM((rows_g, D_MODEL), BF16), pltpu.VMEM((rows_g + 2 * pad, KV_C * LANES), BF16),
               pltpu.VMEM((rows_g + 2 * pad, KV_C * 2 * LANES), BF16), pltpu.VMEM((2, tq, n_keys), F32)]
    if ctx is not None:
        scratch += [pltpu.VMEM((n_ctx, KV_C * LANES), BF16), pltpu.VMEM((n_ctx, KV_C * 2 * LANES), BF16)]
    assert ctx is not None or n <= rb, "the context pass writes one cache block per sequence"
    body = functools.partial(_odd_kernel, n=n, n_ctx=n_ctx, n_seq=n_seq, final=final_g is not None, fresh=i == 0,
                             layer=i)
    return _layer_call(body, x, n_seq, ops, caches, i, scratch, name)


def kernel(x_prompt, x_sample, cache_mla_ckv, cache_mla_krope, cache_swa_k, cache_swa_v, c, c_ctx, ada_w, ada_b, norm_g, ev_w_in, ev_w_out, mla_gq, mla_gkv, mla_w_uq, mla_w_uk, mla_w_uv, conv_w, conv_b, od_w_in, od_w_out, swa_sink, gmlp_g, gmlp_ws, gmlp_bs, final_g):
    bp, n_p, _ = x_prompt.shape
    bs_, n_s, _ = x_sample.shape
    n_past = cache_mla_ckv.shape[2]
    n_even, n_odd = ev_w_in.shape[0], od_w_in.shape[0]

    n_rows = -(-(bs_ + 1) // 8) * 8
    cvec = jnp.concatenate([c, c_ctx[None, :], jnp.zeros((n_rows - bs_ - 1, D_MODEL), F32)], axis=0)
    mod = _modulation(cvec, ada_w, ada_b).reshape(DEPTH, n_rows, 3, D_MODEL)

    cos_a, sin_a = _rope_cs(n_s, ROPE_A)
    cos_c, sin_c = _rope_cs(n_s, HD_C)
    zero = lambda w: np.zeros((n_s, w), np.float32)
    tab_a = jnp.asarray(np.concatenate([np.ones((n_s, NOPE_A), np.float32), cos_a, zero(LANES - QK_A),
                                        zero(NOPE_A), sin_a, zero(LANES - QK_A)], axis=1))
    tab_c = jnp.asarray(np.concatenate([cos_c, cos_c, sin_c, sin_c], axis=1))

    uq = mla_w_uq.reshape(n_even, Q_RANK, H_A, QK_A)
    uk = mla_w_uk.reshape(n_even, KV_RANK, H_A, NOPE_A)
    w_head, w_main = _split_even(jnp.swapaxes(ev_w_in, 1, 2))
    p = {
        "norm_g": norm_g.reshape(DEPTH, 1, D_MODEL),
        "w_head": w_head,
        "w_main": w_main,
        "w_uq": jnp.pad(uq, ((0, 0), (0, 0), (0, 0), (0, LANES - QK_A))).reshape(n_even, Q_RANK, H_A * LANES).astype(BF16),
        "w_uk": jnp.pad(uk, ((0, 0), (0, 0), (0, 0), (0, LANES - NOPE_A))).reshape(n_even, KV_RANK, H_A * LANES).astype(BF16),
        "w_uv": mla_w_uv.astype(BF16),
        "ev_w_out": ev_w_out.astype(BF16),
        "gq": mla_gq.reshape(n_even, 1, Q_RANK),
        "gkv": mla_gkv.reshape(n_even, 1, KV_RANK),
        "conv_w": conv_w,
        "conv_b": conv_b.reshape(n_even, 1, W_B),
        "od_w_in": od_w_in.astype(BF16),
        "od_w_out": od_w_out.astype(BF16),
        "sink": swa_sink,
        "gg": gmlp_g.reshape(n_odd, 1, W_D),
        "ws": jnp.swapaxes(gmlp_ws.reshape(n_odd, G_D // 2, 2, CHUNK, CHUNK), 2, 3)
                 .reshape(n_odd, G_D // 2, CHUNK, 2 * CHUNK).astype(BF16),
        "bs": jnp.repeat(jnp.swapaxes(gmlp_bs, 1, 2), DG_D, axis=2),
    }
    ckr = jnp.pad(cache_mla_krope, ((0, 0), (0, 0), (0, 0), (NOPE_A, LANES - QK_A)))
    ckr = [ckr[:, i] for i in range(n_even)]
    swa_k = jnp.transpose(cache_swa_k, (0, 1, 3, 4, 2)).reshape(bs_, n_odd, KV_C * HD_C, n_past)
    swa_v = jnp.transpose(cache_swa_v, (0, 1, 3, 4, 2)).reshape(bs_, n_odd, KV_C * HD_C, n_past)

    new_ckv = (bp, n_even, n_p, KV_RANK)
    new_kr = (bp, n_even, ROPE_A, n_p)
    new_k = (bp, n_odd, KV_C * HD_C, n_p)
    new_v = (bp, n_odd, KV_C * HD_C, n_p)

    fg = final_g.reshape(1, D_MODEL)
    xp, xs = x_prompt, x_sample
    for l in range(DEPTH):
        i = l // 2
        last = fg if l == DEPTH - 1 else None
        mod_p = _mod_spec(mod, l, bs_)
        mod_s = _mod_spec(mod, l, None)
        if l % 2 == 0:
            xp, new_ckv, new_kr = _even_layer(xp, mod_p, i, p, None, None, (new_ckv, new_kr), last, f"even{l}_ctx")
            (xs,) = _even_layer(xs, mod_s, i, p, (cache_mla_ckv, ckr), tab_a, (), last, f"even{l}_lat")
        else:
            xp, new_k, new_v = _odd_layer(xp, mod_p, i, p, None, None, (new_k, new_v), last, f"odd{l}_ctx")
            (xs,) = _odd_layer(xs, mod_s, i, p, (swa_k, swa_v), tab_c, (), last, f"odd{l}_lat")

    def heads_last(a):
        return jnp.transpose(a.reshape(bp, n_odd, KV_C, HD_C, n_p), (0, 1, 4, 2, 3))

    return (xp, xs, new_ckv, jnp.swapaxes(new_kr, 2, 3), heads_last(new_k), heads_last(new_v))
```

```python
import functools

import numpy as np
import jax
import jax.numpy as jnp
from jax import lax
from jax.experimental import pallas as pl
from jax.experimental.pallas import tpu as pltpu

D_MODEL = 1024
DEPTH = 4
GRID_W = 64
ROPE_BASE = 10000.0
EPS = 1e-6
NEG_INF = -1e30

H_A = 8
NOPE_A = 64
ROPE_A = 32
V_A = 64
Q_RANK = 256
KV_RANK = 128
QK_A = NOPE_A + ROPE_A
W_A = H_A * V_A
W_B = 512
CONV_W = 3
H_C = 8
KV_C = 2
HD_C = 64
WINDOW = 128
W_C = H_C * HD_C
G_D = 8
W_D = 512
CHUNK = 128
DG_D = W_D // G_D

EVEN_SPLIT = (Q_RANK, KV_RANK, ROPE_A, W_A, W_B, W_B, W_B, W_B)
ODD_SPLIT = (W_C, KV_C * HD_C, KV_C * HD_C, W_C, W_D, W_D, W_D)
EVEN_HEAD = Q_RANK + KV_RANK + ROPE_A

LANES = 128
EVEN_MAX_ROWS = 512
ODD_MAX_ROWS = 1024
DENSE_ROWS = 512
WINDOW_ROWS = 256
CONV_PAD = 8
EVEN_CTX_SEQS = 4
ODD_CTX_SEQS = 2
VMEM_LIMIT = 60 * 1024 * 1024

F32 = jnp.float32
BF16 = jnp.bfloat16
LOG2E = float(np.log2(np.e))


def _dot(a, b):
    return jnp.dot(a, b, preferred_element_type=F32)


def _dot_t(a, b):
    return lax.dot_general(a, b, (((1,), (1,)), ((), ())), preferred_element_type=F32)


def _rms(x, g):
    return x * lax.rsqrt(jnp.mean(x * x, axis=-1, keepdims=True) + EPS) * g


def _silu(x):
    return x * (1.0 / (1.0 + jnp.exp(-x)))


def _gelu(x):
    c = np.float32(np.sqrt(2.0 / np.pi))
    return 0.5 * x * (1.0 + jnp.tanh(c * (x + 0.044715 * (x * x * x))))


def _lane(shape):
    return lax.broadcasted_iota(jnp.int32, shape, 1)


def _pair_select(lo, hi):
    return jnp.where(_lane(lo.shape) < LANES // 2, lo, hi)


def _rope(x, tab, quarter):
    fwd = pltpu.roll(x, LANES - quarter, 1)
    bwd = pltpu.roll(x, quarter, 1)
    partner = jnp.where((_lane(x.shape) & quarter) == 0, fwd, bwd)
    return x * tab[:, 0:LANES] + partner * tab[:, LANES:2 * LANES]


def _dup_halves(x):
    sw = pltpu.roll(x, LANES // 2, 1)
    lo = _lane(x.shape) < LANES // 2
    return jnp.where(lo, x, sw), jnp.where(lo, sw, x)


def _modulated_norm(x, mod_ref, ng_ref):
    shift = mod_ref[0, 0, 0:1, :]
    scale = mod_ref[0, 0, 1:2, :]
    return _rms(x, ng_ref[0]) * (1.0 + scale) + shift


def _pipelined(n, produce, consume):
    produce(0)
    outs = []
    for i in range(n):
        if i + 1 < n:
            produce(i + 1)
        outs.append(consume(i))
    return outs


def _normalised(o):
    return o[:, 0:LANES] * (1.0 / o[:, LANES:2 * LANES])


def _loop(n, body):
    if n == 1:
        body(0)
    else:
        def step(i, carry):
            body(i)
            return carry
        lax.fori_loop(0, n, step, 0)


def _grouping(n, n_seq, max_rows):
    gs = max(1, min(n_seq, max_rows // n))
    rows_g = gs * n
    return gs, rows_g, min(rows_g, max_rows)


def _group_rows(ref, g, gs, r0, rb):
    if gs == 1:
        return ref[g, pl.ds(r0, rb), :]
    return ref[pl.ds(g * gs, gs), :, :].reshape(rb, ref.shape[-1])


def _set_group_rows(ref, g, gs, r0, rb, value):
    if gs == 1:
        ref[g, pl.ds(r0, rb), :] = value
    else:
        ref[pl.ds(g * gs, gs), :, :] = value.reshape(gs, rb // gs, ref.shape[-1])


def _row_loop(n_rows, rb, body):
    _loop(n_rows // rb, lambda i: body(i * rb if isinstance(i, int) else pl.multiple_of(i * rb, rb)))


def _even_kernel(*refs, n, n_ctx, n_seq, final, fresh):
    rope = n_ctx > 0
    it = iter(refs)
    x_ref, mod_ref, ng_ref = next(it), next(it), next(it)
    wh_ref, wm_ref, wuq_ref, wuk_ref, wuv_ref, wout_ref = (next(it) for _ in range(6))
    gq_ref, gkv_ref, cw_ref, cb_ref = next(it), next(it), next(it), next(it)
    if rope:
        cckv_ref, ckr_ref, tab_ref = next(it), next(it), next(it)
    if final:
        fg_ref = next(it)
    if not rope and not fresh:
        next(it), next(it)
    o_ref = next(it)
    if not rope:
        ckv_out, kr_out = next(it), next(it)
    h_s, q_s, k_s, v_s, t_s, s_s = (next(it) for _ in range(6))

    gs, rows_g, rb = _grouping(n, n_seq, EVEN_MAX_ROWS)
    nk = n_ctx + n
    tq = min(rb, n, DENSE_ROWS)
    q_scale = np.float32(QK_A ** -0.5 * LOG2E)

    zpad = jnp.zeros((CONV_PAD, W_B), F32)
    t_s[0:CONV_PAD, :] = zpad
    t_s[CONV_PAD + rows_g:2 * CONV_PAD + rows_g, :] = zpad

    def expand_kv(rows, c, kr):
        kn = _dot(c, wuk_ref[0])
        for h in range(H_A):
            k_s[rows, h * LANES:(h + 1) * LANES] = (kn[:, h * LANES:(h + 1) * LANES] + kr).astype(BF16)
        v = _dot(c, wuv_ref[0]).astype(BF16)
        ones = jnp.ones((c.shape[0], LANES), BF16)
        for j in range(H_A // 2):
            v_s[rows, 2 * j * LANES:(2 * j + 1) * LANES] = v[:, j * LANES:(j + 1) * LANES]
            v_s[rows, (2 * j + 1) * LANES:(2 * j + 2) * LANES] = ones

    def group(g):
        if rope:
            expand_kv(slice(0, n_ctx), cckv_ref[0, 0].astype(BF16), ckr_ref[0])

        def phase1(r0):
            rows = pl.ds(r0, rb)
            hb = _modulated_norm(_group_rows(x_ref, g, gs, r0, rb), mod_ref, ng_ref).astype(BF16)
            h_s[rows, :] = hb
            ph = _dot(hb, wh_ref[0])
            qn = _rms(ph[:, 0:Q_RANK], gq_ref[0]).astype(BF16)
            qq = _dot(qn, wuq_ref[0])
            ckv = _rms(ph[:, Q_RANK:Q_RANK + KV_RANK], gkv_ref[0])
            kr = ph[:, Q_RANK + KV_RANK:Q_RANK + KV_RANK + LANES]
            if rope:
                tab = tab_ref[rows, :]
                kr = _rope(kr, tab, ROPE_A // 4)
                for h in range(H_A):
                    qh = _rope(qq[:, h * LANES:(h + 1) * LANES], tab, ROPE_A // 4)
                    q_s[rows, h * LANES:(h + 1) * LANES] = (qh * q_scale).astype(BF16)
            else:
                q_s[rows, :] = (qq * q_scale).astype(BF16)
                for s in range(gs):
                    _store_cache(ckv_out, g * gs + s, ckv[s * n:(s + 1) * n, :])
                    _store_cache(kr_out, g * gs + s, kr[s * n:(s + 1) * n, :].T[NOPE_A:QK_A, :])
            expand_kv(pl.ds(n_ctx + r0, rb), ckv.astype(BF16), kr)
            pc = _dot(hb, wm_ref[0, :, 2 * W_B:4 * W_B])
            t_s[pl.ds(CONV_PAD + r0, rb), :] = pc[:, 0:W_B] * pc[:, W_B:2 * W_B]

        _row_loop(rows_g, rb, phase1)

        def phase3(r0):
            rows = pl.ds(r0, rb)
            hb = h_s[rows, :]

            def keys_of(t):
                return slice(None) if gs == 1 else slice(t * n, (t + 1) * n)

            def scores(i):
                t, h = divmod(i, H_A)
                s_s[i % 2] = _dot_t(q_s[pl.ds(r0 + t * tq, tq), h * LANES:(h + 1) * LANES],
                                    k_s[keys_of(t), h * LANES:(h + 1) * LANES])

            def attend(i):
                t, h = divmod(i, H_A)
                s = s_s[i % 2]
                e = jnp.exp2(s - jnp.max(s, axis=-1, keepdims=True)).astype(BF16)
                j = h // 2
                return _normalised(_dot(e, v_s[keys_of(t), 2 * j * LANES:(2 * j + 2) * LANES]))

            heads = _pipelined((rb // tq) * H_A, scores, attend)

            def head_cols(j0, j1):
                return jnp.concatenate(
                    [jnp.concatenate([_pair_select(heads[t * H_A + 2 * j], heads[t * H_A + 2 * j + 1])
                                      for j in range(j0, j1)], axis=-1) for t in range(rb // tq)], axis=0)

            pa = _dot(hb, wm_ref[0, :, 0:2 * W_B])
            wa = W_A // 2
            a_lo = (head_cols(0, H_A // 4) * _silu(pa[:, 0:wa])).astype(BF16)
            a_hi = (head_cols(H_A // 4, H_A // 2) * _silu(pa[:, wa:W_A])).astype(BF16)
            gate_b = pa[:, W_A:W_A + W_B]
            g_b = _dot(hb, wm_ref[0, :, 4 * W_B:5 * W_B])

            t_cur = t_s[pl.ds(CONV_PAD + r0, rb), :]
            rid = lax.broadcasted_iota(jnp.int32, (rb, W_B), 0)
            if n <= rb:
                t_prev = jnp.where(rid % n == 0, 0.0, pltpu.roll(t_cur, 1, 0))
                t_next = jnp.where(rid % n == n - 1, 0.0, pltpu.roll(t_cur, rb - 1, 0))
            else:
                t_up = t_s[pl.ds(r0, CONV_PAD), :]
                t_dn = t_s[pl.ds(CONV_PAD + r0 + rb, CONV_PAD), :]
                t_prev = jnp.where(rid == 0, t_up[CONV_PAD - 1:CONV_PAD, :], pltpu.roll(t_cur, 1, 0))
                t_next = jnp.where(rid == rb - 1, t_dn[0:1, :], pltpu.roll(t_cur, rb - 1, 0))
            conv = cb_ref[0] + t_prev * cw_ref[0, 0:1, :]
            conv = conv + t_cur * cw_ref[0, 1:2, :]
            conv = conv + t_next * cw_ref[0, 2:3, :]
            b_out = gate_b * conv * _silu(g_b)

            y = (_dot(b_out.astype(BF16), wout_ref[0, W_A:W_A + W_B, :]) + _dot(a_lo, wout_ref[0, 0:wa, :])
                 + _dot(a_hi, wout_ref[0, wa:W_A, :]))
            res = _group_rows(x_ref, g, gs, r0, rb) + mod_ref[0, 0, 2:3, :] * y
            if final:
                res = _rms(res, fg_ref[...])
            _set_group_rows(o_ref, g, gs, r0, rb, res)

        _row_loop(rows_g, rb, phase3)

    _loop(n_seq // gs, group)


def _odd_kernel(*refs, n, n_ctx, n_seq, final, fresh, layer):
    rope = n_ctx > 0
    it = iter(refs)
    x_ref, mod_ref, ng_ref = next(it), next(it), next(it)
    win_ref, wout_ref = next(it), next(it)
    sink_ref, gg_ref, ws_ref, bs_ref = next(it), next(it), next(it), next(it)
    if rope:
        kc_ref, vc_ref, tab_ref = next(it), next(it), next(it)
    if final:
        fg_ref = next(it)
    if not rope and not fresh:
        next(it), next(it)
    o_ref = next(it)
    if not rope:
        k_out, v_out = next(it), next(it)
    h_s, kloc_s, vloc_s, s_s = next(it), next(it), next(it), next(it)
    if rope:
        kctx_s, vctx_s = next(it), next(it)

    pad = WINDOW if rope else 0
    gs, rows_g, rb = _grouping(n, n_seq, ODD_MAX_ROWS)
    tq = min(rb, n, WINDOW_ROWS)
    span = tq + 2 * pad
    kw, vw = LANES, 2 * LANES
    q0, k0, g0 = 0, W_C, W_C + 2 * KV_C * HD_C
    q_scale = np.float32(HD_C ** -0.5 * LOG2E)

    def store_keys(dst, rows, x):
        a, b = _dup_halves(x)
        dst[rows, 0:LANES] = a.astype(BF16)
        dst[rows, LANES:2 * LANES] = b.astype(BF16)

    def store_values(dst, rows, x):
        a, b = _dup_halves(x)
        ones = jnp.ones(x.shape, BF16)
        dst[rows, 0:LANES] = a.astype(BF16)
        dst[rows, LANES:2 * LANES] = ones
        dst[rows, 2 * LANES:3 * LANES] = b.astype(BF16)
        dst[rows, 3 * LANES:4 * LANES] = ones

    def group(g):
        if rope:
            kloc_s[0:pad, :] = jnp.zeros((pad, KV_C * kw), BF16)
            kloc_s[pad + n:2 * pad + n, :] = jnp.zeros((pad, KV_C * kw), BF16)
            vloc_s[0:pad, :] = jnp.zeros((pad, KV_C * vw), BF16)
            vloc_s[pad + n:2 * pad + n, :] = jnp.zeros((pad, KV_C * vw), BF16)
            store_keys(kctx_s, slice(None), kc_ref[0, 0].T)
            store_values(vctx_s, slice(None), vc_ref[0, 0].T)

        def phase1(r0):
            rows = pl.ds(r0, rb)
            hb = _modulated_norm(_group_rows(x_ref, g, gs, r0, rb), mod_ref, ng_ref).astype(BF16)
            h_s[rows, :] = hb
            p1 = _dot(hb, win_ref[0, :, k0:g0])
            k = p1[:, 0:LANES]
            v = p1[:, LANES:2 * LANES]
            if rope:
                k = _rope(k, tab_ref[rows, :], HD_C // 4)
            else:
                for s in range(gs):
                    _store_cache(k_out, g * gs + s, k[s * n:(s + 1) * n, :].T)
                    _store_cache(v_out, g * gs + s, v[s * n:(s + 1) * n, :].T)
            lrows = pl.ds(pad + r0, rb)
            store_keys(kloc_s, lrows, k)
            store_values(vloc_s, lrows, v)

        _row_loop(rows_g, rb, phase1)

        def phase3(r0):
            rows = pl.ds(r0, rb)
            hb = h_s[rows, :]
            pq = _dot(hb, win_ref[0, :, q0:k0])

            tiles = []
            for t in range(rb // tq):
                t0 = r0 + t * tq
                pqt = pq[t * tq:(t + 1) * tq, :]
                if rope:
                    tab = tab_ref[pl.ds(t0, tq), :]
                    kl = kloc_s[pl.ds(t0, span), :]
                    vl = vloc_s[pl.ds(t0, span), :]
                    ii = lax.broadcasted_iota(jnp.int32, (tq, span), 0)
                    jj = lax.broadcasted_iota(jnp.int32, (tq, span), 1)
                    kpos = jj + (t0 - WINDOW)
                    dlt = jj - ii
                    mask = (dlt >= 0) & (dlt <= 2 * WINDOW) & (kpos >= 0) & (kpos < n)
                else:
                    kl, vl, mask = kloc_s[t * n:(t + 1) * n, :], vloc_s[t * n:(t + 1) * n, :], None
                qhs = []
                for m in range(H_C // 2):
                    qb = pqt[:, m * LANES:(m + 1) * LANES]
                    if rope:
                        qb = _rope(qb, tab, HD_C // 4)
                    qb = qb * q_scale
                    lo = _lane(qb.shape) < LANES // 2
                    qhs.append(jnp.where(lo, qb, 0.0).astype(BF16))
                    qhs.append(jnp.where(lo, 0.0, qb).astype(BF16))
                tiles.append((kl, vl, mask, qhs))

            def kv_head(h):
                return h // (H_C // KV_C)

            def scores(i):
                (kl, _, _, qhs), h = tiles[i // H_C], i % H_C
                j = kv_head(h)
                if rope:
                    s_s[i % 2, :, 0:n_ctx] = _dot_t(qhs[h], kctx_s[:, j * kw:(j + 1) * kw])
                    s_s[i % 2, :, n_ctx:] = _dot_t(qhs[h], kl[:, j * kw:(j + 1) * kw])
                else:
                    s_s[i % 2] = _dot_t(qhs[h], kl[:, j * kw:(j + 1) * kw])

            def attend(i):
                (_, vl, mask, _), h = tiles[i // H_C], i % H_C
                j = kv_head(h)
                sk = sink_ref[layer, h] * np.float32(LOG2E)
                if rope:
                    sc = s_s[i % 2, :, 0:n_ctx]
                    sl = jnp.where(mask, s_s[i % 2, :, n_ctx:], NEG_INF)
                    mx = jnp.maximum(jnp.max(sl, axis=-1, keepdims=True), jnp.max(sc, axis=-1, keepdims=True))
                else:
                    sl = s_s[i % 2]
                    mx = jnp.max(sl, axis=-1, keepdims=True)
                mx = jnp.maximum(mx, sk)
                o = _dot(jnp.exp2(sl - mx).astype(BF16), vl[:, j * vw:(j + 1) * vw])
                if rope:
                    o = o + _dot(jnp.exp2(sc - mx).astype(BF16), vctx_s[:, j * vw:(j + 1) * vw])
                return o[:, 0:LANES] * (1.0 / (o[:, LANES:2 * LANES] + jnp.exp2(sk - mx)))

            heads = _pipelined(len(tiles) * H_C, scores, attend)
            pg = _dot(hb, win_ref[0, :, g0:g0 + W_C + 3 * W_D])
            yc = []
            for t in range(len(tiles)):
                attn_t = jnp.concatenate([_pair_select(heads[t * H_C + 2 * m], heads[t * H_C + 2 * m + 1])
                                          for m in range(H_C // 2)], axis=-1)
                c_out = attn_t * _silu(pg[t * tq:(t + 1) * tq, 0:W_C])
                yc.append(_dot(c_out.astype(BF16), wout_ref[0, 0:W_C, :]))
            u = pg[:, W_C:W_C + W_D]
            vv = pg[:, W_C + W_D:W_C + 2 * W_D]
            g_d = pg[:, W_C + 2 * W_D:W_C + 3 * W_D]

            vvn = _rms(_gelu(vv), gg_ref[0]).astype(BF16)
            chunks = []
            for c in range(rb // CHUNK):
                cols = []
                for jp in range(G_D // 2):
                    vp = vvn[c * CHUNK:(c + 1) * CHUNK, jp * LANES:(jp + 1) * LANES]
                    lo = _lane(vp.shape) < LANES // 2
                    zero = jnp.zeros_like(vp)
                    rhs = jnp.concatenate([jnp.where(lo, vp, zero), jnp.where(lo, zero, vp)], axis=0)
                    cols.append(_dot(ws_ref[0, jp], rhs))
                chunks.append(jnp.concatenate(cols, axis=-1) + bs_ref[0])
            mixed = jnp.concatenate(chunks, axis=0)
            d_out = _gelu(u) * mixed * _silu(g_d)

            y = jnp.concatenate(yc, axis=0) + _dot(d_out.astype(BF16), wout_ref[0, W_C:W_C + W_D, :])
            res = _group_rows(x_ref, g, gs, r0, rb) + mod_ref[0, 0, 2:3, :] * y
            if final:
                res = _rms(res, fg_ref[...])
            _set_group_rows(o_ref, g, gs, r0, rb, res)

        _row_loop(rows_g, rb, phase3)

    _loop(n_seq // gs, group)


def _mod_kernel(c_ref, w_ref, b_ref, o_ref):
    a = _silu(c_ref[...]).astype(BF16)
    o_ref[0] = _dot(a, w_ref[0].astype(BF16)) + b_ref[0]


def _modulation(cvec, ada_w, ada_b):
    rows = cvec.shape[0]
    tn = D_MODEL
    return pl.pallas_call(
        _mod_kernel,
        grid=(DEPTH, 3 * D_MODEL // tn),
        in_specs=[
            pl.BlockSpec((rows, D_MODEL), lambda l, j: (0, 0)),
            pl.BlockSpec((1, D_MODEL, tn), lambda l, j: (l, 0, j)),
            pl.BlockSpec((1, 1, tn), lambda l, j: (l, 0, j)),
        ],
        out_specs=pl.BlockSpec((1, rows, tn), lambda l, j: (l, 0, j)),
        out_shape=jax.ShapeDtypeStruct((DEPTH, rows, 3 * D_MODEL), F32),
        compiler_params=pltpu.CompilerParams(dimension_semantics=("arbitrary", "arbitrary")),
        name="modulation",
    )(cvec, ada_w, ada_b.reshape(DEPTH, 1, 3 * D_MODEL))


def _split_even_kernel(wt_ref, head_ref, main_ref):
    kv_end = Q_RANK + KV_RANK

    def put(dst, col, rows_f32):
        dst[0, :, col:col + LANES] = rows_f32.T.astype(BF16)

    for j in range(kv_end // LANES):
        put(head_ref, j * LANES, wt_ref[0, j * LANES:(j + 1) * LANES, :])
    k_r = wt_ref[0, kv_end:EVEN_HEAD, :]
    put(head_ref, kv_end, jnp.concatenate(
        [jnp.zeros((NOPE_A, D_MODEL), F32), k_r, jnp.zeros((LANES - QK_A, D_MODEL), F32)], axis=0))
    for j in range(main_ref.shape[2] // LANES):
        put(main_ref, j * LANES, wt_ref[0, EVEN_HEAD + j * LANES:EVEN_HEAD + (j + 1) * LANES, :])


def _split_even(w_in_t):
    n_layers, cols, d = w_in_t.shape
    n_head, n_main = Q_RANK + KV_RANK + LANES, cols - EVEN_HEAD
    return pl.pallas_call(
        _split_even_kernel,
        grid=(n_layers,),
        in_specs=[pl.BlockSpec((1, cols, d), lambda l: (l, 0, 0))],
        out_specs=[pl.BlockSpec((1, d, n_head), lambda l: (l, 0, 0)),
                   pl.BlockSpec((1, d, n_main), lambda l: (l, 0, 0))],
        out_shape=[jax.ShapeDtypeStruct((n_layers, d, n_head), BF16),
                   jax.ShapeDtypeStruct((n_layers, d, n_main), BF16)],
        compiler_params=pltpu.CompilerParams(dimension_semantics=("arbitrary",), vmem_limit_bytes=VMEM_LIMIT),
        name="split_even_w_in",
    )(w_in_t)


def _full(a):
    nd = a.ndim
    return a, pl.BlockSpec(a.shape, lambda b, _nd=nd: (0,) * _nd)


def _per_step(a, n_seq):
    nd = a.ndim
    return a, pl.BlockSpec((n_seq,) + a.shape[1:], lambda b, _nd=nd: (b,) + (0,) * (_nd - 1))


def _of_layer(a, i):
    nd = a.ndim
    return a, pl.BlockSpec((1,) + a.shape[1:], lambda b, _nd=nd, _i=i: (_i,) + (0,) * (_nd - 1))


def _seq_of_layer(a, i):
    nd = a.ndim
    return a, pl.BlockSpec((1, 1) + a.shape[2:], lambda b, _nd=nd, _i=i: (b, _i) + (0,) * (_nd - 2))


def _mod_spec(mod, l, row):
    if row is None:
        return mod, pl.BlockSpec((1, 1, 3, D_MODEL), lambda b, _l=l: (_l, b, 0, 0))
    return mod, pl.BlockSpec((1, 1, 3, D_MODEL), lambda b, _l=l, _r=row: (_l, _r, 0, 0))


def _rope_cs(n, rot_dim):
    rows = n // GRID_W
    r = np.repeat(np.arange(rows, dtype=np.float32), GRID_W)
    col = np.tile(np.arange(GRID_W, dtype=np.float32), rows)
    nf = rot_dim // 4
    inv = np.float32(ROPE_BASE) ** (-np.arange(nf, dtype=np.float32) / np.float32(nf))
    ar = r[:, None] * inv
    ac = col[:, None] * inv
    cos = np.concatenate([np.cos(ar), np.cos(ar), np.cos(ac), np.cos(ac)], axis=-1)
    sin = np.concatenate([-np.sin(ar), np.sin(ar), -np.sin(ac), np.sin(ac)], axis=-1)
    return cos.astype(np.float32), sin.astype(np.float32)


def _layer_call(body, x, n_seq, operands, caches, slot, scratch, name):
    b, n, _ = x.shape
    fresh = slot == 0
    shapes = [c if fresh else c.shape for c in caches]
    in_specs = [spec for _, spec in operands]
    args = [a for a, _ in operands]
    aliases = {}
    if not fresh:
        in_specs += [pl.BlockSpec(memory_space=pl.ANY) for _ in caches]
        args += list(caches)
        aliases = {len(operands) + k: 1 + k for k in range(len(caches))}
    out_shape = [jax.ShapeDtypeStruct(x.shape, F32)] + [jax.ShapeDtypeStruct(s, F32) for s in shapes]
    out_specs = [pl.BlockSpec((n_seq, n, D_MODEL), lambda i: (i, 0, 0))]
    for s in shapes:
        if fresh:
            out_specs.append(pl.BlockSpec((n_seq,) + tuple(s[1:]), lambda i: (i, 0, 0, 0)))
        else:
            out_specs.append(pl.BlockSpec((n_seq, 1) + tuple(s[2:]), lambda i, _s=slot: (i, _s, 0, 0)))
    return pl.pallas_call(
        body,
        grid=(b // n_seq,),
        in_specs=in_specs,
        out_specs=out_specs,
        out_shape=out_shape,
        input_output_aliases=aliases,
        scratch_shapes=scratch,
        compiler_params=pltpu.CompilerParams(dimension_semantics=("arbitrary",), vmem_limit_bytes=VMEM_LIMIT),
        name=name,
    )(*args)


def _store_cache(ref, sq, value):
    ref[sq, 0] = value
    for l in range(1, ref.shape[1]):
        ref[sq, l] = jnp.zeros(value.shape, value.dtype)


def _seqs_per_step(b, ctx, ctx_seqs):
    return 1 if ctx is not None else int(np.gcd(b, ctx_seqs))


def _even_layer(x, mod_op, i, p, ctx, tab, caches, final_g, name):
    b, n, _ = x.shape
    n_ctx = 0 if ctx is None else ctx[0].shape[2]
    n_seq = _seqs_per_step(b, ctx, EVEN_CTX_SEQS)
    ops = [_per_step(x, n_seq), mod_op, _of_layer(p["norm_g"], 2 * i),
           _of_layer(p["w_head"], i), _of_layer(p["w_main"], i), _of_layer(p["w_uq"], i), _of_layer(p["w_uk"], i),
           _of_layer(p["w_uv"], i), _of_layer(p["ev_w_out"], i),
           _of_layer(p["gq"], i), _of_layer(p["gkv"], i), _of_layer(p["conv_w"], i), _of_layer(p["conv_b"], i)]
    if ctx is not None:
        ops += [_seq_of_layer(ctx[0], i), _per_step(ctx[1][i], 1), _full(tab)]
    if final_g is not None:
        ops.append(_full(final_g))
    _, rows_g, rb = _grouping(n, n_seq, EVEN_MAX_ROWS)
    nk_g = n_ctx + rows_g
    scratch = [pltpu.VMEM((rows_g, D_MODEL), BF16), pltpu.VMEM((rows_g, H_A * LANES), BF16),
               pltpu.VMEM((nk_g, H_A * LANES), BF16), pltpu.VMEM((nk_g, 2 * W_A), BF16),
               pltpu.VMEM((rows_g + 2 * CONV_PAD, W_B), F32),
               pltpu.VMEM((2, min(rb, n, DENSE_ROWS), n_ctx + n), F32)]
    assert ctx is not None or n <= rb, "the context pass writes one cache block per sequence"
    body = functools.partial(_even_kernel, n=n, n_ctx=n_ctx, n_seq=n_seq, final=final_g is not None, fresh=i == 0)
    return _layer_call(body, x, n_seq, ops, caches, i, scratch, name)


def _odd_layer(x, mod_op, i, p, ctx, tab, caches, final_g, name):
    b, n, _ = x.shape
    n_ctx = 0 if ctx is None else ctx[0].shape[3]
    n_seq = _seqs_per_step(b, ctx, ODD_CTX_SEQS)
    ops = [_per_step(x, n_seq), mod_op, _of_layer(p["norm_g"], 2 * i + 1),
           _of_layer(p["od_w_in"], i), _of_layer(p["od_w_out"], i),
           (p["sink"], pl.BlockSpec(memory_space=pltpu.SMEM)), _of_layer(p["gg"], i), _of_layer(p["ws"], i),
           _of_layer(p["bs"], i)]
    if ctx is not None:
        ops += [_seq_of_layer(ctx[0], i), _seq_of_layer(ctx[1], i), _full(tab)]
    if final_g is not None:
        ops.append(_full(final_g))
    pad = WINDOW if ctx is not None else 0
    _, rows_g, rb = _grouping(n, n_seq, ODD_MAX_ROWS)
    tq = min(rb, n, WINDOW_ROWS)
    n_keys = n_ctx + tq + 2 * pad if ctx is not None else n
    scratch = [pltpu.VMEM((rows_g, D_MODEL), BF16), pltpu.VMEM((rows_g + 2 * pad, KV_C * LANES), BF16),
               pltpu.VMEM((rows_g + 2 * pad, KV_C * 2 * LANES), BF16), pltpu.VMEM((2, tq, n_keys), F32)]
    if ctx is not None:
        scratch += [pltpu.VMEM((n_ctx, KV_C * LANES), BF16), pltpu.VMEM((n_ctx, KV_C * 2 * LANES), BF16)]
    assert ctx is not None or n <= rb, "the context pass writes one cache block per sequence"
    body = functools.partial(_odd_kernel, n=n, n_ctx=n_ctx, n_seq=n_seq, final=final_g is not None, fresh=i == 0,
                             layer=i)
    return _layer_call(body, x, n_seq, ops, caches, i, scratch, name)


def kernel(x_prompt, x_sample, cache_mla_ckv, cache_mla_krope, cache_swa_k, cache_swa_v, c, c_ctx, ada_w, ada_b, norm_g, ev_w_in, ev_w_out, mla_gq, mla_gkv, mla_w_uq, mla_w_uk, mla_w_uv, conv_w, conv_b, od_w_in, od_w_out, swa_sink, gmlp_g, gmlp_ws, gmlp_bs, final_g):
    bp, n_p, _ = x_prompt.shape
    bs_, n_s, _ = x_sample.shape
    n_past = cache_mla_ckv.shape[2]
    n_even, n_odd = ev_w_in.shape[0], od_w_in.shape[0]

    n_rows = -(-(bs_ + 1) // 8) * 8
    cvec = jnp.concatenate([c, c_ctx[None, :], jnp.zeros((n_rows - bs_ - 1, D_MODEL), F32)], axis=0)
    mod = _modulation(cvec, ada_w, ada_b).reshape(DEPTH, n_rows, 3, D_MODEL)

    cos_a, sin_a = _rope_cs(n_s, ROPE_A)
    cos_c, sin_c = _rope_cs(n_s, HD_C)
    zero = lambda w: np.zeros((n_s, w), np.float32)
    tab_a = jnp.asarray(np.concatenate([np.ones((n_s, NOPE_A), np.float32), cos_a, zero(LANES - QK_A),
                                        zero(NOPE_A), sin_a, zero(LANES - QK_A)], axis=1))
    tab_c = jnp.asarray(np.concatenate([cos_c, cos_c, sin_c, sin_c], axis=1))

    uq = mla_w_uq.reshape(n_even, Q_RANK, H_A, QK_A)
    uk = mla_w_uk.reshape(n_even, KV_RANK, H_A, NOPE_A)
    w_head, w_main = _split_even(jnp.swapaxes(ev_w_in, 1, 2))
    p = {
        "norm_g": norm_g.reshape(DEPTH, 1, D_MODEL),
        "w_head": w_head,
        "w_main": w_main,
        "w_uq": jnp.pad(uq, ((0, 0), (0, 0), (0, 0), (0, LANES - QK_A))).reshape(n_even, Q_RANK, H_A * LANES).astype(BF16),
        "w_uk": jnp.pad(uk, ((0, 0), (0, 0), (0, 0), (0, LANES - NOPE_A))).reshape(n_even, KV_RANK, H_A * LANES).astype(BF16),
        "w_uv": mla_w_uv.astype(BF16),
        "ev_w_out": ev_w_out.astype(BF16),
        "gq": mla_gq.reshape(n_even, 1, Q_RANK),
        "gkv": mla_gkv.reshape(n_even, 1, KV_RANK),
        "conv_w": conv_w,
        "conv_b": conv_b.reshape(n_even, 1, W_B),
        "od_w_in": od_w_in.astype(BF16),
        "od_w_out": od_w_out.astype(BF16),
        "sink": swa_sink,
        "gg": gmlp_g.reshape(n_odd, 1, W_D),
        "ws": jnp.swapaxes(gmlp_ws.reshape(n_odd, G_D // 2, 2, CHUNK, CHUNK), 2, 3)
                 .reshape(n_odd, G_D // 2, CHUNK, 2 * CHUNK).astype(BF16),
        "bs": jnp.repeat(jnp.swapaxes(gmlp_bs, 1, 2), DG_D, axis=2),
    }
    ckr = jnp.pad(cache_mla_krope, ((0, 0), (0, 0), (0, 0), (NOPE_A, LANES - QK_A)))
    ckr = [ckr[:, i] for i in range(n_even)]
    swa_k = jnp.transpose(cache_swa_k, (0, 1, 3, 4, 2)).reshape(bs_, n_odd, KV_C * HD_C, n_past)
    swa_v = jnp.transpose(cache_swa_v, (0, 1, 3, 4, 2)).reshape(bs_, n_odd, KV_C * HD_C, n_past)

    new_ckv = (bp, n_even, n_p, KV_RANK)
    new_kr = (bp, n_even, ROPE_A, n_p)
    new_k = (bp, n_odd, KV_C * HD_C, n_p)
    new_v = (bp, n_odd, KV_C * HD_C, n_p)

    fg = final_g.reshape(1, D_MODEL)
    xp, xs = x_prompt, x_sample
    for l in range(DEPTH):
        i = l // 2
        last = fg if l == DEPTH - 1 else None
        mod_p = _mod_spec(mod, l, bs_)
        mod_s = _mod_spec(mod, l, None)
        if l % 2 == 0:
            xp, new_ckv, new_kr = _even_layer(xp, mod_p, i, p, None, None, (new_ckv, new_kr), last, f"even{l}_ctx")
            (xs,) = _even_layer(xs, mod_s, i, p, (cache_mla_ckv, ckr), tab_a, (), last, f"even{l}_lat")
        else:
            xp, new_k, new_v = _odd_layer(xp, mod_p, i, p, None, None, (new_k, new_v), last, f"odd{l}_ctx")
            (xs,) = _odd_layer(xs, mod_s, i, p, (swa_k, swa_v), tab_c, (), last, f"odd{l}_lat")

    def heads_last(a):
        return jnp.transpose(a.reshape(bp, n_odd, KV_C, HD_C, n_p), (0, 1, 4, 2, 3))

    return (xp, xs, new_ckv, jnp.swapaxes(new_kr, 2, 3), heads_last(new_k), heads_last(new_v))
```

```python
import functools

import numpy as np
import jax
import jax.numpy as jnp
from jax import lax
from jax.experimental import pallas as pl
from jax.experimental.pallas import tpu as pltpu

D_MODEL = 1024
DEPTH = 4
GRID_W = 64
ROPE_BASE = 10000.0
EPS = 1e-6
NEG_INF = -1e30

H_A = 8
NOPE_A = 64
ROPE_A = 32
V_A = 64
Q_RANK = 256
KV_RANK = 128
QK_A = NOPE_A + ROPE_A
W_A = H_A * V_A
W_B = 512
CONV_W = 3
H_C = 8
KV_C = 2
HD_C = 64
WINDOW = 128
W_C = H_C * HD_C
G_D = 8
W_D = 512
CHUNK = 128
DG_D = W_D // G_D

EVEN_SPLIT = (Q_RANK, KV_RANK, ROPE_A, W_A, W_B, W_B, W_B, W_B)
ODD_SPLIT = (W_C, KV_C * HD_C, KV_C * HD_C, W_C, W_D, W_D, W_D)
EVEN_HEAD = Q_RANK + KV_RANK + ROPE_A

LANES = 128
EVEN_MAX_ROWS = 512
ODD_MAX_ROWS = 1024
DENSE_ROWS = 512
WINDOW_ROWS = 256
CONV_PAD = 8
EVEN_CTX_SEQS = 4
ODD_CTX_SEQS = 2
VMEM_LIMIT = 60 * 1024 * 1024

F32 = jnp.float32
BF16 = jnp.bfloat16
LOG2E = float(np.log2(np.e))


def _dot(a, b):
    return jnp.dot(a, b, preferred_element_type=F32)


def _dot_t(a, b):
    return lax.dot_general(a, b, (((1,), (1,)), ((), ())), preferred_element_type=F32)


def _rms(x, g):
    return x * lax.rsqrt(jnp.mean(x * x, axis=-1, keepdims=True) + EPS) * g


def _silu(x):
    return x * (1.0 / (1.0 + jnp.exp(-x)))


def _gelu(x):
    c = np.float32(np.sqrt(2.0 / np.pi))
    return 0.5 * x * (1.0 + jnp.tanh(c * (x + 0.044715 * (x * x * x))))


def _lane(shape):
    return lax.broadcasted_iota(jnp.int32, shape, 1)


def _pair_select(lo, hi):
    return jnp.where(_lane(lo.shape) < LANES // 2, lo, hi)


def _rope(x, tab, quarter):
    fwd = pltpu.roll(x, LANES - quarter, 1)
    bwd = pltpu.roll(x, quarter, 1)
    partner = jnp.where((_lane(x.shape) & quarter) == 0, fwd, bwd)
    return x * tab[:, 0:LANES] + partner * tab[:, LANES:2 * LANES]


def _dup_halves(x):
    sw = pltpu.roll(x, LANES // 2, 1)
    lo = _lane(x.shape) < LANES // 2
    return jnp.where(lo, x, sw), jnp.where(lo, sw, x)


def _mod_part(mod_ref, row, k):
    return mod_ref[0, pl.ds(row, 1), k * D_MODEL:(k + 1) * D_MODEL]


def _modulated_norm(x, mod_ref, row, ng_ref):
    return _rms(x, ng_ref[0]) * (1.0 + _mod_part(mod_ref, row, 1)) + _mod_part(mod_ref, row, 0)


def _pipelined(n, produce, consume):
    produce(0)
    outs = []
    for i in range(n):
        if i + 1 < n:
            produce(i + 1)
        outs.append(consume(i))
    return outs


def _normalised(o):
    return o[:, 0:LANES] * (1.0 / o[:, LANES:2 * LANES])


def _loop(n, body):
    if n == 1:
        body(0)
    else:
        def step(i, carry):
            body(i)
            return carry
        lax.fori_loop(0, n, step, 0)


def _grouping(n, n_seq, max_rows):
    gs = max(1, min(n_seq, max_rows // n))
    rows_g = gs * n
    return gs, rows_g, min(rows_g, max_rows)


def _group_rows(ref, g, gs, r0, rb):
    if gs == 1:
        return ref[g, pl.ds(r0, rb), :]
    return ref[pl.ds(g * gs, gs), :, :].reshape(rb, ref.shape[-1])


def _set_group_rows(ref, g, gs, r0, rb, value):
    if gs == 1:
        ref[g, pl.ds(r0, rb), :] = value
    else:
        ref[pl.ds(g * gs, gs), :, :] = value.reshape(gs, rb // gs, ref.shape[-1])


def _row_loop(n_rows, rb, body):
    _loop(n_rows // rb, lambda i: body(i * rb if isinstance(i, int) else pl.multiple_of(i * rb, rb)))


def _even_kernel(*refs, n, n_ctx, n_seq, final, fresh, mod_row):
    rope = n_ctx > 0
    row = pl.program_id(0) if mod_row is None else mod_row
    it = iter(refs)
    x_ref, mod_ref, ng_ref = next(it), next(it), next(it)
    wh_ref, wm_ref, wuq_ref, wuk_ref, wuv_ref, wout_ref = (next(it) for _ in range(6))
    gq_ref, gkv_ref, cw_ref, cb_ref = next(it), next(it), next(it), next(it)
    if rope:
        cckv_ref, ckr_ref, tab_ref = next(it), next(it), next(it)
    if final:
        fg_ref = next(it)
    if not rope and not fresh:
        next(it), next(it)
    o_ref = next(it)
    if not rope:
        ckv_out, kr_out = next(it), next(it)
    h_s, q_s, ckv_s, kr_s, k_s, v_s, t_s, s_s = (next(it) for _ in range(8))

    gs, rows_g, rb = _grouping(n, n_seq, EVEN_MAX_ROWS)
    nk = n_ctx + n
    tq = min(rb, n, DENSE_ROWS)
    q_scale = np.float32(QK_A ** -0.5 * LOG2E)

    zpad = jnp.zeros((CONV_PAD, W_B), F32)
    t_s[0:CONV_PAD, :] = zpad
    t_s[CONV_PAD + rows_g:2 * CONV_PAD + rows_g, :] = zpad

    def group(g):
        if rope:
            ckv_s[0:n_ctx, :] = cckv_ref[0, 0].astype(BF16)
            slab = jnp.concatenate([jnp.zeros((NOPE_A, n_ctx), F32), ckr_ref[0, 0],
                                    jnp.zeros((LANES - QK_A, n_ctx), F32)], axis=0)
            kr_s[0:n_ctx, :] = slab.T.astype(BF16)

        def phase1(r0):
            rows = pl.ds(r0, rb)
            hb = _modulated_norm(_group_rows(x_ref, g, gs, r0, rb), mod_ref, row, ng_ref).astype(BF16)
            h_s[rows, :] = hb
            ph = _dot(hb, wh_ref[0])
            qn = _rms(ph[:, 0:Q_RANK], gq_ref[0]).astype(BF16)
            qq = _dot(qn, wuq_ref[0])
            ckv = _rms(ph[:, Q_RANK:Q_RANK + KV_RANK], gkv_ref[0])
            kr = ph[:, Q_RANK + KV_RANK:Q_RANK + KV_RANK + LANES]
            if rope:
                tab = tab_ref[rows, :]
                kr = _rope(kr, tab, ROPE_A // 4)
                for h in range(H_A):
                    qh = _rope(qq[:, h * LANES:(h + 1) * LANES], tab, ROPE_A // 4)
                    q_s[rows, h * LANES:(h + 1) * LANES] = (qh * q_scale).astype(BF16)
            else:
                q_s[rows, :] = (qq * q_scale).astype(BF16)
                for s in range(gs):
                    _store_cache(ckv_out, g * gs + s, ckv[s * n:(s + 1) * n, :])
                    _store_cache(kr_out, g * gs + s, kr[s * n:(s + 1) * n, :].T[NOPE_A:QK_A, :])
            krows = pl.ds(n_ctx + r0, rb)
            ckv_s[krows, :] = ckv.astype(BF16)
            kr_s[krows, :] = kr.astype(BF16)
            pc = _dot(hb, wm_ref[0, :, 2 * W_B:4 * W_B])
            t_s[pl.ds(CONV_PAD + r0, rb), :] = pc[:, 0:W_B] * pc[:, W_B:2 * W_B]

        _row_loop(rows_g, rb, phase1)

        rb2 = int(np.gcd(rb, n_ctx + rows_g))

        def phase2(r0):
            rows = pl.ds(r0, rb2)
            c = ckv_s[rows, :]
            kn = _dot(c, wuk_ref[0])
            krf = kr_s[rows, :].astype(F32)
            for h in range(H_A):
                k_s[rows, h * LANES:(h + 1) * LANES] = (kn[:, h * LANES:(h + 1) * LANES] + krf).astype(BF16)
            v = _dot(c, wuv_ref[0]).astype(BF16)
            ones = jnp.ones((rb2, LANES), BF16)
            for j in range(H_A // 2):
                v_s[rows, 2 * j * LANES:(2 * j + 1) * LANES] = v[:, j * LANES:(j + 1) * LANES]
                v_s[rows, (2 * j + 1) * LANES:(2 * j + 2) * LANES] = ones

        _row_loop(n_ctx + rows_g, rb2, phase2)

        def phase3(r0):
            rows = pl.ds(r0, rb)
            hb = h_s[rows, :]

            def keys_of(t):
                return slice(None) if gs == 1 else slice(t * n, (t + 1) * n)

            def scores(i):
                t, h = divmod(i, H_A)
                s_s[i % 2] = _dot_t(q_s[pl.ds(r0 + t * tq, tq), h * LANES:(h + 1) * LANES],
                                    k_s[keys_of(t), h * LANES:(h + 1) * LANES])

            def attend(i):
                t, h = divmod(i, H_A)
                s = s_s[i % 2]
                e = jnp.exp2(s - jnp.max(s, axis=-1, keepdims=True)).astype(BF16)
                j = h // 2
                return _normalised(_dot(e, v_s[keys_of(t), 2 * j * LANES:(2 * j + 2) * LANES]))

            heads = _pipelined((rb // tq) * H_A, scores, attend)

            def head_cols(j0, j1):
                return jnp.concatenate(
                    [jnp.concatenate([_pair_select(heads[t * H_A + 2 * j], heads[t * H_A + 2 * j + 1])
                                      for j in range(j0, j1)], axis=-1) for t in range(rb // tq)], axis=0)

            pa = _dot(hb, wm_ref[0, :, 0:2 * W_B])
            wa = W_A // 2
            a_lo = (head_cols(0, H_A // 4) * _silu(pa[:, 0:wa])).astype(BF16)
            a_hi = (head_cols(H_A // 4, H_A // 2) * _silu(pa[:, wa:W_A])).astype(BF16)
            gate_b = pa[:, W_A:W_A + W_B]
            g_b = _dot(hb, wm_ref[0, :, 4 * W_B:5 * W_B])

            t_cur = t_s[pl.ds(CONV_PAD + r0, rb), :]
            rid = lax.broadcasted_iota(jnp.int32, (rb, W_B), 0)
            if n <= rb:
                t_prev = jnp.where(rid % n == 0, 0.0, pltpu.roll(t_cur, 1, 0))
                t_next = jnp.where(rid % n == n - 1, 0.0, pltpu.roll(t_cur, rb - 1, 0))
            else:
                t_up = t_s[pl.ds(r0, CONV_PAD), :]
                t_dn = t_s[pl.ds(CONV_PAD + r0 + rb, CONV_PAD), :]
                t_prev = jnp.where(rid == 0, t_up[CONV_PAD - 1:CONV_PAD, :], pltpu.roll(t_cur, 1, 0))
                t_next = jnp.where(rid == rb - 1, t_dn[0:1, :], pltpu.roll(t_cur, rb - 1, 0))
            conv = cb_ref[0] + t_prev * cw_ref[0, 0:1, :]
            conv = conv + t_cur * cw_ref[0, 1:2, :]
            conv = conv + t_next * cw_ref[0, 2:3, :]
            b_out = gate_b * conv * _silu(g_b)

            y = (_dot(b_out.astype(BF16), wout_ref[0, W_A:W_A + W_B, :]) + _dot(a_lo, wout_ref[0, 0:wa, :])
                 + _dot(a_hi, wout_ref[0, wa:W_A, :]))
            res = _group_rows(x_ref, g, gs, r0, rb) + _mod_part(mod_ref, row, 2) * y
            if final:
                res = _rms(res, fg_ref[...])
            _set_group_rows(o_ref, g, gs, r0, rb, res)

        _row_loop(rows_g, rb, phase3)

    _loop(n_seq // gs, group)


def _odd_kernel(*refs, n, n_ctx, n_seq, final, fresh, layer, mod_row):
    rope = n_ctx > 0
    row = pl.program_id(0) if mod_row is None else mod_row
    it = iter(refs)
    x_ref, mod_ref, ng_ref = next(it), next(it), next(it)
    win_ref, wout_ref = next(it), next(it)
    sink_ref, gg_ref, ws_ref, bs_ref = next(it), next(it), next(it), next(it)
    if rope:
        kc_ref, vc_ref, tab_ref = next(it), next(it), next(it)
    if final:
        fg_ref = next(it)
    if not rope and not fresh:
        next(it), next(it)
    o_ref = next(it)
    if not rope:
        k_out, v_out = next(it), next(it)
    h_s, kloc_s, vloc_s, s_s = next(it), next(it), next(it), next(it)
    if rope:
        kctx_s, vctx_s = next(it), next(it)

    pad = WINDOW if rope else 0
    gs, rows_g, rb = _grouping(n, n_seq, ODD_MAX_ROWS)
    tq = min(rb, n, WINDOW_ROWS)
    span = tq + 2 * pad
    kw, vw = LANES, 2 * LANES
    q0, k0, g0 = 0, W_C, W_C + 2 * KV_C * HD_C
    q_scale = np.float32(HD_C ** -0.5 * LOG2E)

    def store_keys(dst, rows, x):
        a, b = _dup_halves(x)
        dst[rows, 0:LANES] = a.astype(BF16)
        dst[rows, LANES:2 * LANES] = b.astype(BF16)

    def store_values(dst, rows, x):
        a, b = _dup_halves(x)
        ones = jnp.ones(x.shape, BF16)
        dst[rows, 0:LANES] = a.astype(BF16)
        dst[rows, LANES:2 * LANES] = ones
        dst[rows, 2 * LANES:3 * LANES] = b.astype(BF16)
        dst[rows, 3 * LANES:4 * LANES] = ones

    def group(g):
        if rope:
            kloc_s[0:pad, :] = jnp.zeros((pad, KV_C * kw), BF16)
            kloc_s[pad + n:2 * pad + n, :] = jnp.zeros((pad, KV_C * kw), BF16)
            vloc_s[0:pad, :] = jnp.zeros((pad, KV_C * vw), BF16)
            vloc_s[pad + n:2 * pad + n, :] = jnp.zeros((pad, KV_C * vw), BF16)
            store_keys(kctx_s, slice(None), kc_ref[0, 0].T)
            store_values(vctx_s, slice(None), vc_ref[0, 0].T)

        def phase1(r0):
            rows = pl.ds(r0, rb)
            hb = _modulated_norm(_group_rows(x_ref, g, gs, r0, rb), mod_ref, row, ng_ref).astype(BF16)
            h_s[rows, :] = hb
            p1 = _dot(hb, win_ref[0, :, k0:g0])
            k = p1[:, 0:LANES]
            v = p1[:, LANES:2 * LANES]
            if rope:
                k = _rope(k, tab_ref[rows, :], HD_C // 4)
            else:
                for s in range(gs):
                    _store_cache(k_out, g * gs + s, k[s * n:(s + 1) * n, :].T)
                    _store_cache(v_out, g * gs + s, v[s * n:(s + 1) * n, :].T)
            lrows = pl.ds(pad + r0, rb)
            store_keys(kloc_s, lrows, k)
            store_values(vloc_s, lrows, v)

        _row_loop(rows_g, rb, phase1)

        def phase3(r0):
            rows = pl.ds(r0, rb)
            hb = h_s[rows, :]
            pq = _dot(hb, win_ref[0, :, q0:k0])

            tiles = []
            for t in range(rb // tq):
                t0 = r0 + t * tq
                pqt = pq[t * tq:(t + 1) * tq, :]
                if rope:
                    tab = tab_ref[pl.ds(t0, tq), :]
                    kl = kloc_s[pl.ds(t0, span), :]
                    vl = vloc_s[pl.ds(t0, span), :]
                    ii = lax.broadcasted_iota(jnp.int32, (tq, span), 0)
                    jj = lax.broadcasted_iota(jnp.int32, (tq, span), 1)
                    kpos = jj + (t0 - WINDOW)
                    dlt = jj - ii
                    mask = (dlt >= 0) & (dlt <= 2 * WINDOW) & (kpos >= 0) & (kpos < n)
                else:
                    kl, vl, mask = kloc_s[t * n:(t + 1) * n, :], vloc_s[t * n:(t + 1) * n, :], None
                qhs = []
                for m in range(H_C // 2):
                    qb = pqt[:, m * LANES:(m + 1) * LANES]
                    if rope:
                        qb = _rope(qb, tab, HD_C // 4)
                    qb = qb * q_scale
                    lo = _lane(qb.shape) < LANES // 2
                    qhs.append(jnp.where(lo, qb, 0.0).astype(BF16))
                    qhs.append(jnp.where(lo, 0.0, qb).astype(BF16))
                tiles.append((kl, vl, mask, qhs))

            def kv_head(h):
                return h // (H_C // KV_C)

            def scores(i):
                (kl, _, _, qhs), h = tiles[i // H_C], i % H_C
                j = kv_head(h)
                if rope:
                    s_s[i % 2, :, 0:n_ctx] = _dot_t(qhs[h], kctx_s[:, j * kw:(j + 1) * kw])
                    s_s[i % 2, :, n_ctx:] = _dot_t(qhs[h], kl[:, j * kw:(j + 1) * kw])
                else:
                    s_s[i % 2] = _dot_t(qhs[h], kl[:, j * kw:(j + 1) * kw])

            def attend(i):
                (_, vl, mask, _), h = tiles[i // H_C], i % H_C
                j = kv_head(h)
                sk = sink_ref[layer, h] * np.float32(LOG2E)
                if rope:
                    sc = s_s[i % 2, :, 0:n_ctx]
                    sl = jnp.where(mask, s_s[i % 2, :, n_ctx:], NEG_INF)
                    mx = jnp.maximum(jnp.max(sl, axis=-1, keepdims=True), jnp.max(sc, axis=-1, keepdims=True))
                else:
                    sl = s_s[i % 2]
                    mx = jnp.max(sl, axis=-1, keepdims=True)
                mx = jnp.maximum(mx, sk)
                o = _dot(jnp.exp2(sl - mx).astype(BF16), vl[:, j * vw:(j + 1) * vw])
                if rope:
                    o = o + _dot(jnp.exp2(sc - mx).astype(BF16), vctx_s[:, j * vw:(j + 1) * vw])
                return o[:, 0:LANES] * (1.0 / (o[:, LANES:2 * LANES] + jnp.exp2(sk - mx)))

            heads = _pipelined(len(tiles) * H_C, scores, attend)
            pg = _dot(hb, win_ref[0, :, g0:g0 + W_C + 3 * W_D])
            yc = []
            for t in range(len(tiles)):
                attn_t = jnp.concatenate([_pair_select(heads[t * H_C + 2 * m], heads[t * H_C + 2 * m + 1])
                                          for m in range(H_C // 2)], axis=-1)
                c_out = attn_t * _silu(pg[t * tq:(t + 1) * tq, 0:W_C])
                yc.append(_dot(c_out.astype(BF16), wout_ref[0, 0:W_C, :]))
            u = pg[:, W_C:W_C + W_D]
            vv = pg[:, W_C + W_D:W_C + 2 * W_D]
            g_d = pg[:, W_C + 2 * W_D:W_C + 3 * W_D]

            vvn = _rms(_gelu(vv), gg_ref[0]).astype(BF16)
            chunks = []
            for c in range(rb // CHUNK):
                cols = []
                for jp in range(G_D // 2):
                    vp = vvn[c * CHUNK:(c + 1) * CHUNK, jp * LANES:(jp + 1) * LANES]
                    lo = _lane(vp.shape) < LANES // 2
                    zero = jnp.zeros_like(vp)
                    rhs = jnp.concatenate([jnp.where(lo, vp, zero), jnp.where(lo, zero, vp)], axis=0)
                    cols.append(_dot(ws_ref[0, jp], rhs))
                chunks.append(jnp.concatenate(cols, axis=-1) + bs_ref[0])
            mixed = jnp.concatenate(chunks, axis=0)
            d_out = _gelu(u) * mixed * _silu(g_d)

            y = jnp.concatenate(yc, axis=0) + _dot(d_out.astype(BF16), wout_ref[0, W_C:W_C + W_D, :])
            res = _group_rows(x_ref, g, gs, r0, rb) + _mod_part(mod_ref, row, 2) * y
            if final:
                res = _rms(res, fg_ref[...])
            _set_group_rows(o_ref, g, gs, r0, rb, res)

        _row_loop(rows_g, rb, phase3)

    _loop(n_seq // gs, group)


def _mod_kernel(c_ref, w_ref, b_ref, o_ref):
    a = _silu(c_ref[...]).astype(BF16)
    o_ref[0] = _dot(a, w_ref[0].astype(BF16)) + b_ref[0]


def _modulation(cvec, ada_w, ada_b):
    rows = cvec.shape[0]
    tn = D_MODEL
    return pl.pallas_call(
        _mod_kernel,
        grid=(DEPTH, 3 * D_MODEL // tn),
        in_specs=[
            pl.BlockSpec((rows, D_MODEL), lambda l, j: (0, 0)),
            pl.BlockSpec((1, D_MODEL, tn), lambda l, j: (l, 0, j)),
            pl.BlockSpec((1, 1, tn), lambda l, j: (l, 0, j)),
        ],
        out_specs=pl.BlockSpec((1, rows, tn), lambda l, j: (l, 0, j)),
        out_shape=jax.ShapeDtypeStruct((DEPTH, rows, 3 * D_MODEL), F32),
        compiler_params=pltpu.CompilerParams(dimension_semantics=("arbitrary", "arbitrary")),
        name="modulation",
    )(cvec, ada_w, ada_b.reshape(DEPTH, 1, 3 * D_MODEL))


def _split_even_kernel(wt_ref, head_ref, main_ref):
    kv_end = Q_RANK + KV_RANK

    def put(dst, col, rows_f32):
        dst[0, :, col:col + LANES] = rows_f32.T.astype(BF16)

    for j in range(kv_end // LANES):
        put(head_ref, j * LANES, wt_ref[0, j * LANES:(j + 1) * LANES, :])
    k_r = wt_ref[0, kv_end:EVEN_HEAD, :]
    put(head_ref, kv_end, jnp.concatenate(
        [jnp.zeros((NOPE_A, D_MODEL), F32), k_r, jnp.zeros((LANES - QK_A, D_MODEL), F32)], axis=0))
    for j in range(main_ref.shape[2] // LANES):
        put(main_ref, j * LANES, wt_ref[0, EVEN_HEAD + j * LANES:EVEN_HEAD + (j + 1) * LANES, :])


def _split_even(w_in_t):
    n_layers, cols, d = w_in_t.shape
    n_head, n_main = Q_RANK + KV_RANK + LANES, cols - EVEN_HEAD
    return pl.pallas_call(
        _split_even_kernel,
        grid=(n_layers,),
        in_specs=[pl.BlockSpec((1, cols, d), lambda l: (l, 0, 0))],
        out_specs=[pl.BlockSpec((1, d, n_head), lambda l: (l, 0, 0)),
                   pl.BlockSpec((1, d, n_main), lambda l: (l, 0, 0))],
        out_shape=[jax.ShapeDtypeStruct((n_layers, d, n_head), BF16),
                   jax.ShapeDtypeStruct((n_layers, d, n_main), BF16)],
        compiler_params=pltpu.CompilerParams(dimension_semantics=("arbitrary",), vmem_limit_bytes=VMEM_LIMIT),
        name="split_even_w_in",
    )(w_in_t)


def _full(a):
    nd = a.ndim
    return a, pl.BlockSpec(a.shape, lambda b, _nd=nd: (0,) * _nd)


def _per_step(a, n_seq):
    nd = a.ndim
    return a, pl.BlockSpec((n_seq,) + a.shape[1:], lambda b, _nd=nd: (b,) + (0,) * (_nd - 1))


def _of_layer(a, i):
    nd = a.ndim
    return a, pl.BlockSpec((1,) + a.shape[1:], lambda b, _nd=nd, _i=i: (_i,) + (0,) * (_nd - 1))


def _seq_of_layer(a, i):
    nd = a.ndim
    return a, pl.BlockSpec((1, 1) + a.shape[2:], lambda b, _nd=nd, _i=i: (b, _i) + (0,) * (_nd - 2))


def _mod_spec(mod, l):
    return mod, pl.BlockSpec((1,) + mod.shape[1:], lambda b, _l=l: (_l, 0, 0))


def _rope_cs(n, rot_dim):
    rows = n // GRID_W
    r = np.repeat(np.arange(rows, dtype=np.float32), GRID_W)
    col = np.tile(np.arange(GRID_W, dtype=np.float32), rows)
    nf = rot_dim // 4
    inv = np.float32(ROPE_BASE) ** (-np.arange(nf, dtype=np.float32) / np.float32(nf))
    ar = r[:, None] * inv
    ac = col[:, None] * inv
    cos = np.concatenate([np.cos(ar), np.cos(ar), np.cos(ac), np.cos(ac)], axis=-1)
    sin = np.concatenate([-np.sin(ar), np.sin(ar), -np.sin(ac), np.sin(ac)], axis=-1)
    return cos.astype(np.float32), sin.astype(np.float32)


def _layer_call(body, x, n_seq, operands, caches, slot, scratch, name):
    b, n, _ = x.shape
    fresh = slot == 0
    shapes = [c if fresh else c.shape for c in caches]
    in_specs = [spec for _, spec in operands]
    args = [a for a, _ in operands]
    aliases = {}
    if not fresh:
        in_specs += [pl.BlockSpec(memory_space=pl.ANY) for _ in caches]
        args += list(caches)
        aliases = {len(operands) + k: 1 + k for k in range(len(caches))}
    out_shape = [jax.ShapeDtypeStruct(x.shape, F32)] + [jax.ShapeDtypeStruct(s, F32) for s in shapes]
    out_specs = [pl.BlockSpec((n_seq, n, D_MODEL), lambda i: (i, 0, 0))]
    for s in shapes:
        if fresh:
            out_specs.append(pl.BlockSpec((n_seq,) + tuple(s[1:]), lambda i: (i, 0, 0, 0)))
        else:
            out_specs.append(pl.BlockSpec((n_seq, 1) + tuple(s[2:]), lambda i, _s=slot: (i, _s, 0, 0)))
    return pl.pallas_call(
        body,
        grid=(b // n_seq,),
        in_specs=in_specs,
        out_specs=out_specs,
        out_shape=out_shape,
        input_output_aliases=aliases,
        scratch_shapes=scratch,
        compiler_params=pltpu.CompilerParams(dimension_semantics=("arbitrary",), vmem_limit_bytes=VMEM_LIMIT),
        name=name,
    )(*args)


def _store_cache(ref, sq, value):
    ref[sq, 0] = value
    for l in range(1, ref.shape[1]):
        ref[sq, l] = jnp.zeros(value.shape, value.dtype)


def _seqs_per_step(b, ctx, ctx_seqs):
    return 1 if ctx is not None else int(np.gcd(b, ctx_seqs))


def _even_layer(x, mod, i, p, ctx, tab, caches, final_g, name):
    mod_op, mod_row = mod
    b, n, _ = x.shape
    n_ctx = 0 if ctx is None else ctx[0].shape[2]
    n_seq = _seqs_per_step(b, ctx, EVEN_CTX_SEQS)
    ops = [_per_step(x, n_seq), mod_op, _of_layer(p["norm_g"], 2 * i),
           _of_layer(p["w_head"], i), _of_layer(p["w_main"], i), _of_layer(p["w_uq"], i), _of_layer(p["w_uk"], i),
           _of_layer(p["w_uv"], i), _of_layer(p["ev_w_out"], i),
           _of_layer(p["gq"], i), _of_layer(p["gkv"], i), _of_layer(p["conv_w"], i), _of_layer(p["conv_b"], i)]
    if ctx is not None:
        ops += [_seq_of_layer(ctx[0], i), _seq_of_layer(ctx[1], i), _full(tab)]
    if final_g is not None:
        ops.append(_full(final_g))
    _, rows_g, rb = _grouping(n, n_seq, EVEN_MAX_ROWS)
    nk_g = n_ctx + rows_g
    scratch = [pltpu.VMEM((rows_g, D_MODEL), BF16), pltpu.VMEM((rows_g, H_A * LANES), BF16),
               pltpu.VMEM((nk_g, LANES), BF16), pltpu.VMEM((nk_g, LANES), BF16),
               pltpu.VMEM((nk_g, H_A * LANES), BF16), pltpu.VMEM((nk_g, 2 * W_A), BF16),
               pltpu.VMEM((rows_g + 2 * CONV_PAD, W_B), F32),
               pltpu.VMEM((2, min(rb, n, DENSE_ROWS), n_ctx + n), F32)]
    assert ctx is not None or n <= rb, "the context pass writes one cache block per sequence"
    body = functools.partial(_even_kernel, n=n, n_ctx=n_ctx, n_seq=n_seq, final=final_g is not None, fresh=i == 0,
                             mod_row=mod_row)
    return _layer_call(body, x, n_seq, ops, caches, i, scratch, name)


def _odd_layer(x, mod, i, p, ctx, tab, caches, final_g, name):
    mod_op, mod_row = mod
    b, n, _ = x.shape
    n_ctx = 0 if ctx is None else ctx[0].shape[3]
    n_seq = _seqs_per_step(b, ctx, ODD_CTX_SEQS)
    ops = [_per_step(x, n_seq), mod_op, _of_layer(p["norm_g"], 2 * i + 1),
           _of_layer(p["od_w_in"], i), _of_layer(p["od_w_out"], i),
           (p["sink"], pl.BlockSpec(memory_space=pltpu.SMEM)), _of_layer(p["gg"], i), _of_layer(p["ws"], i),
           _of_layer(p["bs"], i)]
    if ctx is not None:
        ops += [_seq_of_layer(ctx[0], i), _seq_of_layer(ctx[1], i), _full(tab)]
    if final_g is not None:
        ops.append(_full(final_g))
    pad = WINDOW if ctx is not None else 0
    _, rows_g, rb = _grouping(n, n_seq, ODD_MAX_ROWS)
    tq = min(rb, n, WINDOW_ROWS)
    n_keys = n_ctx + tq + 2 * pad if ctx is not None else n
    scratch = [pltpu.VMEM((rows_g, D_MODEL), BF16), pltpu.VMEM((rows_g + 2 * pad, KV_C * LANES), BF16),
               pltpu.VMEM((rows_g + 2 * pad, KV_C * 2 * LANES), BF16), pltpu.VMEM((2, tq, n_keys), F32)]
    if ctx is not None:
        scratch += [pltpu.VMEM((n_ctx, KV_C * LANES), BF16), pltpu.VMEM((n_ctx, KV_C * 2 * LANES), BF16)]
    assert ctx is not None or n <= rb, "the context pass writes one cache block per sequence"
    body = functools.partial(_odd_kernel, n=n, n_ctx=n_ctx, n_seq=n_seq, final=final_g is not None, fresh=i == 0,
                             layer=i, mod_row=mod_row)
    return _layer_call(body, x, n_seq, ops, caches, i, scratch, name)


def kernel(x_prompt, x_sample, cache_mla_ckv, cache_mla_krope, cache_swa_k, cache_swa_v, c, c_ctx, ada_w, ada_b, norm_g, ev_w_in, ev_w_out, mla_gq, mla_gkv, mla_w_uq, mla_w_uk, mla_w_uv, conv_w, conv_b, od_w_in, od_w_out, swa_sink, gmlp_g, gmlp_ws, gmlp_bs, final_g):
    bp, n_p, _ = x_prompt.shape
    bs_, n_s, _ = x_sample.shape
    n_past = cache_mla_ckv.shape[2]
    n_even, n_odd = ev_w_in.shape[0], od_w_in.shape[0]

    n_rows = -(-(bs_ + 1) // 8) * 8
    cvec = jnp.concatenate([c, c_ctx[None, :], jnp.zeros((n_rows - bs_ - 1, D_MODEL), F32)], axis=0)
    mod = _modulation(cvec, ada_w, ada_b)

    cos_a, sin_a = _rope_cs(n_s, ROPE_A)
    cos_c, sin_c = _rope_cs(n_s, HD_C)
    zero = lambda w: np.zeros((n_s, w), np.float32)
    tab_a = jnp.asarray(np.concatenate([np.ones((n_s, NOPE_A), np.float32), cos_a, zero(LANES - QK_A),
                                        zero(NOPE_A), sin_a, zero(LANES - QK_A)], axis=1))
    tab_c = jnp.asarray(np.concatenate([cos_c, cos_c, sin_c, sin_c], axis=1))

    uq = mla_w_uq.reshape(n_even, Q_RANK, H_A, QK_A)
    uk = mla_w_uk.reshape(n_even, KV_RANK, H_A, NOPE_A)
    w_head, w_main = _split_even(jnp.swapaxes(ev_w_in, 1, 2))
    p = {
        "norm_g": norm_g.reshape(DEPTH, 1, D_MODEL),
        "w_head": w_head,
        "w_main": w_main,
        "w_uq": jnp.pad(uq, ((0, 0), (0, 0), (0, 0), (0, LANES - QK_A))).reshape(n_even, Q_RANK, H_A * LANES).astype(BF16),
        "w_uk": jnp.pad(uk, ((0, 0), (0, 0), (0, 0), (0, LANES - NOPE_A))).reshape(n_even, KV_RANK, H_A * LANES).astype(BF16),
        "w_uv": mla_w_uv.astype(BF16),
        "ev_w_out": ev_w_out.astype(BF16),
        "gq": mla_gq.reshape(n_even, 1, Q_RANK),
        "gkv": mla_gkv.reshape(n_even, 1, KV_RANK),
        "conv_w": conv_w,
        "conv_b": conv_b.reshape(n_even, 1, W_B),
        "od_w_in": od_w_in.astype(BF16),
        "od_w_out": od_w_out.astype(BF16),
        "sink": swa_sink,
        "gg": gmlp_g.reshape(n_odd, 1, W_D),
        "ws": jnp.swapaxes(gmlp_ws.reshape(n_odd, G_D // 2, 2, CHUNK, CHUNK), 2, 3)
                 .reshape(n_odd, G_D // 2, CHUNK, 2 * CHUNK).astype(BF16),
        "bs": jnp.repeat(jnp.swapaxes(gmlp_bs, 1, 2), DG_D, axis=2),
    }
    ckr = jnp.swapaxes(cache_mla_krope, 2, 3)
    swa_k = jnp.transpose(cache_swa_k, (0, 1, 3, 4, 2)).reshape(bs_, n_odd, KV_C * HD_C, n_past)
    swa_v = jnp.transpose(cache_swa_v, (0, 1, 3, 4, 2)).reshape(bs_, n_odd, KV_C * HD_C, n_past)

    new_ckv = (bp, n_even, n_p, KV_RANK)
    new_kr = (bp, n_even, ROPE_A, n_p)
    new_k = (bp, n_odd, KV_C * HD_C, n_p)
    new_v = (bp, n_odd, KV_C * HD_C, n_p)

    fg = final_g.reshape(1, D_MODEL)
    xp, xs = x_prompt, x_sample
    for l in range(DEPTH):
        i = l // 2
        last = fg if l == DEPTH - 1 else None
        mod_p = (_mod_spec(mod, l), bs_)
        mod_s = (_mod_spec(mod, l), None)
        if l % 2 == 0:
            xp, new_ckv, new_kr = _even_layer(xp, mod_p, i, p, None, None, (new_ckv, new_kr), last, f"even{l}_ctx")
            (xs,) = _even_layer(xs, mod_s, i, p, (cache_mla_ckv, ckr), tab_a, (), last, f"even{l}_lat")
        else:
            xp, new_k, new_v = _odd_layer(xp, mod_p, i, p, None, None, (new_k, new_v), last, f"odd{l}_ctx")
            (xs,) = _odd_layer(xs, mod_s, i, p, (swa_k, swa_v), tab_c, (), last, f"odd{l}_lat")

    def heads_last(a):
        return jnp.transpose(a.reshape(bp, n_odd, KV_C, HD_C, n_p), (0, 1, 4, 2, 3))

    return (xp, xs, new_ckv, jnp.swapaxes(new_kr, 2, 3), heads_last(new_k), heads_last(new_v))
```

```python
import functools

import numpy as np
import jax
import jax.numpy as jnp
from jax import lax
from jax.experimental import pallas as pl
from jax.experimental.pallas import tpu as pltpu

D_MODEL = 1024
DEPTH = 4
GRID_W = 64
ROPE_BASE = 10000.0
EPS = 1e-6
NEG_INF = -1e30

H_A = 8
NOPE_A = 64
ROPE_A = 32
V_A = 64
Q_RANK = 256
KV_RANK = 128
QK_A = NOPE_A + ROPE_A
W_A = H_A * V_A
W_B = 512
CONV_W = 3
H_C = 8
KV_C = 2
HD_C = 64
WINDOW = 128
W_C = H_C * HD_C
G_D = 8
W_D = 512
CHUNK = 128
DG_D = W_D // G_D

EVEN_SPLIT = (Q_RANK, KV_RANK, ROPE_A, W_A, W_B, W_B, W_B, W_B)
ODD_SPLIT = (W_C, KV_C * HD_C, KV_C * HD_C, W_C, W_D, W_D, W_D)
EVEN_HEAD = Q_RANK + KV_RANK + ROPE_A

LANES = 128
EVEN_MAX_ROWS = 512
ODD_MAX_ROWS = 1024
DENSE_ROWS = 512
WINDOW_ROWS = 256
CONV_PAD = 8
EVEN_CTX_SEQS = 4
ODD_CTX_SEQS = 2
VMEM_LIMIT = 60 * 1024 * 1024

F32 = jnp.float32
BF16 = jnp.bfloat16
LOG2E = float(np.log2(np.e))


def _dot(a, b):
    return jnp.dot(a, b, preferred_element_type=F32)


def _dot_t(a, b):
    return lax.dot_general(a, b, (((1,), (1,)), ((), ())), preferred_element_type=F32)


def _rms(x, g):
    return x * lax.rsqrt(jnp.mean(x * x, axis=-1, keepdims=True) + EPS) * g


def _silu(x):
    return x * (1.0 / (1.0 + jnp.exp(-x)))


def _gelu(x):
    c = np.float32(np.sqrt(2.0 / np.pi))
    return 0.5 * x * (1.0 + jnp.tanh(c * (x + 0.044715 * (x * x * x))))


def _lane(shape):
    return lax.broadcasted_iota(jnp.int32, shape, 1)


def _pair_select(lo, hi):
    return jnp.where(_lane(lo.shape) < LANES // 2, lo, hi)


def _rope(x, tab, quarter):
    fwd = pltpu.roll(x, LANES - quarter, 1)
    bwd = pltpu.roll(x, quarter, 1)
    partner = jnp.where((_lane(x.shape) & quarter) == 0, fwd, bwd)
    return x * tab[:, 0:LANES] + partner * tab[:, LANES:2 * LANES]


def _dup_halves(x):
    sw = pltpu.roll(x, LANES // 2, 1)
    lo = _lane(x.shape) < LANES // 2
    return jnp.where(lo, x, sw), jnp.where(lo, sw, x)


def _mod_part(mod_ref, row, k):
    return mod_ref[0, pl.ds(row, 1), k * D_MODEL:(k + 1) * D_MODEL]


def _modulated_norm(x, mod_ref, row, ng_ref):
    return _rms(x, ng_ref[0]) * (1.0 + _mod_part(mod_ref, row, 1)) + _mod_part(mod_ref, row, 0)


def _pipelined(n, produce, consume):
    produce(0)
    outs = []
    for i in range(n):
        if i + 1 < n:
            produce(i + 1)
        outs.append(consume(i))
    return outs


def _normalised(o):
    return o[:, 0:LANES] * (1.0 / o[:, LANES:2 * LANES])


def _loop(n, body):
    if n == 1:
        body(0)
    else:
        def step(i, carry):
            body(i)
            return carry
        lax.fori_loop(0, n, step, 0)


def _grouping(n, n_seq, max_rows):
    gs = max(1, min(n_seq, max_rows // n))
    rows_g = gs * n
    return gs, rows_g, min(rows_g, max_rows)


def _group_rows(ref, g, gs, r0, rb):
    if gs == 1:
        return ref[g, pl.ds(r0, rb), :]
    return ref[pl.ds(g * gs, gs), :, :].reshape(rb, ref.shape[-1])


def _set_group_rows(ref, g, gs, r0, rb, value):
    if gs == 1:
        ref[g, pl.ds(r0, rb), :] = value
    else:
        ref[pl.ds(g * gs, gs), :, :] = value.reshape(gs, rb // gs, ref.shape[-1])


def _row_loop(n_rows, rb, body):
    _loop(n_rows // rb, lambda i: body(i * rb if isinstance(i, int) else pl.multiple_of(i * rb, rb)))


def _even_kernel(*refs, n, n_ctx, n_seq, final, fresh, mod_row):
    rope = n_ctx > 0
    row = pl.program_id(0) if mod_row is None else mod_row
    it = iter(refs)
    x_ref, mod_ref, ng_ref = next(it), next(it), next(it)
    wh_ref, wm_ref, wuq_ref, wuk_ref, wuv_ref, wout_ref = (next(it) for _ in range(6))
    gq_ref, gkv_ref, cw_ref, cb_ref = next(it), next(it), next(it), next(it)
    if rope:
        cckv_ref, ckr_ref, tab_ref = next(it), next(it), next(it)
    if final:
        fg_ref = next(it)
    if not rope and not fresh:
        next(it), next(it)
    o_ref = next(it)
    if not rope:
        ckv_out, kr_out = next(it), next(it)
    h_s, q_s, k_s, v_s, t_s, s_s = (next(it) for _ in range(6))

    gs, rows_g, rb = _grouping(n, n_seq, EVEN_MAX_ROWS)
    nk = n_ctx + n
    tq = min(rb, n, DENSE_ROWS)
    q_scale = np.float32(QK_A ** -0.5 * LOG2E)

    zpad = jnp.zeros((CONV_PAD, W_B), F32)
    t_s[0:CONV_PAD, :] = zpad
    t_s[CONV_PAD + rows_g:2 * CONV_PAD + rows_g, :] = zpad

    def expand_kv(rows, c, kr):
        kn = _dot(c, wuk_ref[0])
        for h in range(H_A):
            k_s[rows, h * LANES:(h + 1) * LANES] = (kn[:, h * LANES:(h + 1) * LANES] + kr).astype(BF16)
        v = _dot(c, wuv_ref[0]).astype(BF16)
        ones = jnp.ones((c.shape[0], LANES), BF16)
        for j in range(H_A // 2):
            v_s[rows, 2 * j * LANES:(2 * j + 1) * LANES] = v[:, j * LANES:(j + 1) * LANES]
            v_s[rows, (2 * j + 1) * LANES:(2 * j + 2) * LANES] = ones

    def group(g):
        if rope:
            slab = jnp.concatenate([jnp.zeros((NOPE_A, n_ctx), F32), ckr_ref[0, 0],
                                    jnp.zeros((LANES - QK_A, n_ctx), F32)], axis=0)
            expand_kv(slice(0, n_ctx), cckv_ref[0, 0].astype(BF16), slab.T)

        def phase1(r0):
            rows = pl.ds(r0, rb)
            hb = _modulated_norm(_group_rows(x_ref, g, gs, r0, rb), mod_ref, row, ng_ref).astype(BF16)
            h_s[rows, :] = hb
            ph = _dot(hb, wh_ref[0])
            qn = _rms(ph[:, 0:Q_RANK], gq_ref[0]).astype(BF16)
            qq = _dot(qn, wuq_ref[0])
            ckv = _rms(ph[:, Q_RANK:Q_RANK + KV_RANK], gkv_ref[0])
            kr = ph[:, Q_RANK + KV_RANK:Q_RANK + KV_RANK + LANES]
            if rope:
                tab = tab_ref[rows, :]
                kr = _rope(kr, tab, ROPE_A // 4)
                for h in range(H_A):
                    qh = _rope(qq[:, h * LANES:(h + 1) * LANES], tab, ROPE_A // 4)
                    q_s[rows, h * LANES:(h + 1) * LANES] = (qh * q_scale).astype(BF16)
            else:
                q_s[rows, :] = (qq * q_scale).astype(BF16)
                for s in range(gs):
                    _store_cache(ckv_out, g * gs + s, ckv[s * n:(s + 1) * n, :])
                    _store_cache(kr_out, g * gs + s, kr[s * n:(s + 1) * n, :].T[NOPE_A:QK_A, :])
            expand_kv(pl.ds(n_ctx + r0, rb), ckv.astype(BF16), kr)
            pc = _dot(hb, wm_ref[0, :, 2 * W_B:4 * W_B])
            t_s[pl.ds(CONV_PAD + r0, rb), :] = pc[:, 0:W_B] * pc[:, W_B:2 * W_B]

        _row_loop(rows_g, rb, phase1)

        def phase3(r0):
            rows = pl.ds(r0, rb)
            hb = h_s[rows, :]

            def keys_of(t):
                return slice(None) if gs == 1 else slice(t * n, (t + 1) * n)

            def scores(i):
                t, h = divmod(i, H_A)
                s_s[i % 2] = _dot_t(q_s[pl.ds(r0 + t * tq, tq), h * LANES:(h + 1) * LANES],
                                    k_s[keys_of(t), h * LANES:(h + 1) * LANES])

            def attend(i):
                t, h = divmod(i, H_A)
                s = s_s[i % 2]
                e = jnp.exp2(s - jnp.max(s, axis=-1, keepdims=True)).astype(BF16)
                j = h // 2
                return _normalised(_dot(e, v_s[keys_of(t), 2 * j * LANES:(2 * j + 2) * LANES]))

            heads = _pipelined((rb // tq) * H_A, scores, attend)

            def head_cols(j0, j1):
                return jnp.concatenate(
                    [jnp.concatenate([_pair_select(heads[t * H_A + 2 * j], heads[t * H_A + 2 * j + 1])
                                      for j in range(j0, j1)], axis=-1) for t in range(rb // tq)], axis=0)

            pa = _dot(hb, wm_ref[0, :, 0:2 * W_B])
            wa = W_A // 2
            a_lo = (head_cols(0, H_A // 4) * _silu(pa[:, 0:wa])).astype(BF16)
            a_hi = (head_cols(H_A // 4, H_A // 2) * _silu(pa[:, wa:W_A])).astype(BF16)
            gate_b = pa[:, W_A:W_A + W_B]
            g_b = _dot(hb, wm_ref[0, :, 4 * W_B:5 * W_B])

            t_cur = t_s[pl.ds(CONV_PAD + r0, rb), :]
            rid = lax.broadcasted_iota(jnp.int32, (rb, W_B), 0)
            if n <= rb:
                t_prev = jnp.where(rid % n == 0, 0.0, pltpu.roll(t_cur, 1, 0))
                t_next = jnp.where(rid % n == n - 1, 0.0, pltpu.roll(t_cur, rb - 1, 0))
            else:
                t_up = t_s[pl.ds(r0, CONV_PAD), :]
                t_dn = t_s[pl.ds(CONV_PAD + r0 + rb, CONV_PAD), :]
                t_prev = jnp.where(rid == 0, t_up[CONV_PAD - 1:CONV_PAD, :], pltpu.roll(t_cur, 1, 0))
                t_next = jnp.where(rid == rb - 1, t_dn[0:1, :], pltpu.roll(t_cur, rb - 1, 0))
            conv = cb_ref[0] + t_prev * cw_ref[0, 0:1, :]
            conv = conv + t_cur * cw_ref[0, 1:2, :]
            conv = conv + t_next * cw_ref[0, 2:3, :]
            b_out = gate_b * conv * _silu(g_b)

            y = (_dot(b_out.astype(BF16), wout_ref[0, W_A:W_A + W_B, :]) + _dot(a_lo, wout_ref[0, 0:wa, :])
                 + _dot(a_hi, wout_ref[0, wa:W_A, :]))
            res = _group_rows(x_ref, g, gs, r0, rb) + _mod_part(mod_ref, row, 2) * y
            if final:
                res = _rms(res, fg_ref[...])
            _set_group_rows(o_ref, g, gs, r0, rb, res)

        _row_loop(rows_g, rb, phase3)

    _loop(n_seq // gs, group)


def _odd_kernel(*refs, n, n_ctx, n_seq, final, fresh, layer, mod_row):
    rope = n_ctx > 0
    row = pl.program_id(0) if mod_row is None else mod_row
    it = iter(refs)
    x_ref, mod_ref, ng_ref = next(it), next(it), next(it)
    win_ref, wout_ref = next(it), next(it)
    sink_ref, gg_ref, ws_ref, bs_ref = next(it), next(it), next(it), next(it)
    if rope:
        kc_ref, vc_ref, tab_ref = next(it), next(it), next(it)
    fg_ref = next(it)
    if not rope and not fresh:
        next(it), next(it)
    o_ref = next(it)
    if not rope:
        k_out, v_out = next(it), next(it)
    h_s, kloc_s, vloc_s, s_s = next(it), next(it), next(it), next(it)
    if rope:
        kctx_s, vctx_s = next(it), next(it)

    pad = WINDOW if rope else 0
    gs, rows_g, rb = _grouping(n, n_seq, ODD_MAX_ROWS)
    tq = min(rb, n, WINDOW_ROWS)
    span = tq + 2 * pad
    kw, vw = LANES, 2 * LANES
    q0, k0, g0 = 0, W_C, W_C + 2 * KV_C * HD_C
    q_scale = np.float32(HD_C ** -0.5 * LOG2E)

    def store_keys(dst, rows, x):
        a, b = _dup_halves(x)
        dst[rows, 0:LANES] = a.astype(BF16)
        dst[rows, LANES:2 * LANES] = b.astype(BF16)

    def store_values(dst, rows, x):
        a, b = _dup_halves(x)
        ones = jnp.ones(x.shape, BF16)
        dst[rows, 0:LANES] = a.astype(BF16)
        dst[rows, LANES:2 * LANES] = ones
        dst[rows, 2 * LANES:3 * LANES] = b.astype(BF16)
        dst[rows, 3 * LANES:4 * LANES] = ones

    def group(g):
        if rope:
            kloc_s[0:pad, :] = jnp.zeros((pad, KV_C * kw), BF16)
            kloc_s[pad + n:2 * pad + n, :] = jnp.zeros((pad, KV_C * kw), BF16)
            vloc_s[0:pad, :] = jnp.zeros((pad, KV_C * vw), BF16)
            vloc_s[pad + n:2 * pad + n, :] = jnp.zeros((pad, KV_C * vw), BF16)
            store_keys(kctx_s, slice(None), kc_ref[0, 0].T)
            store_values(vctx_s, slice(None), vc_ref[0, 0].T)

        def phase1(r0):
            rows = pl.ds(r0, rb)
            hb = _modulated_norm(_group_rows(x_ref, g, gs, r0, rb), mod_ref, row, ng_ref).astype(BF16)
            h_s[rows, :] = hb
            p1 = _dot(hb, win_ref[0, :, k0:g0])
            k = p1[:, 0:LANES]
            v = p1[:, LANES:2 * LANES]
            if rope:
                k = _rope(k, tab_ref[rows, :], HD_C // 4)
            else:
                for s in range(gs):
                    _store_cache(k_out, g * gs + s, k[s * n:(s + 1) * n, :].T)
                    _store_cache(v_out, g * gs + s, v[s * n:(s + 1) * n, :].T)
            lrows = pl.ds(pad + r0, rb)
            store_keys(kloc_s, lrows, k)
            store_values(vloc_s, lrows, v)

        _row_loop(rows_g, rb, phase1)

        def phase3(r0):
            rows = pl.ds(r0, rb)
            hb = h_s[rows, :]
            pq = _dot(hb, win_ref[0, :, q0:k0])

            tiles = []
            for t in range(rb // tq):
                t0 = r0 + t * tq
                pqt = pq[t * tq:(t + 1) * tq, :]
                if rope:
                    tab = tab_ref[pl.ds(t0, tq), :]
                    kl = kloc_s[pl.ds(t0, span), :]
                    vl = vloc_s[pl.ds(t0, span), :]
                    ii = lax.broadcasted_iota(jnp.int32, (tq, span), 0)
                    jj = lax.broadcasted_iota(jnp.int32, (tq, span), 1)
                    kpos = jj + (t0 - WINDOW)
                    dlt = jj - ii
                    mask = (dlt >= 0) & (dlt <= 2 * WINDOW) & (kpos >= 0) & (kpos < n)
                else:
                    kl, vl, mask = kloc_s[t * n:(t + 1) * n, :], vloc_s[t * n:(t + 1) * n, :], None
                qhs = []
                for m in range(H_C // 2):
                    qb = pqt[:, m * LANES:(m + 1) * LANES]
                    if rope:
                        qb = _rope(qb, tab, HD_C // 4)
                    qb = qb * q_scale
                    lo = _lane(qb.shape) < LANES // 2
                    qhs.append(jnp.where(lo, qb, 0.0).astype(BF16))
                    qhs.append(jnp.where(lo, 0.0, qb).astype(BF16))
                tiles.append((kl, vl, mask, qhs))

            def kv_head(h):
                return h // (H_C // KV_C)

            def scores(i):
                (kl, _, _, qhs), h = tiles[i // H_C], i % H_C
                j = kv_head(h)
                if rope:
                    s_s[i % 2, :, 0:n_ctx] = _dot_t(qhs[h], kctx_s[:, j * kw:(j + 1) * kw])
                    s_s[i % 2, :, n_ctx:] = _dot_t(qhs[h], kl[:, j * kw:(j + 1) * kw])
                else:
                    s_s[i % 2] = _dot_t(qhs[h], kl[:, j * kw:(j + 1) * kw])

            def attend(i):
                (_, vl, mask, _), h = tiles[i // H_C], i % H_C
                j = kv_head(h)
                sk = sink_ref[layer, h] * np.float32(LOG2E)
                if rope:
                    sc = s_s[i % 2, :, 0:n_ctx]
                    sl = jnp.where(mask, s_s[i % 2, :, n_ctx:], NEG_INF)
                    mx = jnp.maximum(jnp.max(sl, axis=-1, keepdims=True), jnp.max(sc, axis=-1, keepdims=True))
                else:
                    sl = s_s[i % 2]
                    mx = jnp.max(sl, axis=-1, keepdims=True)
                mx = jnp.maximum(mx, sk)
                o = _dot(jnp.exp2(sl - mx).astype(BF16), vl[:, j * vw:(j + 1) * vw])
                if rope:
                    o = o + _dot(jnp.exp2(sc - mx).astype(BF16), vctx_s[:, j * vw:(j + 1) * vw])
                return o[:, 0:LANES] * (1.0 / (o[:, LANES:2 * LANES] + jnp.exp2(sk - mx)))

            heads = _pipelined(len(tiles) * H_C, scores, attend)
            pg = _dot(hb, win_ref[0, :, g0:g0 + W_C + 3 * W_D])
            yc = []
            for t in range(len(tiles)):
                attn_t = jnp.concatenate([_pair_select(heads[t * H_C + 2 * m], heads[t * H_C + 2 * m + 1])
                                          for m in range(H_C // 2)], axis=-1)
                c_out = attn_t * _silu(pg[t * tq:(t + 1) * tq, 0:W_C])
                yc.append(_dot(c_out.astype(BF16), wout_ref[0, 0:W_C, :]))
            u = pg[:, W_C:W_C + W_D]
            vv = pg[:, W_C + W_D:W_C + 2 * W_D]
            g_d = pg[:, W_C + 2 * W_D:W_C + 3 * W_D]

            vvn = _rms(_gelu(vv), gg_ref[0]).astype(BF16)
            chunks = []
            for c in range(rb // CHUNK):
                cols = []
                for jp in range(G_D // 2):
                    vp = vvn[c * CHUNK:(c + 1) * CHUNK, jp * LANES:(jp + 1) * LANES]
                    lo = _lane(vp.shape) < LANES // 2
                    zero = jnp.zeros_like(vp)
                    rhs = jnp.concatenate([jnp.where(lo, vp, zero), jnp.where(lo, zero, vp)], axis=0)
                    cols.append(_dot(ws_ref[0, jp], rhs))
                chunks.append(jnp.concatenate(cols, axis=-1) + bs_ref[0])
            mixed = jnp.concatenate(chunks, axis=0)
            d_out = _gelu(u) * mixed * _silu(g_d)

            y = jnp.concatenate(yc, axis=0) + _dot(d_out.astype(BF16), wout_ref[0, W_C:W_C + W_D, :])
            res = _group_rows(x_ref, g, gs, r0, rb) + _mod_part(mod_ref, row, 2) * y
            if final:
                res = _rms(res, fg_ref[...])
            _set_group_rows(o_ref, g, gs, r0, rb, res)

        _row_loop(rows_g, rb, phase3)

    _loop(n_seq // gs, group)


def _mod_kernel(c_ref, w_ref, b_ref, o_ref):
    a = _silu(c_ref[...]).astype(BF16)
    o_ref[0] = _dot(a, w_ref[0].astype(BF16)) + b_ref[0]


def _modulation(cvec, ada_w, ada_b):
    rows = cvec.shape[0]
    tn = D_MODEL
    return pl.pallas_call(
        _mod_kernel,
        grid=(DEPTH, 3 * D_MODEL // tn),
        in_specs=[
            pl.BlockSpec((rows, D_MODEL), lambda l, j: (0, 0)),
            pl.BlockSpec((1, D_MODEL, tn), lambda l, j: (l, 0, j)),
            pl.BlockSpec((1, 1, tn), lambda l, j: (l, 0, j)),
        ],
        out_specs=pl.BlockSpec((1, rows, tn), lambda l, j: (l, 0, j)),
        out_shape=jax.ShapeDtypeStruct((DEPTH, rows, 3 * D_MODEL), F32),
        compiler_params=pltpu.CompilerParams(dimension_semantics=("arbitrary", "arbitrary")),
        name="modulation",
    )(cvec, ada_w, ada_b.reshape(DEPTH, 1, 3 * D_MODEL))


def _split_even_kernel(wt_ref, head_ref, main_ref):
    kv_end = Q_RANK + KV_RANK

    def put(dst, col, rows_f32):
        dst[0, :, col:col + LANES] = rows_f32.T.astype(BF16)

    for j in range(kv_end // LANES):
        put(head_ref, j * LANES, wt_ref[0, j * LANES:(j + 1) * LANES, :])
    k_r = wt_ref[0, kv_end:EVEN_HEAD, :]
    put(head_ref, kv_end, jnp.concatenate(
        [jnp.zeros((NOPE_A, D_MODEL), F32), k_r, jnp.zeros((LANES - QK_A, D_MODEL), F32)], axis=0))
    for j in range(main_ref.shape[2] // LANES):
        put(main_ref, j * LANES, wt_ref[0, EVEN_HEAD + j * LANES:EVEN_HEAD + (j + 1) * LANES, :])


def _split_even(w_in_t):
    n_layers, cols, d = w_in_t.shape
    n_head, n_main = Q_RANK + KV_RANK + LANES, cols - EVEN_HEAD
    return pl.pallas_call(
        _split_even_kernel,
        grid=(n_layers,),
        in_specs=[pl.BlockSpec((1, cols, d), lambda l: (l, 0, 0))],
        out_specs=[pl.BlockSpec((1, d, n_head), lambda l: (l, 0, 0)),
                   pl.BlockSpec((1, d, n_main), lambda l: (l, 0, 0))],
        out_shape=[jax.ShapeDtypeStruct((n_layers, d, n_head), BF16),
                   jax.ShapeDtypeStruct((n_layers, d, n_main), BF16)],
        compiler_params=pltpu.CompilerParams(dimension_semantics=("arbitrary",), vmem_limit_bytes=VMEM_LIMIT),
        name="split_even_w_in",
    )(w_in_t)


def _full(a):
    nd = a.ndim
    return a, pl.BlockSpec(a.shape, lambda b, _nd=nd: (0,) * _nd)


def _per_step(a, n_seq):
    nd = a.ndim
    return a, pl.BlockSpec((n_seq,) + a.shape[1:], lambda b, _nd=nd: (b,) + (0,) * (_nd - 1))


def _of_layer(a, i):
    nd = a.ndim
    return a, pl.BlockSpec((1,) + a.shape[1:], lambda b, _nd=nd, _i=i: (_i,) + (0,) * (_nd - 1))


def _seq_of_layer(a, i):
    nd = a.ndim
    return a, pl.BlockSpec((1, 1) + a.shape[2:], lambda b, _nd=nd, _i=i: (b, _i) + (0,) * (_nd - 2))


def _mod_spec(mod, l):
    return mod, pl.BlockSpec((1,) + mod.shape[1:], lambda b, _l=l: (_l, 0, 0))


def _rope_cs(n, rot_dim):
    rows = n // GRID_W
    r = np.repeat(np.arange(rows, dtype=np.float32), GRID_W)
    col = np.tile(np.arange(GRID_W, dtype=np.float32), rows)
    nf = rot_dim // 4
    inv = np.float32(ROPE_BASE) ** (-np.arange(nf, dtype=np.float32) / np.float32(nf))
    ar = r[:, None] * inv
    ac = col[:, None] * inv
    cos = np.concatenate([np.cos(ar), np.cos(ar), np.cos(ac), np.cos(ac)], axis=-1)
    sin = np.concatenate([-np.sin(ar), np.sin(ar), -np.sin(ac), np.sin(ac)], axis=-1)
    return cos.astype(np.float32), sin.astype(np.float32)


def _layer_call(body, x, n_seq, operands, caches, slot, scratch, name):
    b, n, _ = x.shape
    fresh = slot == 0
    shapes = [c if fresh else c.shape for c in caches]
    in_specs = [spec for _, spec in operands]
    args = [a for a, _ in operands]
    aliases = {}
    if not fresh:
        in_specs += [pl.BlockSpec(memory_space=pl.ANY) for _ in caches]
        args += list(caches)
        aliases = {len(operands) + k: 1 + k for k in range(len(caches))}
    out_shape = [jax.ShapeDtypeStruct(x.shape, F32)] + [jax.ShapeDtypeStruct(s, F32) for s in shapes]
    out_specs = [pl.BlockSpec((n_seq, n, D_MODEL), lambda i: (i, 0, 0))]
    for s in shapes:
        if fresh:
            out_specs.append(pl.BlockSpec((n_seq,) + tuple(s[1:]), lambda i: (i, 0, 0, 0)))
        else:
            out_specs.append(pl.BlockSpec((n_seq, 1) + tuple(s[2:]), lambda i, _s=slot: (i, _s, 0, 0)))
    return pl.pallas_call(
        body,
        grid=(b // n_seq,),
        in_specs=in_specs,
        out_specs=out_specs,
        out_shape=out_shape,
        input_output_aliases=aliases,
        scratch_shapes=scratch,
        compiler_params=pltpu.CompilerParams(dimension_semantics=("arbitrary",), vmem_limit_bytes=VMEM_LIMIT),
        name=name,
    )(*args)


def _store_cache(ref, sq, value):
    ref[sq, 0] = value
    for l in range(1, ref.shape[1]):
        ref[sq, l] = jnp.zeros(value.shape, value.dtype)


def _seqs_per_step(b, ctx, ctx_seqs):
    return 1 if ctx is not None else int(np.gcd(b, ctx_seqs))


def _even_layer(x, mod, i, p, ctx, tab, caches, final_g, name):
    mod_op, mod_row = mod
    b, n, _ = x.shape
    n_ctx = 0 if ctx is None else ctx[0].shape[2]
    n_seq = _seqs_per_step(b, ctx, EVEN_CTX_SEQS)
    ops = [_per_step(x, n_seq), mod_op, _of_layer(p["norm_g"], 2 * i),
           _of_layer(p["w_head"], i), _of_layer(p["w_main"], i), _of_layer(p["w_uq"], i), _of_layer(p["w_uk"], i),
           _of_layer(p["w_uv"], i), _of_layer(p["ev_w_out"], i),
           _of_layer(p["gq"], i), _of_layer(p["gkv"], i), _of_layer(p["conv_w"], i), _of_layer(p["conv_b"], i)]
    if ctx is not None:
        ops += [_seq_of_layer(ctx[0], i), _seq_of_layer(ctx[1], i), _full(tab)]
    if final_g is not None:
        ops.append(_full(final_g))
    _, rows_g, rb = _grouping(n, n_seq, EVEN_MAX_ROWS)
    nk_g = n_ctx + rows_g
    scratch = [pltpu.VMEM((rows_g, D_MODEL), BF16), pltpu.VMEM((rows_g, H_A * LANES), BF16),
               pltpu.VMEM((nk_g, H_A * LANES), BF16), pltpu.VMEM((nk_g, 2 * W_A), BF16),
               pltpu.VMEM((rows_g + 2 * CONV_PAD, W_B), F32),
               pltpu.VMEM((2, min(rb, n, DENSE_ROWS), n_ctx + n), F32)]
    assert ctx is not None or n <= rb, "the context pass writes one cache block per sequence"
    body = functools.partial(_even_kernel, n=n, n_ctx=n_ctx, n_seq=n_seq, final=final_g is not None, fresh=i == 0,
                             mod_row=mod_row)
    return _layer_call(body, x, n_seq, ops, caches, i, scratch, name)


def _odd_layer(x, mod, i, p, ctx, tab, caches, final_g, is_final, name):
    mod_op, mod_row = mod
    b, n, _ = x.shape
    n_ctx = 0 if ctx is None else ctx[0].shape[3]
    n_seq = _seqs_per_step(b, ctx, ODD_CTX_SEQS)
    ops = [_per_step(x, n_seq), mod_op, _of_layer(p["norm_g"], 2 * i + 1),
           _of_layer(p["od_w_in"], i), _of_layer(p["od_w_out"], i),
           (p["sink"], pl.BlockSpec(memory_space=pltpu.SMEM)), _of_layer(p["gg"], i), _of_layer(p["ws"], i),
           _of_layer(p["bs"], i)]
    if ctx is not None:
        ops += [_seq_of_layer(ctx[0], i), _seq_of_layer(ctx[1], i), _full(tab)]
    ops.append(_full(final_g))
    pad = WINDOW if ctx is not None else 0
    _, rows_g, rb = _grouping(n, n_seq, ODD_MAX_ROWS)
    tq = min(rb, n, WINDOW_ROWS)
    n_keys = n_ctx + tq + 2 * pad if ctx is not None else n
    scratch = [pltpu.VMEM((rows_g, D_MODEL), BF16), pltpu.VMEM((rows_g + 2 * pad, KV_C * LANES), BF16),
               pltpu.VMEM((rows_g + 2 * pad, KV_C * 2 * LANES), BF16), pltpu.VMEM((2, tq, n_keys), F32)]
    if ctx is not None:
        scratch += [pltpu.VMEM((n_ctx, KV_C * LANES), BF16), pltpu.VMEM((n_ctx, KV_C * 2 * LANES), BF16)]
    assert ctx is not None or n <= rb, "the context pass writes one cache block per sequence"
    body = functools.partial(_odd_kernel, n=n, n_ctx=n_ctx, n_seq=n_seq, final=is_final, fresh=i == 0,
                             layer=i, mod_row=mod_row)
    return _layer_call(body, x, n_seq, ops, caches, i, scratch, name)


def kernel(x_prompt, x_sample, cache_mla_ckv, cache_mla_krope, cache_swa_k, cache_swa_v, c, c_ctx, ada_w, ada_b, norm_g, ev_w_in, ev_w_out, mla_gq, mla_gkv, mla_w_uq, mla_w_uk, mla_w_uv, conv_w, conv_b, od_w_in, od_w_out, swa_sink, gmlp_g, gmlp_ws, gmlp_bs, final_g):
    bp, n_p, _ = x_prompt.shape
    bs_, n_s, _ = x_sample.shape
    n_past = cache_mla_ckv.shape[2]
    n_even, n_odd = ev_w_in.shape[0], od_w_in.shape[0]

    n_rows = -(-(bs_ + 1) // 8) * 8
    cvec = jnp.concatenate([c, c_ctx[None, :], jnp.zeros((n_rows - bs_ - 1, D_MODEL), F32)], axis=0)
    mod = _modulation(cvec, ada_w, ada_b)

    cos_a, sin_a = _rope_cs(n_s, ROPE_A)
    cos_c, sin_c = _rope_cs(n_s, HD_C)
    zero = lambda w: np.zeros((n_s, w), np.float32)
    tab_a = jnp.asarray(np.concatenate([np.ones((n_s, NOPE_A), np.float32), cos_a, zero(LANES - QK_A),
                                        zero(NOPE_A), sin_a, zero(LANES - QK_A)], axis=1))
    tab_c = jnp.asarray(np.concatenate([cos_c, cos_c, sin_c, sin_c], axis=1))

    uq = mla_w_uq.reshape(n_even, Q_RANK, H_A, QK_A)
    uk = mla_w_uk.reshape(n_even, KV_RANK, H_A, NOPE_A)
    w_head, w_main = _split_even(jnp.swapaxes(ev_w_in, 1, 2))
    p = {
        "norm_g": norm_g.reshape(DEPTH, 1, D_MODEL),
        "w_head": w_head,
        "w_main": w_main,
        "w_uq": jnp.pad(uq, ((0, 0), (0, 0), (0, 0), (0, LANES - QK_A))).reshape(n_even, Q_RANK, H_A * LANES).astype(BF16),
        "w_uk": jnp.pad(uk, ((0, 0), (0, 0), (0, 0), (0, LANES - NOPE_A))).reshape(n_even, KV_RANK, H_A * LANES).astype(BF16),
        "w_uv": mla_w_uv.astype(BF16),
        "ev_w_out": ev_w_out.astype(BF16),
        "gq": mla_gq.reshape(n_even, 1, Q_RANK),
        "gkv": mla_gkv.reshape(n_even, 1, KV_RANK),
        "conv_w": conv_w,
        "conv_b": conv_b.reshape(n_even, 1, W_B),
        "od_w_in": od_w_in.astype(BF16),
        "od_w_out": od_w_out.astype(BF16),
        "sink": swa_sink,
        "gg": gmlp_g.reshape(n_odd, 1, W_D),
        "ws": jnp.swapaxes(gmlp_ws.reshape(n_odd, G_D // 2, 2, CHUNK, CHUNK), 2, 3)
                 .reshape(n_odd, G_D // 2, CHUNK, 2 * CHUNK).astype(BF16),
        "bs": jnp.repeat(jnp.swapaxes(gmlp_bs, 1, 2), DG_D, axis=2),
    }
    ckr = jnp.swapaxes(cache_mla_krope, 2, 3)
    swa_k = jnp.transpose(cache_swa_k, (0, 1, 3, 4, 2)).reshape(bs_, n_odd, KV_C * HD_C, n_past)
    swa_v = jnp.transpose(cache_swa_v, (0, 1, 3, 4, 2)).reshape(bs_, n_odd, KV_C * HD_C, n_past)

    new_ckv = (bp, n_even, n_p, KV_RANK)
    new_kr = (bp, n_even, ROPE_A, n_p)
    new_k = (bp, n_odd, KV_C * HD_C, n_p)
    new_v = (bp, n_odd, KV_C * HD_C, n_p)

    fg = final_g.reshape(1, D_MODEL)
    xp, xs = x_prompt, x_sample
    for l in range(DEPTH):
        i = l // 2
        last = fg if l == DEPTH - 1 else None
        mod_p = (_mod_spec(mod, l), bs_)
        mod_s = (_mod_spec(mod, l), None)
        if l % 2 == 0:
            xp, new_ckv, new_kr = _even_layer(xp, mod_p, i, p, None, None, (new_ckv, new_kr), last, f"even{l}_ctx")
            (xs,) = _even_layer(xs, mod_s, i, p, (cache_mla_ckv, ckr), tab_a, (), last, f"even{l}_lat")
        else:
            is_last = l == DEPTH - 1
            xp, new_k, new_v = _odd_layer(xp, mod_p, i, p, None, None, (new_k, new_v), fg, is_last, f"odd{l}_ctx")
            (xs,) = _odd_layer(xs, mod_s, i, p, (swa_k, swa_v), tab_c, (), fg, is_last, f"odd{l}_lat")

    def heads_last(a):
        return jnp.transpose(a.reshape(bp, n_odd, KV_C, HD_C, n_p), (0, 1, 4, 2, 3))

    return (xp, xs, new_ckv, jnp.swapaxes(new_kr, 2, 3), heads_last(new_k), heads_last(new_v))
```

```python
import functools

import numpy as np
import jax
import jax.numpy as jnp
from jax import lax
from jax.experimental import pallas as pl
from jax.experimental.pallas import tpu as pltpu

D_MODEL = 1024
DEPTH = 4
GRID_W = 64
ROPE_BASE = 10000.0
EPS = 1e-6
NEG_INF = -1e30

H_A = 8
NOPE_A = 64
ROPE_A = 32
V_A = 64
Q_RANK = 256
KV_RANK = 128
QK_A = NOPE_A + ROPE_A
W_A = H_A * V_A
W_B = 512
CONV_W = 3
H_C = 8
KV_C = 2
HD_C = 64
WINDOW = 128
W_C = H_C * HD_C
G_D = 8
W_D = 512
CHUNK = 128
DG_D = W_D // G_D

EVEN_SPLIT = (Q_RANK, KV_RANK, ROPE_A, W_A, W_B, W_B, W_B, W_B)
ODD_SPLIT = (W_C, KV_C * HD_C, KV_C * HD_C, W_C, W_D, W_D, W_D)
EVEN_HEAD = Q_RANK + KV_RANK + ROPE_A

LANES = 128
EVEN_MAX_ROWS = 512
ODD_MAX_ROWS = 1024
DENSE_ROWS = 512
WINDOW_ROWS = 256
CONV_PAD = 8
EVEN_CTX_SEQS = 4
ODD_CTX_SEQS = 2
VMEM_LIMIT = 60 * 1024 * 1024

F32 = jnp.float32
BF16 = jnp.bfloat16
LOG2E = float(np.log2(np.e))


def _dot(a, b):
    return jnp.dot(a, b, preferred_element_type=F32)


def _dot_t(a, b):
    return lax.dot_general(a, b, (((1,), (1,)), ((), ())), preferred_element_type=F32)


def _rms(x, g):
    return x * lax.rsqrt(jnp.mean(x * x, axis=-1, keepdims=True) + EPS) * g


def _silu(x):
    return x * (1.0 / (1.0 + jnp.exp(-x)))


def _gelu(x):
    c = np.float32(np.sqrt(2.0 / np.pi))
    return 0.5 * x * (1.0 + jnp.tanh(c * (x + 0.044715 * (x * x * x))))


def _lane(shape):
    return lax.broadcasted_iota(jnp.int32, shape, 1)


def _pair_select(lo, hi):
    return jnp.where(_lane(lo.shape) < LANES // 2, lo, hi)


def _rope(x, tab, quarter):
    fwd = pltpu.roll(x, LANES - quarter, 1)
    bwd = pltpu.roll(x, quarter, 1)
    partner = jnp.where((_lane(x.shape) & quarter) == 0, fwd, bwd)
    return x * tab[:, 0:LANES] + partner * tab[:, LANES:2 * LANES]


def _dup_halves(x):
    sw = pltpu.roll(x, LANES // 2, 1)
    lo = _lane(x.shape) < LANES // 2
    return jnp.where(lo, x, sw), jnp.where(lo, sw, x)


def _modulated_norm(x, mod_ref, ng_ref):
    shift = mod_ref[0, 0, 0:1, :]
    scale = mod_ref[0, 0, 1:2, :]
    return _rms(x, ng_ref[0]) * (1.0 + scale) + shift


def _pipelined(n, produce, consume):
    produce(0)
    outs = []
    for i in range(n):
        if i + 1 < n:
            produce(i + 1)
        outs.append(consume(i))
    return outs


def _normalised(o):
    return o[:, 0:LANES] * (1.0 / o[:, LANES:2 * LANES])


def _loop(n, body):
    if n == 1:
        body(0)
    else:
        def step(i, carry):
            body(i)
            return carry
        lax.fori_loop(0, n, step, 0)


def _grouping(n, n_seq, max_rows):
    gs = max(1, min(n_seq, max_rows // n))
    rows_g = gs * n
    return gs, rows_g, min(rows_g, max_rows)


def _group_rows(ref, g, gs, r0, rb):
    if gs == 1:
        return ref[g, pl.ds(r0, rb), :]
    return ref[pl.ds(g * gs, gs), :, :].reshape(rb, ref.shape[-1])


def _set_group_rows(ref, g, gs, r0, rb, value):
    if gs == 1:
        ref[g, pl.ds(r0, rb), :] = value
    else:
        ref[pl.ds(g * gs, gs), :, :] = value.reshape(gs, rb // gs, ref.shape[-1])


def _row_loop(n_rows, rb, body):
    _loop(n_rows // rb, lambda i: body(i * rb if isinstance(i, int) else pl.multiple_of(i * rb, rb)))


def _late_weight_copy(w_hbm, layer, dst, sem, first_step):
    copy = pltpu.make_async_copy(w_hbm.at[pl.ds(layer, 1)], dst, sem)

    @pl.when(first_step)
    def _():
        copy.start()

    return copy


def _await_late_weight(copy, first_step, g):
    first = first_step if isinstance(g, int) and g == 0 else jnp.logical_and(first_step, g == 0)

    @pl.when(first)
    def _():
        copy.wait()


def _even_kernel(*refs, n, n_ctx, n_seq, final, fresh, layer):
    rope = n_ctx > 0
    it = iter(refs)
    x_ref, mod_ref, ng_ref = next(it), next(it), next(it)
    wh_ref, wm_ref, wuq_ref, wuk_ref, wuv_ref, wout_hbm = (next(it) for _ in range(6))
    gq_ref, gkv_ref, cw_ref, cb_ref = next(it), next(it), next(it), next(it)
    if rope:
        cckv_ref, ckr_ref, tab_ref = next(it), next(it), next(it)
    if final:
        fg_ref = next(it)
    if not rope and not fresh:
        next(it), next(it)
    o_ref = next(it)
    if not rope:
        ckv_out, kr_out = next(it), next(it)
    h_s, q_s, ckv_s, kr_s, k_s, v_s, t_s, s_s, wout_ref, wout_sem = (next(it) for _ in range(10))
    first_step = pl.program_id(0) == 0
    wout_copy = _late_weight_copy(wout_hbm, layer, wout_ref, wout_sem, first_step)

    gs, rows_g, rb = _grouping(n, n_seq, EVEN_MAX_ROWS)
    nk = n_ctx + n
    tq = min(rb, n, DENSE_ROWS)
    q_scale = np.float32(QK_A ** -0.5 * LOG2E)

    zpad = jnp.zeros((CONV_PAD, W_B), F32)
    t_s[0:CONV_PAD, :] = zpad
    t_s[CONV_PAD + rows_g:2 * CONV_PAD + rows_g, :] = zpad

    def group(g):
        if rope:
            ckv_s[0:n_ctx, :] = cckv_ref[0, 0].astype(BF16)
            kr_s[0:n_ctx, :] = ckr_ref[0].astype(BF16)

        def phase1(r0):
            rows = pl.ds(r0, rb)
            hb = _modulated_norm(_group_rows(x_ref, g, gs, r0, rb), mod_ref, ng_ref).astype(BF16)
            h_s[rows, :] = hb
            ph = _dot(hb, wh_ref[0])
            qn = _rms(ph[:, 0:Q_RANK], gq_ref[0]).astype(BF16)
            qq = _dot(qn, wuq_ref[0])
            ckv = _rms(ph[:, Q_RANK:Q_RANK + KV_RANK], gkv_ref[0])
            kr = ph[:, Q_RANK + KV_RANK:Q_RANK + KV_RANK + LANES]
            if rope:
                tab = tab_ref[rows, :]
                kr = _rope(kr, tab, ROPE_A // 4)
                for h in range(H_A):
                    qh = _rope(qq[:, h * LANES:(h + 1) * LANES], tab, ROPE_A // 4)
                    q_s[rows, h * LANES:(h + 1) * LANES] = (qh * q_scale).astype(BF16)
            else:
                q_s[rows, :] = (qq * q_scale).astype(BF16)
                for s in range(gs):
                    _store_cache(ckv_out, g * gs + s, ckv[s * n:(s + 1) * n, :])
                    _store_cache(kr_out, g * gs + s, kr[s * n:(s + 1) * n, :].T[NOPE_A:QK_A, :])
            krows = pl.ds(n_ctx + r0, rb)
            ckv_s[krows, :] = ckv.astype(BF16)
            kr_s[krows, :] = kr.astype(BF16)
            pc = _dot(hb, wm_ref[0, :, 2 * W_B:4 * W_B])
            t_s[pl.ds(CONV_PAD + r0, rb), :] = pc[:, 0:W_B] * pc[:, W_B:2 * W_B]

        _row_loop(rows_g, rb, phase1)

        rb2 = int(np.gcd(rb, n_ctx + rows_g))

        def phase2(r0):
            rows = pl.ds(r0, rb2)
            c = ckv_s[rows, :]
            kn = _dot(c, wuk_ref[0])
            krf = kr_s[rows, :].astype(F32)
            for h in range(H_A):
                k_s[rows, h * LANES:(h + 1) * LANES] = (kn[:, h * LANES:(h + 1) * LANES] + krf).astype(BF16)
            v = _dot(c, wuv_ref[0]).astype(BF16)
            ones = jnp.ones((rb2, LANES), BF16)
            for j in range(H_A // 2):
                v_s[rows, 2 * j * LANES:(2 * j + 1) * LANES] = v[:, j * LANES:(j + 1) * LANES]
                v_s[rows, (2 * j + 1) * LANES:(2 * j + 2) * LANES] = ones

        _row_loop(n_ctx + rows_g, rb2, phase2)

        def phase3(r0):
            rows = pl.ds(r0, rb)
            hb = h_s[rows, :]

            def keys_of(t):
                return slice(None) if gs == 1 else slice(t * n, (t + 1) * n)

            def scores(i):
                t, h = divmod(i, H_A)
                s_s[i % 2] = _dot_t(q_s[pl.ds(r0 + t * tq, tq), h * LANES:(h + 1) * LANES],
                                    k_s[keys_of(t), h * LANES:(h + 1) * LANES])

            def attend(i):
                t, h = divmod(i, H_A)
                s = s_s[i % 2]
                e = jnp.exp2(s - jnp.max(s, axis=-1, keepdims=True)).astype(BF16)
                j = h // 2
                return _normalised(_dot(e, v_s[keys_of(t), 2 * j * LANES:(2 * j + 2) * LANES]))

            heads = _pipelined((rb // tq) * H_A, scores, attend)

            def head_cols(j0, j1):
                return jnp.concatenate(
                    [jnp.concatenate([_pair_select(heads[t * H_A + 2 * j], heads[t * H_A + 2 * j + 1])
                                      for j in range(j0, j1)], axis=-1) for t in range(rb // tq)], axis=0)

            pa = _dot(hb, wm_ref[0, :, 0:2 * W_B])
            wa = W_A // 2
            a_lo = (head_cols(0, H_A // 4) * _silu(pa[:, 0:wa])).astype(BF16)
            a_hi = (head_cols(H_A // 4, H_A // 2) * _silu(pa[:, wa:W_A])).astype(BF16)
            gate_b = pa[:, W_A:W_A + W_B]
            g_b = _dot(hb, wm_ref[0, :, 4 * W_B:5 * W_B])

            t_cur = t_s[pl.ds(CONV_PAD + r0, rb), :]
            rid = lax.broadcasted_iota(jnp.int32, (rb, W_B), 0)
            if n <= rb:
                t_prev = jnp.where(rid % n == 0, 0.0, pltpu.roll(t_cur, 1, 0))
                t_next = jnp.where(rid % n == n - 1, 0.0, pltpu.roll(t_cur, rb - 1, 0))
            else:
                t_up = t_s[pl.ds(r0, CONV_PAD), :]
                t_dn = t_s[pl.ds(CONV_PAD + r0 + rb, CONV_PAD), :]
                t_prev = jnp.where(rid == 0, t_up[CONV_PAD - 1:CONV_PAD, :], pltpu.roll(t_cur, 1, 0))
                t_next = jnp.where(rid == rb - 1, t_dn[0:1, :], pltpu.roll(t_cur, rb - 1, 0))
            conv = cb_ref[0] + t_prev * cw_ref[0, 0:1, :]
            conv = conv + t_cur * cw_ref[0, 1:2, :]
            conv = conv + t_next * cw_ref[0, 2:3, :]
            b_out = gate_b * conv * _silu(g_b)

            y = (_dot(b_out.astype(BF16), wout_ref[0, W_A:W_A + W_B, :]) + _dot(a_lo, wout_ref[0, 0:wa, :])
                 + _dot(a_hi, wout_ref[0, wa:W_A, :]))
            res = _group_rows(x_ref, g, gs, r0, rb) + mod_ref[0, 0, 2:3, :] * y
            if final:
                res = _rms(res, fg_ref[...])
            _set_group_rows(o_ref, g, gs, r0, rb, res)

        _await_late_weight(wout_copy, first_step, g)
        _row_loop(rows_g, rb, phase3)

    _loop(n_seq // gs, group)


def _odd_kernel(*refs, n, n_ctx, n_seq, final, fresh, layer):
    rope = n_ctx > 0
    it = iter(refs)
    x_ref, mod_ref, ng_ref = next(it), next(it), next(it)
    win_ref, wout_hbm = next(it), next(it)
    sink_ref, gg_ref, ws_ref, bs_ref = next(it), next(it), next(it), next(it)
    if rope:
        kc_ref, vc_ref, tab_ref = next(it), next(it), next(it)
    if final:
        fg_ref = next(it)
    if not rope and not fresh:
        next(it), next(it)
    o_ref = next(it)
    if not rope:
        k_out, v_out = next(it), next(it)
    h_s, kloc_s, vloc_s, s_s = next(it), next(it), next(it), next(it)
    if rope:
        kctx_s, vctx_s = next(it), next(it)
    wout_ref, wout_sem = next(it), next(it)
    first_step = pl.program_id(0) == 0
    wout_copy = _late_weight_copy(wout_hbm, layer, wout_ref, wout_sem, first_step)

    pad = WINDOW if rope else 0
    gs, rows_g, rb = _grouping(n, n_seq, ODD_MAX_ROWS)
    tq = min(rb, n, WINDOW_ROWS)
    span = tq + 2 * pad
    kw, vw = LANES, 2 * LANES
    q0, k0, g0 = 0, W_C, W_C + 2 * KV_C * HD_C
    q_scale = np.float32(HD_C ** -0.5 * LOG2E)

    def store_keys(dst, rows, x):
        a, b = _dup_halves(x)
        dst[rows, 0:LANES] = a.astype(BF16)
        dst[rows, LANES:2 * LANES] = b.astype(BF16)

    def store_values(dst, rows, x):
        a, b = _dup_halves(x)
        ones = jnp.ones(x.shape, BF16)
        dst[rows, 0:LANES] = a.astype(BF16)
        dst[rows, LANES:2 * LANES] = ones
        dst[rows, 2 * LANES:3 * LANES] = b.astype(BF16)
        dst[rows, 3 * LANES:4 * LANES] = ones

    def group(g):
        if rope:
            kloc_s[0:pad, :] = jnp.zeros((pad, KV_C * kw), BF16)
            kloc_s[pad + n:2 * pad + n, :] = jnp.zeros((pad, KV_C * kw), BF16)
            vloc_s[0:pad, :] = jnp.zeros((pad, KV_C * vw), BF16)
            vloc_s[pad + n:2 * pad + n, :] = jnp.zeros((pad, KV_C * vw), BF16)
            store_keys(kctx_s, slice(None), kc_ref[0, 0].T)
            store_values(vctx_s, slice(None), vc_ref[0, 0].T)

        def phase1(r0):
            rows = pl.ds(r0, rb)
            hb = _modulated_norm(_group_rows(x_ref, g, gs, r0, rb), mod_ref, ng_ref).astype(BF16)
            h_s[rows, :] = hb
            p1 = _dot(hb, win_ref[0, :, k0:g0])
            k = p1[:, 0:LANES]
            v = p1[:, LANES:2 * LANES]
            if rope:
                k = _rope(k, tab_ref[rows, :], HD_C // 4)
            else:
                for s in range(gs):
                    _store_cache(k_out, g * gs + s, k[s * n:(s + 1) * n, :].T)
                    _store_cache(v_out, g * gs + s, v[s * n:(s + 1) * n, :].T)
            lrows = pl.ds(pad + r0, rb)
            store_keys(kloc_s, lrows, k)
            store_values(vloc_s, lrows, v)

        _row_loop(rows_g, rb, phase1)

        def phase3(r0):
            rows = pl.ds(r0, rb)
            hb = h_s[rows, :]
            pq = _dot(hb, win_ref[0, :, q0:k0])

            tiles = []
            for t in range(rb // tq):
                t0 = r0 + t * tq
                pqt = pq[t * tq:(t + 1) * tq, :]
                if rope:
                    tab = tab_ref[pl.ds(t0, tq), :]
                    kl = kloc_s[pl.ds(t0, span), :]
                    vl = vloc_s[pl.ds(t0, span), :]
                    ii = lax.broadcasted_iota(jnp.int32, (tq, span), 0)
                    jj = lax.broadcasted_iota(jnp.int32, (tq, span), 1)
                    kpos = jj + (t0 - WINDOW)
                    dlt = jj - ii
                    mask = (dlt >= 0) & (dlt <= 2 * WINDOW) & (kpos >= 0) & (kpos < n)
                else:
                    kl, vl, mask = kloc_s[t * n:(t + 1) * n, :], vloc_s[t * n:(t + 1) * n, :], None
                qhs = []
                for m in range(H_C // 2):
                    qb = pqt[:, m * LANES:(m + 1) * LANES]
                    if rope:
                        qb = _rope(qb, tab, HD_C // 4)
                    qb = qb * q_scale
                    lo = _lane(qb.shape) < LANES // 2
                    qhs.append(jnp.where(lo, qb, 0.0).astype(BF16))
                    qhs.append(jnp.where(lo, 0.0, qb).astype(BF16))
                tiles.append((kl, vl, mask, qhs))

            def kv_head(h):
                return h // (H_C // KV_C)

            def scores(i):
                (kl, _, _, qhs), h = tiles[i // H_C], i % H_C
                j = kv_head(h)
                if rope:
                    s_s[i % 2, :, 0:n_ctx] = _dot_t(qhs[h], kctx_s[:, j * kw:(j + 1) * kw])
                    s_s[i % 2, :, n_ctx:] = _dot_t(qhs[h], kl[:, j * kw:(j + 1) * kw])
                else:
                    s_s[i % 2] = _dot_t(qhs[h], kl[:, j * kw:(j + 1) * kw])

            def attend(i):
                (_, vl, mask, _), h = tiles[i // H_C], i % H_C
                j = kv_head(h)
                sk = sink_ref[layer, h] * np.float32(LOG2E)
                if rope:
                    sc = s_s[i % 2, :, 0:n_ctx]
                    sl = jnp.where(mask, s_s[i % 2, :, n_ctx:], NEG_INF)
                    mx = jnp.maximum(jnp.max(sl, axis=-1, keepdims=True), jnp.max(sc, axis=-1, keepdims=True))
                else:
                    sl = s_s[i % 2]
                    mx = jnp.max(sl, axis=-1, keepdims=True)
                mx = jnp.maximum(mx, sk)
                o = _dot(jnp.exp2(sl - mx).astype(BF16), vl[:, j * vw:(j + 1) * vw])
                if rope:
                    o = o + _dot(jnp.exp2(sc - mx).astype(BF16), vctx_s[:, j * vw:(j + 1) * vw])
                return o[:, 0:LANES] * (1.0 / (o[:, LANES:2 * LANES] + jnp.exp2(sk - mx)))

            heads = _pipelined(len(tiles) * H_C, scores, attend)
            pg = _dot(hb, win_ref[0, :, g0:g0 + W_C + 3 * W_D])
            yc = []
            for t in range(len(tiles)):
                attn_t = jnp.concatenate([_pair_select(heads[t * H_C + 2 * m], heads[t * H_C + 2 * m + 1])
                                          for m in range(H_C // 2)], axis=-1)
                c_out = attn_t * _silu(pg[t * tq:(t + 1) * tq, 0:W_C])
                yc.append(_dot(c_out.astype(BF16), wout_ref[0, 0:W_C, :]))
            u = pg[:, W_C:W_C + W_D]
            vv = pg[:, W_C + W_D:W_C + 2 * W_D]
            g_d = pg[:, W_C + 2 * W_D:W_C + 3 * W_D]

            vvn = _rms(_gelu(vv), gg_ref[0]).astype(BF16)
            chunks = []
            for c in range(rb // CHUNK):
                cols = []
                for jp in range(G_D // 2):
                    vp = vvn[c * CHUNK:(c + 1) * CHUNK, jp * LANES:(jp + 1) * LANES]
                    lo = _lane(vp.shape) < LANES // 2
                    zero = jnp.zeros_like(vp)
                    rhs = jnp.concatenate([jnp.where(lo, vp, zero), jnp.where(lo, zero, vp)], axis=0)
                    cols.append(_dot(ws_ref[0, jp], rhs))
                chunks.append(jnp.concatenate(cols, axis=-1) + bs_ref[0])
            mixed = jnp.concatenate(chunks, axis=0)
            d_out = _gelu(u) * mixed * _silu(g_d)

            y = jnp.concatenate(yc, axis=0) + _dot(d_out.astype(BF16), wout_ref[0, W_C:W_C + W_D, :])
            res = _group_rows(x_ref, g, gs, r0, rb) + mod_ref[0, 0, 2:3, :] * y
            if final:
                res = _rms(res, fg_ref[...])
            _set_group_rows(o_ref, g, gs, r0, rb, res)

        _await_late_weight(wout_copy, first_step, g)
        _row_loop(rows_g, rb, phase3)

    _loop(n_seq // gs, group)


def _mod_kernel(c_ref, w_ref, b_ref, o_ref):
    a = _silu(c_ref[...]).astype(BF16)
    o_ref[0] = _dot(a, w_ref[0].astype(BF16)) + b_ref[0]


def _modulation(cvec, ada_w, ada_b):
    rows = cvec.shape[0]
    tn = D_MODEL
    return pl.pallas_call(
        _mod_kernel,
        grid=(DEPTH, 3 * D_MODEL // tn),
        in_specs=[
            pl.BlockSpec((rows, D_MODEL), lambda l, j: (0, 0)),
            pl.BlockSpec((1, D_MODEL, tn), lambda l, j: (l, 0, j)),
            pl.BlockSpec((1, 1, tn), lambda l, j: (l, 0, j)),
        ],
        out_specs=pl.BlockSpec((1, rows, tn), lambda l, j: (l, 0, j)),
        out_shape=jax.ShapeDtypeStruct((DEPTH, rows, 3 * D_MODEL), F32),
        compiler_params=pltpu.CompilerParams(dimension_semantics=("arbitrary", "arbitrary")),
        name="modulation",
    )(cvec, ada_w, ada_b.reshape(DEPTH, 1, 3 * D_MODEL))


def _split_even_kernel(wt_ref, head_ref, main_ref):
    kv_end = Q_RANK + KV_RANK

    def put(dst, col, rows_f32):
        dst[0, :, col:col + LANES] = rows_f32.T.astype(BF16)

    for j in range(kv_end // LANES):
        put(head_ref, j * LANES, wt_ref[0, j * LANES:(j + 1) * LANES, :])
    k_r = wt_ref[0, kv_end:EVEN_HEAD, :]
    put(head_ref, kv_end, jnp.concatenate(
        [jnp.zeros((NOPE_A, D_MODEL), F32), k_r, jnp.zeros((LANES - QK_A, D_MODEL), F32)], axis=0))
    for j in range(main_ref.shape[2] // LANES):
        put(main_ref, j * LANES, wt_ref[0, EVEN_HEAD + j * LANES:EVEN_HEAD + (j + 1) * LANES, :])


def _split_even(w_in_t):
    n_layers, cols, d = w_in_t.shape
    n_head, n_main = Q_RANK + KV_RANK + LANES, cols - EVEN_HEAD
    return pl.pallas_call(
        _split_even_kernel,
        grid=(n_layers,),
        in_specs=[pl.BlockSpec((1, cols, d), lambda l: (l, 0, 0))],
        out_specs=[pl.BlockSpec((1, d, n_head), lambda l: (l, 0, 0)),
                   pl.BlockSpec((1, d, n_main), lambda l: (l, 0, 0))],
        out_shape=[jax.ShapeDtypeStruct((n_layers, d, n_head), BF16),
                   jax.ShapeDtypeStruct((n_layers, d, n_main), BF16)],
        compiler_params=pltpu.CompilerParams(dimension_semantics=("arbitrary",), vmem_limit_bytes=VMEM_LIMIT),
        name="split_even_w_in",
    )(w_in_t)


def _full(a):
    nd = a.ndim
    return a, pl.BlockSpec(a.shape, lambda b, _nd=nd: (0,) * _nd)


def _per_step(a, n_seq):
    nd = a.ndim
    return a, pl.BlockSpec((n_seq,) + a.shape[1:], lambda b, _nd=nd: (b,) + (0,) * (_nd - 1))


def _of_layer(a, i):
    nd = a.ndim
    return a, pl.BlockSpec((1,) + a.shape[1:], lambda b, _nd=nd, _i=i: (_i,) + (0,) * (_nd - 1))


def _seq_of_layer(a, i):
    nd = a.ndim
    return a, pl.BlockSpec((1, 1) + a.shape[2:], lambda b, _nd=nd, _i=i: (b, _i) + (0,) * (_nd - 2))


def _mod_spec(mod, l, row):
    if row is None:
        return mod, pl.BlockSpec((1, 1, 3, D_MODEL), lambda b, _l=l: (_l, b, 0, 0))
    return mod, pl.BlockSpec((1, 1, 3, D_MODEL), lambda b, _l=l, _r=row: (_l, _r, 0, 0))


def _rope_cs(n, rot_dim):
    rows = n // GRID_W
    r = np.repeat(np.arange(rows, dtype=np.float32), GRID_W)
    col = np.tile(np.arange(GRID_W, dtype=np.float32), rows)
    nf = rot_dim // 4
    inv = np.float32(ROPE_BASE) ** (-np.arange(nf, dtype=np.float32) / np.float32(nf))
    ar = r[:, None] * inv
    ac = col[:, None] * inv
    cos = np.concatenate([np.cos(ar), np.cos(ar), np.cos(ac), np.cos(ac)], axis=-1)
    sin = np.concatenate([-np.sin(ar), np.sin(ar), -np.sin(ac), np.sin(ac)], axis=-1)
    return cos.astype(np.float32), sin.astype(np.float32)


def _layer_call(body, x, n_seq, operands, caches, slot, scratch, name):
    b, n, _ = x.shape
    fresh = slot == 0
    shapes = [c if fresh else c.shape for c in caches]
    in_specs = [spec for _, spec in operands]
    args = [a for a, _ in operands]
    aliases = {}
    if not fresh:
        in_specs += [pl.BlockSpec(memory_space=pl.ANY) for _ in caches]
        args += list(caches)
        aliases = {len(operands) + k: 1 + k for k in range(len(caches))}
    out_shape = [jax.ShapeDtypeStruct(x.shape, F32)] + [jax.ShapeDtypeStruct(s, F32) for s in shapes]
    out_specs = [pl.BlockSpec((n_seq, n, D_MODEL), lambda i: (i, 0, 0))]
    for s in shapes:
        if fresh:
            out_specs.append(pl.BlockSpec((n_seq,) + tuple(s[1:]), lambda i: (i, 0, 0, 0)))
        else:
            out_specs.append(pl.BlockSpec((n_seq, 1) + tuple(s[2:]), lambda i, _s=slot: (i, _s, 0, 0)))
    return pl.pallas_call(
        body,
        grid=(b // n_seq,),
        in_specs=in_specs,
        out_specs=out_specs,
        out_shape=out_shape,
        input_output_aliases=aliases,
        scratch_shapes=scratch,
        compiler_params=pltpu.CompilerParams(dimension_semantics=("arbitrary",), vmem_limit_bytes=VMEM_LIMIT),
        name=name,
    )(*args)


def _store_cache(ref, sq, value):
    ref[sq, 0] = value
    for l in range(1, ref.shape[1]):
        ref[sq, l] = jnp.zeros(value.shape, value.dtype)


def _seqs_per_step(b, ctx, ctx_seqs):
    return 1 if ctx is not None else int(np.gcd(b, ctx_seqs))


def _even_layer(x, mod_op, i, p, ctx, tab, caches, final_g, name):
    b, n, _ = x.shape
    n_ctx = 0 if ctx is None else ctx[0].shape[2]
    n_seq = _seqs_per_step(b, ctx, EVEN_CTX_SEQS)
    ops = [_per_step(x, n_seq), mod_op, _of_layer(p["norm_g"], 2 * i),
           _of_layer(p["w_head"], i), _of_layer(p["w_main"], i), _of_layer(p["w_uq"], i), _of_layer(p["w_uk"], i),
           _of_layer(p["w_uv"], i), (p["ev_w_out"], pl.BlockSpec(memory_space=pl.ANY)),
           _of_layer(p["gq"], i), _of_layer(p["gkv"], i), _of_layer(p["conv_w"], i), _of_layer(p["conv_b"], i)]
    if ctx is not None:
        ops += [_seq_of_layer(ctx[0], i), _per_step(ctx[1][i], 1), _full(tab)]
    if final_g is not None:
        ops.append(_full(final_g))
    _, rows_g, rb = _grouping(n, n_seq, EVEN_MAX_ROWS)
    nk_g = n_ctx + rows_g
    scratch = [pltpu.VMEM((rows_g, D_MODEL), BF16), pltpu.VMEM((rows_g, H_A * LANES), BF16),
               pltpu.VMEM((nk_g, LANES), BF16), pltpu.VMEM((nk_g, LANES), BF16),
               pltpu.VMEM((nk_g, H_A * LANES), BF16), pltpu.VMEM((nk_g, 2 * W_A), BF16),
               pltpu.VMEM((rows_g + 2 * CONV_PAD, W_B), F32),
               pltpu.VMEM((2, min(rb, n, DENSE_ROWS), n_ctx + n), F32),
               pltpu.VMEM((1,) + p["ev_w_out"].shape[1:], BF16), pltpu.SemaphoreType.DMA(())]
    assert ctx is not None or n <= rb, "the context pass writes one cache block per sequence"
    body = functools.partial(_even_kernel, n=n, n_ctx=n_ctx, n_seq=n_seq, final=final_g is not None, fresh=i == 0,
                             layer=i)
    return _layer_call(body, x, n_seq, ops, caches, i, scratch, name)


def _odd_layer(x, mod_op, i, p, ctx, tab, caches, final_g, name):
    b, n, _ = x.shape
    n_ctx = 0 if ctx is None else ctx[0].shape[3]
    n_seq = _seqs_per_step(b, ctx, ODD_CTX_SEQS)
    ops = [_per_step(x, n_seq), mod_op, _of_layer(p["norm_g"], 2 * i + 1),
           _of_layer(p["od_w_in"], i), (p["od_w_out"], pl.BlockSpec(memory_space=pl.ANY)),
           (p["sink"], pl.BlockSpec(memory_space=pltpu.SMEM)), _of_layer(p["gg"], i), _of_layer(p["ws"], i),
           _of_layer(p["bs"], i)]
    if ctx is not None:
        ops += [_seq_of_layer(ctx[0], i), _seq_of_layer(ctx[1], i), _full(tab)]
    if final_g is not None:
        ops.append(_full(final_g))
    pad = WINDOW if ctx is not None else 0
    _, rows_g, rb = _grouping(n, n_seq, ODD_MAX_ROWS)
    tq = min(rb, n, WINDOW_ROWS)
    n_keys = n_ctx + tq + 2 * pad if ctx is not None else n
    scratch = [pltpu.VMEM((rows_g, D_MODEL), BF16), pltpu.VMEM((rows_g + 2 * pad, KV_C * LANES), BF16),
               pltpu.VMEM((rows_g + 2 * pad, KV_C * 2 * LANES), BF16), pltpu.VMEM((2, tq, n_keys), F32)]
    if ctx is not None:
        scratch += [pltpu.VMEM((n_ctx, KV_C * LANES), BF16), pltpu.VMEM((n_ctx, KV_C * 2 * LANES), BF16)]
    scratch += [pltpu.VMEM((1,) + p["od_w_out"].shape[1:], BF16), pltpu.SemaphoreType.DMA(())]
    assert ctx is not None or n <= rb, "the context pass writes one cache block per sequence"
    body = functools.partial(_odd_kernel, n=n, n_ctx=n_ctx, n_seq=n_seq, final=final_g is not None, fresh=i == 0,
                             layer=i)
    return _layer_call(body, x, n_seq, ops, caches, i, scratch, name)


def kernel(x_prompt, x_sample, cache_mla_ckv, cache_mla_krope, cache_swa_k, cache_swa_v, c, c_ctx, ada_w, ada_b, norm_g, ev_w_in, ev_w_out, mla_gq, mla_gkv, mla_w_uq, mla_w_uk, mla_w_uv, conv_w, conv_b, od_w_in, od_w_out, swa_sink, gmlp_g, gmlp_ws, gmlp_bs, final_g):
    bp, n_p, _ = x_prompt.shape
    bs_, n_s, _ = x_sample.shape
    n_past = cache_mla_ckv.shape[2]
    n_even, n_odd = ev_w_in.shape[0], od_w_in.shape[0]

    n_rows = -(-(bs_ + 1) // 8) * 8
    cvec = jnp.concatenate([c, c_ctx[None, :], jnp.zeros((n_rows - bs_ - 1, D_MODEL), F32)], axis=0)
    mod = _modulation(cvec, ada_w, ada_b).reshape(DEPTH, n_rows, 3, D_MODEL)

    cos_a, sin_a = _rope_cs(n_s, ROPE_A)
    cos_c, sin_c = _rope_cs(n_s, HD_C)
    zero = lambda w: np.zeros((n_s, w), np.float32)
    tab_a = jnp.asarray(np.concatenate([np.ones((n_s, NOPE_A), np.float32), cos_a, zero(LANES - QK_A),
                                        zero(NOPE_A), sin_a, zero(LANES - QK_A)], axis=1))
    tab_c = jnp.asarray(np.concatenate([cos_c, cos_c, sin_c, sin_c], axis=1))

    uq = mla_w_uq.reshape(n_even, Q_RANK, H_A, QK_A)
    uk = mla_w_uk.reshape(n_even, KV_RANK, H_A, NOPE_A)
    w_head, w_main = _split_even(jnp.swapaxes(ev_w_in, 1, 2))
    p = {
        "norm_g": norm_g.reshape(DEPTH, 1, D_MODEL),
        "w_head": w_head,
        "w_main": w_main,
        "w_uq": jnp.pad(uq, ((0, 0), (0, 0), (0, 0), (0, LANES - QK_A))).reshape(n_even, Q_RANK, H_A * LANES).astype(BF16),
        "w_uk": jnp.pad(uk, ((0, 0), (0, 0), (0, 0), (0, LANES - NOPE_A))).reshape(n_even, KV_RANK, H_A * LANES).astype(BF16),
        "w_uv": mla_w_uv.astype(BF16),
        "ev_w_out": ev_w_out.astype(BF16),
        "gq": mla_gq.reshape(n_even, 1, Q_RANK),
        "gkv": mla_gkv.reshape(n_even, 1, KV_RANK),
        "conv_w": conv_w,
        "conv_b": conv_b.reshape(n_even, 1, W_B),
        "od_w_in": od_w_in.astype(BF16),
        "od_w_out": od_w_out.astype(BF16),
        "sink": swa_sink,
        "gg": gmlp_g.reshape(n_odd, 1, W_D),
        "ws": jnp.swapaxes(gmlp_ws.reshape(n_odd, G_D // 2, 2, CHUNK, CHUNK), 2, 3)
                 .reshape(n_odd, G_D // 2, CHUNK, 2 * CHUNK).astype(BF16),
        "bs": jnp.repeat(jnp.swapaxes(gmlp_bs, 1, 2), DG_D, axis=2),
    }
    ckr = jnp.pad(cache_mla_krope, ((0, 0), (0, 0), (0, 0), (NOPE_A, LANES - QK_A)))
    ckr = [ckr[:, i] for i in range(n_even)]
    swa_k = jnp.transpose(cache_swa_k, (0, 1, 3, 4, 2)).reshape(bs_, n_odd, KV_C * HD_C, n_past)
    swa_v = jnp.transpose(cache_swa_v, (0, 1, 3, 4, 2)).reshape(bs_, n_odd, KV_C * HD_C, n_past)

    new_ckv = (bp, n_even, n_p, KV_RANK)
    new_kr = (bp, n_even, ROPE_A, n_p)
    new_k = (bp, n_odd, KV_C * HD_C, n_p)
    new_v = (bp, n_odd, KV_C * HD_C, n_p)

    fg = final_g.reshape(1, D_MODEL)
    xp, xs = x_prompt, x_sample
    for l in range(DEPTH):
        i = l // 2
        last = fg if l == DEPTH - 1 else None
        mod_p = _mod_spec(mod, l, bs_)
        mod_s = _mod_spec(mod, l, None)
        if l % 2 == 0:
            xp, new_ckv, new_kr = _even_layer(xp, mod_p, i, p, None, None, (new_ckv, new_kr), last, f"even{l}_ctx")
            (xs,) = _even_layer(xs, mod_s, i, p, (cache_mla_ckv, ckr), tab_a, (), last, f"even{l}_lat")
        else:
            xp, new_k, new_v = _odd_layer(xp, mod_p, i, p, None, None, (new_k, new_v), last, f"odd{l}_ctx")
            (xs,) = _odd_layer(xs, mod_s, i, p, (swa_k, swa_v), tab_c, (), last, f"odd{l}_lat")

    def heads_last(a):
        return jnp.transpose(a.reshape(bp, n_odd, KV_C, HD_C, n_p), (0, 1, 4, 2, 3))

    return (xp, xs, new_ckv, jnp.swapaxes(new_kr, 2, 3), heads_last(new_k), heads_last(new_v))
```
